```python
import math
import jax, jax.numpy as jnp
from jax import lax
import numpy as np

D_MODEL = 1024
BATCH = 4
SEQ = 8192
DEPTH = 1
DEC_BATCH = 128
DEC_SEQ = 4
PAST_LEN = 8192
PAGE_SIZE = 128

HEAD_DIM = 64
HEADS_PER_GROUP = 4
DILATED_GROUPS = ((128, 1), (512, 4), (2048, 16))
N_ATTN_HEADS = len(DILATED_GROUPS) * HEADS_PER_GROUP
ATTN_WIDTH = N_ATTN_HEADS * HEAD_DIM
ATTN_OUT_WIDTH = HEADS_PER_GROUP * HEAD_DIM
ATTN_SCALE = HEAD_DIM ** -0.5
CONV_CH = D_MODEL // 2
CONV_WIDTH = 31
N_BUCKETS = 32
MAX_DISTANCE = 2048
N_EXPERT_GROUPS = 4
EXPERTS_PER_GROUP = 8
N_EXPERTS = N_EXPERT_GROUPS * EXPERTS_PER_GROUP
TOP_K_INNER = 2
D_FF_EXPERT = D_MODEL // 2
MOE_BLOCK = 128
IN_COLS = 3 * ATTN_WIDTH + 2 * CONV_CH + 2 * D_MODEL
SPLITS = (ATTN_WIDTH, 2 * ATTN_WIDTH, 3 * ATTN_WIDTH, 3 * ATTN_WIDTH + CONV_CH,
          3 * ATTN_WIDTH + 2 * CONV_CH, 3 * ATTN_WIDTH + 2 * CONV_CH + D_MODEL)
EPS = 1e-6
NEG_INF = -1e30

kernel_name = 'hybrid_conformer_dilated_attn_hmoe_step'


def _rmsnorm(x, g):
    xf = x.astype(jnp.float32)
    r = lax.rsqrt(jnp.mean(xf * xf, axis=-1, keepdims=True) + EPS)
    return (xf * r).astype(x.dtype) * g


def _modulation(c, w_mod, b_mod):
    m = jax.nn.silu(c) @ w_mod + b_mod
    return jnp.split(m[:, None, :], 6, axis=-1)


def _t5_bucket(dist):
    max_exact = N_BUCKETS // 2
    d_f = jnp.maximum(dist, 1).astype(jnp.float32)
    large = max_exact + (jnp.log(d_f / max_exact) / math.log(MAX_DISTANCE / max_exact)
                         * (N_BUCKETS - max_exact)).astype(jnp.int32)
    large = jnp.minimum(large, N_BUCKETS - 1)
    return jnp.where(dist < max_exact, dist, large)


def _dilated_prompt(q, k, v, bias_g, dil, n_keys):
    B, S, H, Dh = q.shape
    L = S // dil
    blk = n_keys
    nb = -(-L // blk)
    Lp = nb * blk
    N = B * dil

    def stride_split(t):
        t = t.reshape(B, L, dil, H, Dh).transpose(0, 2, 1, 3, 4).reshape(N, L, H, Dh)
        return jnp.pad(t, ((0, 0), (0, Lp - L), (0, 0), (0, 0)))

    def with_prev(t):
        tp = jnp.pad(t, ((0, 0), (blk, 0), (0, 0), (0, 0))).reshape(N, nb + 1, blk, H, Dh)
        return jnp.concatenate([tp[:, :-1], tp[:, 1:]], axis=2)

    qb = stride_split(q).reshape(N, nb, blk, H, Dh)
    kb = with_prev(stride_split(k))
    vb = with_prev(stride_split(v))
    i = jnp.arange(blk)[:, None]
    j = jnp.arange(2 * blk)[None, :]
    rel = i - j + blk
    blk_idx = jnp.arange(nb)[:, None, None]
    valid = (rel >= 0) & (rel <= n_keys) & (blk_idx * blk - blk + j >= 0)
    bias = bias_g[_t5_bucket(jnp.clip(rel, 0, n_keys) * dil)].transpose(2, 0, 1)
    logits = jnp.einsum('nbqhd,nbkhd->nbhqk', qb, kb,
                        preferred_element_type=jnp.float32) * ATTN_SCALE + bias[None, None]
    logits = jnp.where(valid[None, :, None], logits, NEG_INF)
    lse = jax.nn.logsumexp(logits, axis=-1)
    p = jnp.exp(logits - lse[..., None])
    o = jnp.einsum('nbhqk,nbkhd->nbqhd', p.astype(v.dtype), vb)
    o = o.reshape(N, Lp, H, Dh)[:, :L].reshape(B, dil, L, H, Dh).transpose(0, 2, 1, 3, 4).reshape(B, S, H, Dh)
    lse = lse.transpose(0, 1, 3, 2).reshape(N, Lp, H)[:, :L].reshape(B, dil, L, H).transpose(0, 2, 1, 3).reshape(B, S, H)
    return o, lse


def _dilated_sample(q, k_new, v_new, kv_buf, bias_g, win, dil, n_keys):
    Bd, T, H, Dh = q.shape
    Lc = kv_buf.shape[1]
    kv_all = jnp.concatenate([kv_buf, jnp.stack([k_new, v_new], axis=2)], axis=1)
    steps = jnp.arange(n_keys + 1)
    idx = Lc + jnp.arange(T)[:, None] - steps[None, :] * dil
    valid = idx >= 0
    g = jnp.take(kv_all, jnp.clip(idx, 0), axis=1)
    bias = bias_g[_t5_bucket(steps * dil)].T
    logits = jnp.einsum('bthd,btkhd->bhtk', q, g[:, :, :, 0],
                        preferred_element_type=jnp.float32) * ATTN_SCALE + bias[None, :, None, :]
    logits = jnp.where(valid[None, None], logits, NEG_INF)
    lse = jax.nn.logsumexp(logits, axis=-1)
    p = jnp.exp(logits - lse[..., None])
    o = jnp.einsum('bhtk,btkhd->bthd', p.astype(v_new.dtype), g[:, :, :, 1])
    new_buf = kv_all[:, max(0, Lc + T - win):]
    return o, lse.transpose(0, 2, 1), new_buf


def _depthwise_causal(u_ext, dw_w, dw_b):
    out = lax.conv_general_dilated(u_ext, dw_w[:, None, :].astype(u_ext.dtype), window_strides=(1,),
                                   padding='VALID', dimension_numbers=('NWC', 'WIO', 'NWC'),
                                   feature_group_count=CONV_CH)
    return out + dw_b


def _conv_tail(y, g, b):
    yf = y.astype(jnp.float32)
    mu = jnp.mean(yf, axis=-1, keepdims=True)
    var = jnp.mean(jnp.square(yf - mu), axis=-1, keepdims=True)
    yn = ((yf - mu) * lax.rsqrt(var + EPS)).astype(y.dtype) * g + b
    return jax.nn.silu(yn)


def _hier_moe(h, w_rg, b_rg, w_re, b_re, w_eg, w_eu, w_ed):
    T, D = h.shape
    g_logits = (h @ w_rg).astype(jnp.float32) + b_rg
    g_prob = jax.nn.softmax(g_logits, axis=-1)
    _, grp = lax.top_k(g_logits, 1)
    p_grp = jnp.take_along_axis(g_prob, grp, axis=-1)
    e_all = jnp.einsum('td,dge->tge', h, w_re).astype(jnp.float32) + b_re
    e_logits = jnp.take_along_axis(e_all, grp[:, :, None], axis=1)[:, 0]
    top_v, top_i = lax.top_k(e_logits, TOP_K_INNER)
    weights = p_grp * jax.nn.softmax(top_v, axis=-1)
    expert = grp * EXPERTS_PER_GROUP + top_i
    A = T * TOP_K_INNER
    e_flat = expert.reshape(A)
    tok_flat = jnp.repeat(jnp.arange(T, dtype=jnp.int32), TOP_K_INNER)
    order = jnp.argsort(e_flat)
    e_sorted = e_flat[order]
    counts = jnp.bincount(e_flat, length=N_EXPERTS)
    padded = (counts + MOE_BLOCK - 1) // MOE_BLOCK * MOE_BLOCK
    start = jnp.cumsum(counts) - counts
    pend = jnp.cumsum(padded)
    pstart = pend - padded
    dest_sorted = pstart[e_sorted] + jnp.arange(A) - start[e_sorted]
    n_blocks = -(-A // MOE_BLOCK) + N_EXPERTS
    cap = n_blocks * MOE_BLOCK
    slot_tok = jnp.full((cap,), T, jnp.int32).at[dest_sorted].set(tok_flat[order])
    block_expert = jnp.minimum(jnp.searchsorted(pend, jnp.arange(n_blocks) * MOE_BLOCK, side='right'),
                               N_EXPERTS - 1)
    h_pad = jnp.concatenate([h, jnp.zeros((1, D), h.dtype)], axis=0)
    xs = h_pad[slot_tok].reshape(n_blocks, MOE_BLOCK, D)

    def run_block(args):
        xb, e = args
        return (jax.nn.silu(xb @ w_eg[e]) * (xb @ w_eu[e])) @ w_ed[e]

    ys = lax.map(run_block, (xs, block_expert)).reshape(cap, D)
    dest = jnp.zeros((A,), jnp.int32).at[order].set(dest_sorted)
    y = ys[dest].reshape(T, TOP_K_INNER, D) * weights[..., None].astype(ys.dtype)
    return jnp.sum(y, axis=1)


def _layer(x, c, kv_bufs, conv_buf, rel_bias, lw):
    (norm_mix_g, norm_ffn_g, w_mod, b_mod, w_in, dw_w, dw_b, ln_conv_g, ln_conv_b,
     w_conv_out, w_attn_out, w_out, w_rg, b_rg, w_re, b_re, w_eg, w_eu, w_ed) = lw
    B, T, D = x.shape
    sh1, sc1, g1, sh2, sc2, g2 = _modulation(c, w_mod, b_mod)
    h = _rmsnorm(x, norm_mix_g) * (1 + sc1) + sh1
    z = h @ w_in
    q, k, v, u_lin, u_gate, gate_a, gate_b = jnp.split(z, SPLITS, axis=-1)
    q = q.reshape(B, T, N_ATTN_HEADS, HEAD_DIM)
    k = k.reshape(B, T, N_ATTN_HEADS, HEAD_DIM)
    v = v.reshape(B, T, N_ATTN_HEADS, HEAD_DIM)
    u = u_lin * jax.nn.sigmoid(u_gate)
    hist = jnp.zeros((B, CONV_WIDTH - 1, CONV_CH), u.dtype) if conv_buf is None else conv_buf
    u_ext = jnp.concatenate([hist, u], axis=1)
    new_conv = u_ext[:, -(CONV_WIDTH - 1):]
    a = _conv_tail(_depthwise_causal(u_ext, dw_w, dw_b), ln_conv_g, ln_conv_b) @ w_conv_out
    outs, lses, new_kv = [], [], []
    for gi, (win, dil) in enumerate(DILATED_GROUPS):
        sl = slice(gi * HEADS_PER_GROUP, (gi + 1) * HEADS_PER_GROUP)
        n_keys = win // dil
        qg, kg, vg = q[:, :, sl], k[:, :, sl], v[:, :, sl]
        bg = rel_bias[:, sl]
        if kv_bufs is None:
            o, lse = _dilated_prompt(qg, kg, vg, bg, dil, n_keys)
            kv = jnp.stack([kg, vg], axis=2)[:, -min(win, T):]
        else:
            o, lse, kv = _dilated_sample(qg, kg, vg, kv_bufs[gi], bg, win, dil, n_keys)
        outs.append(o)
        lses.append(lse)
        new_kv.append(kv)
    mix_w = jax.nn.softmax(jnp.stack(lses), axis=0)
    o = jnp.einsum('gbth,gbthd->bthd', mix_w.astype(outs[0].dtype), jnp.stack(outs))
    b = o.reshape(B, T, ATTN_OUT_WIDTH) @ w_attn_out
    mixed = jax.nn.sigmoid(gate_a) * a + jax.nn.sigmoid(gate_b) * b
    x = x + g1 * (mixed @ w_out)
    h2 = _rmsnorm(x, norm_ffn_g) * (1 + sc2) + sh2
    f = _hier_moe(h2.reshape(B * T, D), w_rg, b_rg, w_re, b_re, w_eg, w_eu, w_ed).reshape(B, T, D)
    x = x + g2 * f
    return x, new_kv, new_conv


def setup_inputs(seed: int = 0) -> dict:
    key = jax.random.key(seed)
    ks = iter(jax.random.split(key, 48))

    def nrm(shape, s):
        return jax.random.normal(next(ks), shape, jnp.float32) * s

    D, C, F = D_MODEL, CONV_CH, D_FF_EXPERT
    inp = {}
    inp['x_prompt'] = nrm((BATCH, SEQ, D), 1.0)
    inp['x_sample'] = nrm((DEC_BATCH, DEC_SEQ, D), 1.0)
    inp['c_prompt'] = nrm((BATCH, D), 1.0)
    inp['c_sample'] = nrm((DEC_BATCH, D), 1.0)
    inp['cache_kv_w128'] = nrm((DEPTH, DEC_BATCH, min(128, PAST_LEN), 2, HEADS_PER_GROUP, HEAD_DIM), 1.0)
    inp['cache_kv_w512'] = nrm((DEPTH, DEC_BATCH, min(512, PAST_LEN), 2, HEADS_PER_GROUP, HEAD_DIM), 1.0)
    inp['cache_kv_w2048'] = nrm((DEPTH, DEC_BATCH, min(2048, PAST_LEN), 2, HEADS_PER_GROUP, HEAD_DIM), 1.0)
    inp['state_conv'] = nrm((DEPTH, DEC_BATCH, CONV_WIDTH - 1, C), 0.5)
    inp['rel_bias'] = nrm((N_BUCKETS, N_ATTN_HEADS), 0.5)
    inp['norm_mix_g'] = 1.0 + nrm((DEPTH, D), 0.02)
    inp['norm_ffn_g'] = 1.0 + nrm((DEPTH, D), 0.02)
    inp['w_mod'] = nrm((DEPTH, D, 6 * D), 0.5 * D ** -0.5)
    inp['b_mod'] = nrm((DEPTH, 6 * D), 0.02)
    inp['w_in'] = nrm((DEPTH, D, IN_COLS), D ** -0.5)
    inp['dw_w'] = nrm((DEPTH, CONV_WIDTH, C), CONV_WIDTH ** -0.5)
    inp['dw_b'] = nrm((DEPTH, C), 0.02)
    inp['ln_conv_g'] = 1.0 + nrm((DEPTH, C), 0.02)
    inp['ln_conv_b'] = nrm((DEPTH, C), 0.02)
    inp['w_conv_out'] = nrm((DEPTH, C, D), C ** -0.5)
    inp['w_attn_out'] = nrm((DEPTH, ATTN_OUT_WIDTH, D), ATTN_OUT_WIDTH ** -0.5)
    inp['w_out'] = nrm((DEPTH, D, D), D ** -0.5)
    inp['w_router_group'] = nrm((DEPTH, D, N_EXPERT_GROUPS), D ** -0.5)
    inp['b_router_group'] = nrm((DEPTH, N_EXPERT_GROUPS), 0.01)
    inp['w_router_expert'] = nrm((DEPTH, D, N_EXPERT_GROUPS, EXPERTS_PER_GROUP), D ** -0.5)
    inp['b_router_expert'] = nrm((DEPTH, N_EXPERT_GROUPS, EXPERTS_PER_GROUP), 0.01)
    inp['w_exp_gate'] = nrm((DEPTH, N_EXPERTS, D, F), D ** -0.5)
    inp['w_exp_up'] = nrm((DEPTH, N_EXPERTS, D, F), D ** -0.5)
    inp['w_exp_down'] = nrm((DEPTH, N_EXPERTS, F, D), F ** -0.5)
    inp['norm_final_g'] = 1.0 + nrm((D,), 0.02)
    return inp


def reference(x_prompt, x_sample, c_prompt, c_sample, cache_kv_w128, cache_kv_w512, cache_kv_w2048,
              state_conv, rel_bias, norm_mix_g, norm_ffn_g, w_mod, b_mod, w_in, dw_w, dw_b,
              ln_conv_g, ln_conv_b, w_conv_out, w_attn_out, w_out, w_router_group, b_router_group,
              w_router_expert, b_router_expert, w_exp_gate, w_exp_up, w_exp_down, norm_final_g):
    yp, ys = x_prompt, x_sample
    p128, p512, p2048, pconv = [], [], [], []
    s128, s512, s2048, sconv = [], [], [], []
    for l in range(DEPTH):
        lw = (norm_mix_g[l], norm_ffn_g[l], w_mod[l], b_mod[l], w_in[l], dw_w[l], dw_b[l],
              ln_conv_g[l], ln_conv_b[l], w_conv_out[l], w_attn_out[l], w_out[l],
              w_router_group[l], b_router_group[l], w_router_expert[l], b_router_expert[l],
              w_exp_gate[l], w_exp_up[l], w_exp_down[l])
        yp, kv_p, conv_p = _layer(yp, c_prompt, None, None, rel_bias, lw)
        ys, kv_s, conv_s = _layer(ys, c_sample, (cache_kv_w128[l], cache_kv_w512[l], cache_kv_w2048[l]),
                                  state_conv[l], rel_bias, lw)
        p128.append(kv_p[0]); p512.append(kv_p[1]); p2048.append(kv_p[2]); pconv.append(conv_p)
        s128.append(kv_s[0]); s512.append(kv_s[1]); s2048.append(kv_s[2]); sconv.append(conv_s)
    yp = _rmsnorm(yp, norm_final_g)
    ys = _rmsnorm(ys, norm_final_g)
    return (yp, ys, jnp.stack(p128), jnp.stack(p512), jnp.stack(p2048), jnp.stack(pconv),
            jnp.stack(s128), jnp.stack(s512), jnp.stack(s2048), jnp.stack(sconv))
```

```python
import functools
import math

import jax
import jax.numpy as jnp
from jax import lax
from jax.experimental import pallas as pl
from jax.experimental.pallas import tpu as pltpu

F32 = jnp.float32
BF16 = jnp.bfloat16

D_MODEL = 1024
HEAD_DIM = 64
HEADS_PER_GROUP = 4
GROUP_WIDTH = HEADS_PER_GROUP * HEAD_DIM
DILATED_GROUPS = ((128, 1), (512, 4), (2048, 16))
N_GROUPS = len(DILATED_GROUPS)
ATTN_WIDTH = N_GROUPS * GROUP_WIDTH
CONV_CH = D_MODEL // 2
CONV_WIDTH = 31
CONV_HIST = 32
N_BUCKETS = 32
MAX_DISTANCE = 2048
N_EXPERT_GROUPS = 4
EXPERTS_PER_GROUP = 8
N_EXPERTS = N_EXPERT_GROUPS * EXPERTS_PER_GROUP
D_FF_EXPERT = D_MODEL // 2
EPS = 1e-6
NEG_INF = -1e30

COL_KV = ATTN_WIDTH
COL_ULIN = COL_KV + 2 * ATTN_WIDTH
COL_UGATE = COL_ULIN + CONV_CH
COL_GA = COL_UGATE + CONV_CH
COL_GB = COL_GA + D_MODEL
IN_COLS = COL_GB + D_MODEL

ROUTER_ROWS = 8 + N_EXPERTS

V7X_VMEM_LIMIT = 56 * 1024 * 1024
TOKEN_TILE = 512
ATTN_ROWS = 512
ATTN_BLOCK = 128
MOE_ROWS = 256
COMBINE_TILE = 256


def _sigmoid(x):
    return 1.0 / (1.0 + jnp.exp(-x))


def _silu(x):
    return x * _sigmoid(x)


def _rms_modulate(x, g, sc, sh):
    r = lax.rsqrt(jnp.mean(x * x, axis=-1, keepdims=True) + EPS)
    return ((x * r) * g) * (1.0 + sc) + sh


def _conv_tail(y, g, b):
    mu = jnp.mean(y, axis=-1, keepdims=True)
    yc = y - mu
    var = jnp.mean(yc * yc, axis=-1, keepdims=True)
    return _silu((yc * lax.rsqrt(var + EPS)) * g + b)


def _params(semantics):
    return pltpu.CompilerParams(dimension_semantics=semantics, vmem_limit_bytes=V7X_VMEM_LIMIT)


def _const_spec(shape):
    nd = len(shape)
    return pl.BlockSpec(shape, lambda *_: (0,) * nd)


def _weight_spec(shape):
    nd = len(shape)
    return pl.BlockSpec(shape, lambda *_: (0,) * nd, pipeline_mode=pl.Buffered(1))


def _mod_kernel(c_ref, w_ref, b_ref, o_ref):
    s = _silu(c_ref[...]).astype(BF16)
    o_ref[...] = jnp.dot(s, w_ref[...].astype(BF16), preferred_element_type=F32) + b_ref[...]


def _modulation(c_all, w_mod, b_mod):
    rows = c_all.shape[0]
    n_out = w_mod.shape[1]
    chunk = D_MODEL
    return pl.pallas_call(
        _mod_kernel,
        grid=(n_out // chunk,),
        in_specs=[_const_spec((rows, D_MODEL)),
                  pl.BlockSpec((D_MODEL, chunk), lambda j: (0, j)),
                  pl.BlockSpec((1, chunk), lambda j: (0, j))],
        out_specs=pl.BlockSpec((rows, chunk), lambda j: (0, j)),
        out_shape=jax.ShapeDtypeStruct((rows, n_out), F32),
        compiler_params=_params(("arbitrary",)),
        name="modulation",
    )(c_all, w_mod, b_mod)


def _project_common(hb, win_ref, outs):
    (q_refs, k_refs, v_refs) = outs

    def proj(lo, width):
        return jnp.dot(hb, win_ref[:, lo:lo + width], preferred_element_type=F32)

    zq = proj(0, ATTN_WIDTH)
    for g in range(N_GROUPS):
        q_refs[g][...] = zq[:, g * GROUP_WIDTH:(g + 1) * GROUP_WIDTH].astype(q_refs[g].dtype)
    zkvs = []
    for g in range(N_GROUPS):
        zkv = proj(COL_KV + 2 * GROUP_WIDTH * g, 2 * GROUP_WIDTH)
        k_refs[g][...] = zkv[:, :GROUP_WIDTH].astype(k_refs[g].dtype)
        v_refs[g][...] = zkv[:, GROUP_WIDTH:].astype(v_refs[g].dtype)
        zkvs.append(zkv)
    u = proj(COL_ULIN, CONV_CH) * _sigmoid(proj(COL_UGATE, CONV_CH))
    return zkvs, u, proj


def _k1_kernel(tail_rows, n_tiles,
               x_ref, sh_ref, sc_ref, g_ref, win_ref, dww_ref, dwb_ref, lng_ref, lnb_ref, wco_ref,
               q0, q1, q2, k0, v0, k1, v1, k2, v2, kvt0, kvt1, kvt2, ga_ref, sgb_ref, convp_ref, uext):
    i = pl.program_id(1)
    tm = x_ref.shape[1]

    @pl.when(i == 0)
    def _():
        uext[0:CONV_HIST, :] = jnp.zeros((CONV_HIST, CONV_CH), F32)

    h = _rms_modulate(x_ref[0], g_ref[...], sc_ref[0], sh_ref[0])
    hb = h.astype(BF16)
    zkvs, u, proj = _project_common(hb, win_ref, ((q0, q1, q2), (k0, k1, k2), (v0, v1, v2)))
    for g, kvt in enumerate((kvt0, kvt1, kvt2)):
        tr = kvt.shape[1]
        first = n_tiles - tail_rows[g] // tr

        @pl.when(i >= first)
        def _(kvt=kvt, zkv=zkvs[g], tr=tr):
            kvt[0] = zkv[tm - tr:, :]

    uext[CONV_HIST:CONV_HIST + tm, :] = u
    acc = jnp.zeros((tm, CONV_CH), F32) + dwb_ref[...]
    base = CONV_HIST - (CONV_WIDTH - 1)
    for k in range(CONV_WIDTH):
        acc = acc + dww_ref[k:k + 1, :] * uext[base + k:base + k + tm, :]
    last = uext[tm:tm + CONV_HIST, :]
    uext[0:CONV_HIST, :] = last
    convp_ref[0] = last
    s = _conv_tail(acc, lng_ref[...], lnb_ref[...])
    a = jnp.dot(s.astype(BF16), wco_ref[...], preferred_element_type=F32)
    ga_ref[...] = (_sigmoid(proj(COL_GA, D_MODEL)) * a).astype(BF16)
    sgb_ref[...] = _sigmoid(proj(COL_GB, D_MODEL)).astype(BF16)


def _k1_prompt(x, sh1, sc1, norm_g, w_in_b, dw_w, dw_b, ln_g, ln_b, w_co_b):
    B, S, _ = x.shape
    tm = min(TOKEN_TILE, S)
    nt = S // tm
    tail_rows = tuple(min(win, S) for win, _ in DILATED_GROUPS)
    tail_blk = tuple(min(t, tm) for t in tail_rows)

    def tok_spec(width):
        return pl.BlockSpec((tm, width), lambda b, i: (b * nt + i, 0))

    def tail_spec(g):
        first = nt - tail_rows[g] // tail_blk[g]
        return pl.BlockSpec((1, tail_blk[g], 2 * GROUP_WIDTH), lambda b, i: (b, jnp.maximum(i - first, 0), 0))

    mod_spec = pl.BlockSpec((1, 1, D_MODEL), lambda b, i: (b, 0, 0))
    out_shape = ([jax.ShapeDtypeStruct((B * S, GROUP_WIDTH), BF16)] * 9
                 + [jax.ShapeDtypeStruct((B, tail_rows[g], 2 * GROUP_WIDTH), F32) for g in range(N_GROUPS)]
                 + [jax.ShapeDtypeStruct((B * S, D_MODEL), BF16)] * 2
                 + [jax.ShapeDtypeStruct((B, CONV_HIST, CONV_CH), F32)])
    out_specs = ([tok_spec(GROUP_WIDTH)] * 9 + [tail_spec(g) for g in range(N_GROUPS)]
                 + [tok_spec(D_MODEL)] * 2 + [pl.BlockSpec((1, CONV_HIST, CONV_CH), lambda b, i: (b, 0, 0))])
    return pl.pallas_call(
        functools.partial(_k1_kernel, tail_rows, nt),
        grid=(B, nt),
        in_specs=[pl.BlockSpec((1, tm, D_MODEL), lambda b, i: (b, i, 0)), mod_spec, mod_spec,
                  _const_spec((1, D_MODEL)), _weight_spec((D_MODEL, IN_COLS)),
                  _const_spec((CONV_WIDTH, CONV_CH)), _const_spec((1, CONV_CH)),
                  _const_spec((1, CONV_CH)), _const_spec((1, CONV_CH)), _weight_spec((CONV_CH, D_MODEL))],
        out_specs=out_specs,
        out_shape=out_shape,
        scratch_shapes=[pltpu.VMEM((CONV_HIST + tm, CONV_CH), F32)],
        compiler_params=_params(("arbitrary", "arbitrary")),
        name="inproj_prompt",
    )(x, sh1, sc1, norm_g, w_in_b, dw_w, dw_b, ln_g, ln_b, w_co_b)


def _k1s_kernel(n_steps, x_ref, sh_ref, sc_ref, g_ref, win_ref, st_ref, dww_ref, dwb_ref, lng_ref, lnb_ref, wco_ref,
                q0, q1, q2, k0, v0, k1, v1, k2, v2, ga_ref, sgb_ref, conv_ref):
    bd = st_ref.shape[1]
    hist = st_ref.shape[0]
    h = _rms_modulate(x_ref[...], g_ref[...], sc_ref[...], sh_ref[...])
    hb = h.astype(BF16)
    _, u, proj = _project_common(hb, win_ref, ((q0, q1, q2), (k0, k1, k2), (v0, v1, v2)))

    def ext(j):
        return st_ref[j] if j < hist else u[(j - hist) * bd:(j - hist + 1) * bd, :]

    outs = []
    for t in range(n_steps):
        acc = jnp.zeros((bd, CONV_CH), F32) + dwb_ref[...]
        for k in range(CONV_WIDTH):
            acc = acc + dww_ref[k:k + 1, :] * ext(t + k + hist - (CONV_WIDTH - 1))
        outs.append(acc)
    for j in range(hist):
        conv_ref[j] = ext(j + n_steps)
    s = _conv_tail(jnp.concatenate(outs, axis=0), lng_ref[...], lnb_ref[...])
    a = jnp.dot(s.astype(BF16), wco_ref[...], preferred_element_type=F32)
    ga_ref[...] = (_sigmoid(proj(COL_GA, D_MODEL)) * a).astype(BF16)
    sgb_ref[...] = _sigmoid(proj(COL_GB, D_MODEL)).astype(BF16)


def _k1_sample(x_tm, sh1, sc1, norm_g, w_in_b, state_tm, dw_w, dw_b, ln_g, ln_b, w_co_b, n_steps):
    T = x_tm.shape[0]
    hist, bd, _ = state_tm.shape
    out_shape = ([jax.ShapeDtypeStruct((T, GROUP_WIDTH), F32)] * 9
                 + [jax.ShapeDtypeStruct((T, D_MODEL), BF16)] * 2
                 + [jax.ShapeDtypeStruct((hist, bd, CONV_CH), F32)])
    out_specs = ([_const_spec((T, GROUP_WIDTH))] * 9 + [_const_spec((T, D_MODEL))] * 2
                 + [_const_spec((hist, bd, CONV_CH))])
    return pl.pallas_call(
        functools.partial(_k1s_kernel, n_steps),
        grid=(1,),
        in_specs=[_const_spec((T, D_MODEL)), _const_spec((T, D_MODEL)), _const_spec((T, D_MODEL)),
                  _const_spec((1, D_MODEL)), _const_spec((D_MODEL, IN_COLS)), _const_spec((hist, bd, CONV_CH)),
                  _const_spec((CONV_WIDTH, CONV_CH)), _const_spec((1, CONV_CH)),
                  _const_spec((1, CONV_CH)), _const_spec((1, CONV_CH)), _const_spec((CONV_CH, D_MODEL))],
        out_specs=out_specs,
        out_shape=out_shape,
        compiler_params=_params(("arbitrary",)),
        name="inproj_sample",
    )(x_tm, sh1, sc1, norm_g, w_in_b, state_tm, dw_w, dw_b, ln_g, ln_b, w_co_b)


def _t5_bucket(dist):
    max_exact = N_BUCKETS // 2
    d_f = jnp.maximum(dist, 1).astype(F32)
    large = max_exact + (jnp.log(d_f / max_exact) / math.log(MAX_DISTANCE / max_exact)
                         * (N_BUCKETS - max_exact)).astype(jnp.int32)
    large = jnp.minimum(large, N_BUCKETS - 1)
    return jnp.where(dist < max_exact, dist, large)


def _prompt_bias(rel_bias_g, dil, n_keys):
    blk = n_keys
    i = jnp.arange(blk)[:, None]
    j = jnp.arange(2 * blk)[None, :]
    rel = i - j + blk
    valid = (rel >= 0) & (rel <= n_keys)
    bias = rel_bias_g[_t5_bucket(jnp.clip(rel, 0, n_keys) * dil)]
    bias = jnp.where(valid[:, :, None], bias, NEG_INF).transpose(2, 0, 1)
    return bias.reshape(HEADS_PER_GROUP * blk, 2 * blk).astype(F32)


def _sample_bias(rel_bias_g, dil, n_keys, lc, n_steps, q_rows):
    t = jnp.arange(q_rows)[:, None]
    pos = jnp.arange(lc)[None, :]
    dist_c = lc + t - pos
    dist_n = t - (jnp.arange(128)[None, :] - (128 - n_steps))

    def table(dist, extra):
        ok = (dist >= 0) & (dist % dil == 0) & (dist // dil <= n_keys) & extra & (t < n_steps)
        b = rel_bias_g[_t5_bucket(jnp.clip(dist, 0, None))]
        b = jnp.where(ok[:, :, None], b, NEG_INF)
        b = jnp.where((t >= n_steps)[:, :, None], 0.0, b)
        return b.transpose(2, 0, 1).astype(F32)

    lane_ok = jnp.arange(128)[None, :] >= 128 - n_steps
    return table(dist_c, True), table(dist_n, lane_ok)


def _k2_kernel(q_ref, k_ref, kh_ref, v_ref, vh_ref, bias_ref, o_ref, lse_ref, kbuf, vbuf):
    i = pl.program_id(2)
    rows = q_ref.shape[1]
    blk = ATTN_BLOCK
    kbuf[0:blk, :] = kh_ref[0]
    kbuf[blk:blk + rows, :] = k_ref[0]
    vbuf[0:blk, :] = vh_ref[0]
    vbuf[blk:blk + rows, :] = v_ref[0]
    lane_head = lax.broadcasted_iota(jnp.int32, (blk, GROUP_WIDTH), 1) // HEAD_DIM
    col = lax.broadcasted_iota(jnp.int32, (HEADS_PER_GROUP * blk, 2 * blk), 1)
    first_mask = jnp.where((col < blk) & (i == 0), NEG_INF, 0.0).astype(F32)
    bias = bias_ref[...]
    for j in range(rows // blk):
        qb = q_ref[0, j * blk:(j + 1) * blk, :]
        q4 = jnp.concatenate([jnp.where(lane_head == h, qb, jnp.zeros_like(qb)) for h in range(HEADS_PER_GROUP)], axis=0)
        kc = kbuf[j * blk:(j + 2) * blk, :]
        vc = vbuf[j * blk:(j + 2) * blk, :]
        s = lax.dot_general(q4, kc, (((1,), (1,)), ((), ())), preferred_element_type=F32) + bias
        if j == 0:
            s = s + first_mask
        m = jnp.max(s, axis=-1, keepdims=True)
        p = jnp.exp(s - m)
        l = jnp.sum(p, axis=-1, keepdims=True)
        o4 = jnp.dot(p.astype(BF16), vc, preferred_element_type=F32) * (1.0 / l)
        lse4 = m + jnp.log(l)
        o = jnp.zeros((blk, GROUP_WIDTH), F32)
        lse = jnp.zeros((blk, GROUP_WIDTH), F32)
        for h in range(HEADS_PER_GROUP):
            sel = lane_head == h
            o = jnp.where(sel, o4[h * blk:(h + 1) * blk, :], o)
            lse = jnp.where(sel, lse4[h * blk:(h + 1) * blk, :], lse)
        o_ref[0, j * blk:(j + 1) * blk, :] = o.astype(o_ref.dtype)
        lse_ref[0, j * blk:(j + 1) * blk, :] = lse


def _k2_prompt(q, k, v, bias, B, S, dil):
    L = S // dil
    rows = min(ATTN_ROWS, L)
    per = rows // ATTN_BLOCK
    view = (B, L, dil * GROUP_WIDTH)
    main = pl.BlockSpec((1, rows, GROUP_WIDTH), lambda b, r, i: (b, i, r))
    halo = pl.BlockSpec((1, ATTN_BLOCK, GROUP_WIDTH), lambda b, r, i: (b, jnp.maximum(i * per - 1, 0), r))
    o, lse = pl.pallas_call(
        _k2_kernel,
        grid=(B, dil, L // rows),
        in_specs=[main, main, halo, main, halo, _const_spec(bias.shape)],
        out_specs=[main, main],
        out_shape=[jax.ShapeDtypeStruct(view, BF16), jax.ShapeDtypeStruct(view, F32)],
        scratch_shapes=[pltpu.VMEM((ATTN_BLOCK + rows, GROUP_WIDTH), BF16)] * 2,
        compiler_params=_params(("arbitrary", "arbitrary", "arbitrary")),
        name=f"attn_prompt_d{dil}",
    )(q.reshape(view), k.reshape(view), k.reshape(view), v.reshape(view), v.reshape(view), bias)
    return o.reshape(B * S, GROUP_WIDTH), lse.reshape(B * S, GROUP_WIDTH)


def _k2s_kernel(n_steps, q_ref, c0, c1, c2, t0, t1, t2, bc0, bc1, bc2, bn0, bn1, bn2,
                nc0, nc1, nc2, o0, o1, o2, l0, l1, l2):
    lane = lax.broadcasted_iota(jnp.int32, (2 * GROUP_WIDTH, 128), 1)
    keep = 128 - n_steps
    groups = ((c0, t0, bc0, bn0, nc0, o0, l0), (c1, t1, bc1, bn1, nc1, o1, l1), (c2, t2, bc2, bn2, nc2, o2, l2))
    for g, (c_ref, t_ref, bc_ref, bn_ref, nc_ref, o_ref, l_ref) in enumerate(groups):
        lc = c_ref.shape[2]
        n_tiles = lc // 128
        tail = t_ref[0]
        for c in range(n_tiles):
            cur = pltpu.roll(c_ref[0, :, c * 128:(c + 1) * 128], keep, 1)
            if c + 1 < n_tiles:
                nxt = pltpu.roll(c_ref[0, :, (c + 1) * 128:(c + 2) * 128], keep, 1)
            else:
                nxt = tail
            nc_ref[0, :, c * 128:(c + 1) * 128] = jnp.where(lane < keep, cur, nxt)
        for h in range(HEADS_PER_GROUP):
            lo = g * GROUP_WIDTH + h * HEAD_DIM
            qh = q_ref[0, :, lo:lo + HEAD_DIM].astype(BF16)
            kh = c_ref[0, h * HEAD_DIM:(h + 1) * HEAD_DIM, :].astype(BF16)
            vh = c_ref[0, GROUP_WIDTH + h * HEAD_DIM:GROUP_WIDTH + (h + 1) * HEAD_DIM, :].astype(BF16)
            kt = tail[h * HEAD_DIM:(h + 1) * HEAD_DIM, :].astype(BF16)
            vt = tail[GROUP_WIDTH + h * HEAD_DIM:GROUP_WIDTH + (h + 1) * HEAD_DIM, :].astype(BF16)
            sc = jnp.dot(qh, kh, preferred_element_type=F32) + bc_ref[h]
            sn = jnp.dot(qh, kt, preferred_element_type=F32) + bn_ref[h]
            m = jnp.maximum(jnp.max(sc, axis=-1, keepdims=True), jnp.max(sn, axis=-1, keepdims=True))
            pc = jnp.exp(sc - m)
            pn = jnp.exp(sn - m)
            l = jnp.sum(pc, axis=-1, keepdims=True) + jnp.sum(pn, axis=-1, keepdims=True)
            nt_dims = (((1,), (1,)), ((), ()))
            o = (lax.dot_general(pc.astype(BF16), vh, nt_dims, preferred_element_type=F32)
                 + lax.dot_general(pn.astype(BF16), vt, nt_dims, preferred_element_type=F32)) * (1.0 / l)
            o_ref[0, :, h * HEAD_DIM:(h + 1) * HEAD_DIM] = o
            l_ref[0, :, h * HEAD_DIM:(h + 1) * HEAD_DIM] = jnp.broadcast_to(m + jnp.log(l), o.shape)


def _k2_sample(q_b, caches, tails, biases_c, biases_n, n_steps):
    bd, q_rows, _ = q_b.shape
    per_b = lambda shape: pl.BlockSpec((1,) + shape[1:], lambda b: (b,) + (0,) * (len(shape) - 1))
    ins = [q_b] + list(caches) + list(tails) + list(biases_c) + list(biases_n)
    in_specs = ([per_b(q_b.shape)] + [per_b(c.shape) for c in caches] + [per_b(t.shape) for t in tails]
                + [_const_spec(b.shape) for b in biases_c] + [_const_spec(b.shape) for b in biases_n])
    o_shape = jax.ShapeDtypeStruct((bd, q_rows, GROUP_WIDTH), F32)
    out_shape = [jax.ShapeDtypeStruct(c.shape, F32) for c in caches] + [o_shape] * 6
    out_specs = [per_b(c.shape) for c in caches] + [per_b(o_shape.shape)] * 6
    return pl.pallas_call(
        functools.partial(_k2s_kernel, n_steps),
        grid=(bd,),
        in_specs=in_specs,
        out_specs=out_specs,
        out_shape=out_shape,
        compiler_params=_params(("arbitrary",)),
        name="attn_sample",
    )(*ins)


def _k3_kernel(x_ref, o0, o1, o2, l0, l1, l2, ga_ref, sgb_ref, g1_ref, sh2_ref, sc2_ref, gf_ref,
               wao_ref, wo_ref, wr_ref, br_ref, x2_ref, h2_ref, eid_ref, wts_ref):
    ls = [l0[...], l1[...], l2[...]]
    m = jnp.maximum(jnp.maximum(ls[0], ls[1]), ls[2])
    ws = [jnp.exp(l - m) for l in ls]
    den = ws[0] + ws[1] + ws[2]
    o = (ws[0] * o0[...].astype(F32) + ws[1] * o1[...].astype(F32) + ws[2] * o2[...].astype(F32)) / den
    b = jnp.dot(o.astype(BF16), wao_ref[...], preferred_element_type=F32)
    mixed = ga_ref[...].astype(F32) + sgb_ref[...].astype(F32) * b
    x2 = x_ref[...] + g1_ref[0] * jnp.dot(mixed.astype(BF16), wo_ref[...], preferred_element_type=F32)
    x2_ref[...] = x2
    h2 = _rms_modulate(x2, gf_ref[...], sc2_ref[0], sh2_ref[0])
    h2_ref[...] = h2
    lt = lax.dot_general(wr_ref[...], h2, (((1,), (1,)), ((), ())), preferred_element_type=F32,
                         precision=lax.Precision.HIGHEST) + br_ref[...]
    tm = h2.shape[0]
    gl = lt[0:8, :]
    gmax = jnp.max(gl, axis=0, keepdims=True)
    r8 = lax.broadcasted_iota(jnp.int32, (8, tm), 0)
    grp = jnp.min(jnp.where(gl == gmax, r8, 8), axis=0, keepdims=True)
    p_grp = 1.0 / jnp.sum(jnp.exp(gl - gmax), axis=0, keepdims=True)
    es = jnp.zeros((EXPERTS_PER_GROUP, tm), F32)
    for g in range(N_EXPERT_GROUPS):
        es = jnp.where(grp == g, lt[8 + 8 * g:16 + 8 * g, :], es)
    v1 = jnp.max(es, axis=0, keepdims=True)
    i1 = jnp.min(jnp.where(es == v1, r8, 8), axis=0, keepdims=True)
    rest = jnp.where(r8 == i1, -jnp.inf, es)
    v2 = jnp.max(rest, axis=0, keepdims=True)
    i2 = jnp.min(jnp.where(rest == v2, r8, 8), axis=0, keepdims=True)
    e21 = jnp.exp(v2 - v1)
    w1 = p_grp / (1.0 + e21)
    eid_ref[0, 0:1, :] = grp * EXPERTS_PER_GROUP + i1
    eid_ref[0, 1:2, :] = grp * EXPERTS_PER_GROUP + i2
    wts_ref[0, 0:1, :] = w1
    wts_ref[0, 1:2, :] = w1 * e21


def _k3(x, o_l, g1, sh2, sc2, ga, sgb, norm_ffn_g, w_ao_b, w_o_b, w_r, b_r, tiles_per_mod):
    T = x.shape[0]
    tm = min(TOKEN_TILE, T)
    nt = T // tm
    tok = lambda w: pl.BlockSpec((tm, w), lambda t: (t, 0))
    mod = pl.BlockSpec((1,) + g1.shape[1:], lambda t: (t // tiles_per_mod, 0, 0))
    small = pl.BlockSpec((1, 2, tm), lambda t: (t, 0, 0))
    return pl.pallas_call(
        _k3_kernel,
        grid=(nt,),
        in_specs=[tok(D_MODEL)] + [tok(GROUP_WIDTH)] * 6 + [tok(D_MODEL)] * 2 + [mod] * 3
                 + [_const_spec((1, D_MODEL)), _weight_spec(w_ao_b.shape), _weight_spec(w_o_b.shape),
                    _const_spec(w_r.shape), _const_spec(b_r.shape)],
        out_specs=[tok(D_MODEL), tok(D_MODEL), small, small],
        out_shape=[jax.ShapeDtypeStruct((T, D_MODEL), F32), jax.ShapeDtypeStruct((T, D_MODEL), F32),
                   jax.ShapeDtypeStruct((nt, 2, tm), jnp.int32), jax.ShapeDtypeStruct((nt, 2, tm), F32)],
        compiler_params=_params(("arbitrary",)),
        name="merge_router",
    )(x, o_l[0][0], o_l[1][0], o_l[2][0], o_l[0][1], o_l[1][1], o_l[2][1], ga, sgb, g1, sh2, sc2,
      norm_ffn_g, w_ao_b, w_o_b, w_r, b_r)


def _row_gather_start(idx_ref, idx_row, n_rows, src_hbm, dst, sem):
    def body(r, carry):
        tok = idx_ref[0, idx_row, r]
        pltpu.make_async_copy(src_hbm.at[pl.ds(tok, 1)], dst.at[pl.ds(r, 1)], sem).start()
        return carry

    lax.fori_loop(0, n_rows, body, 0)


def _row_gather_wait(n_rows, src_hbm, dst, sem):
    pltpu.make_async_copy(src_hbm.at[pl.ds(0, n_rows)], dst, sem).wait()


def _k4_kernel(be_ref, idx_ref, idx_next_ref, h_hbm, wg_ref, wu_ref, wd_ref, y_ref, xbuf, sem):
    i = pl.program_id(0)
    n = pl.num_programs(0)
    slot = i % 2
    rows = xbuf.shape[1]

    @pl.when(i == 0)
    def _():
        _row_gather_start(idx_ref, 0, rows, h_hbm, xbuf.at[0], sem.at[0])

    @pl.when(i + 1 < n)
    def _():
        _row_gather_start(idx_next_ref, 0, rows, h_hbm, xbuf.at[1 - slot], sem.at[1 - slot])

    _row_gather_wait(rows, h_hbm, xbuf.at[slot], sem.at[slot])
    xb = xbuf[slot].astype(BF16)
    gate = jnp.dot(xb, wg_ref[0], preferred_element_type=F32)
    up = jnp.dot(xb, wu_ref[0], preferred_element_type=F32)
    mid = (_silu(gate) * up).astype(BF16)
    y_ref[...] = jnp.dot(mid, wd_ref[0], preferred_element_type=F32)


def _k4(block_expert, slot_tok, h2, wg_b, wu_b, wd_b):
    n_blocks = block_expert.shape[0]
    rows = MOE_ROWS
    idx = slot_tok.reshape(n_blocks, 1, rows)
    smem_idx = lambda f: pl.BlockSpec((1, 1, rows), f, memory_space=pltpu.SMEM)
    grid_spec = pltpu.PrefetchScalarGridSpec(
        num_scalar_prefetch=1,
        grid=(n_blocks,),
        in_specs=[smem_idx(lambda i, be: (i, 0, 0)),
                  smem_idx(lambda i, be: (jnp.minimum(i + 1, n_blocks - 1), 0, 0)),
                  pl.BlockSpec(memory_space=pl.ANY),
                  pl.BlockSpec((1, D_MODEL, D_FF_EXPERT), lambda i, be: (be[i], 0, 0)),
                  pl.BlockSpec((1, D_MODEL, D_FF_EXPERT), lambda i, be: (be[i], 0, 0)),
                  pl.BlockSpec((1, D_FF_EXPERT, D_MODEL), lambda i, be: (be[i], 0, 0))],
        out_specs=pl.BlockSpec((rows, D_MODEL), lambda i, be: (i, 0)),
        scratch_shapes=[pltpu.VMEM((2, rows, D_MODEL), F32), pltpu.SemaphoreType.DMA((2,))],
    )
    return pl.pallas_call(
        _k4_kernel,
        grid_spec=grid_spec,
        out_shape=jax.ShapeDtypeStruct((n_blocks * rows, D_MODEL), F32),
        compiler_params=_params(("arbitrary",)),
        name="expert_blocks",
    )(block_expert, idx, idx, h2, wg_b, wu_b, wd_b)


def _k5_kernel(idx_ref, idx_next_ref, x2_ref, w_ref, g2_ref, gfin_ref, ys_hbm, y_ref, ybuf, sem):
    i = pl.program_id(0)
    n = pl.num_programs(0)
    slot = i % 2
    tm = x2_ref.shape[0]

    def start(ref, s):
        for k in range(2):
            _row_gather_start(ref, k, tm, ys_hbm, ybuf.at[s, k], sem.at[s, k])

    @pl.when(i == 0)
    def _():
        start(idx_ref, 0)

    @pl.when(i + 1 < n)
    def _():
        start(idx_next_ref, 1 - slot)

    for k in range(2):
        _row_gather_wait(tm, ys_hbm, ybuf.at[slot, k], sem.at[slot, k])
    w = w_ref[...]
    f = ybuf[slot, 0] * w[:, 0:1] + ybuf[slot, 1] * w[:, 1:2]
    y = x2_ref[...] + g2_ref[0] * f
    r = lax.rsqrt(jnp.mean(y * y, axis=-1, keepdims=True) + EPS)
    y_ref[...] = (y * r) * gfin_ref[...]


def _k5(dest, x2, wts, g2, norm_final_g, ys, tiles_per_mod):
    T = x2.shape[0]
    tm = min(COMBINE_TILE, T)
    nt = T // tm
    idx = dest.reshape(nt, tm, 2).transpose(0, 2, 1)
    smem_idx = lambda f: pl.BlockSpec((1, 2, tm), f, memory_space=pltpu.SMEM)
    return pl.pallas_call(
        _k5_kernel,
        grid=(nt,),
        in_specs=[smem_idx(lambda t: (t, 0, 0)), smem_idx(lambda t: (jnp.minimum(t + 1, nt - 1), 0, 0)),
                  pl.BlockSpec((tm, D_MODEL), lambda t: (t, 0)), pl.BlockSpec((tm, 2), lambda t: (t, 0)),
                  pl.BlockSpec((1,) + g2.shape[1:], lambda t: (t // tiles_per_mod, 0, 0)),
                  _const_spec((1, D_MODEL)), pl.BlockSpec(memory_space=pl.ANY)],
        out_specs=pl.BlockSpec((tm, D_MODEL), lambda t: (t, 0)),
        out_shape=jax.ShapeDtypeStruct((T, D_MODEL), F32),
        scratch_shapes=[pltpu.VMEM((2, 2, tm, D_MODEL), F32), pltpu.SemaphoreType.DMA((2, 2))],
        compiler_params=_params(("arbitrary",)),
        name="combine_norm",
    )(idx, idx, x2, wts, g2, norm_final_g, ys)


def _dispatch_tables(eid):
    T = eid.shape[0]
    A = 2 * T
    e_flat = eid.reshape(A)
    tok_flat = jnp.arange(A, dtype=jnp.int32) // 2
    order = jnp.argsort(e_flat)
    e_sorted = e_flat[order]
    counts = jnp.bincount(e_flat, length=N_EXPERTS)
    padded = (counts + MOE_ROWS - 1) // MOE_ROWS * MOE_ROWS
    start = jnp.cumsum(counts) - counts
    pend = jnp.cumsum(padded)
    pstart = pend - padded
    dest_sorted = (pstart[e_sorted] + jnp.arange(A) - start[e_sorted]).astype(jnp.int32)
    n_blocks = -(-A // MOE_ROWS) + N_EXPERTS
    slot_tok = jnp.zeros((n_blocks * MOE_ROWS,), jnp.int32).at[dest_sorted].set(tok_flat[order])
    block_expert = jnp.minimum(jnp.searchsorted(pend, jnp.arange(n_blocks) * MOE_ROWS, side='right'),
                               N_EXPERTS - 1).astype(jnp.int32)
    dest = jnp.zeros((A,), jnp.int32).at[order].set(dest_sorted).reshape(T, 2)
    return dest, slot_tok, block_expert


def _moe_and_norm(x2, h2, eid_t, wts_t, g2, norm_final_g, wg_b, wu_b, wd_b, tiles_per_mod):
    T = x2.shape[0]
    eid = eid_t.transpose(0, 2, 1).reshape(T, 2)
    wts = wts_t.transpose(0, 2, 1).reshape(T, 2)
    dest, slot_tok, block_expert = _dispatch_tables(eid)
    ys = _k4(block_expert, slot_tok, h2, wg_b, wu_b, wd_b)
    return _k5(dest, x2, wts, g2, norm_final_g, ys, tiles_per_mod)


def _cache_view(cache):
    bd, lc = cache.shape[:2]
    return cache.transpose(0, 2, 3, 4, 1).reshape(bd, 2 * GROUP_WIDTH, lc)


def _cache_unview(view):
    bd, _, lc = view.shape
    return view.reshape(bd, 2, HEADS_PER_GROUP, HEAD_DIM, lc).transpose(0, 4, 1, 2, 3)


def kernel(x_prompt, x_sample, c_prompt, c_sample, cache_kv_w128, cache_kv_w512, cache_kv_w2048, state_conv, rel_bias, norm_mix_g, norm_ffn_g, w_mod, b_mod, w_in, dw_w, dw_b, ln_conv_g, ln_conv_b, w_conv_out, w_attn_out, w_out, w_router_group, b_router_group, w_router_expert, b_router_expert, w_exp_gate, w_exp_up, w_exp_down, norm_final_g):
    assert norm_mix_g.shape[0] == 1, "single layer"
    B, S, D = x_prompt.shape
    Bd, Td, _ = x_sample.shape
    caches_in = (cache_kv_w128[0], cache_kv_w512[0], cache_kv_w2048[0])
    for (win, dil), c in zip(DILATED_GROUPS, caches_in):
        assert c.shape[1] >= (win // dil) * dil and c.shape[1] % 128 == 0 and S % (dil * ATTN_BLOCK) == 0

    wi = w_in[0]
    q_cols = wi[:, :ATTN_WIDTH] * (HEAD_DIM ** -0.5)
    kv_cols = []
    for g in range(N_GROUPS):
        kv_cols += [wi[:, ATTN_WIDTH + g * GROUP_WIDTH:ATTN_WIDTH + (g + 1) * GROUP_WIDTH],
                    wi[:, 2 * ATTN_WIDTH + g * GROUP_WIDTH:2 * ATTN_WIDTH + (g + 1) * GROUP_WIDTH]]
    w_in_b = jnp.concatenate([q_cols] + kv_cols + [wi[:, 3 * ATTN_WIDTH:]], axis=1).astype(BF16)
    w_co_b = w_conv_out[0].astype(BF16)
    w_ao_b = w_attn_out[0].astype(BF16)
    w_o_b = w_out[0].astype(BF16)
    wg_b, wu_b, wd_b = w_exp_gate[0].astype(BF16), w_exp_up[0].astype(BF16), w_exp_down[0].astype(BF16)
    w_r = jnp.zeros((ROUTER_ROWS, D), F32)
    w_r = w_r.at[0:N_EXPERT_GROUPS].set(w_router_group[0].T)
    w_r = w_r.at[8:].set(w_router_expert[0].reshape(D, N_EXPERTS).T)
    b_r = jnp.full((ROUTER_ROWS, 1), NEG_INF, F32)
    b_r = b_r.at[0:N_EXPERT_GROUPS, 0].set(b_router_group[0])
    b_r = b_r.at[8:, 0].set(b_router_expert[0].reshape(N_EXPERTS))
    row = lambda v: v.reshape(1, -1)

    n_seq = B + Bd
    n_seq_pad = -(-n_seq // 16) * 16
    c_all = jnp.concatenate([c_prompt, c_sample, jnp.zeros((n_seq_pad - n_seq, D), F32)], axis=0)
    mod = _modulation(c_all, w_mod[0], row(b_mod[0]))
    mod_p = [mod[:B, j * D:(j + 1) * D].reshape(B, 1, D) for j in range(6)]
    mod_s = [jnp.tile(mod[B:B + Bd, j * D:(j + 1) * D], (Td, 1)) for j in range(6)]

    (q0, q1, q2, k0, v0, k1, v1, k2, v2, kvt0, kvt1, kvt2, ga, sgb, convp) = _k1_prompt(
        x_prompt, mod_p[0], mod_p[1], row(norm_mix_g[0]), w_in_b, dw_w[0], row(dw_b[0]),
        row(ln_conv_g[0]), row(ln_conv_b[0]), w_co_b)
    o_l = []
    for g, ((win, dil), qg, kg, vg) in enumerate(zip(DILATED_GROUPS, (q0, q1, q2), (k0, k1, k2), (v0, v1, v2))):
        bias = _prompt_bias(rel_bias[:, g * HEADS_PER_GROUP:(g + 1) * HEADS_PER_GROUP], dil, win // dil)
        o_l.append(_k2_prompt(qg, kg, vg, bias, B, S, dil))
    xp = x_prompt.reshape(B * S, D)
    tm3 = min(TOKEN_TILE, B * S)
    x2, h2, eid, wts = _k3(xp, o_l, mod_p[2], mod_p[3], mod_p[4], ga, sgb, row(norm_ffn_g[0]),
                           w_ao_b, w_o_b, w_r, b_r, S // tm3)
    y_prompt = _moe_and_norm(x2, h2, eid, wts, mod_p[5], row(norm_final_g), wg_b, wu_b, wd_b,
                             S // min(COMBINE_TILE, S)).reshape(B, S, D)
    kv_p = [kvt.reshape(1, B, kvt.shape[1], 2, HEADS_PER_GROUP, HEAD_DIM) for kvt in (kvt0, kvt1, kvt2)]
    conv_p = convp[:, CONV_HIST - (CONV_WIDTH - 1):, :][None]

    Ts = Td * Bd
    xs = x_sample.transpose(1, 0, 2).reshape(Ts, D)
    state_tm = state_conv[0].transpose(1, 0, 2)
    (sq0, sq1, sq2, sk0, sv0, sk1, sv1, sk2, sv2, ga_s, sgb_s, conv_tm) = _k1_sample(
        xs, mod_s[0], mod_s[1], row(norm_mix_g[0]), w_in_b, state_tm, dw_w[0], row(dw_b[0]),
        row(ln_conv_g[0]), row(ln_conv_b[0]), w_co_b, Td)
    q_rows = 16
    q_b = jnp.concatenate([sq0, sq1, sq2], axis=1).reshape(Td, Bd, ATTN_WIDTH).transpose(1, 0, 2)
    q_b = jnp.pad(q_b, ((0, 0), (0, q_rows - Td), (0, 0)))
    views, tails, biases_c, biases_n = [], [], [], []
    for g, ((win, dil), c, sk, sv) in enumerate(zip(DILATED_GROUPS, caches_in, (sk0, sk1, sk2), (sv0, sv1, sv2))):
        lc = c.shape[1]
        views.append(_cache_view(c))
        new_kv = jnp.concatenate([sk, sv], axis=1).reshape(Td, Bd, 2 * GROUP_WIDTH).transpose(1, 2, 0)
        tails.append(jnp.pad(new_kv, ((0, 0), (0, 0), (128 - Td, 0))))
        bc, bn = _sample_bias(rel_bias[:, g * HEADS_PER_GROUP:(g + 1) * HEADS_PER_GROUP], dil, win // dil, lc, Td, q_rows)
        biases_c.append(bc)
        biases_n.append(bn)
    (nc0, nc1, nc2, so0, so1, so2, sl0, sl1, sl2) = _k2_sample(q_b, views, tails, biases_c, biases_n, Td)
    to_tm = lambda a: a[:, :Td, :].transpose(1, 0, 2).reshape(Ts, GROUP_WIDTH)
    o_l_s = [(to_tm(so0), to_tm(sl0)), (to_tm(so1), to_tm(sl1)), (to_tm(so2), to_tm(sl2))]
    tm3s = min(TOKEN_TILE, Ts)
    tiled = lambda m: m.reshape(Ts // tm3s, tm3s, D)
    x2s, h2s, eid_s, wts_s = _k3(xs, o_l_s, tiled(mod_s[2]), tiled(mod_s[3]), tiled(mod_s[4]), ga_s, sgb_s,
                                 row(norm_ffn_g[0]), w_ao_b, w_o_b, w_r, b_r, 1)
    tm5s = min(COMBINE_TILE, Ts)
    y_s = _moe_and_norm(x2s, h2s, eid_s, wts_s, mod_s[5].reshape(Ts // tm5s, tm5s, D), row(norm_final_g),
                        wg_b, wu_b, wd_b, 1)
    y_sample = y_s.reshape(Td, Bd, D).transpose(1, 0, 2)
    kv_s = [_cache_unview(nc)[None] for nc in (nc0, nc1, nc2)]
    conv_s = conv_tm.transpose(1, 0, 2)[None]

    return (y_prompt, y_sample, kv_p[0], kv_p[1], kv_p[2], conv_p,
            kv_s[0], kv_s[1], kv_s[2], conv_s)
```

```python
import functools
import math

import jax
import jax.numpy as jnp
from jax import lax
from jax.experimental import pallas as pl
from jax.experimental.pallas import tpu as pltpu

F32 = jnp.float32
BF16 = jnp.bfloat16

D_MODEL = 1024
HEAD_DIM = 64
HEADS_PER_GROUP = 4
GROUP_WIDTH = HEADS_PER_GROUP * HEAD_DIM
DILATED_GROUPS = ((128, 1), (512, 4), (2048, 16))
N_GROUPS = len(DILATED_GROUPS)
ATTN_WIDTH = N_GROUPS * GROUP_WIDTH
CONV_CH = D_MODEL // 2
CONV_WIDTH = 31
CONV_HIST = 32
N_BUCKETS = 32
MAX_DISTANCE = 2048
N_EXPERT_GROUPS = 4
EXPERTS_PER_GROUP = 8
N_EXPERTS = N_EXPERT_GROUPS * EXPERTS_PER_GROUP
D_FF_EXPERT = D_MODEL // 2
EPS = 1e-6
NEG_INF = -1e30

COL_KV = ATTN_WIDTH
COL_ULIN = COL_KV + 2 * ATTN_WIDTH
COL_UGATE = COL_ULIN + CONV_CH
COL_GA = COL_UGATE + CONV_CH
COL_GB = COL_GA + D_MODEL
IN_COLS = COL_GB + D_MODEL

ROUTER_ROWS = 8 + N_EXPERTS

V7X_VMEM_LIMIT = 56 * 1024 * 1024
TOKEN_TILE = 512
ATTN_ROWS = 512
ATTN_BLOCK = 128
MOE_ROWS = 256


def _sigmoid(x):
    return 1.0 / (1.0 + jnp.exp(-x))


def _silu(x):
    return x * _sigmoid(x)


def _rms_modulate(x, g, sc, sh):
    r = lax.rsqrt(jnp.mean(x * x, axis=-1, keepdims=True) + EPS)
    return ((x * r) * g) * (1.0 + sc) + sh


def _conv_tail(y, g, b):
    mu = jnp.mean(y, axis=-1, keepdims=True)
    yc = y - mu
    var = jnp.mean(yc * yc, axis=-1, keepdims=True)
    return _silu((yc * lax.rsqrt(var + EPS)) * g + b)


def _params(semantics):
    return pltpu.CompilerParams(dimension_semantics=semantics, vmem_limit_bytes=V7X_VMEM_LIMIT)


def _const_spec(shape):
    nd = len(shape)
    return pl.BlockSpec(shape, lambda *_: (0,) * nd)


def _weight_spec(shape):
    nd = len(shape)
    return pl.BlockSpec(shape, lambda *_: (0,) * nd, pipeline_mode=pl.Buffered(1))


def _mod_kernel(c_ref, w_ref, b_ref, o_ref):
    s = _silu(c_ref[...]).astype(BF16)
    o_ref[...] = jnp.dot(s, w_ref[...].astype(BF16), preferred_element_type=F32) + b_ref[...]


def _modulation(c_all, w_mod, b_mod):
    rows = c_all.shape[0]
    n_out = w_mod.shape[1]
    chunk = D_MODEL
    return pl.pallas_call(
        _mod_kernel,
        grid=(n_out // chunk,),
        in_specs=[_const_spec((rows, D_MODEL)),
                  pl.BlockSpec((D_MODEL, chunk), lambda j: (0, j)),
                  pl.BlockSpec((1, chunk), lambda j: (0, j))],
        out_specs=pl.BlockSpec((rows, chunk), lambda j: (0, j)),
        out_shape=jax.ShapeDtypeStruct((rows, n_out), F32),
        compiler_params=_params(("arbitrary",)),
        name="modulation",
    )(c_all, w_mod, b_mod)


def _project_common(hb, win_ref, outs):
    (q_refs, k_refs, v_refs) = outs

    def proj(lo, width):
        return jnp.dot(hb, win_ref[:, lo:lo + width], preferred_element_type=F32)

    zq = proj(0, ATTN_WIDTH)
    for g in range(N_GROUPS):
        q_refs[g][...] = zq[:, g * GROUP_WIDTH:(g + 1) * GROUP_WIDTH].astype(q_refs[g].dtype)
    zkvs = []
    for g in range(N_GROUPS):
        zkv = proj(COL_KV + 2 * GROUP_WIDTH * g, 2 * GROUP_WIDTH)
        k_refs[g][...] = zkv[:, :GROUP_WIDTH].astype(k_refs[g].dtype)
        v_refs[g][...] = zkv[:, GROUP_WIDTH:].astype(v_refs[g].dtype)
        zkvs.append(zkv)
    u = proj(COL_ULIN, CONV_CH) * _sigmoid(proj(COL_UGATE, CONV_CH))
    return zkvs, u, proj


def _k1_kernel(tail_rows, n_tiles,
               x_ref, sh_ref, sc_ref, g_ref, win_ref, dww_ref, dwb_ref, lng_ref, lnb_ref, wco_ref,
               q0, q1, q2, k0, v0, k1, v1, k2, v2, kvt0, kvt1, kvt2, ga_ref, sgb_ref, convp_ref, uext):
    i = pl.program_id(1)
    tm = x_ref.shape[1]

    @pl.when(i == 0)
    def _():
        uext[0:CONV_HIST, :] = jnp.zeros((CONV_HIST, CONV_CH), F32)

    h = _rms_modulate(x_ref[0], g_ref[...], sc_ref[0], sh_ref[0])
    hb = h.astype(BF16)
    zkvs, u, proj = _project_common(hb, win_ref, ((q0, q1, q2), (k0, k1, k2), (v0, v1, v2)))
    for g, kvt in enumerate((kvt0, kvt1, kvt2)):
        tr = kvt.shape[1]
        first = n_tiles - tail_rows[g] // tr

        @pl.when(i >= first)
        def _(kvt=kvt, zkv=zkvs[g], tr=tr):
            kvt[0] = zkv[tm - tr:, :]

    uext[CONV_HIST:CONV_HIST + tm, :] = u
    acc = jnp.zeros((tm, CONV_CH), F32) + dwb_ref[...]
    base = CONV_HIST - (CONV_WIDTH - 1)
    for k in range(CONV_WIDTH):
        acc = acc + dww_ref[k:k + 1, :] * uext[base + k:base + k + tm, :]
    last = uext[tm:tm + CONV_HIST, :]
    uext[0:CONV_HIST, :] = last
    convp_ref[0] = last
    s = _conv_tail(acc, lng_ref[...], lnb_ref[...])
    a = jnp.dot(s.astype(BF16), wco_ref[...], preferred_element_type=F32)
    ga_ref[...] = (_sigmoid(proj(COL_GA, D_MODEL)) * a).astype(BF16)
    sgb_ref[...] = _sigmoid(proj(COL_GB, D_MODEL)).astype(BF16)


def _k1_prompt(x, sh1, sc1, norm_g, w_in_b, dw_w, dw_b, ln_g, ln_b, w_co_b):
    B, S, _ = x.shape
    tm = min(TOKEN_TILE, S)
    nt = S // tm
    tail_rows = tuple(min(win, S) for win, _ in DILATED_GROUPS)
    tail_blk = tuple(min(t, tm) for t in tail_rows)

    def tok_spec(width):
        return pl.BlockSpec((tm, width), lambda b, i: (b * nt + i, 0))

    def tail_spec(g):
        first = nt - tail_rows[g] // tail_blk[g]
        return pl.BlockSpec((1, tail_blk[g], 2 * GROUP_WIDTH), lambda b, i: (b, jnp.maximum(i - first, 0), 0))

    mod_spec = pl.BlockSpec((1, 1, D_MODEL), lambda b, i: (b, 0, 0))
    out_shape = ([jax.ShapeDtypeStruct((B * S, GROUP_WIDTH), BF16)] * 9
                 + [jax.ShapeDtypeStruct((B, tail_rows[g], 2 * GROUP_WIDTH), F32) for g in range(N_GROUPS)]
                 + [jax.ShapeDtypeStruct((B * S, D_MODEL), BF16)] * 2
                 + [jax.ShapeDtypeStruct((B, CONV_HIST, CONV_CH), F32)])
    out_specs = ([tok_spec(GROUP_WIDTH)] * 9 + [tail_spec(g) for g in range(N_GROUPS)]
                 + [tok_spec(D_MODEL)] * 2 + [pl.BlockSpec((1, CONV_HIST, CONV_CH), lambda b, i: (b, 0, 0))])
    return pl.pallas_call(
        functools.partial(_k1_kernel, tail_rows, nt),
        grid=(B, nt),
        in_specs=[pl.BlockSpec((1, tm, D_MODEL), lambda b, i: (b, i, 0)), mod_spec, mod_spec,
                  _const_spec((1, D_MODEL)), _weight_spec((D_MODEL, IN_COLS)),
                  _const_spec((CONV_WIDTH, CONV_CH)), _const_spec((1, CONV_CH)),
                  _const_spec((1, CONV_CH)), _const_spec((1, CONV_CH)), _weight_spec((CONV_CH, D_MODEL))],
        out_specs=out_specs,
        out_shape=out_shape,
        scratch_shapes=[pltpu.VMEM((CONV_HIST + tm, CONV_CH), F32)],
        compiler_params=_params(("arbitrary", "arbitrary")),
        name="inproj_prompt",
    )(x, sh1, sc1, norm_g, w_in_b, dw_w, dw_b, ln_g, ln_b, w_co_b)


def _k1s_kernel(n_steps, x_ref, sh_ref, sc_ref, g_ref, win_ref, st_ref, dww_ref, dwb_ref, lng_ref, lnb_ref, wco_ref,
                q0, q1, q2, k0, v0, k1, v1, k2, v2, ga_ref, sgb_ref, conv_ref):
    bd = st_ref.shape[1]
    hist = st_ref.shape[0]
    h = _rms_modulate(x_ref[...], g_ref[...], sc_ref[...], sh_ref[...])
    hb = h.astype(BF16)
    _, u, proj = _project_common(hb, win_ref, ((q0, q1, q2), (k0, k1, k2), (v0, v1, v2)))

    def ext(j):
        return st_ref[j] if j < hist else u[(j - hist) * bd:(j - hist + 1) * bd, :]

    outs = []
    for t in range(n_steps):
        acc = jnp.zeros((bd, CONV_CH), F32) + dwb_ref[...]
        for k in range(CONV_WIDTH):
            acc = acc + dww_ref[k:k + 1, :] * ext(t + k + hist - (CONV_WIDTH - 1))
        outs.append(acc)
    for j in range(hist):
        conv_ref[j] = ext(j + n_steps)
    s = _conv_tail(jnp.concatenate(outs, axis=0), lng_ref[...], lnb_ref[...])
    a = jnp.dot(s.astype(BF16), wco_ref[...], preferred_element_type=F32)
    ga_ref[...] = (_sigmoid(proj(COL_GA, D_MODEL)) * a).astype(BF16)
    sgb_ref[...] = _sigmoid(proj(COL_GB, D_MODEL)).astype(BF16)


def _k1_sample(x_tm, sh1, sc1, norm_g, w_in_b, state_tm, dw_w, dw_b, ln_g, ln_b, w_co_b, n_steps):
    T = x_tm.shape[0]
    hist, bd, _ = state_tm.shape
    out_shape = ([jax.ShapeDtypeStruct((T, GROUP_WIDTH), F32)] * 9
                 + [jax.ShapeDtypeStruct((T, D_MODEL), BF16)] * 2
                 + [jax.ShapeDtypeStruct((hist, bd, CONV_CH), F32)])
    out_specs = ([_const_spec((T, GROUP_WIDTH))] * 9 + [_const_spec((T, D_MODEL))] * 2
                 + [_const_spec((hist, bd, CONV_CH))])
    return pl.pallas_call(
        functools.partial(_k1s_kernel, n_steps),
        grid=(1,),
        in_specs=[_const_spec((T, D_MODEL)), _const_spec((T, D_MODEL)), _const_spec((T, D_MODEL)),
                  _const_spec((1, D_MODEL)), _const_spec((D_MODEL, IN_COLS)), _const_spec((hist, bd, CONV_CH)),
                  _const_spec((CONV_WIDTH, CONV_CH)), _const_spec((1, CONV_CH)),
                  _const_spec((1, CONV_CH)), _const_spec((1, CONV_CH)), _const_spec((CONV_CH, D_MODEL))],
        out_specs=out_specs,
        out_shape=out_shape,
        compiler_params=_params(("arbitrary",)),
        name="inproj_sample",
    )(x_tm, sh1, sc1, norm_g, w_in_b, state_tm, dw_w, dw_b, ln_g, ln_b, w_co_b)


def _t5_bucket(dist):
    max_exact = N_BUCKETS // 2
    d_f = jnp.maximum(dist, 1).astype(F32)
    large = max_exact + (jnp.log(d_f / max_exact) / math.log(MAX_DISTANCE / max_exact)
                         * (N_BUCKETS - max_exact)).astype(jnp.int32)
    large = jnp.minimum(large, N_BUCKETS - 1)
    return jnp.where(dist < max_exact, dist, large)


def _prompt_bias(rel_bias_g, dil, n_keys):
    blk = n_keys
    i = jnp.arange(blk)[:, None]
    j = jnp.arange(2 * blk)[None, :]
    rel = i - j + blk
    valid = (rel >= 0) & (rel <= n_keys)
    bias = rel_bias_g[_t5_bucket(jnp.clip(rel, 0, n_keys) * dil)]
    bias = jnp.where(valid[:, :, None], bias, NEG_INF).transpose(2, 0, 1)
    return bias.reshape(HEADS_PER_GROUP * blk, 2 * blk).astype(F32)


def _sample_bias(rel_bias_g, dil, n_keys, lc, n_steps, q_rows):
    t = jnp.arange(q_rows)[:, None]
    pos = jnp.arange(lc)[None, :]
    dist_c = lc + t - pos
    dist_n = t - (jnp.arange(128)[None, :] - (128 - n_steps))

    def table(dist, extra):
        ok = (dist >= 0) & (dist % dil == 0) & (dist // dil <= n_keys) & extra & (t < n_steps)
        b = rel_bias_g[_t5_bucket(jnp.clip(dist, 0, None))]
        b = jnp.where(ok[:, :, None], b, NEG_INF)
        b = jnp.where((t >= n_steps)[:, :, None], 0.0, b)
        return b.transpose(2, 0, 1).astype(F32)

    lane_ok = jnp.arange(128)[None, :] >= 128 - n_steps
    return table(dist_c, True), table(dist_n, lane_ok)


def _k2_kernel(q_ref, k_ref, kh_ref, v_ref, vh_ref, bias_ref, o_ref, lse_ref, kbuf, vbuf):
    i = pl.program_id(2)
    rows = q_ref.shape[1]
    blk = ATTN_BLOCK
    kbuf[0:blk, :] = kh_ref[0]
    kbuf[blk:blk + rows, :] = k_ref[0]
    vbuf[0:blk, :] = vh_ref[0]
    vbuf[blk:blk + rows, :] = v_ref[0]
    lane_head = lax.broadcasted_iota(jnp.int32, (blk, GROUP_WIDTH), 1) // HEAD_DIM
    col = lax.broadcasted_iota(jnp.int32, (HEADS_PER_GROUP * blk, 2 * blk), 1)
    first_mask = jnp.where((col < blk) & (i == 0), NEG_INF, 0.0).astype(F32)
    bias = bias_ref[...]
    for j in range(rows // blk):
        qb = q_ref[0, j * blk:(j + 1) * blk, :]
        q4 = jnp.concatenate([jnp.where(lane_head == h, qb, jnp.zeros_like(qb)) for h in range(HEADS_PER_GROUP)], axis=0)
        kc = kbuf[j * blk:(j + 2) * blk, :]
        vc = vbuf[j * blk:(j + 2) * blk, :]
        s = lax.dot_general(q4, kc, (((1,), (1,)), ((), ())), preferred_element_type=F32) + bias
        if j == 0:
            s = s + first_mask
        m = jnp.max(s, axis=-1, keepdims=True)
        p = jnp.exp(s - m)
        l = jnp.sum(p, axis=-1, keepdims=True)
        o4 = jnp.dot(p.astype(BF16), vc, preferred_element_type=F32) * (1.0 / l)
        lse4 = m + jnp.log(l)
        o = jnp.zeros((blk, GROUP_WIDTH), F32)
        lse = jnp.zeros((blk, GROUP_WIDTH), F32)
        for h in range(HEADS_PER_GROUP):
            sel = lane_head == h
            o = jnp.where(sel, o4[h * blk:(h + 1) * blk, :], o)
            lse = jnp.where(sel, lse4[h * blk:(h + 1) * blk, :], lse)
        o_ref[0, j * blk:(j + 1) * blk, :] = o.astype(o_ref.dtype)
        lse_ref[0, j * blk:(j + 1) * blk, :] = lse


def _k2_prompt(q, k, v, bias, B, S, dil):
    L = S // dil
    rows = min(ATTN_ROWS, L)
    per = rows // ATTN_BLOCK
    view = (B, L, dil * GROUP_WIDTH)
    main = pl.BlockSpec((1, rows, GROUP_WIDTH), lambda b, r, i: (b, i, r))
    halo = pl.BlockSpec((1, ATTN_BLOCK, GROUP_WIDTH), lambda b, r, i: (b, jnp.maximum(i * per - 1, 0), r))
    o, lse = pl.pallas_call(
        _k2_kernel,
        grid=(B, dil, L // rows),
        in_specs=[main, main, halo, main, halo, _const_spec(bias.shape)],
        out_specs=[main, main],
        out_shape=[jax.ShapeDtypeStruct(view, BF16), jax.ShapeDtypeStruct(view, F32)],
        scratch_shapes=[pltpu.VMEM((ATTN_BLOCK + rows, GROUP_WIDTH), BF16)] * 2,
        compiler_params=_params(("arbitrary", "arbitrary", "arbitrary")),
        name=f"attn_prompt_d{dil}",
    )(q.reshape(view), k.reshape(view), k.reshape(view), v.reshape(view), v.reshape(view), bias)
    return o.reshape(B * S, GROUP_WIDTH), lse.reshape(B * S, GROUP_WIDTH)


def _k2s_kernel(n_steps, q_ref, c0, c1, c2, t0, t1, t2, bc0, bc1, bc2, bn0, bn1, bn2,
                nc0, nc1, nc2, o0, o1, o2, l0, l1, l2):
    lane = lax.broadcasted_iota(jnp.int32, (2 * GROUP_WIDTH, 128), 1)
    keep = 128 - n_steps
    groups = ((c0, t0, bc0, bn0, nc0, o0, l0), (c1, t1, bc1, bn1, nc1, o1, l1), (c2, t2, bc2, bn2, nc2, o2, l2))
    for g, (c_ref, t_ref, bc_ref, bn_ref, nc_ref, o_ref, l_ref) in enumerate(groups):
        lc = c_ref.shape[2]
        n_tiles = lc // 128
        tail = t_ref[0]
        for c in range(n_tiles):
            cur = pltpu.roll(c_ref[0, :, c * 128:(c + 1) * 128], keep, 1)
            if c + 1 < n_tiles:
                nxt = pltpu.roll(c_ref[0, :, (c + 1) * 128:(c + 2) * 128], keep, 1)
            else:
                nxt = tail
            nc_ref[0, :, c * 128:(c + 1) * 128] = jnp.where(lane < keep, cur, nxt)
        for h in range(HEADS_PER_GROUP):
            lo = g * GROUP_WIDTH + h * HEAD_DIM
            qh = q_ref[0, :, lo:lo + HEAD_DIM].astype(BF16)
            kh = c_ref[0, h * HEAD_DIM:(h + 1) * HEAD_DIM, :].astype(BF16)
            vh = c_ref[0, GROUP_WIDTH + h * HEAD_DIM:GROUP_WIDTH + (h + 1) * HEAD_DIM, :].astype(BF16)
            kt = tail[h * HEAD_DIM:(h + 1) * HEAD_DIM, :].astype(BF16)
            vt = tail[GROUP_WIDTH + h * HEAD_DIM:GROUP_WIDTH + (h + 1) * HEAD_DIM, :].astype(BF16)
            sc = jnp.dot(qh, kh, preferred_element_type=F32) + bc_ref[h]
            sn = jnp.dot(qh, kt, preferred_element_type=F32) + bn_ref[h]
            m = jnp.maximum(jnp.max(sc, axis=-1, keepdims=True), jnp.max(sn, axis=-1, keepdims=True))
            pc = jnp.exp(sc - m)
            pn = jnp.exp(sn - m)
            l = jnp.sum(pc, axis=-1, keepdims=True) + jnp.sum(pn, axis=-1, keepdims=True)
            nt_dims = (((1,), (1,)), ((), ()))
            o = (lax.dot_general(pc.astype(BF16), vh, nt_dims, preferred_element_type=F32)
                 + lax.dot_general(pn.astype(BF16), vt, nt_dims, preferred_element_type=F32)) * (1.0 / l)
            o_ref[0, :, h * HEAD_DIM:(h + 1) * HEAD_DIM] = o
            l_ref[0, :, h * HEAD_DIM:(h + 1) * HEAD_DIM] = jnp.broadcast_to(m + jnp.log(l), o.shape)


def _k2_sample(q_b, caches, tails, biases_c, biases_n, n_steps):
    bd, q_rows, _ = q_b.shape
    per_b = lambda shape: pl.BlockSpec((1,) + shape[1:], lambda b: (b,) + (0,) * (len(shape) - 1))
    ins = [q_b] + list(caches) + list(tails) + list(biases_c) + list(biases_n)
    in_specs = ([per_b(q_b.shape)] + [per_b(c.shape) for c in caches] + [per_b(t.shape) for t in tails]
                + [_const_spec(b.shape) for b in biases_c] + [_const_spec(b.shape) for b in biases_n])
    o_shape = jax.ShapeDtypeStruct((bd, q_rows, GROUP_WIDTH), F32)
    out_shape = [jax.ShapeDtypeStruct(c.shape, F32) for c in caches] + [o_shape] * 6
    out_specs = [per_b(c.shape) for c in caches] + [per_b(o_shape.shape)] * 6
    return pl.pallas_call(
        functools.partial(_k2s_kernel, n_steps),
        grid=(bd,),
        in_specs=in_specs,
        out_specs=out_specs,
        out_shape=out_shape,
        compiler_params=_params(("arbitrary",)),
        name="attn_sample",
    )(*ins)


def _k3_kernel(x_ref, o0, o1, o2, l0, l1, l2, ga_ref, sgb_ref, g1_ref, sh2_ref, sc2_ref, gf_ref,
               wao_ref, wo_ref, wr_ref, br_ref, cnt_in_ref,
               x2_ref, h2_ref, eid_ref, wts_ref, rank_ref, cnt_ref, carry):
    @pl.when(pl.program_id(0) == 0)
    def _():
        carry[...] = cnt_in_ref[...]

    ls = [l0[...], l1[...], l2[...]]
    m = jnp.maximum(jnp.maximum(ls[0], ls[1]), ls[2])
    ws = [jnp.exp(l - m) for l in ls]
    den = ws[0] + ws[1] + ws[2]
    o = (ws[0] * o0[...].astype(F32) + ws[1] * o1[...].astype(F32) + ws[2] * o2[...].astype(F32)) / den
    b = jnp.dot(o.astype(BF16), wao_ref[...], preferred_element_type=F32)
    mixed = ga_ref[...].astype(F32) + sgb_ref[...].astype(F32) * b
    x2 = x_ref[...] + g1_ref[0] * jnp.dot(mixed.astype(BF16), wo_ref[...], preferred_element_type=F32)
    x2_ref[...] = x2
    h2 = _rms_modulate(x2, gf_ref[...], sc2_ref[0], sh2_ref[0])
    h2_ref[...] = h2
    lt = lax.dot_general(wr_ref[...], h2, (((1,), (1,)), ((), ())), preferred_element_type=F32,
                         precision=lax.Precision.HIGHEST) + br_ref[...]
    tm = h2.shape[0]
    gl = lt[0:8, :]
    gmax = jnp.max(gl, axis=0, keepdims=True)
    r8 = lax.broadcasted_iota(jnp.int32, (8, tm), 0)
    grp = jnp.min(jnp.where(gl == gmax, r8, 8), axis=0, keepdims=True)
    p_grp = 1.0 / jnp.sum(jnp.exp(gl - gmax), axis=0, keepdims=True)
    es = jnp.zeros((EXPERTS_PER_GROUP, tm), F32)
    for g in range(N_EXPERT_GROUPS):
        es = jnp.where(grp == g, lt[8 + 8 * g:16 + 8 * g, :], es)
    v1 = jnp.max(es, axis=0, keepdims=True)
    i1 = jnp.min(jnp.where(es == v1, r8, 8), axis=0, keepdims=True)
    rest = jnp.where(r8 == i1, -jnp.inf, es)
    v2 = jnp.max(rest, axis=0, keepdims=True)
    i2 = jnp.min(jnp.where(rest == v2, r8, 8), axis=0, keepdims=True)
    e21 = jnp.exp(v2 - v1)
    w1 = p_grp / (1.0 + e21)
    e1 = grp * EXPERTS_PER_GROUP + i1
    e2 = grp * EXPERTS_PER_GROUP + i2
    eid_ref[0, 0:1, :] = e1
    eid_ref[0, 1:2, :] = e2
    wts_ref[0, 0:1, :] = w1
    wts_ref[0, 1:2, :] = w1 * e21
    r_e = lax.broadcasted_iota(jnp.int32, (N_EXPERTS, tm), 0)
    hit1 = r_e == e1
    hit2 = r_e == e2
    both = jnp.where(hit1 | hit2, 1.0, 0.0)
    s_idx = lax.broadcasted_iota(jnp.int32, (tm, tm), 0)
    t_idx = lax.broadcasted_iota(jnp.int32, (tm, tm), 1)
    before = jnp.where(s_idx < t_idx, 1.0, 0.0).astype(BF16)
    base = carry[...] + jnp.dot(both.astype(BF16), before, preferred_element_type=F32)
    rank_ref[0, 0:1, :] = jnp.sum(jnp.where(hit1, base, 0.0), axis=0, keepdims=True).astype(jnp.int32)
    rank_ref[0, 1:2, :] = jnp.sum(jnp.where(hit2, base, 0.0), axis=0, keepdims=True).astype(jnp.int32)
    total = carry[...] + jnp.sum(both, axis=1, keepdims=True)
    carry[...] = total
    cnt_ref[...] = total


def _k3(x, o_l, g1, sh2, sc2, ga, sgb, norm_ffn_g, w_ao_b, w_o_b, w_r, b_r, cnt_in, tiles_per_mod):
    T = x.shape[0]
    tm = min(TOKEN_TILE, T)
    nt = T // tm
    tok = lambda w: pl.BlockSpec((tm, w), lambda t: (t, 0))
    mod = pl.BlockSpec((1,) + g1.shape[1:], lambda t: (t // tiles_per_mod, 0, 0))
    small = pl.BlockSpec((1, 2, tm), lambda t: (t, 0, 0))
    small_i = jax.ShapeDtypeStruct((nt, 2, tm), jnp.int32)
    return pl.pallas_call(
        _k3_kernel,
        grid=(nt,),
        in_specs=[tok(D_MODEL)] + [tok(GROUP_WIDTH)] * 6 + [tok(D_MODEL)] * 2 + [mod] * 3
                 + [_const_spec((1, D_MODEL)), _weight_spec(w_ao_b.shape), _weight_spec(w_o_b.shape),
                    _const_spec(w_r.shape), _const_spec(b_r.shape), _const_spec((N_EXPERTS, 1))],
        out_specs=[tok(D_MODEL), tok(D_MODEL), small, small, small, _const_spec((N_EXPERTS, 1))],
        out_shape=[jax.ShapeDtypeStruct((T, D_MODEL), F32), jax.ShapeDtypeStruct((T, D_MODEL), F32),
                   small_i, jax.ShapeDtypeStruct((nt, 2, tm), F32), small_i,
                   jax.ShapeDtypeStruct((N_EXPERTS, 1), F32)],
        scratch_shapes=[pltpu.VMEM((N_EXPERTS, 1), F32)],
        compiler_params=_params(("arbitrary",)),
        name="merge_router",
    )(x, o_l[0][0], o_l[1][0], o_l[2][0], o_l[0][1], o_l[1][1], o_l[2][1], ga, sgb, g1, sh2, sc2,
      norm_ffn_g, w_ao_b, w_o_b, w_r, b_r, cnt_in)


def _rows_wait(n_rows, hbm, vmem, sem):
    pltpu.make_async_copy(hbm.at[pl.ds(0, n_rows)], vmem, sem).wait()


def _dispatch_kernel(n_tiles, dest_ref, h_ref, xs_in_ref, xs_ref, buf, sem):
    del xs_in_ref
    i = pl.program_id(0)
    slot = i % 2 if n_tiles > 1 else 0
    tm = h_ref.shape[0]

    def drain(s):
        for _ in range(2):
            _rows_wait(tm, xs_ref, buf.at[s], sem.at[s])

    if n_tiles > 2:
        @pl.when(i >= 2)
        def _():
            drain(slot)

    buf[slot] = h_ref[...]
    for r in range(tm):
        for k in range(2):
            pltpu.make_async_copy(buf.at[slot, pl.ds(r, 1)], xs_ref.at[pl.ds(dest_ref[0, k, r], 1)], sem.at[slot]).start()

    if n_tiles == 1:
        drain(0)
    else:
        @pl.when(i == n_tiles - 1)
        def _():
            drain(slot)
            drain(1 - slot)


def _dispatch(dest, h2, xs):
    nt, _, tm = dest.shape
    return pl.pallas_call(
        functools.partial(_dispatch_kernel, nt),
        grid=(nt,),
        in_specs=[pl.BlockSpec((1, 2, tm), lambda t: (t, 0, 0), memory_space=pltpu.SMEM),
                  pl.BlockSpec((tm, D_MODEL), lambda t: (t, 0)),
                  pl.BlockSpec(memory_space=pl.ANY)],
        out_specs=pl.BlockSpec(memory_space=pl.ANY),
        out_shape=jax.ShapeDtypeStruct(xs.shape, xs.dtype),
        input_output_aliases={2: 0},
        scratch_shapes=[pltpu.VMEM((2, tm, D_MODEL), F32), pltpu.SemaphoreType.DMA((2,))],
        compiler_params=_params(("arbitrary",)),
        name="dispatch_rows",
    )(dest, h2, xs)


def _k4_kernel(be_ref, nused_ref, x_ref, wg_ref, wu_ref, wd_ref, y_ref):
    i = pl.program_id(0)

    @pl.when(i < nused_ref[0])
    def _():
        xb = x_ref[...].astype(BF16)
        gate = jnp.dot(xb, wg_ref[0], preferred_element_type=F32)
        up = jnp.dot(xb, wu_ref[0], preferred_element_type=F32)
        mid = (_silu(gate) * up).astype(BF16)
        y_ref[...] = jnp.dot(mid, wd_ref[0], preferred_element_type=F32)

    @pl.when(i >= nused_ref[0])
    def _():
        y_ref[...] = jnp.zeros(y_ref.shape, F32)


def _k4(block_expert, n_used, xs, wg_b, wu_b, wd_b):
    n_blocks = block_expert.shape[0]
    rows = MOE_ROWS
    grid_spec = pltpu.PrefetchScalarGridSpec(
        num_scalar_prefetch=2,
        grid=(n_blocks,),
        in_specs=[pl.BlockSpec((rows, D_MODEL), lambda i, be, nu: (i, 0)),
                  pl.BlockSpec((1, D_MODEL, D_FF_EXPERT), lambda i, be, nu: (be[i], 0, 0)),
                  pl.BlockSpec((1, D_MODEL, D_FF_EXPERT), lambda i, be, nu: (be[i], 0, 0)),
                  pl.BlockSpec((1, D_FF_EXPERT, D_MODEL), lambda i, be, nu: (be[i], 0, 0))],
        out_specs=pl.BlockSpec((rows, D_MODEL), lambda i, be, nu: (i, 0)),
    )
    return pl.pallas_call(
        _k4_kernel,
        grid_spec=grid_spec,
        out_shape=jax.ShapeDtypeStruct((n_blocks * rows, D_MODEL), F32),
        compiler_params=_params(("arbitrary",)),
        name="expert_blocks",
    )(block_expert, n_used, xs, wg_b, wu_b, wd_b)


def _k5_kernel(n_tiles, idx_ref, x2_ref, w_ref, g2_ref, gfin_ref, ys_hbm, y_ref, ybuf, sem):
    i = pl.program_id(0)
    tm = x2_ref.shape[0]

    @pl.when(i < n_tiles)
    def _():
        slot = i % 2
        for r in range(tm):
            for k in range(2):
                pltpu.make_async_copy(ys_hbm.at[pl.ds(idx_ref[0, k, r], 1)], ybuf.at[slot, k, pl.ds(r, 1)],
                                      sem.at[slot]).start()

    @pl.when(i >= 1)
    def _():
        slot = (i - 1) % 2
        for k in range(2):
            _rows_wait(tm, ys_hbm, ybuf.at[slot, k], sem.at[slot])
        w = w_ref[...]
        f = ybuf[slot, 0] * w[:, 0:1] + ybuf[slot, 1] * w[:, 1:2]
        y = x2_ref[...] + g2_ref[0] * f
        r = lax.rsqrt(jnp.mean(y * y, axis=-1, keepdims=True) + EPS)
        y_ref[...] = (y * r) * gfin_ref[...]


def _k5(dest, x2, wts, g2, norm_final_g, ys, tiles_per_mod):
    nt, _, tm = dest.shape
    T = x2.shape[0]
    prev = lambda t: jnp.maximum(t - 1, 0)
    return pl.pallas_call(
        functools.partial(_k5_kernel, nt),
        grid=(nt + 1,),
        in_specs=[pl.BlockSpec((1, 2, tm), lambda t: (jnp.minimum(t, nt - 1), 0, 0), memory_space=pltpu.SMEM),
                  pl.BlockSpec((tm, D_MODEL), lambda t: (prev(t), 0)),
                  pl.BlockSpec((tm, 2), lambda t: (prev(t), 0)),
                  pl.BlockSpec((1,) + g2.shape[1:], lambda t: (prev(t) // tiles_per_mod, 0, 0)),
                  _const_spec((1, D_MODEL)), pl.BlockSpec(memory_space=pl.ANY)],
        out_specs=pl.BlockSpec((tm, D_MODEL), lambda t: (prev(t), 0)),
        out_shape=jax.ShapeDtypeStruct((T, D_MODEL), F32),
        scratch_shapes=[pltpu.VMEM((2, 2, tm, D_MODEL), F32), pltpu.SemaphoreType.DMA((2,))],
        compiler_params=_params(("arbitrary",)),
        name="combine_norm",
    )(dest, x2, wts, g2, norm_final_g, ys)


def _slot_tables(counts, n_blocks):
    padded = (counts + MOE_ROWS - 1) // MOE_ROWS * MOE_ROWS
    pend = jnp.cumsum(padded)
    pstart = pend - padded
    block_lo = jnp.arange(n_blocks, dtype=jnp.int32) * MOE_ROWS
    block_expert = jnp.minimum(jnp.sum(pend[None, :] <= block_lo[:, None], axis=1), N_EXPERTS - 1).astype(jnp.int32)
    n_used = (pend[-1] // MOE_ROWS).astype(jnp.int32).reshape(1)
    return pstart.astype(jnp.int32), block_expert, n_used


def _slots(eid, rank, pstart):
    sel = eid[..., None] == jnp.arange(N_EXPERTS, dtype=jnp.int32)
    return jnp.sum(jnp.where(sel, pstart, 0), axis=-1).astype(jnp.int32) + rank


def _cache_view(cache):
    bd, lc = cache.shape[:2]
    return cache.transpose(0, 2, 3, 4, 1).reshape(bd, 2 * GROUP_WIDTH, lc)


def _cache_unview(view):
    bd, _, lc = view.shape
    return view.reshape(bd, 2, HEADS_PER_GROUP, HEAD_DIM, lc).transpose(0, 4, 1, 2, 3)


def kernel(x_prompt, x_sample, c_prompt, c_sample, cache_kv_w128, cache_kv_w512, cache_kv_w2048, state_conv, rel_bias, norm_mix_g, norm_ffn_g, w_mod, b_mod, w_in, dw_w, dw_b, ln_conv_g, ln_conv_b, w_conv_out, w_attn_out, w_out, w_router_group, b_router_group, w_router_expert, b_router_expert, w_exp_gate, w_exp_up, w_exp_down, norm_final_g):
    assert norm_mix_g.shape[0] == 1, "single layer"
    B, S, D = x_prompt.shape
    Bd, Td, _ = x_sample.shape
    caches_in = (cache_kv_w128[0], cache_kv_w512[0], cache_kv_w2048[0])
    for (win, dil), c in zip(DILATED_GROUPS, caches_in):
        assert c.shape[1] >= (win // dil) * dil and c.shape[1] % 128 == 0 and S % (dil * ATTN_BLOCK) == 0

    wi = w_in[0]
    q_cols = wi[:, :ATTN_WIDTH] * (HEAD_DIM ** -0.5)
    kv_cols = []
    for g in range(N_GROUPS):
        kv_cols += [wi[:, ATTN_WIDTH + g * GROUP_WIDTH:ATTN_WIDTH + (g + 1) * GROUP_WIDTH],
                    wi[:, 2 * ATTN_WIDTH + g * GROUP_WIDTH:2 * ATTN_WIDTH + (g + 1) * GROUP_WIDTH]]
    w_in_b = jnp.concatenate([q_cols] + kv_cols + [wi[:, 3 * ATTN_WIDTH:]], axis=1).astype(BF16)
    w_co_b = w_conv_out[0].astype(BF16)
    w_ao_b = w_attn_out[0].astype(BF16)
    w_o_b = w_out[0].astype(BF16)
    wg_b, wu_b, wd_b = w_exp_gate[0].astype(BF16), w_exp_up[0].astype(BF16), w_exp_down[0].astype(BF16)
    w_r = jnp.zeros((ROUTER_ROWS, D), F32)
    w_r = w_r.at[0:N_EXPERT_GROUPS].set(w_router_group[0].T)
    w_r = w_r.at[8:].set(w_router_expert[0].reshape(D, N_EXPERTS).T)
    b_r = jnp.full((ROUTER_ROWS, 1), NEG_INF, F32)
    b_r = b_r.at[0:N_EXPERT_GROUPS, 0].set(b_router_group[0])
    b_r = b_r.at[8:, 0].set(b_router_expert[0].reshape(N_EXPERTS))
    row = lambda v: v.reshape(1, -1)

    n_seq = B + Bd
    n_seq_pad = -(-n_seq // 16) * 16
    c_all = jnp.concatenate([c_prompt, c_sample, jnp.zeros((n_seq_pad - n_seq, D), F32)], axis=0)
    mod = _modulation(c_all, w_mod[0], row(b_mod[0]))
    mod_p = [mod[:B, j * D:(j + 1) * D].reshape(B, 1, D) for j in range(6)]
    mod_s = [jnp.tile(mod[B:B + Bd, j * D:(j + 1) * D], (Td, 1)) for j in range(6)]

    (q0, q1, q2, k0, v0, k1, v1, k2, v2, kvt0, kvt1, kvt2, ga, sgb, convp) = _k1_prompt(
        x_prompt, mod_p[0], mod_p[1], row(norm_mix_g[0]), w_in_b, dw_w[0], row(dw_b[0]),
        row(ln_conv_g[0]), row(ln_conv_b[0]), w_co_b)
    o_l = []
    for g, ((win, dil), qg, kg, vg) in enumerate(zip(DILATED_GROUPS, (q0, q1, q2), (k0, k1, k2), (v0, v1, v2))):
        bias = _prompt_bias(rel_bias[:, g * HEADS_PER_GROUP:(g + 1) * HEADS_PER_GROUP], dil, win // dil)
        o_l.append(_k2_prompt(qg, kg, vg, bias, B, S, dil))
    xp = x_prompt.reshape(B * S, D)
    tm3 = min(TOKEN_TILE, B * S)
    x2, h2, eid, wts, rank, cnt_p = _k3(xp, o_l, mod_p[2], mod_p[3], mod_p[4], ga, sgb, row(norm_ffn_g[0]),
                                        w_ao_b, w_o_b, w_r, b_r, jnp.zeros((N_EXPERTS, 1), F32), S // tm3)
    kv_p =[kvt.reshape(1, B, kvt.shape[1], 2, HEADS_PER_GROUP, HEAD_DIM) for kvt in (kvt0, kvt1, kvt2)]
    conv_p = convp[:, CONV_HIST - (CONV_WIDTH - 1):, :][None]

    Ts = Td * Bd
    xs = x_sample.transpose(1, 0, 2).reshape(Ts, D)
    state_tm = state_conv[0].transpose(1, 0, 2)
    (sq0, sq1, sq2, sk0, sv0, sk1, sv1, sk2, sv2, ga_s, sgb_s, conv_tm) = _k1_sample(
        xs, mod_s[0], mod_s[1], row(norm_mix_g[0]), w_in_b, state_tm, dw_w[0], row(dw_b[0]),
        row(ln_conv_g[0]), row(ln_conv_b[0]), w_co_b, Td)
    q_rows = 16
    q_b = jnp.concatenate([sq0, sq1, sq2], axis=1).reshape(Td, Bd, ATTN_WIDTH).transpose(1, 0, 2)
    q_b = jnp.pad(q_b, ((0, 0), (0, q_rows - Td), (0, 0)))
    views, tails, biases_c, biases_n = [], [], [], []
    for g, ((win, dil), c, sk, sv) in enumerate(zip(DILATED_GROUPS, caches_in, (sk0, sk1, sk2), (sv0, sv1, sv2))):
        lc = c.shape[1]
        views.append(_cache_view(c))
        new_kv = jnp.concatenate([sk, sv], axis=1).reshape(Td, Bd, 2 * GROUP_WIDTH).transpose(1, 2, 0)
        tails.append(jnp.pad(new_kv, ((0, 0), (0, 0), (128 - Td, 0))))
        bc, bn = _sample_bias(rel_bias[:, g * HEADS_PER_GROUP:(g + 1) * HEADS_PER_GROUP], dil, win // dil, lc, Td, q_rows)
        biases_c.append(bc)
        biases_n.append(bn)
    (nc0, nc1, nc2, so0, so1, so2, sl0, sl1, sl2) = _k2_sample(q_b, views, tails, biases_c, biases_n, Td)
    to_tm = lambda a: a[:, :Td, :].transpose(1, 0, 2).reshape(Ts, GROUP_WIDTH)
    o_l_s = [(to_tm(so0), to_tm(sl0)), (to_tm(so1), to_tm(sl1)), (to_tm(so2), to_tm(sl2))]
    tm3s = min(TOKEN_TILE, Ts)
    tiled = lambda m: m.reshape(Ts // tm3s, tm3s, D)
    x2s, h2s, eid_s, wts_s, rank_s, cnt_all = _k3(xs, o_l_s, tiled(mod_s[2]), tiled(mod_s[3]), tiled(mod_s[4]),
                                                  ga_s, sgb_s, row(norm_ffn_g[0]), w_ao_b, w_o_b, w_r, b_r, cnt_p, 1)

    n_assign = 2 * (B * S + Ts)
    n_blocks = -(-n_assign // MOE_ROWS) + N_EXPERTS
    pstart, block_expert, n_used = _slot_tables(cnt_all[:, 0].astype(jnp.int32), n_blocks)
    dest_p = _slots(eid, rank, pstart)
    dest_s = _slots(eid_s, rank_s, pstart)
    slots = jnp.zeros((n_blocks * MOE_ROWS, D), F32)
    slots = _dispatch(dest_s, h2s, _dispatch(dest_p, h2, slots))
    ys = _k4(block_expert, n_used, slots, wg_b, wu_b, wd_b)
    per_token = lambda a: a.transpose(0, 2, 1).reshape(-1, 2)
    y_prompt = _k5(dest_p, x2, per_token(wts), mod_p[5], row(norm_final_g), ys, S // tm3).reshape(B, S, D)
    y_s = _k5(dest_s, x2s, per_token(wts_s), tiled(mod_s[5]), row(norm_final_g), ys, 1)
    y_sample = y_s.reshape(Td, Bd, D).transpose(1, 0, 2)
    kv_s = [_cache_unview(nc)[None] for nc in (nc0, nc1, nc2)]
    conv_s = conv_tm.transpose(1, 0, 2)[None]

    return (y_prompt, y_sample, kv_p[0], kv_p[1], kv_p[2], conv_p,
            kv_s[0], kv_s[1], kv_s[2], conv_s)
```

```python
import functools
import math

import jax
import jax.numpy as jnp
from jax import lax
from jax.experimental import pallas as pl
from jax.experimental.pallas import tpu as pltpu

F32 = jnp.float32
BF16 = jnp.bfloat16

D_MODEL = 1024
HEAD_DIM = 64
HEADS_PER_GROUP = 4
GROUP_WIDTH = HEADS_PER_GROUP * HEAD_DIM
LANES = 128
STAGE_HALVES = GROUP_WIDTH // LANES
DILATED_GROUPS = ((128, 1), (512, 4), (2048, 16))
N_GROUPS = len(DILATED_GROUPS)
ATTN_WIDTH = N_GROUPS * GROUP_WIDTH
CONV_CH = D_MODEL // 2
CONV_WIDTH = 31
CONV_HIST = 32
N_BUCKETS = 32
MAX_DISTANCE = 2048
N_EXPERT_GROUPS = 4
EXPERTS_PER_GROUP = 8
N_EXPERTS = N_EXPERT_GROUPS * EXPERTS_PER_GROUP
D_FF_EXPERT = D_MODEL // 2
EPS = 1e-6
NEG_INF = -1e30

COL_KV = ATTN_WIDTH
COL_ULIN = COL_KV + 2 * ATTN_WIDTH
COL_UGATE = COL_ULIN + CONV_CH
COL_GA = COL_UGATE + CONV_CH
COL_GB = COL_GA + D_MODEL
IN_COLS = COL_GB + D_MODEL

ROUTER_ROWS = 8 + N_EXPERTS

V7X_VMEM_LIMIT = 56 * 1024 * 1024
TOKEN_TILE = 512
ATTN_ROWS = 512
ATTN_BLOCK = 128
MOE_ROWS = 256


def _sigmoid(x):
    return 1.0 / (1.0 + jnp.exp(-x))


def _silu(x):
    return x * _sigmoid(x)


def _rms_modulate(x, g, sc, sh):
    r = lax.rsqrt(jnp.mean(x * x, axis=-1, keepdims=True) + EPS)
    return ((x * r) * g) * (1.0 + sc) + sh


def _conv_tail(y, g, b):
    mu = jnp.mean(y, axis=-1, keepdims=True)
    yc = y - mu
    var = jnp.mean(yc * yc, axis=-1, keepdims=True)
    return _silu((yc * lax.rsqrt(var + EPS)) * g + b)


def _params(semantics):
    return pltpu.CompilerParams(dimension_semantics=semantics, vmem_limit_bytes=V7X_VMEM_LIMIT)


def _const_spec(shape):
    nd = len(shape)
    return pl.BlockSpec(shape, lambda *_: (0,) * nd)


def _weight_spec(shape):
    nd = len(shape)
    return pl.BlockSpec(shape, lambda *_: (0,) * nd, pipeline_mode=pl.Buffered(1))


def _mod_kernel(c_ref, w_ref, b_ref, o_ref):
    s = _silu(c_ref[...]).astype(BF16)
    o_ref[...] = jnp.dot(s, w_ref[...].astype(BF16), preferred_element_type=F32) + b_ref[...]


def _modulation(c_all, w_mod, b_mod):
    rows = c_all.shape[0]
    n_out = w_mod.shape[1]
    chunk = D_MODEL
    return pl.pallas_call(
        _mod_kernel,
        grid=(n_out // chunk,),
        in_specs=[_const_spec((rows, D_MODEL)),
                  pl.BlockSpec((D_MODEL, chunk), lambda j: (0, j)),
                  pl.BlockSpec((1, chunk), lambda j: (0, j))],
        out_specs=pl.BlockSpec((rows, chunk), lambda j: (0, j)),
        out_shape=jax.ShapeDtypeStruct((rows, n_out), F32),
        compiler_params=_params(("arbitrary",)),
        name="modulation",
    )(c_all, w_mod, b_mod)


def _to_streams(ref, val, dil, stage):
    if dil == 1:
        ref[...] = val.astype(ref.dtype).reshape(ref.shape)
        return
    n = val.shape[0] // dil
    for c, half in enumerate(stage):
        half[...] = val[:, c * LANES:(c + 1) * LANES]
        for r in range(dil):
            lo = r * GROUP_WIDTH + c * LANES
            ref[0, :, lo:lo + LANES] = half[pl.ds(r, n, stride=dil), :].astype(ref.dtype)


def _from_streams(ref, dil, stage):
    if dil == 1:
        return ref[...].astype(F32).reshape(ref.shape[-2:])
    n = ref.shape[1]
    for c, half in enumerate(stage):
        for r in range(dil):
            lo = r * GROUP_WIDTH + c * LANES
            half[pl.ds(r, n, stride=dil), :] = ref[0, :, lo:lo + LANES].astype(F32)
    return jnp.concatenate([half[...] for half in stage], axis=1)


def _project_common(hb, win_ref, outs, dils, stages):
    (q_refs, k_refs, v_refs) = outs

    def proj(lo, width):
        return jnp.dot(hb, win_ref[:, lo:lo + width], preferred_element_type=F32)

    zq = proj(0, ATTN_WIDTH)
    for g in range(N_GROUPS):
        _to_streams(q_refs[g], zq[:, g * GROUP_WIDTH:(g + 1) * GROUP_WIDTH], dils[g], stages[g][0])
    zkvs = []
    for g in range(N_GROUPS):
        zkv = proj(COL_KV + 2 * GROUP_WIDTH * g, 2 * GROUP_WIDTH)
        _to_streams(k_refs[g], zkv[:, :GROUP_WIDTH], dils[g], stages[g][1])
        _to_streams(v_refs[g], zkv[:, GROUP_WIDTH:], dils[g], stages[g][2])
        zkvs.append(zkv)
    u = proj(COL_ULIN, CONV_CH) * _sigmoid(proj(COL_UGATE, CONV_CH))
    return zkvs, u, proj


def _k1_kernel(tail_rows, n_tiles,
               x_ref, sh_ref, sc_ref, g_ref, win_ref, dww_ref, dwb_ref, lng_ref, lnb_ref, wco_ref,
               q0, q1, q2, k0, v0, k1, v1, k2, v2, kvt0, kvt1, kvt2, ga_ref, sgb_ref, convp_ref, uext, *stage_refs):
    i = pl.program_id(1)
    tm = x_ref.shape[1]

    @pl.when(i == 0)
    def _():
        uext[0:CONV_HIST, :] = jnp.zeros((CONV_HIST, CONV_CH), F32)

    h = _rms_modulate(x_ref[0], g_ref[...], sc_ref[0], sh_ref[0])
    hb = h.astype(BF16)
    dils = tuple(dil for _, dil in DILATED_GROUPS)
    pairs = [stage_refs[j:j + STAGE_HALVES] for j in range(0, len(stage_refs), STAGE_HALVES)]
    stages = [(None,) * 3] + [pairs[3 * (g - 1):3 * g] for g in range(1, N_GROUPS)]
    zkvs, u, proj = _project_common(hb, win_ref, ((q0, q1, q2), (k0, k1, k2), (v0, v1, v2)), dils, stages)
    for g, kvt in enumerate((kvt0, kvt1, kvt2)):
        tr = kvt.shape[1]
        first = n_tiles - tail_rows[g] // tr

        @pl.when(i >= first)
        def _(kvt=kvt, zkv=zkvs[g], tr=tr):
            kvt[0] = zkv[tm - tr:, :]

    uext[CONV_HIST:CONV_HIST + tm, :] = u
    acc = jnp.zeros((tm, CONV_CH), F32) + dwb_ref[...]
    base = CONV_HIST - (CONV_WIDTH - 1)
    for k in range(CONV_WIDTH):
        acc = acc + dww_ref[k:k + 1, :] * uext[base + k:base + k + tm, :]
    last = uext[tm:tm + CONV_HIST, :]
    uext[0:CONV_HIST, :] = last
    convp_ref[0] = last
    s = _conv_tail(acc, lng_ref[...], lnb_ref[...])
    a = jnp.dot(s.astype(BF16), wco_ref[...], preferred_element_type=F32)
    ga_ref[...] = (_sigmoid(proj(COL_GA, D_MODEL)) * a).astype(BF16)
    sgb_ref[...] = _sigmoid(proj(COL_GB, D_MODEL)).astype(BF16)


def _k1_prompt(x, sh1, sc1, norm_g, w_in_b, dw_w, dw_b, ln_g, ln_b, w_co_b):
    B, S, _ = x.shape
    tm = min(TOKEN_TILE, S)
    nt = S // tm
    tail_rows = tuple(min(win, S) for win, _ in DILATED_GROUPS)
    tail_blk = tuple(min(t, tm) for t in tail_rows)

    def tok_spec(width):
        return pl.BlockSpec((tm, width), lambda b, i: (b * nt + i, 0))

    def tail_spec(g):
        first = nt - tail_rows[g] // tail_blk[g]
        return pl.BlockSpec((1, tail_blk[g], 2 * GROUP_WIDTH), lambda b, i: (b, jnp.maximum(i - first, 0), 0))

    def stream_shape(dil):
        return jax.ShapeDtypeStruct((B, S // dil, dil * GROUP_WIDTH), BF16)

    def stream_spec(dil):
        return pl.BlockSpec((1, tm // dil, dil * GROUP_WIDTH), lambda b, i: (b, i, 0))

    dils = [dil for _, dil in DILATED_GROUPS]
    qkv_order = [dils[0], dils[1], dils[2]] + [d for d in dils for _ in range(2)]
    mod_spec = pl.BlockSpec((1, 1, D_MODEL), lambda b, i: (b, 0, 0))
    out_shape = ([stream_shape(d) for d in qkv_order]
                 + [jax.ShapeDtypeStruct((B, tail_rows[g], 2 * GROUP_WIDTH), F32) for g in range(N_GROUPS)]
                 + [jax.ShapeDtypeStruct((B * S, D_MODEL), BF16)] * 2
                 + [jax.ShapeDtypeStruct((B, CONV_HIST, CONV_CH), F32)])
    out_specs = ([stream_spec(d) for d in qkv_order] + [tail_spec(g) for g in range(N_GROUPS)]
                 + [tok_spec(D_MODEL)] * 2 + [pl.BlockSpec((1, CONV_HIST, CONV_CH), lambda b, i: (b, 0, 0))])
    n_stage = 3 * sum(1 for d in dils if d > 1)
    return pl.pallas_call(
        functools.partial(_k1_kernel, tail_rows, nt),
        grid=(B, nt),
        in_specs=[pl.BlockSpec((1, tm, D_MODEL), lambda b, i: (b, i, 0)), mod_spec, mod_spec,
                  _const_spec((1, D_MODEL)), _weight_spec((D_MODEL, IN_COLS)),
                  _const_spec((CONV_WIDTH, CONV_CH)), _const_spec((1, CONV_CH)),
                  _const_spec((1, CONV_CH)), _const_spec((1, CONV_CH)), _weight_spec((CONV_CH, D_MODEL))],
        out_specs=out_specs,
        out_shape=out_shape,
        scratch_shapes=([pltpu.VMEM((CONV_HIST + tm, CONV_CH), F32)]
                        + [pltpu.VMEM((tm, LANES), F32)] * (n_stage * STAGE_HALVES)),
        compiler_params=_params(("arbitrary", "arbitrary")),
        name="inproj_prompt",
    )(x, sh1, sc1, norm_g, w_in_b, dw_w, dw_b, ln_g, ln_b, w_co_b)


def _k1s_kernel(n_steps, x_ref, sh_ref, sc_ref, g_ref, win_ref, st_ref, dww_ref, dwb_ref, lng_ref, lnb_ref, wco_ref,
                q0, q1, q2, k0, v0, k1, v1, k2, v2, ga_ref, sgb_ref, conv_ref):
    bd = st_ref.shape[1]
    hist = st_ref.shape[0]
    h = _rms_modulate(x_ref[...], g_ref[...], sc_ref[...], sh_ref[...])
    hb = h.astype(BF16)
    _, u, proj = _project_common(hb, win_ref, ((q0, q1, q2), (k0, k1, k2), (v0, v1, v2)),
                                 (1,) * N_GROUPS, [(None,) * 3] * N_GROUPS)

    def ext(j):
        return st_ref[j] if j < hist else u[(j - hist) * bd:(j - hist + 1) * bd, :]

    outs = []
    for t in range(n_steps):
        acc = jnp.zeros((bd, CONV_CH), F32) + dwb_ref[...]
        for k in range(CONV_WIDTH):
            acc = acc + dww_ref[k:k + 1, :] * ext(t + k + hist - (CONV_WIDTH - 1))
        outs.append(acc)
    for j in range(hist):
        conv_ref[j] = ext(j + n_steps)
    s = _conv_tail(jnp.concatenate(outs, axis=0), lng_ref[...], lnb_ref[...])
    a = jnp.dot(s.astype(BF16), wco_ref[...], preferred_element_type=F32)
    ga_ref[...] = (_sigmoid(proj(COL_GA, D_MODEL)) * a).astype(BF16)
    sgb_ref[...] = _sigmoid(proj(COL_GB, D_MODEL)).astype(BF16)


def _k1_sample(x_tm, sh1, sc1, norm_g, w_in_b, state_tm, dw_w, dw_b, ln_g, ln_b, w_co_b, n_steps):
    T = x_tm.shape[0]
    hist, bd, _ = state_tm.shape
    out_shape = ([jax.ShapeDtypeStruct((T, GROUP_WIDTH), F32)] * 9
                 + [jax.ShapeDtypeStruct((T, D_MODEL), BF16)] * 2
                 + [jax.ShapeDtypeStruct((hist, bd, CONV_CH), F32)])
    out_specs = ([_const_spec((T, GROUP_WIDTH))] * 9 + [_const_spec((T, D_MODEL))] * 2
                 + [_const_spec((hist, bd, CONV_CH))])
    return pl.pallas_call(
        functools.partial(_k1s_kernel, n_steps),
        grid=(1,),
        in_specs=[_const_spec((T, D_MODEL)), _const_spec((T, D_MODEL)), _const_spec((T, D_MODEL)),
                  _const_spec((1, D_MODEL)), _const_spec((D_MODEL, IN_COLS)), _const_spec((hist, bd, CONV_CH)),
                  _const_spec((CONV_WIDTH, CONV_CH)), _const_spec((1, CONV_CH)),
                  _const_spec((1, CONV_CH)), _const_spec((1, CONV_CH)), _const_spec((CONV_CH, D_MODEL))],
        out_specs=out_specs,
        out_shape=out_shape,
        compiler_params=_params(("arbitrary",)),
        name="inproj_sample",
    )(x_tm, sh1, sc1, norm_g, w_in_b, state_tm, dw_w, dw_b, ln_g, ln_b, w_co_b)


def _t5_bucket(dist):
    max_exact = N_BUCKETS // 2
    d_f = jnp.maximum(dist, 1).astype(F32)
    large = max_exact + (jnp.log(d_f / max_exact) / math.log(MAX_DISTANCE / max_exact)
                         * (N_BUCKETS - max_exact)).astype(jnp.int32)
    large = jnp.minimum(large, N_BUCKETS - 1)
    return jnp.where(dist < max_exact, dist, large)


def _bucket_lookup(rel_bias_g, dist):
    bucket = _t5_bucket(dist)
    out = jnp.zeros((rel_bias_g.shape[1],) + dist.shape, F32)
    for b in range(N_BUCKETS):
        out = jnp.where(bucket[None] == b, rel_bias_g[b].reshape((-1,) + (1,) * dist.ndim), out)
    return out


def _prompt_bias(rel_bias_g, dil, n_keys):
    blk = n_keys
    i = jnp.arange(blk)[:, None]
    j = jnp.arange(2 * blk)[None, :]
    rel = i - j + blk
    valid = (rel >= 0) & (rel <= n_keys)
    bias = _bucket_lookup(rel_bias_g, jnp.clip(rel, 0, n_keys) * dil)
    bias = jnp.where(valid[None], bias, NEG_INF)
    return bias.reshape(HEADS_PER_GROUP * blk, 2 * blk).astype(F32)


def _sample_bias(rel_bias_g, dil, n_keys, lc, n_steps, q_rows):
    t = jnp.arange(q_rows)[:, None]
    pos = jnp.arange(lc)[None, :]
    dist_c = lc + t - pos
    dist_n = t - (jnp.arange(128)[None, :] - (128 - n_steps))

    def table(dist, extra):
        ok = (dist >= 0) & (dist % dil == 0) & (dist // dil <= n_keys) & extra & (t < n_steps)
        b = _bucket_lookup(rel_bias_g, jnp.clip(dist, 0, None))
        b = jnp.where(ok[None], b, NEG_INF)
        return jnp.where((t >= n_steps)[None], 0.0, b).astype(F32)

    lane_ok = jnp.arange(128)[None, :] >= 128 - n_steps
    return table(dist_c, True), table(dist_n, lane_ok)


def _k2_kernel(q_ref, k_ref, kh_ref, v_ref, vh_ref, bias_ref, o_ref, lse_ref, kbuf, vbuf):
    i = pl.program_id(2)
    rows = q_ref.shape[1]
    blk = ATTN_BLOCK
    kbuf[0:blk, :] = kh_ref[0]
    kbuf[blk:blk + rows, :] = k_ref[0]
    vbuf[0:blk, :] = vh_ref[0]
    vbuf[blk:blk + rows, :] = v_ref[0]
    lane_head = lax.broadcasted_iota(jnp.int32, (blk, GROUP_WIDTH), 1) // HEAD_DIM
    col = lax.broadcasted_iota(jnp.int32, (HEADS_PER_GROUP * blk, 2 * blk), 1)
    first_mask = jnp.where((col < blk) & (i == 0), NEG_INF, 0.0).astype(F32)
    bias = bias_ref[...]
    for j in range(rows // blk):
        qb = q_ref[0, j * blk:(j + 1) * blk, :]
        q4 = jnp.concatenate([jnp.where(lane_head == h, qb, jnp.zeros_like(qb)) for h in range(HEADS_PER_GROUP)], axis=0)
        kc = kbuf[j * blk:(j + 2) * blk, :]
        vc = vbuf[j * blk:(j + 2) * blk, :]
        s = lax.dot_general(q4, kc, (((1,), (1,)), ((), ())), preferred_element_type=F32) + bias
        if j == 0:
            s = s + first_mask
        m = jnp.max(s, axis=-1, keepdims=True)
        p = jnp.exp(s - m)
        l = jnp.sum(p, axis=-1, keepdims=True)
        o4 = jnp.dot(p.astype(BF16), vc, preferred_element_type=F32) * (1.0 / l)
        lse4 = m + jnp.log(l)
        o = jnp.zeros((blk, GROUP_WIDTH), F32)
        lse = jnp.zeros((blk, GROUP_WIDTH), F32)
        for h in range(HEADS_PER_GROUP):
            sel = lane_head == h
            o = jnp.where(sel, o4[h * blk:(h + 1) * blk, :], o)
            lse = jnp.where(sel, lse4[h * blk:(h + 1) * blk, :], lse)
        o_ref[0, j * blk:(j + 1) * blk, :] = o.astype(o_ref.dtype)
        lse_ref[0, j * blk:(j + 1) * blk, :] = lse


def _k2_prompt(q, k, v, bias, dil):
    B, L, _ = q.shape
    rows = min(ATTN_ROWS, L)
    per = rows // ATTN_BLOCK
    main = pl.BlockSpec((1, rows, GROUP_WIDTH), lambda b, r, i: (b, i, r))
    halo = pl.BlockSpec((1, ATTN_BLOCK, GROUP_WIDTH), lambda b, r, i: (b, jnp.maximum(i * per - 1, 0), r))
    return pl.pallas_call(
        _k2_kernel,
        grid=(B, dil, L // rows),
        in_specs=[main, main, halo, main, halo, _const_spec(bias.shape)],
        out_specs=[main, main],
        out_shape=[jax.ShapeDtypeStruct(q.shape, BF16), jax.ShapeDtypeStruct(q.shape, F32)],
        scratch_shapes=[pltpu.VMEM((ATTN_BLOCK + rows, GROUP_WIDTH), BF16)] * 2,
        compiler_params=_params(("arbitrary", "arbitrary", "arbitrary")),
        name=f"attn_prompt_d{dil}",
    )(q, k, k, v, v, bias)


def _k2s_kernel(n_steps, q_ref, c0, c1, c2, t0, t1, t2, bc0, bc1, bc2, bn0, bn1, bn2,
                nc0, nc1, nc2, o0, o1, o2, l0, l1, l2):
    lane = lax.broadcasted_iota(jnp.int32, (2 * GROUP_WIDTH, 128), 1)
    keep = 128 - n_steps
    groups = ((c0, t0, bc0, bn0, nc0, o0, l0), (c1, t1, bc1, bn1, nc1, o1, l1), (c2, t2, bc2, bn2, nc2, o2, l2))
    for g, (c_ref, t_ref, bc_ref, bn_ref, nc_ref, o_ref, l_ref) in enumerate(groups):
        lc = c_ref.shape[2]
        n_tiles = lc // 128
        tail = t_ref[0]
        for c in range(n_tiles):
            cur = pltpu.roll(c_ref[0, :, c * 128:(c + 1) * 128], keep, 1)
            if c + 1 < n_tiles:
                nxt = pltpu.roll(c_ref[0, :, (c + 1) * 128:(c + 2) * 128], keep, 1)
            else:
                nxt = tail
            nc_ref[0, :, c * 128:(c + 1) * 128] = jnp.where(lane < keep, cur, nxt)
        for h in range(HEADS_PER_GROUP):
            lo = g * GROUP_WIDTH + h * HEAD_DIM
            qh = q_ref[0, :, lo:lo + HEAD_DIM].astype(BF16)
            kh = c_ref[0, h * HEAD_DIM:(h + 1) * HEAD_DIM, :].astype(BF16)
            vh = c_ref[0, GROUP_WIDTH + h * HEAD_DIM:GROUP_WIDTH + (h + 1) * HEAD_DIM, :].astype(BF16)
            kt = tail[h * HEAD_DIM:(h + 1) * HEAD_DIM, :].astype(BF16)
            vt = tail[GROUP_WIDTH + h * HEAD_DIM:GROUP_WIDTH + (h + 1) * HEAD_DIM, :].astype(BF16)
            sc = jnp.dot(qh, kh, preferred_element_type=F32) + bc_ref[h]
            sn = jnp.dot(qh, kt, preferred_element_type=F32) + bn_ref[h]
            m = jnp.maximum(jnp.max(sc, axis=-1, keepdims=True), jnp.max(sn, axis=-1, keepdims=True))
            pc = jnp.exp(sc - m)
            pn = jnp.exp(sn - m)
            l = jnp.sum(pc, axis=-1, keepdims=True) + jnp.sum(pn, axis=-1, keepdims=True)
            nt_dims = (((1,), (1,)), ((), ()))
            o = (lax.dot_general(pc.astype(BF16), vh, nt_dims, preferred_element_type=F32)
                 + lax.dot_general(pn.astype(BF16), vt, nt_dims, preferred_element_type=F32)) * (1.0 / l)
            o_ref[0, :, h * HEAD_DIM:(h + 1) * HEAD_DIM] = o
            l_ref[0, :, h * HEAD_DIM:(h + 1) * HEAD_DIM] = jnp.broadcast_to(m + jnp.log(l), o.shape)


def _k2_sample(q_b, caches, tails, biases_c, biases_n, n_steps):
    bd, q_rows, _ = q_b.shape
    per_b = lambda shape: pl.BlockSpec((1,) + shape[1:], lambda b: (b,) + (0,) * (len(shape) - 1))
    ins = [q_b] + list(caches) + list(tails) + list(biases_c) + list(biases_n)
    in_specs = ([per_b(q_b.shape)] + [per_b(c.shape) for c in caches] + [per_b(t.shape) for t in tails]
                + [_const_spec(b.shape) for b in biases_c] + [_const_spec(b.shape) for b in biases_n])
    o_shape = jax.ShapeDtypeStruct((bd, q_rows, GROUP_WIDTH), F32)
    out_shape = [jax.ShapeDtypeStruct(c.shape, F32) for c in caches] + [o_shape] * 6
    out_specs = [per_b(c.shape) for c in caches] + [per_b(o_shape.shape)] * 6
    return pl.pallas_call(
        functools.partial(_k2s_kernel, n_steps),
        grid=(bd,),
        in_specs=in_specs,
        out_specs=out_specs,
        out_shape=out_shape,
        compiler_params=_params(("arbitrary",)),
        name="attn_sample",
    )(*ins)


def _k3_kernel(x_ref, o0, o1, o2, l0, l1, l2, ga_ref, sgb_ref, g1_ref, sh2_ref, sc2_ref, gf_ref,
               wao_ref, wo_ref, wr_ref, br_ref, cnt_in_ref,
               x2_ref, h2_ref, eid_ref, wts_ref, rank_ref, cnt_ref, carry, *stage_refs):
    @pl.when(pl.program_id(0) == 0)
    def _():
        carry[...] = cnt_in_ref[...]

    dils = tuple(x_ref.shape[0] // r.shape[-2] for r in (o0, o1, o2))
    pairs = [stage_refs[j:j + STAGE_HALVES] for j in range(0, len(stage_refs), STAGE_HALVES)]
    os_ = [_from_streams(r, d, pairs[2 * g]) for g, (r, d) in enumerate(zip((o0, o1, o2), dils))]
    ls = [_from_streams(r, d, pairs[2 * g + 1]) for g, (r, d) in enumerate(zip((l0, l1, l2), dils))]
    m = jnp.maximum(jnp.maximum(ls[0], ls[1]), ls[2])
    ws = [jnp.exp(l - m) for l in ls]
    den = ws[0] + ws[1] + ws[2]
    o = (ws[0] * os_[0] + ws[1] * os_[1] + ws[2] * os_[2]) / den
    b = jnp.dot(o.astype(BF16), wao_ref[...], preferred_element_type=F32)
    mixed = ga_ref[...].astype(F32) + sgb_ref[...].astype(F32) * b
    x2 = x_ref[...] + g1_ref[0] * jnp.dot(mixed.astype(BF16), wo_ref[...], preferred_element_type=F32)
    x2_ref[...] = x2
    h2 = _rms_modulate(x2, gf_ref[...], sc2_ref[0], sh2_ref[0])
    h2_ref[...] = h2
    lt = lax.dot_general(wr_ref[...], h2, (((1,), (1,)), ((), ())), preferred_element_type=F32,
                         precision=lax.Precision.HIGHEST) + br_ref[...]
    tm = h2.shape[0]
    gl = lt[0:8, :]
    gmax = jnp.max(gl, axis=0, keepdims=True)
    r8 = lax.broadcasted_iota(jnp.int32, (8, tm), 0)
    grp = jnp.min(jnp.where(gl == gmax, r8, 8), axis=0, keepdims=True)
    p_grp = 1.0 / jnp.sum(jnp.exp(gl - gmax), axis=0, keepdims=True)
    es = jnp.zeros((EXPERTS_PER_GROUP, tm), F32)
    for g in range(N_EXPERT_GROUPS):
        es = jnp.where(grp == g, lt[8 + 8 * g:16 + 8 * g, :], es)
    v1 = jnp.max(es, axis=0, keepdims=True)
    i1 = jnp.min(jnp.where(es == v1, r8, 8), axis=0, keepdims=True)
    rest = jnp.where(r8 == i1, -jnp.inf, es)
    v2 = jnp.max(rest, axis=0, keepdims=True)
    i2 = jnp.min(jnp.where(rest == v2, r8, 8), axis=0, keepdims=True)
    e21 = jnp.exp(v2 - v1)
    w1 = p_grp / (1.0 + e21)
    e1 = grp * EXPERTS_PER_GROUP + i1
    e2 = grp * EXPERTS_PER_GROUP + i2
    eid_ref[0, 0:1, :] = e1
    eid_ref[0, 1:2, :] = e2
    wts_ref[0, 0:1, :] = w1
    wts_ref[0, 1:2, :] = w1 * e21
    r_e = lax.broadcasted_iota(jnp.int32, (N_EXPERTS, tm), 0)
    hit1 = r_e == e1
    hit2 = r_e == e2
    both = jnp.where(hit1 | hit2, 1.0, 0.0)
    s_idx = lax.broadcasted_iota(jnp.int32, (tm, tm), 0)
    t_idx = lax.broadcasted_iota(jnp.int32, (tm, tm), 1)
    before = jnp.where(s_idx < t_idx, 1.0, 0.0).astype(BF16)
    base = carry[...] + jnp.dot(both.astype(BF16), before, preferred_element_type=F32)
    rank_ref[0, 0:1, :] = jnp.sum(jnp.where(hit1, base, 0.0), axis=0, keepdims=True).astype(jnp.int32)
    rank_ref[0, 1:2, :] = jnp.sum(jnp.where(hit2, base, 0.0), axis=0, keepdims=True).astype(jnp.int32)
    total = carry[...] + jnp.sum(both, axis=1, keepdims=True)
    carry[...] = total
    cnt_ref[...] = total


def _k3(x, o_l, g1, sh2, sc2, ga, sgb, norm_ffn_g, w_ao_b, w_o_b, w_r, b_r, cnt_in, tiles_per_mod):
    T = x.shape[0]
    tm = min(TOKEN_TILE, T)
    nt = T // tm
    tok = lambda w: pl.BlockSpec((tm, w), lambda t: (t, 0))
    mod = pl.BlockSpec((1,) + g1.shape[1:], lambda t: (t // tiles_per_mod, 0, 0))
    small = pl.BlockSpec((1, 2, tm), lambda t: (t, 0, 0))
    small_i = jax.ShapeDtypeStruct((nt, 2, tm), jnp.int32)

    def attn_spec(a):
        if a.ndim == 2:
            return tok(GROUP_WIDTH)
        dil = a.shape[2] // GROUP_WIDTH
        return pl.BlockSpec((1, tm // dil, a.shape[2]), lambda t: (t // tiles_per_mod, t % tiles_per_mod, 0))

    attn_in = [o_l[0][0], o_l[1][0], o_l[2][0], o_l[0][1], o_l[1][1], o_l[2][1]]
    return pl.pallas_call(
        _k3_kernel,
        grid=(nt,),
        in_specs=[tok(D_MODEL)] + [attn_spec(a) for a in attn_in] + [tok(D_MODEL)] * 2 + [mod] * 3
                 + [_const_spec((1, D_MODEL)), _weight_spec(w_ao_b.shape), _weight_spec(w_o_b.shape),
                    _const_spec(w_r.shape), _const_spec(b_r.shape), _const_spec((N_EXPERTS, 1))],
        out_specs=[tok(D_MODEL), tok(D_MODEL), small, small, small, _const_spec((N_EXPERTS, 1))],
        out_shape=[jax.ShapeDtypeStruct((T, D_MODEL), F32), jax.ShapeDtypeStruct((T, D_MODEL), F32),
                   small_i, jax.ShapeDtypeStruct((nt, 2, tm), F32), small_i,
                   jax.ShapeDtypeStruct((N_EXPERTS, 1), F32)],
        scratch_shapes=([pltpu.VMEM((N_EXPERTS, 1), F32)]
                        + [pltpu.VMEM((tm, LANES), F32)] * (2 * N_GROUPS * STAGE_HALVES)),
        compiler_params=_params(("arbitrary",)),
        name="merge_router",
    )(x, *attn_in, ga, sgb, g1, sh2, sc2, norm_ffn_g, w_ao_b, w_o_b, w_r, b_r, cnt_in)


def _rows_wait(n_rows, hbm, vmem, sem):
    pltpu.make_async_copy(hbm.at[pl.ds(0, n_rows)], vmem, sem).wait()


def _dispatch_kernel(n_tiles, n_first, dest_ref, fill_ref, ha_ref, hb_ref, xs_ref, buf, sem, zbuf, zsem):
    i = pl.program_id(0)
    slot = i % 2 if n_tiles > 1 else 0
    tm = ha_ref.shape[0]

    def drain(s):
        for _ in range(2):
            _rows_wait(tm, xs_ref, buf.at[s], sem.at[s])

    @pl.when(i == 0)
    def _():
        zbuf[...] = jnp.zeros(zbuf.shape, F32)

        def fill(row):
            return pltpu.make_async_copy(zbuf, xs_ref.at[pl.ds(pl.multiple_of(row, MOE_ROWS), MOE_ROWS)], zsem)

        n_blocks = xs_ref.shape[0] // MOE_ROWS
        n_used = fill_ref[N_EXPERTS]

        def start_block(j, carry):
            fill(j * MOE_ROWS).start()
            return carry

        def wait_block(j, carry):
            fill(j * MOE_ROWS).wait()
            return carry

        for e in range(N_EXPERTS):
            @pl.when(fill_ref[e] >= 0)
            def _(e=e):
                fill(fill_ref[e]).start()
        lax.fori_loop(n_used, n_blocks, start_block, 0)
        for e in range(N_EXPERTS):
            @pl.when(fill_ref[e] >= 0)
            def _(e=e):
                fill(fill_ref[e]).wait()
        lax.fori_loop(n_used, n_blocks, wait_block, 0)

    if n_tiles > 2:
        @pl.when(i >= 2)
        def _():
            drain(slot)

    @pl.when(i < n_first)
    def _():
        buf[slot] = ha_ref[...]

    @pl.when(i >= n_first)
    def _():
        buf[slot] = hb_ref[...]

    for r in range(tm):
        for k in range(2):
            pltpu.make_async_copy(buf.at[slot, pl.ds(r, 1)], xs_ref.at[pl.ds(dest_ref[0, k, r], 1)], sem.at[slot]).start()

    if n_tiles == 1:
        drain(0)
    else:
        @pl.when(i == n_tiles - 1)
        def _():
            drain(slot)
            drain(1 - slot)


def _dispatch(dest, fill_rows, h_a, h_b, n_slots):
    nt, _, tm = dest.shape
    n_first = h_a.shape[0] // tm
    assert h_a.shape[0] % tm == 0 and h_b.shape[0] == (nt - n_first) * tm
    return pl.pallas_call(
        functools.partial(_dispatch_kernel, nt, n_first),
        grid=(nt,),
        in_specs=[pl.BlockSpec((1, 2, tm), lambda t: (t, 0, 0), memory_space=pltpu.SMEM),
                  pl.BlockSpec(memory_space=pltpu.SMEM),
                  pl.BlockSpec((tm, D_MODEL), lambda t: (jnp.minimum(t, n_first - 1), 0)),
                  pl.BlockSpec((tm, D_MODEL), lambda t: (jnp.maximum(t - n_first, 0), 0))],
        out_specs=pl.BlockSpec(memory_space=pl.ANY),
        out_shape=jax.ShapeDtypeStruct((n_slots, D_MODEL), F32),
        scratch_shapes=[pltpu.VMEM((2, tm, D_MODEL), F32), pltpu.SemaphoreType.DMA((2,)),
                        pltpu.VMEM((MOE_ROWS, D_MODEL), F32), pltpu.SemaphoreType.DMA(())],
        compiler_params=_params(("arbitrary",)),
        name="dispatch_rows",
    )(dest, fill_rows, h_a, h_b)


def _k4_kernel(be_ref, nused_ref, x_ref, wg_ref, wu_ref, wd_ref, y_ref, wg_b, wu_b, wd_b):
    i = pl.program_id(0)

    @pl.when((i == 0) | (be_ref[i] != be_ref[jnp.maximum(i - 1, 0)]))
    def _():
        wg_b[...] = wg_ref[0].astype(BF16)
        wu_b[...] = wu_ref[0].astype(BF16)
        wd_b[...] = wd_ref[0].astype(BF16)

    @pl.when(i < nused_ref[0])
    def _():
        xb = x_ref[...].astype(BF16)
        gate = jnp.dot(xb, wg_b[...], preferred_element_type=F32)
        up = jnp.dot(xb, wu_b[...], preferred_element_type=F32)
        mid = (_silu(gate) * up).astype(BF16)
        y_ref[...] = jnp.dot(mid, wd_b[...], preferred_element_type=F32)

    @pl.when(i >= nused_ref[0])
    def _():
        y_ref[...] = jnp.zeros(y_ref.shape, F32)


def _k4(block_expert, n_used, xs, w_gate, w_up, w_down):
    n_blocks = block_expert.shape[0]
    rows = MOE_ROWS
    grid_spec = pltpu.PrefetchScalarGridSpec(
        num_scalar_prefetch=2,
        grid=(n_blocks,),
        in_specs=[pl.BlockSpec((rows, D_MODEL), lambda i, be, nu: (jnp.minimum(i, nu[0] - 1), 0)),
                  pl.BlockSpec((1, D_MODEL, D_FF_EXPERT), lambda i, be, nu: (be[i], 0, 0)),
                  pl.BlockSpec((1, D_MODEL, D_FF_EXPERT), lambda i, be, nu: (be[i], 0, 0)),
                  pl.BlockSpec((1, D_FF_EXPERT, D_MODEL), lambda i, be, nu: (be[i], 0, 0))],
        out_specs=pl.BlockSpec((rows, D_MODEL), lambda i, be, nu: (i, 0)),
        scratch_shapes=[pltpu.VMEM((D_MODEL, D_FF_EXPERT), BF16), pltpu.VMEM((D_MODEL, D_FF_EXPERT), BF16),
                        pltpu.VMEM((D_FF_EXPERT, D_MODEL), BF16)],
    )
    return pl.pallas_call(
        _k4_kernel,
        grid_spec=grid_spec,
        out_shape=jax.ShapeDtypeStruct((n_blocks * rows, D_MODEL), F32),
        compiler_params=_params(("arbitrary",)),
        name="expert_blocks",
    )(block_expert, n_used, xs, w_gate, w_up, w_down)


def _k5_kernel(n_tiles, idx_ref, x2_ref, w_ref, g2_ref, gfin_ref, ys_hbm, y_ref, ybuf, sem):
    i = pl.program_id(0)
    tm = x2_ref.shape[0]

    @pl.when(i < n_tiles)
    def _():
        slot = i % 2
        for r in range(tm):
            for k in range(2):
                pltpu.make_async_copy(ys_hbm.at[pl.ds(idx_ref[0, k, r], 1)], ybuf.at[slot, k, pl.ds(r, 1)],
                                      sem.at[slot]).start()

    @pl.when(i >= 1)
    def _():
        slot = (i - 1) % 2
        for k in range(2):
            _rows_wait(tm, ys_hbm, ybuf.at[slot, k], sem.at[slot])
        w = w_ref[...]
        f = ybuf[slot, 0] * w[:, 0:1] + ybuf[slot, 1] * w[:, 1:2]
        y = x2_ref[...] + g2_ref[0] * f
        r = lax.rsqrt(jnp.mean(y * y, axis=-1, keepdims=True) + EPS)
        y_ref[...] = (y * r) * gfin_ref[...]


def _k5(dest, x2, wts, g2, norm_final_g, ys, tiles_per_mod):
    nt, _, tm = dest.shape
    T = x2.shape[0]
    prev = lambda t: jnp.maximum(t - 1, 0)
    return pl.pallas_call(
        functools.partial(_k5_kernel, nt),
        grid=(nt + 1,),
        in_specs=[pl.BlockSpec((1, 2, tm), lambda t: (jnp.minimum(t, nt - 1), 0, 0), memory_space=pltpu.SMEM),
                  pl.BlockSpec((tm, D_MODEL), lambda t: (prev(t), 0)),
                  pl.BlockSpec((tm, 2), lambda t: (prev(t), 0)),
                  pl.BlockSpec((1,) + g2.shape[1:], lambda t: (prev(t) // tiles_per_mod, 0, 0)),
                  _const_spec((1, D_MODEL)), pl.BlockSpec(memory_space=pl.ANY)],
        out_specs=pl.BlockSpec((tm, D_MODEL), lambda t: (prev(t), 0)),
        out_shape=jax.ShapeDtypeStruct((T, D_MODEL), F32),
        scratch_shapes=[pltpu.VMEM((2, 2, tm, D_MODEL), F32), pltpu.SemaphoreType.DMA((2,))],
        compiler_params=_params(("arbitrary",)),
        name="combine_norm",
    )(dest, x2, wts, g2, norm_final_g, ys)


def _slot_tables(counts, n_blocks):
    padded = (counts + MOE_ROWS - 1) // MOE_ROWS * MOE_ROWS
    pend = jnp.cumsum(padded)
    pstart = pend - padded
    block_lo = jnp.arange(n_blocks, dtype=jnp.int32) * MOE_ROWS
    block_expert = jnp.minimum(jnp.sum(pend[None, :] <= block_lo[:, None], axis=1), N_EXPERTS - 1).astype(jnp.int32)
    n_used = (pend[-1] // MOE_ROWS).astype(jnp.int32).reshape(1)
    fill_rows = jnp.concatenate([jnp.where(padded > 0, pend - MOE_ROWS, -1).astype(jnp.int32), n_used])
    return pstart.astype(jnp.int32), block_expert, n_used, fill_rows


def _slots(eid, rank, pstart):
    sel = eid[..., None] == jnp.arange(N_EXPERTS, dtype=jnp.int32)
    return jnp.sum(jnp.where(sel, pstart, 0), axis=-1).astype(jnp.int32) + rank


def _cache_view(cache):
    bd, lc = cache.shape[:2]
    return cache.transpose(0, 2, 3, 4, 1).reshape(bd, 2 * GROUP_WIDTH, lc)


def _cache_unview(view):
    bd, _, lc = view.shape
    return view.reshape(bd, 2, HEADS_PER_GROUP, HEAD_DIM, lc).transpose(0, 4, 1, 2, 3)


def kernel(x_prompt, x_sample, c_prompt, c_sample, cache_kv_w128, cache_kv_w512, cache_kv_w2048, state_conv, rel_bias, norm_mix_g, norm_ffn_g, w_mod, b_mod, w_in, dw_w, dw_b, ln_conv_g, ln_conv_b, w_conv_out, w_attn_out, w_out, w_router_group, b_router_group, w_router_expert, b_router_expert, w_exp_gate, w_exp_up, w_exp_down, norm_final_g):
    assert norm_mix_g.shape[0] == 1, "single layer"
    B, S, D = x_prompt.shape
    Bd, Td, _ = x_sample.shape
    caches_in = (cache_kv_w128[0], cache_kv_w512[0], cache_kv_w2048[0])
    for (win, dil), c in zip(DILATED_GROUPS, caches_in):
        assert c.shape[1] >= (win // dil) * dil and c.shape[1] % 128 == 0 and S % (dil * ATTN_BLOCK) == 0

    wi = w_in[0]
    q_cols = wi[:, :ATTN_WIDTH] * (HEAD_DIM ** -0.5)
    kv_cols = []
    for g in range(N_GROUPS):
        kv_cols += [wi[:, ATTN_WIDTH + g * GROUP_WIDTH:ATTN_WIDTH + (g + 1) * GROUP_WIDTH],
                    wi[:, 2 * ATTN_WIDTH + g * GROUP_WIDTH:2 * ATTN_WIDTH + (g + 1) * GROUP_WIDTH]]
    w_in_b = jnp.concatenate([q_cols] + kv_cols + [wi[:, 3 * ATTN_WIDTH:]], axis=1).astype(BF16)
    w_co_b = w_conv_out[0].astype(BF16)
    w_ao_b = w_attn_out[0].astype(BF16)
    w_o_b = w_out[0].astype(BF16)
    w_r = jnp.zeros((ROUTER_ROWS, D), F32)
    w_r = w_r.at[0:N_EXPERT_GROUPS].set(w_router_group[0].T)
    w_r = w_r.at[8:].set(w_router_expert[0].reshape(D, N_EXPERTS).T)
    b_r = jnp.full((ROUTER_ROWS, 1), NEG_INF, F32)
    b_r = b_r.at[0:N_EXPERT_GROUPS, 0].set(b_router_group[0])
    b_r = b_r.at[8:, 0].set(b_router_expert[0].reshape(N_EXPERTS))
    row = lambda v: v.reshape(1, -1)

    n_seq = B + Bd
    n_seq_pad = -(-n_seq // 16) * 16
    c_all = jnp.concatenate([c_prompt, c_sample, jnp.zeros((n_seq_pad - n_seq, D), F32)], axis=0)
    mod = _modulation(c_all, w_mod[0], row(b_mod[0]))
    mod_p = [mod[:B, j * D:(j + 1) * D].reshape(B, 1, D) for j in range(6)]
    mod_s = [jnp.tile(mod[B:B + Bd, j * D:(j + 1) * D], (Td, 1)) for j in range(6)]

    (q0, q1, q2, k0, v0, k1, v1, k2, v2, kvt0, kvt1, kvt2, ga, sgb, convp) = _k1_prompt(
        x_prompt, mod_p[0], mod_p[1], row(norm_mix_g[0]), w_in_b, dw_w[0], row(dw_b[0]),
        row(ln_conv_g[0]), row(ln_conv_b[0]), w_co_b)
    o_l = []
    for g, ((win, dil), qg, kg, vg) in enumerate(zip(DILATED_GROUPS, (q0, q1, q2), (k0, k1, k2), (v0, v1, v2))):
        bias = _prompt_bias(rel_bias[:, g * HEADS_PER_GROUP:(g + 1) * HEADS_PER_GROUP], dil, win // dil)
        o_l.append(_k2_prompt(qg, kg, vg, bias, dil))
    xp = x_prompt.reshape(B * S, D)
    tm3 = min(TOKEN_TILE, B * S)
    x2, h2, eid, wts, rank, cnt_p = _k3(xp, o_l, mod_p[2], mod_p[3], mod_p[4], ga, sgb, row(norm_ffn_g[0]),
                                        w_ao_b, w_o_b, w_r, b_r, jnp.zeros((N_EXPERTS, 1), F32), S // tm3)
    kv_p =[kvt.reshape(1, B, kvt.shape[1], 2, HEADS_PER_GROUP, HEAD_DIM) for kvt in (kvt0, kvt1, kvt2)]
    conv_p = convp[:, CONV_HIST - (CONV_WIDTH - 1):, :][None]

    Ts = Td * Bd
    xs = x_sample.transpose(1, 0, 2).reshape(Ts, D)
    state_tm = state_conv[0].transpose(1, 0, 2)
    (sq0, sq1, sq2, sk0, sv0, sk1, sv1, sk2, sv2, ga_s, sgb_s, conv_tm) = _k1_sample(
        xs, mod_s[0], mod_s[1], row(norm_mix_g[0]), w_in_b, state_tm, dw_w[0], row(dw_b[0]),
        row(ln_conv_g[0]), row(ln_conv_b[0]), w_co_b, Td)
    q_rows = 16
    q_b = jnp.concatenate([sq0, sq1, sq2], axis=1).reshape(Td, Bd, ATTN_WIDTH).transpose(1, 0, 2)
    q_b = jnp.pad(q_b, ((0, 0), (0, q_rows - Td), (0, 0)))
    views, tails, biases_c, biases_n = [], [], [], []
    for g, ((win, dil), c, sk, sv) in enumerate(zip(DILATED_GROUPS, caches_in, (sk0, sk1, sk2), (sv0, sv1, sv2))):
        lc = c.shape[1]
        views.append(_cache_view(c))
        new_kv = jnp.concatenate([sk, sv], axis=1).reshape(Td, Bd, 2 * GROUP_WIDTH).transpose(1, 2, 0)
        tails.append(jnp.pad(new_kv, ((0, 0), (0, 0), (128 - Td, 0))))
        bc, bn = _sample_bias(rel_bias[:, g * HEADS_PER_GROUP:(g + 1) * HEADS_PER_GROUP], dil, win // dil, lc, Td, q_rows)
        biases_c.append(bc)
        biases_n.append(bn)
    (nc0, nc1, nc2, so0, so1, so2, sl0, sl1, sl2) = _k2_sample(q_b, views, tails, biases_c, biases_n, Td)
    to_tm = lambda a: a[:, :Td, :].transpose(1, 0, 2).reshape(Ts, GROUP_WIDTH)
    o_l_s = [(to_tm(so0), to_tm(sl0)), (to_tm(so1), to_tm(sl1)), (to_tm(so2), to_tm(sl2))]
    tm3s = min(TOKEN_TILE, Ts)
    tiled = lambda m: m.reshape(Ts // tm3s, tm3s, D)
    x2s, h2s, eid_s, wts_s, rank_s, cnt_all = _k3(xs, o_l_s, tiled(mod_s[2]), tiled(mod_s[3]), tiled(mod_s[4]),
                                                  ga_s, sgb_s, row(norm_ffn_g[0]), w_ao_b, w_o_b, w_r, b_r, cnt_p, 1)

    n_assign = 2 * (B * S + Ts)
    n_blocks = -(-n_assign // MOE_ROWS) + N_EXPERTS
    pstart, block_expert, n_used, fill_rows = _slot_tables(cnt_all[:, 0].astype(jnp.int32), n_blocks)
    dest_p = _slots(eid, rank, pstart)
    dest_s = _slots(eid_s, rank_s, pstart)
    n_slots = n_blocks * MOE_ROWS
    assert tm3s == tm3, "prompt and sample token tiles must match to share the dispatch"
    slots = _dispatch(jnp.concatenate([dest_p, dest_s], axis=0), fill_rows, h2, h2s, n_slots)
    ys = _k4(block_expert, n_used, slots, w_exp_gate[0], w_exp_up[0], w_exp_down[0])
    per_token = lambda a: a.transpose(0, 2, 1).reshape(-1, 2)
    y_prompt = _k5(dest_p, x2, per_token(wts), mod_p[5], row(norm_final_g), ys, S // tm3).reshape(B, S, D)
    y_s = _k5(dest_s, x2s, per_token(wts_s), tiled(mod_s[5]), row(norm_final_g), ys, 1)
    y_sample = y_s.reshape(Td, Bd, D).transpose(1, 0, 2)
    kv_s = [_cache_unview(nc)[None] for nc in (nc0, nc1, nc2)]
    conv_s = conv_tm.transpose(1, 0, 2)[None]

    return (y_prompt, y_sample, kv_p[0], kv_p[1], kv_p[2], conv_p,
            kv_s[0], kv_s[1], kv_s[2], conv_s)
```

```python
import functools
import math

import jax
import jax.numpy as jnp
from jax import lax
from jax.experimental import pallas as pl
from jax.experimental.pallas import tpu as pltpu

F32 = jnp.float32
BF16 = jnp.bfloat16

D_MODEL = 1024
HEAD_DIM = 64
HEADS_PER_GROUP = 4
GROUP_WIDTH = HEADS_PER_GROUP * HEAD_DIM
LANES = 128
SUBLANES = 8
CONV_CHUNK = 32
STAGE_HALVES = GROUP_WIDTH // LANES
DILATED_GROUPS = ((128, 1), (512, 4), (2048, 16))
N_GROUPS = len(DILATED_GROUPS)
ATTN_WIDTH = N_GROUPS * GROUP_WIDTH
CONV_CH = D_MODEL // 2
CONV_WIDTH = 31
CONV_HIST = 32
N_BUCKETS = 32
MAX_DISTANCE = 2048
N_EXPERT_GROUPS = 4
EXPERTS_PER_GROUP = 8
N_EXPERTS = N_EXPERT_GROUPS * EXPERTS_PER_GROUP
D_FF_EXPERT = D_MODEL // 2
EPS = 1e-6
NEG_INF = -1e30

COL_KV = ATTN_WIDTH
COL_ULIN = COL_KV + 2 * ATTN_WIDTH
COL_UGATE = COL_ULIN + CONV_CH
COL_GA = COL_UGATE + CONV_CH
COL_GB = COL_GA + D_MODEL
IN_COLS = COL_GB + D_MODEL

ROUTER_ROWS = 8 + N_EXPERTS

V7X_VMEM_LIMIT = 56 * 1024 * 1024
TOKEN_TILE = 512
ATTN_ROWS = 512
ATTN_BLOCK = 128
MOE_ROWS = 256


def _sigmoid(x):
    return 1.0 / (1.0 + jnp.exp(-x))


def _silu(x):
    return x * _sigmoid(x)


def _rms_modulate(x, g, sc, sh):
    r = lax.rsqrt(jnp.mean(x * x, axis=-1, keepdims=True) + EPS)
    return ((x * r) * g) * (1.0 + sc) + sh


def _conv_tail(y, g, b):
    mu = jnp.mean(y, axis=-1, keepdims=True)
    yc = y - mu
    var = jnp.mean(yc * yc, axis=-1, keepdims=True)
    return _silu((yc * lax.rsqrt(var + EPS)) * g + b)


def _params(semantics):
    return pltpu.CompilerParams(dimension_semantics=semantics, vmem_limit_bytes=V7X_VMEM_LIMIT)


def _const_spec(shape):
    nd = len(shape)
    return pl.BlockSpec(shape, lambda *_: (0,) * nd)


def _weight_spec(shape):
    nd = len(shape)
    return pl.BlockSpec(shape, lambda *_: (0,) * nd, pipeline_mode=pl.Buffered(1))


def _mod_kernel(c_ref, w_ref, b_ref, o_ref):
    s = _silu(c_ref[...]).astype(BF16)
    o_ref[...] = jnp.dot(s, w_ref[...].astype(BF16), preferred_element_type=F32) + b_ref[...]


def _modulation(c_all, w_mod, b_mod):
    rows = c_all.shape[0]
    n_out = w_mod.shape[1]
    chunk = D_MODEL
    return pl.pallas_call(
        _mod_kernel,
        grid=(n_out // chunk,),
        in_specs=[_const_spec((rows, D_MODEL)),
                  pl.BlockSpec((D_MODEL, chunk), lambda j: (0, j)),
                  pl.BlockSpec((1, chunk), lambda j: (0, j))],
        out_specs=pl.BlockSpec((rows, chunk), lambda j: (0, j)),
        out_shape=jax.ShapeDtypeStruct((rows, n_out), F32),
        compiler_params=_params(("arbitrary",)),
        name="modulation",
    )(c_all, w_mod, b_mod)


def _to_streams(ref, val, dil, stage):
    if dil == 1:
        ref[...] = val.astype(ref.dtype).reshape(ref.shape)
        return
    n = val.shape[0] // dil
    for c, half in enumerate(stage):
        half[...] = val[:, c * LANES:(c + 1) * LANES]
        for r in range(dil):
            lo = r * GROUP_WIDTH + c * LANES
            ref[0, :, lo:lo + LANES] = half[pl.ds(r, n, stride=dil), :].astype(ref.dtype)


def _from_streams(ref, dil, stage):
    if dil == 1:
        return ref[...].astype(F32).reshape(ref.shape[-2:])
    n = ref.shape[1]
    for c, half in enumerate(stage):
        for r in range(dil):
            lo = r * GROUP_WIDTH + c * LANES
            half[pl.ds(r, n, stride=dil), :] = ref[0, :, lo:lo + LANES].astype(F32)
    return jnp.concatenate([half[...] for half in stage], axis=1)


def _project_common(hb, win_ref, outs, dils, stages):
    (q_refs, k_refs, v_refs) = outs

    def proj(lo, width):
        return jnp.dot(hb, win_ref[:, lo:lo + width], preferred_element_type=F32)

    zq = proj(0, ATTN_WIDTH)
    for g in range(N_GROUPS):
        _to_streams(q_refs[g], zq[:, g * GROUP_WIDTH:(g + 1) * GROUP_WIDTH], dils[g], stages[g][0])
    zkvs = []
    for g in range(N_GROUPS):
        zkv = proj(COL_KV + 2 * GROUP_WIDTH * g, 2 * GROUP_WIDTH)
        _to_streams(k_refs[g], zkv[:, :GROUP_WIDTH], dils[g], stages[g][1])
        _to_streams(v_refs[g], zkv[:, GROUP_WIDTH:], dils[g], stages[g][2])
        zkvs.append(zkv)
    u = proj(COL_ULIN, CONV_CH) * _sigmoid(proj(COL_UGATE, CONV_CH))
    return zkvs, u, proj


def _k1_kernel(tail_rows, n_tiles,
               x_ref, sh_ref, sc_ref, g_ref, win_ref, dww_ref, dwb_ref, lng_ref, lnb_ref, wco_ref,
               q0, q1, q2, k0, v0, k1, v1, k2, v2, kvt0, kvt1, kvt2, ga_ref, sgb_ref, convp_ref, uext, convbuf, *stage_refs):
    i = pl.program_id(1)
    tm = x_ref.shape[1]

    @pl.when(i == 0)
    def _():
        uext[0:CONV_HIST, :] = jnp.zeros((CONV_HIST, CONV_CH), F32)

    h = _rms_modulate(x_ref[0], g_ref[...], sc_ref[0], sh_ref[0])
    hb = h.astype(BF16)
    dils = tuple(dil for _, dil in DILATED_GROUPS)
    pairs = [stage_refs[j:j + STAGE_HALVES] for j in range(0, len(stage_refs), STAGE_HALVES)]
    stages = [(None,) * 3] + [pairs[3 * (g - 1):3 * g] for g in range(1, N_GROUPS)]
    zkvs, u, proj = _project_common(hb, win_ref, ((q0, q1, q2), (k0, k1, k2), (v0, v1, v2)), dils, stages)
    for g, kvt in enumerate((kvt0, kvt1, kvt2)):
        tr = kvt.shape[1]
        first = n_tiles - tail_rows[g] // tr

        @pl.when(i >= first)
        def _(kvt=kvt, zkv=zkvs[g], tr=tr):
            kvt[0] = zkv[tm - tr:, :]

    uext[CONV_HIST:CONV_HIST + tm, :] = u
    base = CONV_HIST - (CONV_WIDTH - 1)

    def conv_rows(j, carry):
        r0 = pl.multiple_of(j * CONV_CHUNK, CONV_CHUNK)
        wide = uext[pl.ds(r0, CONV_CHUNK + CONV_HIST), :]
        acc = jnp.zeros((CONV_CHUNK, CONV_CH), F32) + dwb_ref[...]
        for b in range(SUBLANES):
            offs = [base + k for k in range(CONV_WIDTH) if (base + k) % SUBLANES == b]
            if not offs:
                continue
            shifted = wide[b:b + max(offs) - b + CONV_CHUNK, :]
            for off in offs:
                k = off - base
                acc = acc + dww_ref[k:k + 1, :] * shifted[off - b:off - b + CONV_CHUNK, :]
        convbuf[pl.ds(r0, CONV_CHUNK), :] = acc
        return carry

    lax.fori_loop(0, tm // CONV_CHUNK, conv_rows, 0)
    last = uext[tm:tm + CONV_HIST, :]
    uext[0:CONV_HIST, :] = last
    convp_ref[0] = last
    s = _conv_tail(convbuf[...], lng_ref[...], lnb_ref[...])
    a = jnp.dot(s.astype(BF16), wco_ref[...], preferred_element_type=F32)
    ga_ref[...] = (_sigmoid(proj(COL_GA, D_MODEL)) * a).astype(BF16)
    sgb_ref[...] = _sigmoid(proj(COL_GB, D_MODEL)).astype(BF16)


def _k1_prompt(x, sh1, sc1, norm_g, w_in_b, dw_w, dw_b, ln_g, ln_b, w_co_b):
    B, S, _ = x.shape
    tm = min(TOKEN_TILE, S)
    nt = S // tm
    tail_rows = tuple(min(win, S) for win, _ in DILATED_GROUPS)
    tail_blk = tuple(min(t, tm) for t in tail_rows)

    def tok_spec(width):
        return pl.BlockSpec((tm, width), lambda b, i: (b * nt + i, 0))

    def tail_spec(g):
        first = nt - tail_rows[g] // tail_blk[g]
        return pl.BlockSpec((1, tail_blk[g], 2 * GROUP_WIDTH), lambda b, i: (b, jnp.maximum(i - first, 0), 0))

    def stream_shape(dil):
        return jax.ShapeDtypeStruct((B, S // dil, dil * GROUP_WIDTH), BF16)

    def stream_spec(dil):
        return pl.BlockSpec((1, tm // dil, dil * GROUP_WIDTH), lambda b, i: (b, i, 0))

    dils = [dil for _, dil in DILATED_GROUPS]
    qkv_order = [dils[0], dils[1], dils[2]] + [d for d in dils for _ in range(2)]
    mod_spec = pl.BlockSpec((1, 1, D_MODEL), lambda b, i: (b, 0, 0))
    out_shape = ([stream_shape(d) for d in qkv_order]
                 + [jax.ShapeDtypeStruct((B, tail_rows[g], 2 * GROUP_WIDTH), F32) for g in range(N_GROUPS)]
                 + [jax.ShapeDtypeStruct((B * S, D_MODEL), BF16)] * 2
                 + [jax.ShapeDtypeStruct((B, CONV_HIST, CONV_CH), F32)])
    out_specs = ([stream_spec(d) for d in qkv_order] + [tail_spec(g) for g in range(N_GROUPS)]
                 + [tok_spec(D_MODEL)] * 2 + [pl.BlockSpec((1, CONV_HIST, CONV_CH), lambda b, i: (b, 0, 0))])
    n_stage = 3 * sum(1 for d in dils if d > 1)
    return pl.pallas_call(
        functools.partial(_k1_kernel, tail_rows, nt),
        grid=(B, nt),
        in_specs=[pl.BlockSpec((1, tm, D_MODEL), lambda b, i: (b, i, 0)), mod_spec, mod_spec,
                  _const_spec((1, D_MODEL)), _weight_spec((D_MODEL, IN_COLS)),
                  _const_spec((CONV_WIDTH, CONV_CH)), _const_spec((1, CONV_CH)),
                  _const_spec((1, CONV_CH)), _const_spec((1, CONV_CH)), _weight_spec((CONV_CH, D_MODEL))],
        out_specs=out_specs,
        out_shape=out_shape,
        scratch_shapes=([pltpu.VMEM((CONV_HIST + tm, CONV_CH), F32), pltpu.VMEM((tm, CONV_CH), F32)]
                        + [pltpu.VMEM((tm, LANES), F32)] * (n_stage * STAGE_HALVES)),
        compiler_params=_params(("arbitrary", "arbitrary")),
        name="inproj_prompt",
    )(x, sh1, sc1, norm_g, w_in_b, dw_w, dw_b, ln_g, ln_b, w_co_b)


def _k1s_kernel(n_steps, x_ref, sh_ref, sc_ref, g_ref, win_ref, st_ref, dww_ref, dwb_ref, lng_ref, lnb_ref, wco_ref,
                q0, q1, q2, k0, v0, k1, v1, k2, v2, ga_ref, sgb_ref, conv_ref):
    bd = st_ref.shape[1]
    hist = st_ref.shape[0]
    h = _rms_modulate(x_ref[...], g_ref[...], sc_ref[...], sh_ref[...])
    hb = h.astype(BF16)
    _, u, proj = _project_common(hb, win_ref, ((q0, q1, q2), (k0, k1, k2), (v0, v1, v2)),
                                 (1,) * N_GROUPS, [(None,) * 3] * N_GROUPS)

    def ext(j):
        return st_ref[j] if j < hist else u[(j - hist) * bd:(j - hist + 1) * bd, :]

    outs = []
    for t in range(n_steps):
        acc = jnp.zeros((bd, CONV_CH), F32) + dwb_ref[...]
        for k in range(CONV_WIDTH):
            acc = acc + dww_ref[k:k + 1, :] * ext(t + k + hist - (CONV_WIDTH - 1))
        outs.append(acc)
    for j in range(hist):
        conv_ref[j] = ext(j + n_steps)
    s = _conv_tail(jnp.concatenate(outs, axis=0), lng_ref[...], lnb_ref[...])
    a = jnp.dot(s.astype(BF16), wco_ref[...], preferred_element_type=F32)
    ga_ref[...] = (_sigmoid(proj(COL_GA, D_MODEL)) * a).astype(BF16)
    sgb_ref[...] = _sigmoid(proj(COL_GB, D_MODEL)).astype(BF16)


def _k1_sample(x_tm, sh1, sc1, norm_g, w_in_b, state_tm, dw_w, dw_b, ln_g, ln_b, w_co_b, n_steps):
    T = x_tm.shape[0]
    hist, bd, _ = state_tm.shape
    out_shape = ([jax.ShapeDtypeStruct((T, GROUP_WIDTH), F32)] * 9
                 + [jax.ShapeDtypeStruct((T, D_MODEL), BF16)] * 2
                 + [jax.ShapeDtypeStruct((hist, bd, CONV_CH), F32)])
    out_specs = ([_const_spec((T, GROUP_WIDTH))] * 9 + [_const_spec((T, D_MODEL))] * 2
                 + [_const_spec((hist, bd, CONV_CH))])
    return pl.pallas_call(
        functools.partial(_k1s_kernel, n_steps),
        grid=(1,),
        in_specs=[_const_spec((T, D_MODEL)), _const_spec((T, D_MODEL)), _const_spec((T, D_MODEL)),
                  _const_spec((1, D_MODEL)), _const_spec((D_MODEL, IN_COLS)), _const_spec((hist, bd, CONV_CH)),
                  _const_spec((CONV_WIDTH, CONV_CH)), _const_spec((1, CONV_CH)),
                  _const_spec((1, CONV_CH)), _const_spec((1, CONV_CH)), _const_spec((CONV_CH, D_MODEL))],
        out_specs=out_specs,
        out_shape=out_shape,
        compiler_params=_params(("arbitrary",)),
        name="inproj_sample",
    )(x_tm, sh1, sc1, norm_g, w_in_b, state_tm, dw_w, dw_b, ln_g, ln_b, w_co_b)


def _t5_bucket(dist):
    max_exact = N_BUCKETS // 2
    d_f = jnp.maximum(dist, 1).astype(F32)
    large = max_exact + (jnp.log(d_f / max_exact) / math.log(MAX_DISTANCE / max_exact)
                         * (N_BUCKETS - max_exact)).astype(jnp.int32)
    large = jnp.minimum(large, N_BUCKETS - 1)
    return jnp.where(dist < max_exact, dist, large)


def _bucket_lookup(rel_bias_g, dist):
    bucket = _t5_bucket(dist)
    out = jnp.zeros((rel_bias_g.shape[1],) + dist.shape, F32)
    for b in range(N_BUCKETS):
        out = jnp.where(bucket[None] == b, rel_bias_g[b].reshape((-1,) + (1,) * dist.ndim), out)
    return out


def _prompt_bias(rel_bias_g, dil, n_keys):
    blk = n_keys
    i = jnp.arange(blk)[:, None]
    j = jnp.arange(2 * blk)[None, :]
    rel = i - j + blk
    valid = (rel >= 0) & (rel <= n_keys)
    bias = _bucket_lookup(rel_bias_g, jnp.clip(rel, 0, n_keys) * dil)
    bias = jnp.where(valid[None], bias, NEG_INF)
    return bias.reshape(HEADS_PER_GROUP * blk, 2 * blk).astype(F32)


def _sample_bias(rel_bias_g, dil, n_keys, lc, n_steps, q_rows):
    t = jnp.arange(q_rows)[:, None]
    pos = jnp.arange(lc)[None, :]
    dist_c = lc + t - pos
    dist_n = t - (jnp.arange(128)[None, :] - (128 - n_steps))

    def table(dist, extra):
        ok = (dist >= 0) & (dist % dil == 0) & (dist // dil <= n_keys) & extra & (t < n_steps)
        b = _bucket_lookup(rel_bias_g, jnp.clip(dist, 0, None))
        b = jnp.where(ok[None], b, NEG_INF)
        return jnp.where((t >= n_steps)[None], 0.0, b).astype(F32)

    lane_ok = jnp.arange(128)[None, :] >= 128 - n_steps
    return table(dist_c, True), table(dist_n, lane_ok)


def _k2_kernel(q_ref, k_ref, kh_ref, v_ref, vh_ref, bias_ref, o_ref, lse_ref, kbuf, vbuf):
    i = pl.program_id(2)
    rows = q_ref.shape[1]
    blk = ATTN_BLOCK
    kbuf[0:blk, :] = kh_ref[0]
    kbuf[blk:blk + rows, :] = k_ref[0]
    vbuf[0:blk, :] = vh_ref[0]
    vbuf[blk:blk + rows, :] = v_ref[0]
    lane_head = lax.broadcasted_iota(jnp.int32, (blk, GROUP_WIDTH), 1) // HEAD_DIM
    col = lax.broadcasted_iota(jnp.int32, (HEADS_PER_GROUP * blk, 2 * blk), 1)
    first_mask = jnp.where((col < blk) & (i == 0), NEG_INF, 0.0).astype(F32)
    bias = bias_ref[...]
    for j in range(rows // blk):
        qb = q_ref[0, j * blk:(j + 1) * blk, :]
        q4 = jnp.concatenate([jnp.where(lane_head == h, qb, jnp.zeros_like(qb)) for h in range(HEADS_PER_GROUP)], axis=0)
        kc = kbuf[j * blk:(j + 2) * blk, :]
        vc = vbuf[j * blk:(j + 2) * blk, :]
        s = lax.dot_general(q4, kc, (((1,), (1,)), ((), ())), preferred_element_type=F32) + bias
        if j == 0:
            s = s + first_mask
        m = jnp.max(s, axis=-1, keepdims=True)
        p = jnp.exp(s - m)
        l = jnp.sum(p, axis=-1, keepdims=True)
        o4 = jnp.dot(p.astype(BF16), vc, preferred_element_type=F32) * (1.0 / l)
        lse4 = m + jnp.log(l)
        o = jnp.zeros((blk, GROUP_WIDTH), F32)
        lse = jnp.zeros((blk, GROUP_WIDTH), F32)
        for h in range(HEADS_PER_GROUP):
            sel = lane_head == h
            o = jnp.where(sel, o4[h * blk:(h + 1) * blk, :], o)
            lse = jnp.where(sel, lse4[h * blk:(h + 1) * blk, :], lse)
        o_ref[0, j * blk:(j + 1) * blk, :] = o.astype(o_ref.dtype)
        lse_ref[0, j * blk:(j + 1) * blk, :] = lse


def _k2_prompt(q, k, v, bias, dil):
    B, L, _ = q.shape
    rows = min(ATTN_ROWS, L)
    per = rows // ATTN_BLOCK
    main = pl.BlockSpec((1, rows, GROUP_WIDTH), lambda b, r, i: (b, i, r))
    halo = pl.BlockSpec((1, ATTN_BLOCK, GROUP_WIDTH), lambda b, r, i: (b, jnp.maximum(i * per - 1, 0), r))
    return pl.pallas_call(
        _k2_kernel,
        grid=(B, dil, L // rows),
        in_specs=[main, main, halo, main, halo, _const_spec(bias.shape)],
        out_specs=[main, main],
        out_shape=[jax.ShapeDtypeStruct(q.shape, BF16), jax.ShapeDtypeStruct(q.shape, F32)],
        scratch_shapes=[pltpu.VMEM((ATTN_BLOCK + rows, GROUP_WIDTH), BF16)] * 2,
        compiler_params=_params(("arbitrary", "arbitrary", "arbitrary")),
        name=f"attn_prompt_d{dil}",
    )(q, k, k, v, v, bias)


def _k2s_kernel(n_steps, q_ref, c0, c1, c2, t0, t1, t2, bc0, bc1, bc2, bn0, bn1, bn2,
                nc0, nc1, nc2, o0, o1, o2, l0, l1, l2):
    lane = lax.broadcasted_iota(jnp.int32, (2 * GROUP_WIDTH, 128), 1)
    keep = 128 - n_steps
    groups = ((c0, t0, bc0, bn0, nc0, o0, l0), (c1, t1, bc1, bn1, nc1, o1, l1), (c2, t2, bc2, bn2, nc2, o2, l2))
    for g, (c_ref, t_ref, bc_ref, bn_ref, nc_ref, o_ref, l_ref) in enumerate(groups):
        lc = c_ref.shape[2]
        n_tiles = lc // 128
        tail = t_ref[0]
        cur = pltpu.roll(c_ref[0, :, 0:128], keep, 1)
        for c in range(n_tiles):
            nxt = pltpu.roll(c_ref[0, :, (c + 1) * 128:(c + 2) * 128], keep, 1) if c + 1 < n_tiles else tail
            nc_ref[0, :, c * 128:(c + 1) * 128] = jnp.where(lane < keep, cur, nxt)
            cur = nxt
        for h in range(HEADS_PER_GROUP):
            lo = g * GROUP_WIDTH + h * HEAD_DIM
            qh = q_ref[0, :, lo:lo + HEAD_DIM].astype(BF16)
            kh = c_ref[0, h * HEAD_DIM:(h + 1) * HEAD_DIM, :].astype(BF16)
            vh = c_ref[0, GROUP_WIDTH + h * HEAD_DIM:GROUP_WIDTH + (h + 1) * HEAD_DIM, :].astype(BF16)
            kt = tail[h * HEAD_DIM:(h + 1) * HEAD_DIM, :].astype(BF16)
            vt = tail[GROUP_WIDTH + h * HEAD_DIM:GROUP_WIDTH + (h + 1) * HEAD_DIM, :].astype(BF16)
            sc = jnp.dot(qh, kh, preferred_element_type=F32) + bc_ref[h]
            sn = jnp.dot(qh, kt, preferred_element_type=F32) + bn_ref[h]
            m = jnp.maximum(jnp.max(sc, axis=-1, keepdims=True), jnp.max(sn, axis=-1, keepdims=True))
            pc = jnp.exp(sc - m)
            pn = jnp.exp(sn - m)
            l = jnp.sum(pc, axis=-1, keepdims=True) + jnp.sum(pn, axis=-1, keepdims=True)
            nt_dims = (((1,), (1,)), ((), ()))
            o = (lax.dot_general(pc.astype(BF16), vh, nt_dims, preferred_element_type=F32)
                 + lax.dot_general(pn.astype(BF16), vt, nt_dims, preferred_element_type=F32)) * (1.0 / l)
            o_ref[0, :, h * HEAD_DIM:(h + 1) * HEAD_DIM] = o
            l_ref[0, :, h * HEAD_DIM:(h + 1) * HEAD_DIM] = jnp.broadcast_to(m + jnp.log(l), o.shape)


def _k2_sample(q_b, caches, tails, biases_c, biases_n, n_steps):
    bd, q_rows, _ = q_b.shape
    per_b = lambda shape: pl.BlockSpec((1,) + shape[1:], lambda b: (b,) + (0,) * (len(shape) - 1))
    ins = [q_b] + list(caches) + list(tails) + list(biases_c) + list(biases_n)
    in_specs = ([per_b(q_b.shape)] + [per_b(c.shape) for c in caches] + [per_b(t.shape) for t in tails]
                + [_const_spec(b.shape) for b in biases_c] + [_const_spec(b.shape) for b in biases_n])
    o_shape = jax.ShapeDtypeStruct((bd, q_rows, GROUP_WIDTH), F32)
    out_shape = [jax.ShapeDtypeStruct(c.shape, F32) for c in caches] + [o_shape] * 6
    out_specs = [per_b(c.shape) for c in caches] + [per_b(o_shape.shape)] * 6
    return pl.pallas_call(
        functools.partial(_k2s_kernel, n_steps),
        grid=(bd,),
        in_specs=in_specs,
        out_specs=out_specs,
        out_shape=out_shape,
        compiler_params=_params(("arbitrary",)),
        name="attn_sample",
    )(*ins)


def _k3_kernel(x_ref, o0, o1, o2, l0, l1, l2, ga_ref, sgb_ref, g1_ref, sh2_ref, sc2_ref, gf_ref,
               wao_ref, wo_ref, wr_ref, br_ref, cnt_in_ref,
               x2_ref, h2_ref, eid_ref, wts_ref, rank_ref, cnt_ref, carry, *stage_refs):
    @pl.when(pl.program_id(0) == 0)
    def _():
        carry[...] = cnt_in_ref[...]

    dils = tuple(x_ref.shape[0] // r.shape[-2] for r in (o0, o1, o2))
    pairs = [stage_refs[j:j + STAGE_HALVES] for j in range(0, len(stage_refs), STAGE_HALVES)]
    os_ = [_from_streams(r, d, pairs[2 * g]) for g, (r, d) in enumerate(zip((o0, o1, o2), dils))]
    ls = [_from_streams(r, d, pairs[2 * g + 1]) for g, (r, d) in enumerate(zip((l0, l1, l2), dils))]
    m = jnp.maximum(jnp.maximum(ls[0], ls[1]), ls[2])
    ws = [jnp.exp(l - m) for l in ls]
    den = ws[0] + ws[1] + ws[2]
    o = (ws[0] * os_[0] + ws[1] * os_[1] + ws[2] * os_[2]) / den
    b = jnp.dot(o.astype(BF16), wao_ref[...], preferred_element_type=F32)
    mixed = ga_ref[...].astype(F32) + sgb_ref[...].astype(F32) * b
    x2 = x_ref[...] + g1_ref[0] * jnp.dot(mixed.astype(BF16), wo_ref[...], preferred_element_type=F32)
    x2_ref[...] = x2
    h2 = _rms_modulate(x2, gf_ref[...], sc2_ref[0], sh2_ref[0])
    h2_ref[...] = h2
    lt = lax.dot_general(wr_ref[...], h2, (((1,), (1,)), ((), ())), preferred_element_type=F32,
                         precision=lax.Precision.HIGHEST) + br_ref[...]
    tm = h2.shape[0]
    gl = lt[0:8, :]
    gmax = jnp.max(gl, axis=0, keepdims=True)
    r8 = lax.broadcasted_iota(jnp.int32, (8, tm), 0)
    grp = jnp.min(jnp.where(gl == gmax, r8, 8), axis=0, keepdims=True)
    p_grp = 1.0 / jnp.sum(jnp.exp(gl - gmax), axis=0, keepdims=True)
    es = jnp.zeros((EXPERTS_PER_GROUP, tm), F32)
    for g in range(N_EXPERT_GROUPS):
        es = jnp.where(grp == g, lt[8 + 8 * g:16 + 8 * g, :], es)
    v1 = jnp.max(es, axis=0, keepdims=True)
    i1 = jnp.min(jnp.where(es == v1, r8, 8), axis=0, keepdims=True)
    rest = jnp.where(r8 == i1, -jnp.inf, es)
    v2 = jnp.max(rest, axis=0, keepdims=True)
    i2 = jnp.min(jnp.where(rest == v2, r8, 8), axis=0, keepdims=True)
    e21 = jnp.exp(v2 - v1)
    w1 = p_grp / (1.0 + e21)
    e1 = grp * EXPERTS_PER_GROUP + i1
    e2 = grp * EXPERTS_PER_GROUP + i2
    eid_ref[0, 0:1, :] = e1
    eid_ref[0, 1:2, :] = e2
    wts_ref[0, 0:1, :] = w1
    wts_ref[0, 1:2, :] = w1 * e21
    r_e = lax.broadcasted_iota(jnp.int32, (N_EXPERTS, tm), 0)
    hit1 = r_e == e1
    hit2 = r_e == e2
    both = jnp.where(hit1 | hit2, 1.0, 0.0)
    s_idx = lax.broadcasted_iota(jnp.int32, (tm, tm), 0)
    t_idx = lax.broadcasted_iota(jnp.int32, (tm, tm), 1)
    before = jnp.where(s_idx < t_idx, 1.0, 0.0).astype(BF16)
    base = carry[...] + jnp.dot(both.astype(BF16), before, preferred_element_type=F32)
    rank_ref[0, 0:1, :] = jnp.sum(jnp.where(hit1, base, 0.0), axis=0, keepdims=True).astype(jnp.int32)
    rank_ref[0, 1:2, :] = jnp.sum(jnp.where(hit2, base, 0.0), axis=0, keepdims=True).astype(jnp.int32)
    total = carry[...] + jnp.sum(both, axis=1, keepdims=True)
    carry[...] = total
    cnt_ref[...] = total


def _k3(x, o_l, g1, sh2, sc2, ga, sgb, norm_ffn_g, w_ao_b, w_o_b, w_r, b_r, cnt_in, tiles_per_mod):
    T = x.shape[0]
    tm = min(TOKEN_TILE, T)
    nt = T // tm
    tok = lambda w: pl.BlockSpec((tm, w), lambda t: (t, 0))
    mod = pl.BlockSpec((1,) + g1.shape[1:], lambda t: (t // tiles_per_mod, 0, 0))
    small = pl.BlockSpec((1, 2, tm), lambda t: (t, 0, 0))
    small_i = jax.ShapeDtypeStruct((nt, 2, tm), jnp.int32)

    def attn_spec(a):
        if a.ndim == 2:
            return tok(GROUP_WIDTH)
        dil = a.shape[2] // GROUP_WIDTH
        return pl.BlockSpec((1, tm // dil, a.shape[2]), lambda t: (t // tiles_per_mod, t % tiles_per_mod, 0))

    attn_in = [o_l[0][0], o_l[1][0], o_l[2][0], o_l[0][1], o_l[1][1], o_l[2][1]]
    return pl.pallas_call(
        _k3_kernel,
        grid=(nt,),
        in_specs=[tok(D_MODEL)] + [attn_spec(a) for a in attn_in] + [tok(D_MODEL)] * 2 + [mod] * 3
                 + [_const_spec((1, D_MODEL)), _weight_spec(w_ao_b.shape), _weight_spec(w_o_b.shape),
                    _const_spec(w_r.shape), _const_spec(b_r.shape), _const_spec((N_EXPERTS, 1))],
        out_specs=[tok(D_MODEL), tok(D_MODEL), small, small, small, _const_spec((N_EXPERTS, 1))],
        out_shape=[jax.ShapeDtypeStruct((T, D_MODEL), F32), jax.ShapeDtypeStruct((T, D_MODEL), F32),
                   small_i, jax.ShapeDtypeStruct((nt, 2, tm), F32), small_i,
                   jax.ShapeDtypeStruct((N_EXPERTS, 1), F32)],
        scratch_shapes=([pltpu.VMEM((N_EXPERTS, 1), F32)]
                        + [pltpu.VMEM((tm, LANES), F32)] * (2 * N_GROUPS * STAGE_HALVES)),
        compiler_params=_params(("arbitrary",)),
        name="merge_router",
    )(x, *attn_in, ga, sgb, g1, sh2, sc2, norm_ffn_g, w_ao_b, w_o_b, w_r, b_r, cnt_in)


def _rows_wait(n_rows, hbm, vmem, sem):
    pltpu.make_async_copy(hbm.at[pl.ds(0, n_rows)], vmem, sem).wait()


def _dispatch_kernel(n_tiles, n_first, dest_ref, fill_ref, ha_ref, hb_ref, xs_ref, buf, sem, zbuf, zsem):
    i = pl.program_id(0)
    slot = i % 2 if n_tiles > 1 else 0
    tm = ha_ref.shape[0]

    def drain(s):
        for _ in range(2):
            _rows_wait(tm, xs_ref, buf.at[s], sem.at[s])

    @pl.when(i == 0)
    def _():
        zbuf[...] = jnp.zeros(zbuf.shape, F32)

        def fill(row):
            return pltpu.make_async_copy(zbuf, xs_ref.at[pl.ds(pl.multiple_of(row, MOE_ROWS), MOE_ROWS)], zsem)

        n_blocks = xs_ref.shape[0] // MOE_ROWS
        n_used = fill_ref[N_EXPERTS]

        def start_block(j, carry):
            fill(j * MOE_ROWS).start()
            return carry

        def wait_block(j, carry):
            fill(j * MOE_ROWS).wait()
            return carry

        for e in range(N_EXPERTS):
            @pl.when(fill_ref[e] >= 0)
            def _(e=e):
                fill(fill_ref[e]).start()
        lax.fori_loop(n_used, n_blocks, start_block, 0)
        for e in range(N_EXPERTS):
            @pl.when(fill_ref[e] >= 0)
            def _(e=e):
                fill(fill_ref[e]).wait()
        lax.fori_loop(n_used, n_blocks, wait_block, 0)

    if n_tiles > 2:
        @pl.when(i >= 2)
        def _():
            drain(slot)

    @pl.when(i < n_first)
    def _():
        buf[slot] = ha_ref[...]

    @pl.when(i >= n_first)
    def _():
        buf[slot] = hb_ref[...]

    for r in range(tm):
        for k in range(2):
            pltpu.make_async_copy(buf.at[slot, pl.ds(r, 1)], xs_ref.at[pl.ds(dest_ref[0, k, r], 1)], sem.at[slot]).start()

    if n_tiles == 1:
        drain(0)
    else:
        @pl.when(i == n_tiles - 1)
        def _():
            drain(slot)
            drain(1 - slot)


def _dispatch(dest, fill_rows, h_a, h_b, n_slots):
    nt, _, tm = dest.shape
    n_first = h_a.shape[0] // tm
    assert h_a.shape[0] % tm == 0 and h_b.shape[0] == (nt - n_first) * tm
    return pl.pallas_call(
        functools.partial(_dispatch_kernel, nt, n_first),
        grid=(nt,),
        in_specs=[pl.BlockSpec((1, 2, tm), lambda t: (t, 0, 0), memory_space=pltpu.SMEM),
                  pl.BlockSpec(memory_space=pltpu.SMEM),
                  pl.BlockSpec((tm, D_MODEL), lambda t: (jnp.minimum(t, n_first - 1), 0)),
                  pl.BlockSpec((tm, D_MODEL), lambda t: (jnp.maximum(t - n_first, 0), 0))],
        out_specs=pl.BlockSpec(memory_space=pl.ANY),
        out_shape=jax.ShapeDtypeStruct((n_slots, D_MODEL), F32),
        scratch_shapes=[pltpu.VMEM((2, tm, D_MODEL), F32), pltpu.SemaphoreType.DMA((2,)),
                        pltpu.VMEM((MOE_ROWS, D_MODEL), F32), pltpu.SemaphoreType.DMA(())],
        compiler_params=_params(("arbitrary",)),
        name="dispatch_rows",
    )(dest, fill_rows, h_a, h_b)


def _k4_kernel(be_ref, nused_ref, x_ref, wg_ref, wu_ref, wd_ref, y_ref, wg_b, wu_b, wd_b):
    i = pl.program_id(0)

    @pl.when((i == 0) | (be_ref[i] != be_ref[jnp.maximum(i - 1, 0)]))
    def _():
        wg_b[...] = wg_ref[0].astype(BF16)
        wu_b[...] = wu_ref[0].astype(BF16)
        wd_b[...] = wd_ref[0].astype(BF16)

    @pl.when(i < nused_ref[0])
    def _():
        xb = x_ref[...].astype(BF16)
        gate = jnp.dot(xb, wg_b[...], preferred_element_type=F32)
        up = jnp.dot(xb, wu_b[...], preferred_element_type=F32)
        mid = (_silu(gate) * up).astype(BF16)
        y_ref[...] = jnp.dot(mid, wd_b[...], preferred_element_type=F32)

    @pl.when(i >= nused_ref[0])
    def _():
        y_ref[...] = jnp.zeros(y_ref.shape, F32)


def _k4(block_expert, n_used, xs, w_gate, w_up, w_down):
    n_blocks = block_expert.shape[0]
    rows = MOE_ROWS
    grid_spec = pltpu.PrefetchScalarGridSpec(
        num_scalar_prefetch=2,
        grid=(n_blocks,),
        in_specs=[pl.BlockSpec((rows, D_MODEL), lambda i, be, nu: (jnp.minimum(i, nu[0] - 1), 0)),
                  pl.BlockSpec((1, D_MODEL, D_FF_EXPERT), lambda i, be, nu: (be[i], 0, 0)),
                  pl.BlockSpec((1, D_MODEL, D_FF_EXPERT), lambda i, be, nu: (be[i], 0, 0)),
                  pl.BlockSpec((1, D_FF_EXPERT, D_MODEL), lambda i, be, nu: (be[i], 0, 0))],
        out_specs=pl.BlockSpec((rows, D_MODEL), lambda i, be, nu: (i, 0)),
        scratch_shapes=[pltpu.VMEM((D_MODEL, D_FF_EXPERT), BF16), pltpu.VMEM((D_MODEL, D_FF_EXPERT), BF16),
                        pltpu.VMEM((D_FF_EXPERT, D_MODEL), BF16)],
    )
    return pl.pallas_call(
        _k4_kernel,
        grid_spec=grid_spec,
        out_shape=jax.ShapeDtypeStruct((n_blocks * rows, D_MODEL), F32),
        compiler_params=_params(("arbitrary",)),
        name="expert_blocks",
    )(block_expert, n_used, xs, w_gate, w_up, w_down)


def _k5_kernel(n_tiles, idx_ref, x2_ref, w_ref, g2_ref, gfin_ref, ys_hbm, y_ref, ybuf, sem):
    i = pl.program_id(0)
    tm = x2_ref.shape[0]

    @pl.when(i < n_tiles)
    def _():
        slot = i % 2
        for r in range(tm):
            for k in range(2):
                pltpu.make_async_copy(ys_hbm.at[pl.ds(idx_ref[0, k, r], 1)], ybuf.at[slot, k, pl.ds(r, 1)],
                                      sem.at[slot]).start()

    @pl.when(i >= 1)
    def _():
        slot = (i - 1) % 2
        for k in range(2):
            _rows_wait(tm, ys_hbm, ybuf.at[slot, k], sem.at[slot])
        w = w_ref[...]
        f = ybuf[slot, 0] * w[:, 0:1] + ybuf[slot, 1] * w[:, 1:2]
        y = x2_ref[...] + g2_ref[0] * f
        r = lax.rsqrt(jnp.mean(y * y, axis=-1, keepdims=True) + EPS)
        y_ref[...] = (y * r) * gfin_ref[...]


def _k5(dest, x2, wts, g2, norm_final_g, ys, tiles_per_mod):
    nt, _, tm = dest.shape
    T = x2.shape[0]
    prev = lambda t: jnp.maximum(t - 1, 0)
    return pl.pallas_call(
        functools.partial(_k5_kernel, nt),
        grid=(nt + 1,),
        in_specs=[pl.BlockSpec((1, 2, tm), lambda t: (jnp.minimum(t, nt - 1), 0, 0), memory_space=pltpu.SMEM),
                  pl.BlockSpec((tm, D_MODEL), lambda t: (prev(t), 0)),
                  pl.BlockSpec((tm, 2), lambda t: (prev(t), 0)),
                  pl.BlockSpec((1,) + g2.shape[1:], lambda t: (prev(t) // tiles_per_mod, 0, 0)),
                  _const_spec((1, D_MODEL)), pl.BlockSpec(memory_space=pl.ANY)],
        out_specs=pl.BlockSpec((tm, D_MODEL), lambda t: (prev(t), 0)),
        out_shape=jax.ShapeDtypeStruct((T, D_MODEL), F32),
        scratch_shapes=[pltpu.VMEM((2, 2, tm, D_MODEL), F32), pltpu.SemaphoreType.DMA((2,))],
        compiler_params=_params(("arbitrary",)),
        name="combine_norm",
    )(dest, x2, wts, g2, norm_final_g, ys)


def _slot_tables(counts, n_blocks):
    padded = (counts + MOE_ROWS - 1) // MOE_ROWS * MOE_ROWS
    pend = jnp.cumsum(padded)
    pstart = pend - padded
    block_lo = jnp.arange(n_blocks, dtype=jnp.int32) * MOE_ROWS
    block_expert = jnp.minimum(jnp.sum(pend[None, :] <= block_lo[:, None], axis=1), N_EXPERTS - 1).astype(jnp.int32)
    n_used = (pend[-1] // MOE_ROWS).astype(jnp.int32).reshape(1)
    fill_rows = jnp.concatenate([jnp.where(padded > 0, pend - MOE_ROWS, -1).astype(jnp.int32), n_used])
    return pstart.astype(jnp.int32), block_expert, n_used, fill_rows


def _slots(eid, rank, pstart):
    sel = eid[..., None] == jnp.arange(N_EXPERTS, dtype=jnp.int32)
    return jnp.sum(jnp.where(sel, pstart, 0), axis=-1).astype(jnp.int32) + rank


def _cache_view(cache):
    bd, lc = cache.shape[:2]
    return cache.transpose(0, 2, 3, 4, 1).reshape(bd, 2 * GROUP_WIDTH, lc)


def _cache_unview(view):
    bd, _, lc = view.shape
    return view.reshape(bd, 2, HEADS_PER_GROUP, HEAD_DIM, lc).transpose(0, 4, 1, 2, 3)


def kernel(x_prompt, x_sample, c_prompt, c_sample, cache_kv_w128, cache_kv_w512, cache_kv_w2048, state_conv, rel_bias, norm_mix_g, norm_ffn_g, w_mod, b_mod, w_in, dw_w, dw_b, ln_conv_g, ln_conv_b, w_conv_out, w_attn_out, w_out, w_router_group, b_router_group, w_router_expert, b_router_expert, w_exp_gate, w_exp_up, w_exp_down, norm_final_g):
    assert norm_mix_g.shape[0] == 1, "single layer"
    B, S, D = x_prompt.shape
    Bd, Td, _ = x_sample.shape
    caches_in = (cache_kv_w128[0], cache_kv_w512[0], cache_kv_w2048[0])
    for (win, dil), c in zip(DILATED_GROUPS, caches_in):
        assert c.shape[1] >= (win // dil) * dil and c.shape[1] % 128 == 0 and S % (dil * ATTN_BLOCK) == 0

    wi = w_in[0]
    q_cols = wi[:, :ATTN_WIDTH] * (HEAD_DIM ** -0.5)
    kv_cols = []
    for g in range(N_GROUPS):
        kv_cols += [wi[:, ATTN_WIDTH + g * GROUP_WIDTH:ATTN_WIDTH + (g + 1) * GROUP_WIDTH],
                    wi[:, 2 * ATTN_WIDTH + g * GROUP_WIDTH:2 * ATTN_WIDTH + (g + 1) * GROUP_WIDTH]]
    w_in_b = jnp.concatenate([q_cols] + kv_cols + [wi[:, 3 * ATTN_WIDTH:]], axis=1).astype(BF16)
    w_co_b = w_conv_out[0].astype(BF16)
    w_ao_b = w_attn_out[0].astype(BF16)
    w_o_b = w_out[0].astype(BF16)
    w_r = jnp.zeros((ROUTER_ROWS, D), F32)
    w_r = w_r.at[0:N_EXPERT_GROUPS].set(w_router_group[0].T)
    w_r = w_r.at[8:].set(w_router_expert[0].reshape(D, N_EXPERTS).T)
    b_r = jnp.full((ROUTER_ROWS, 1), NEG_INF, F32)
    b_r = b_r.at[0:N_EXPERT_GROUPS, 0].set(b_router_group[0])
    b_r = b_r.at[8:, 0].set(b_router_expert[0].reshape(N_EXPERTS))
    row = lambda v: v.reshape(1, -1)

    n_seq = B + Bd
    n_seq_pad = -(-n_seq // 16) * 16
    c_all = jnp.concatenate([c_prompt, c_sample, jnp.zeros((n_seq_pad - n_seq, D), F32)], axis=0)
    mod = _modulation(c_all, w_mod[0], row(b_mod[0]))
    mod_p = [mod[:B, j * D:(j + 1) * D].reshape(B, 1, D) for j in range(6)]
    mod_s = [jnp.tile(mod[B:B + Bd, j * D:(j + 1) * D], (Td, 1)) for j in range(6)]

    (q0, q1, q2, k0, v0, k1, v1, k2, v2, kvt0, kvt1, kvt2, ga, sgb, convp) = _k1_prompt(
        x_prompt, mod_p[0], mod_p[1], row(norm_mix_g[0]), w_in_b, dw_w[0], row(dw_b[0]),
        row(ln_conv_g[0]), row(ln_conv_b[0]), w_co_b)
    o_l = []
    for g, ((win, dil), qg, kg, vg) in enumerate(zip(DILATED_GROUPS, (q0, q1, q2), (k0, k1, k2), (v0, v1, v2))):
        bias = _prompt_bias(rel_bias[:, g * HEADS_PER_GROUP:(g + 1) * HEADS_PER_GROUP], dil, win // dil)
        o_l.append(_k2_prompt(qg, kg, vg, bias, dil))
    xp = x_prompt.reshape(B * S, D)
    tm3 = min(TOKEN_TILE, B * S)
    x2, h2, eid, wts, rank, cnt_p = _k3(xp, o_l, mod_p[2], mod_p[3], mod_p[4], ga, sgb, row(norm_ffn_g[0]),
                                        w_ao_b, w_o_b, w_r, b_r, jnp.zeros((N_EXPERTS, 1), F32), S // tm3)
    kv_p =[kvt.reshape(1, B, kvt.shape[1], 2, HEADS_PER_GROUP, HEAD_DIM) for kvt in (kvt0, kvt1, kvt2)]
    conv_p = convp[:, CONV_HIST - (CONV_WIDTH - 1):, :][None]

    Ts = Td * Bd
    xs = x_sample.transpose(1, 0, 2).reshape(Ts, D)
    state_tm = state_conv[0].transpose(1, 0, 2)
    (sq0, sq1, sq2, sk0, sv0, sk1, sv1, sk2, sv2, ga_s, sgb_s, conv_tm) = _k1_sample(
        xs, mod_s[0], mod_s[1], row(norm_mix_g[0]), w_in_b, state_tm, dw_w[0], row(dw_b[0]),
        row(ln_conv_g[0]), row(ln_conv_b[0]), w_co_b, Td)
    q_rows = 16
    q_b = jnp.concatenate([sq0, sq1, sq2], axis=1).reshape(Td, Bd, ATTN_WIDTH).transpose(1, 0, 2)
    q_b = jnp.pad(q_b, ((0, 0), (0, q_rows - Td), (0, 0)))
    views, tails, biases_c, biases_n = [], [], [], []
    for g, ((win, dil), c, sk, sv) in enumerate(zip(DILATED_GROUPS, caches_in, (sk0, sk1, sk2), (sv0, sv1, sv2))):
        lc = c.shape[1]
        views.append(_cache_view(c))
        new_kv = jnp.concatenate([sk, sv], axis=1).reshape(Td, Bd, 2 * GROUP_WIDTH).transpose(1, 2, 0)
        tails.append(jnp.pad(new_kv, ((0, 0), (0, 0), (128 - Td, 0))))
        bc, bn = _sample_bias(rel_bias[:, g * HEADS_PER_GROUP:(g + 1) * HEADS_PER_GROUP], dil, win // dil, lc, Td, q_rows)
        biases_c.append(bc)
        biases_n.append(bn)
    (nc0, nc1, nc2, so0, so1, so2, sl0, sl1, sl2) = _k2_sample(q_b, views, tails, biases_c, biases_n, Td)
    to_tm = lambda a: a[:, :Td, :].transpose(1, 0, 2).reshape(Ts, GROUP_WIDTH)
    o_l_s = [(to_tm(so0), to_tm(sl0)), (to_tm(so1), to_tm(sl1)), (to_tm(so2), to_tm(sl2))]
    tm3s = min(TOKEN_TILE, Ts)
    tiled = lambda m: m.reshape(Ts // tm3s, tm3s, D)
    x2s, h2s, eid_s, wts_s, rank_s, cnt_all = _k3(xs, o_l_s, tiled(mod_s[2]), tiled(mod_s[3]), tiled(mod_s[4]),
                                                  ga_s, sgb_s, row(norm_ffn_g[0]), w_ao_b, w_o_b, w_r, b_r, cnt_p, 1)

    n_assign = 2 * (B * S + Ts)
    n_blocks = -(-n_assign // MOE_ROWS) + N_EXPERTS
    pstart, block_expert, n_used, fill_rows = _slot_tables(cnt_all[:, 0].astype(jnp.int32), n_blocks)
    dest_p = _slots(eid, rank, pstart)
    dest_s = _slots(eid_s, rank_s, pstart)
    n_slots = n_blocks * MOE_ROWS
    assert tm3s == tm3, "prompt and sample token tiles must match to share the dispatch"
    slots = _dispatch(jnp.concatenate([dest_p, dest_s], axis=0), fill_rows, h2, h2s, n_slots)
    ys = _k4(block_expert, n_used, slots, w_exp_gate[0], w_exp_up[0], w_exp_down[0])
    per_token = lambda a: a.transpose(0, 2, 1).reshape(-1, 2)
    y_prompt = _k5(dest_p, x2, per_token(wts), mod_p[5], row(norm_final_g), ys, S // tm3).reshape(B, S, D)
    y_s = _k5(dest_s, x2s, per_token(wts_s), tiled(mod_s[5]), row(norm_final_g), ys, 1)
    y_sample = y_s.reshape(Td, Bd, D).transpose(1, 0, 2)
    kv_s = [_cache_unview(nc)[None] for nc in (nc0, nc1, nc2)]
    conv_s = conv_tm.transpose(1, 0, 2)[None]

    return (y_prompt, y_sample, kv_p[0], kv_p[1], kv_p[2], conv_p,
            kv_s[0], kv_s[1], kv_s[2], conv_s)
```

```python
import functools
import math

import jax
import jax.numpy as jnp
from jax import lax
from jax.experimental import pallas as pl
from jax.experimental.pallas import tpu as pltpu

F32 = jnp.float32
BF16 = jnp.bfloat16

D_MODEL = 1024
HEAD_DIM = 64
HEADS_PER_GROUP = 4
GROUP_WIDTH = HEADS_PER_GROUP * HEAD_DIM
LANES = 128
SUBLANES = 8
CONV_CHUNK = 32
STAGE_HALVES = GROUP_WIDTH // LANES
DILATED_GROUPS = ((128, 1), (512, 4), (2048, 16))
N_GROUPS = len(DILATED_GROUPS)
ATTN_WIDTH = N_GROUPS * GROUP_WIDTH
CONV_CH = D_MODEL // 2
CONV_WIDTH = 31
CONV_HIST = 32
N_BUCKETS = 32
MAX_DISTANCE = 2048
N_EXPERT_GROUPS = 4
EXPERTS_PER_GROUP = 8
N_EXPERTS = N_EXPERT_GROUPS * EXPERTS_PER_GROUP
D_FF_EXPERT = D_MODEL // 2
EPS = 1e-6
NEG_INF = -1e30

COL_KV = ATTN_WIDTH
COL_ULIN = COL_KV + 2 * ATTN_WIDTH
COL_UGATE = COL_ULIN + CONV_CH
COL_GA = COL_UGATE + CONV_CH
COL_GB = COL_GA + D_MODEL
IN_COLS = COL_GB + D_MODEL

ROUTER_ROWS = 8 + N_EXPERTS

V7X_VMEM_LIMIT = 56 * 1024 * 1024
TOKEN_TILE = 512
ATTN_ROWS = 512
ATTN_BLOCK = 128
MOE_ROWS = 512


def _sigmoid(x):
    return 1.0 / (1.0 + jnp.exp(-x))


def _silu(x):
    return x * _sigmoid(x)


def _rms_modulate(x, g, sc, sh):
    r = lax.rsqrt(jnp.mean(x * x, axis=-1, keepdims=True) + EPS)
    return ((x * r) * g) * (1.0 + sc) + sh


def _conv_tail(y, g, b):
    mu = jnp.mean(y, axis=-1, keepdims=True)
    yc = y - mu
    var = jnp.mean(yc * yc, axis=-1, keepdims=True)
    return _silu((yc * lax.rsqrt(var + EPS)) * g + b)


def _params(semantics):
    return pltpu.CompilerParams(dimension_semantics=semantics, vmem_limit_bytes=V7X_VMEM_LIMIT)


def _const_spec(shape):
    nd = len(shape)
    return pl.BlockSpec(shape, lambda *_: (0,) * nd)


def _weight_spec(shape):
    nd = len(shape)
    return pl.BlockSpec(shape, lambda *_: (0,) * nd, pipeline_mode=pl.Buffered(1))


def _mod_kernel(c_ref, w_ref, b_ref, o_ref):
    s = _silu(c_ref[...]).astype(BF16)
    o_ref[...] = jnp.dot(s, w_ref[...].astype(BF16), preferred_element_type=F32) + b_ref[...]


def _modulation(c_all, w_mod, b_mod):
    rows = c_all.shape[0]
    n_out = w_mod.shape[1]
    chunk = D_MODEL
    return pl.pallas_call(
        _mod_kernel,
        grid=(n_out // chunk,),
        in_specs=[_const_spec((rows, D_MODEL)),
                  pl.BlockSpec((D_MODEL, chunk), lambda j: (0, j)),
                  pl.BlockSpec((1, chunk), lambda j: (0, j))],
        out_specs=pl.BlockSpec((rows, chunk), lambda j: (0, j)),
        out_shape=jax.ShapeDtypeStruct((rows, n_out), F32),
        compiler_params=_params(("arbitrary",)),
        name="modulation",
    )(c_all, w_mod, b_mod)


def _to_streams(ref, val, dil, stage):
    if dil == 1:
        ref[...] = val.astype(ref.dtype).reshape(ref.shape)
        return
    n = val.shape[0] // dil
    for c, half in enumerate(stage):
        half[...] = val[:, c * LANES:(c + 1) * LANES]
        for r in range(dil):
            lo = r * GROUP_WIDTH + c * LANES
            ref[0, :, lo:lo + LANES] = half[pl.ds(r, n, stride=dil), :].astype(ref.dtype)


def _from_streams(ref, dil, stage):
    if dil == 1:
        return ref[...].astype(F32).reshape(ref.shape[-2:])
    n = ref.shape[1]
    for c, half in enumerate(stage):
        for r in range(dil):
            lo = r * GROUP_WIDTH + c * LANES
            half[pl.ds(r, n, stride=dil), :] = ref[0, :, lo:lo + LANES].astype(F32)
    return jnp.concatenate([half[...] for half in stage], axis=1)


def _project_common(hb, win_ref, outs, dils, stages):
    (q_refs, k_refs, v_refs) = outs

    def proj(lo, width):
        return jnp.dot(hb, win_ref[:, lo:lo + width], preferred_element_type=F32)

    zq = proj(0, ATTN_WIDTH)
    for g in range(N_GROUPS):
        _to_streams(q_refs[g], zq[:, g * GROUP_WIDTH:(g + 1) * GROUP_WIDTH], dils[g], stages[g][0])
    zkvs = []
    for g in range(N_GROUPS):
        zkv = proj(COL_KV + 2 * GROUP_WIDTH * g, 2 * GROUP_WIDTH)
        _to_streams(k_refs[g], zkv[:, :GROUP_WIDTH], dils[g], stages[g][1])
        _to_streams(v_refs[g], zkv[:, GROUP_WIDTH:], dils[g], stages[g][2])
        zkvs.append(zkv)
    u = proj(COL_ULIN, CONV_CH) * _sigmoid(proj(COL_UGATE, CONV_CH))
    return zkvs, u, proj


def _k1_kernel(tail_rows, n_tiles,
               x_ref, sh_ref, sc_ref, g_ref, win_ref, dww_ref, dwb_ref, lng_ref, lnb_ref, wco_ref,
               q0, q1, q2, k0, v0, k1, v1, k2, v2, kvt0, kvt1, kvt2, ga_ref, sgb_ref, convp_ref, uext, sbuf, ushift, sga, *stage_refs):
    i = pl.program_id(1)
    tm = x_ref.shape[1]

    @pl.when(i == 0)
    def _():
        uext[0:CONV_HIST, :] = jnp.zeros((CONV_HIST, CONV_CH), F32)

    h = _rms_modulate(x_ref[0], g_ref[...], sc_ref[0], sh_ref[0])
    hb = h.astype(BF16)

    def proj(lo, width):
        return jnp.dot(hb, win_ref[:, lo:lo + width], preferred_element_type=F32)

    uext[CONV_HIST:CONV_HIST + tm, :] = proj(COL_ULIN, CONV_CH) * _sigmoid(proj(COL_UGATE, CONV_CH))
    base = CONV_HIST - (CONV_WIDTH - 1)
    span = tm + CONV_HIST - SUBLANES
    for b in range(1, SUBLANES):
        ushift[b - 1, 0:span, :] = uext[b:b + span, :]

    def conv_chunk(r0):
        accs = [dwb_ref[...]] * (CONV_CHUNK // SUBLANES)
        for k in range(CONV_WIDTH):
            b = (base + k) % SUBLANES
            lo = r0 + base + k - b
            w8 = dww_ref[k]
            for q in range(len(accs)):
                rows = slice(lo + q * SUBLANES, lo + (q + 1) * SUBLANES)
                src = uext[rows, :] if b == 0 else ushift[b - 1, rows, :]
                accs[q] = accs[q] + w8 * src
        s = _conv_tail(jnp.concatenate(accs, axis=0), lng_ref[...], lnb_ref[...])
        sbuf[r0:r0 + CONV_CHUNK, :] = s.astype(BF16)

    dils = tuple(dil for _, dil in DILATED_GROUPS)
    pairs = [stage_refs[j:j + STAGE_HALVES] for j in range(0, len(stage_refs), STAGE_HALVES)]
    stages = [(None,) * 3] + [pairs[3 * (g - 1):3 * g] for g in range(1, N_GROUPS)]
    pieces = []
    for g, q_ref in enumerate((q0, q1, q2)):
        def q_piece(g=g, q_ref=q_ref):
            _to_streams(q_ref, proj(g * GROUP_WIDTH, GROUP_WIDTH), dils[g], stages[g][0])
        pieces.append(q_piece)
    for g, (k_ref, v_ref, kvt) in enumerate(((k0, v0, kvt0), (k1, v1, kvt1), (k2, v2, kvt2))):
        for half, ref in enumerate((k_ref, v_ref)):
            def kv_piece(g=g, half=half, ref=ref, kvt=kvt):
                z = proj(COL_KV + (2 * g + half) * GROUP_WIDTH, GROUP_WIDTH)
                _to_streams(ref, z, dils[g], stages[g][1 + half])
                tr = kvt.shape[1]

                @pl.when(i >= n_tiles - tail_rows[g] // tr)
                def _():
                    kvt[0, :, half * GROUP_WIDTH:(half + 1) * GROUP_WIDTH] = z[tm - tr:, :]
            pieces.append(kv_piece)
    for c in range(D_MODEL // GROUP_WIDTH):
        cols = slice(c * GROUP_WIDTH, (c + 1) * GROUP_WIDTH)

        def gb_piece(c=c, cols=cols):
            sgb_ref[:, cols] = _sigmoid(proj(COL_GB + c * GROUP_WIDTH, GROUP_WIDTH)).astype(BF16)

        def ga_piece(c=c, cols=cols):
            sga[:, cols] = _sigmoid(proj(COL_GA + c * GROUP_WIDTH, GROUP_WIDTH))
        pieces += [gb_piece, ga_piece]
    n_qkv = 3 * N_GROUPS
    for piece in pieces[:n_qkv]:
        piece()
    for r0 in range(0, tm, CONV_CHUNK):
        conv_chunk(r0)
    for piece in pieces[n_qkv:]:
        piece()
    last = uext[tm:tm + CONV_HIST, :]
    uext[0:CONV_HIST, :] = last
    convp_ref[0] = last
    a = jnp.dot(sbuf[...], wco_ref[...], preferred_element_type=F32)
    ga_ref[...] = (sga[...] * a).astype(BF16)


def _k1_prompt(x, sh1, sc1, norm_g, w_in_b, dw_w, dw_b, ln_g, ln_b, w_co_b):
    B, S, _ = x.shape
    tm = min(TOKEN_TILE, S)
    nt = S // tm
    tail_rows = tuple(min(win, S) for win, _ in DILATED_GROUPS)
    tail_blk = tuple(min(t, tm) for t in tail_rows)

    def tok_spec(width):
        return pl.BlockSpec((tm, width), lambda b, i: (b * nt + i, 0))

    def tail_spec(g):
        first = nt - tail_rows[g] // tail_blk[g]
        return pl.BlockSpec((1, tail_blk[g], 2 * GROUP_WIDTH), lambda b, i: (b, jnp.maximum(i - first, 0), 0))

    def stream_shape(dil):
        return jax.ShapeDtypeStruct((B, S // dil, dil * GROUP_WIDTH), BF16)

    def stream_spec(dil):
        return pl.BlockSpec((1, tm // dil, dil * GROUP_WIDTH), lambda b, i: (b, i, 0))

    dils = [dil for _, dil in DILATED_GROUPS]
    qkv_order = [dils[0], dils[1], dils[2]] + [d for d in dils for _ in range(2)]
    mod_spec = pl.BlockSpec((1, 1, D_MODEL), lambda b, i: (b, 0, 0))
    out_shape = ([stream_shape(d) for d in qkv_order]
                 + [jax.ShapeDtypeStruct((B, tail_rows[g], 2 * GROUP_WIDTH), F32) for g in range(N_GROUPS)]
                 + [jax.ShapeDtypeStruct((B * S, D_MODEL), BF16)] * 2
                 + [jax.ShapeDtypeStruct((B, CONV_HIST, CONV_CH), F32)])
    out_specs = ([stream_spec(d) for d in qkv_order] + [tail_spec(g) for g in range(N_GROUPS)]
                 + [tok_spec(D_MODEL)] * 2 + [pl.BlockSpec((1, CONV_HIST, CONV_CH), lambda b, i: (b, 0, 0))])
    n_stage = 3 * sum(1 for d in dils if d > 1)
    return pl.pallas_call(
        functools.partial(_k1_kernel, tail_rows, nt),
        grid=(B, nt),
        in_specs=[pl.BlockSpec((1, tm, D_MODEL), lambda b, i: (b, i, 0)), mod_spec, mod_spec,
                  _const_spec((1, D_MODEL)), _weight_spec((D_MODEL, IN_COLS)),
                  _const_spec((CONV_WIDTH, SUBLANES, CONV_CH)), _const_spec((SUBLANES, CONV_CH)),
                  _const_spec((1, CONV_CH)), _const_spec((1, CONV_CH)), _weight_spec((CONV_CH, D_MODEL))],
        out_specs=out_specs,
        out_shape=out_shape,
        scratch_shapes=([pltpu.VMEM((CONV_HIST + tm, CONV_CH), F32), pltpu.VMEM((tm, CONV_CH), BF16),
                         pltpu.VMEM((SUBLANES - 1, CONV_HIST + tm - SUBLANES, CONV_CH), F32),
                         pltpu.VMEM((tm, D_MODEL), F32)]
                        + [pltpu.VMEM((tm, LANES), F32)] * (n_stage * STAGE_HALVES)),
        compiler_params=_params(("arbitrary", "arbitrary")),
        name="inproj_prompt",
    )(x, sh1, sc1, norm_g, w_in_b, dw_w, dw_b, ln_g, ln_b, w_co_b)


def _k1s_kernel(n_steps, x_ref, sh_ref, sc_ref, g_ref, win_ref, st_ref, dww_ref, dwb_ref, lng_ref, lnb_ref, wco_ref,
                q0, q1, q2, k0, v0, k1, v1, k2, v2, ga_ref, sgb_ref, conv_ref):
    bd = st_ref.shape[1]
    hist = st_ref.shape[0]
    h = _rms_modulate(x_ref[...], g_ref[...], sc_ref[...], sh_ref[...])
    hb = h.astype(BF16)
    _, u, proj = _project_common(hb, win_ref, ((q0, q1, q2), (k0, k1, k2), (v0, v1, v2)),
                                 (1,) * N_GROUPS, [(None,) * 3] * N_GROUPS)

    def ext(j):
        return st_ref[j] if j < hist else u[(j - hist) * bd:(j - hist + 1) * bd, :]

    outs = []
    for t in range(n_steps):
        acc = jnp.zeros((bd, CONV_CH), F32) + dwb_ref[...]
        for k in range(CONV_WIDTH):
            acc = acc + dww_ref[k:k + 1, :] * ext(t + k + hist - (CONV_WIDTH - 1))
        outs.append(acc)
    for j in range(hist):
        conv_ref[j] = ext(j + n_steps)
    s = _conv_tail(jnp.concatenate(outs, axis=0), lng_ref[...], lnb_ref[...])
    a = jnp.dot(s.astype(BF16), wco_ref[...], preferred_element_type=F32)
    ga_ref[...] = (_sigmoid(proj(COL_GA, D_MODEL)) * a).astype(BF16)
    sgb_ref[...] = _sigmoid(proj(COL_GB, D_MODEL)).astype(BF16)


def _k1_sample(x_tm, sh1, sc1, norm_g, w_in_b, state_tm, dw_w, dw_b, ln_g, ln_b, w_co_b, n_steps):
    T = x_tm.shape[0]
    hist, bd, _ = state_tm.shape
    out_shape = ([jax.ShapeDtypeStruct((T, GROUP_WIDTH), F32)] * 9
                 + [jax.ShapeDtypeStruct((T, D_MODEL), BF16)] * 2
                 + [jax.ShapeDtypeStruct((hist, bd, CONV_CH), F32)])
    out_specs = ([_const_spec((T, GROUP_WIDTH))] * 9 + [_const_spec((T, D_MODEL))] * 2
                 + [_const_spec((hist, bd, CONV_CH))])
    return pl.pallas_call(
        functools.partial(_k1s_kernel, n_steps),
        grid=(1,),
        in_specs=[_const_spec((T, D_MODEL)), _const_spec((T, D_MODEL)), _const_spec((T, D_MODEL)),
                  _const_spec((1, D_MODEL)), _const_spec((D_MODEL, IN_COLS)), _const_spec((hist, bd, CONV_CH)),
                  _const_spec((CONV_WIDTH, CONV_CH)), _const_spec((1, CONV_CH)),
                  _const_spec((1, CONV_CH)), _const_spec((1, CONV_CH)), _const_spec((CONV_CH, D_MODEL))],
        out_specs=out_specs,
        out_shape=out_shape,
        compiler_params=_params(("arbitrary",)),
        name="inproj_sample",
    )(x_tm, sh1, sc1, norm_g, w_in_b, state_tm, dw_w, dw_b, ln_g, ln_b, w_co_b)


def _t5_bucket(dist):
    max_exact = N_BUCKETS // 2
    d_f = jnp.maximum(dist, 1).astype(F32)
    large = max_exact + (jnp.log(d_f / max_exact) / math.log(MAX_DISTANCE / max_exact)
                         * (N_BUCKETS - max_exact)).astype(jnp.int32)
    large = jnp.minimum(large, N_BUCKETS - 1)
    return jnp.where(dist < max_exact, dist, large)


def _bucket_lookup(rel_bias_g, dist):
    bucket = _t5_bucket(dist)
    out = jnp.zeros((rel_bias_g.shape[1],) + dist.shape, F32)
    for b in range(N_BUCKETS):
        out = jnp.where(bucket[None] == b, rel_bias_g[b].reshape((-1,) + (1,) * dist.ndim), out)
    return out


def _prompt_bias(rel_bias_g, dil, n_keys):
    blk = n_keys
    i = jnp.arange(blk)[:, None]
    j = jnp.arange(2 * blk)[None, :]
    rel = i - j + blk
    valid = (rel >= 0) & (rel <= n_keys)
    bias = _bucket_lookup(rel_bias_g, jnp.clip(rel, 0, n_keys) * dil)
    bias = jnp.where(valid[None], bias, NEG_INF)
    return bias.reshape(HEADS_PER_GROUP * blk, 2 * blk).astype(F32)


def _sample_bias(rel_bias_g, dil, n_keys, lc, n_steps, q_rows):
    t = jnp.arange(q_rows)[:, None]
    pos = jnp.arange(lc)[None, :]
    dist_c = lc + t - pos
    dist_n = t - (jnp.arange(128)[None, :] - (128 - n_steps))

    def table(dist, extra):
        ok = (dist >= 0) & (dist % dil == 0) & (dist // dil <= n_keys) & extra & (t < n_steps)
        b = _bucket_lookup(rel_bias_g, jnp.clip(dist, 0, None))
        b = jnp.where(ok[None], b, NEG_INF)
        return jnp.where((t >= n_steps)[None], 0.0, b).astype(F32)

    lane_ok = jnp.arange(128)[None, :] >= 128 - n_steps
    return table(dist_c, True), table(dist_n, lane_ok)


def _k2_kernel(q_ref, k_ref, kh_ref, v_ref, vh_ref, bias_ref, o_ref, lse_ref, kbuf, vbuf):
    i = pl.program_id(2)
    rows = q_ref.shape[1]
    blk = ATTN_BLOCK
    kbuf[0:blk, :] = kh_ref[0]
    kbuf[blk:blk + rows, :] = k_ref[0]
    vbuf[0:blk, :] = vh_ref[0]
    vbuf[blk:blk + rows, :] = v_ref[0]
    lane_head = lax.broadcasted_iota(jnp.int32, (blk, GROUP_WIDTH), 1) // HEAD_DIM
    col = lax.broadcasted_iota(jnp.int32, (HEADS_PER_GROUP * blk, 2 * blk), 1)
    first_mask = jnp.where((col < blk) & (i == 0), NEG_INF, 0.0).astype(F32)
    bias = bias_ref[...]
    for j in range(rows // blk):
        qb = q_ref[0, j * blk:(j + 1) * blk, :]
        q4 = jnp.concatenate([jnp.where(lane_head == h, qb, jnp.zeros_like(qb)) for h in range(HEADS_PER_GROUP)], axis=0)
        kc = kbuf[j * blk:(j + 2) * blk, :]
        vc = vbuf[j * blk:(j + 2) * blk, :]
        s = lax.dot_general(q4, kc, (((1,), (1,)), ((), ())), preferred_element_type=F32) + bias
        if j == 0:
            s = s + first_mask
        m = jnp.max(s, axis=-1, keepdims=True)
        p = jnp.exp(s - m)
        l = jnp.sum(p, axis=-1, keepdims=True)
        o4 = jnp.dot(p.astype(BF16), vc, preferred_element_type=F32) * (1.0 / l)
        lse4 = m + jnp.log(l)
        o = jnp.zeros((blk, GROUP_WIDTH), F32)
        lse = jnp.zeros((blk, GROUP_WIDTH), F32)
        for h in range(HEADS_PER_GROUP):
            sel = lane_head == h
            o = jnp.where(sel, o4[h * blk:(h + 1) * blk, :], o)
            lse = jnp.where(sel, lse4[h * blk:(h + 1) * blk, :], lse)
        o_ref[0, j * blk:(j + 1) * blk, :] = o.astype(o_ref.dtype)
        lse_ref[0, j * blk:(j + 1) * blk, :] = lse


def _k2_prompt(q, k, v, bias, dil):
    B, L, _ = q.shape
    rows = min(ATTN_ROWS, L)
    per = rows // ATTN_BLOCK
    main = pl.BlockSpec((1, rows, GROUP_WIDTH), lambda b, r, i: (b, i, r))
    halo = pl.BlockSpec((1, ATTN_BLOCK, GROUP_WIDTH), lambda b, r, i: (b, jnp.maximum(i * per - 1, 0), r))
    return pl.pallas_call(
        _k2_kernel,
        grid=(B, dil, L // rows),
        in_specs=[main, main, halo, main, halo, _const_spec(bias.shape)],
        out_specs=[main, main],
        out_shape=[jax.ShapeDtypeStruct(q.shape, BF16), jax.ShapeDtypeStruct(q.shape, F32)],
        scratch_shapes=[pltpu.VMEM((ATTN_BLOCK + rows, GROUP_WIDTH), BF16)] * 2,
        compiler_params=_params(("arbitrary", "arbitrary", "arbitrary")),
        name=f"attn_prompt_d{dil}",
    )(q, k, k, v, v, bias)


def _k2s_kernel(n_steps, q_ref, c0, c1, c2, t0, t1, t2, bc0, bc1, bc2, bn0, bn1, bn2,
                nc0, nc1, nc2, o0, o1, o2, l0, l1, l2):
    lane = lax.broadcasted_iota(jnp.int32, (2 * GROUP_WIDTH, 128), 1)
    keep = 128 - n_steps
    groups = ((c0, t0, bc0, bn0, nc0, o0, l0), (c1, t1, bc1, bn1, nc1, o1, l1), (c2, t2, bc2, bn2, nc2, o2, l2))
    for g, (c_ref, t_ref, bc_ref, bn_ref, nc_ref, o_ref, l_ref) in enumerate(groups):
        lc = c_ref.shape[2]
        n_tiles = lc // 128
        tail = t_ref[0]
        cur = pltpu.roll(c_ref[0, :, 0:128], keep, 1)
        for c in range(n_tiles):
            nxt = pltpu.roll(c_ref[0, :, (c + 1) * 128:(c + 2) * 128], keep, 1) if c + 1 < n_tiles else tail
            nc_ref[0, :, c * 128:(c + 1) * 128] = jnp.where(lane < keep, cur, nxt)
            cur = nxt
        for h in range(HEADS_PER_GROUP):
            lo = g * GROUP_WIDTH + h * HEAD_DIM
            qh = q_ref[0, :, lo:lo + HEAD_DIM].astype(BF16)
            kh = c_ref[0, h * HEAD_DIM:(h + 1) * HEAD_DIM, :].astype(BF16)
            vh = c_ref[0, GROUP_WIDTH + h * HEAD_DIM:GROUP_WIDTH + (h + 1) * HEAD_DIM, :].astype(BF16)
            kt = tail[h * HEAD_DIM:(h + 1) * HEAD_DIM, :].astype(BF16)
            vt = tail[GROUP_WIDTH + h * HEAD_DIM:GROUP_WIDTH + (h + 1) * HEAD_DIM, :].astype(BF16)
            sc = jnp.dot(qh, kh, preferred_element_type=F32) + bc_ref[h]
            sn = jnp.dot(qh, kt, preferred_element_type=F32) + bn_ref[h]
            m = jnp.maximum(jnp.max(sc, axis=-1, keepdims=True), jnp.max(sn, axis=-1, keepdims=True))
            pc = jnp.exp(sc - m)
            pn = jnp.exp(sn - m)
            l = jnp.sum(pc, axis=-1, keepdims=True) + jnp.sum(pn, axis=-1, keepdims=True)
            nt_dims = (((1,), (1,)), ((), ()))
            o = (lax.dot_general(pc.astype(BF16), vh, nt_dims, preferred_element_type=F32)
                 + lax.dot_general(pn.astype(BF16), vt, nt_dims, preferred_element_type=F32)) * (1.0 / l)
            o_ref[0, :, h * HEAD_DIM:(h + 1) * HEAD_DIM] = o
            l_ref[0, :, h * HEAD_DIM:(h + 1) * HEAD_DIM] = jnp.broadcast_to(m + jnp.log(l), o.shape)


def _k2_sample(q_b, caches, tails, biases_c, biases_n, n_steps):
    bd, q_rows, _ = q_b.shape
    per_b = lambda shape: pl.BlockSpec((1,) + shape[1:], lambda b: (b,) + (0,) * (len(shape) - 1))
    ins = [q_b] + list(caches) + list(tails) + list(biases_c) + list(biases_n)
    in_specs = ([per_b(q_b.shape)] + [per_b(c.shape) for c in caches] + [per_b(t.shape) for t in tails]
                + [_const_spec(b.shape) for b in biases_c] + [_const_spec(b.shape) for b in biases_n])
    o_shape = jax.ShapeDtypeStruct((bd, q_rows, GROUP_WIDTH), F32)
    out_shape = [jax.ShapeDtypeStruct(c.shape, F32) for c in caches] + [o_shape] * 6
    out_specs = [per_b(c.shape) for c in caches] + [per_b(o_shape.shape)] * 6
    return pl.pallas_call(
        functools.partial(_k2s_kernel, n_steps),
        grid=(bd,),
        in_specs=in_specs,
        out_specs=out_specs,
        out_shape=out_shape,
        compiler_params=_params(("arbitrary",)),
        name="attn_sample",
    )(*ins)


def _k3_kernel(x_ref, o0, o1, o2, l0, l1, l2, ga_ref, sgb_ref, g1_ref, sh2_ref, sc2_ref, gf_ref,
               wao_ref, wo_ref, wr_ref, br_ref, cnt_in_ref,
               x2_ref, h2_ref, eid_ref, wts_ref, rank_ref, cnt_ref, carry, *stage_refs):
    @pl.when(pl.program_id(0) == 0)
    def _():
        carry[...] = cnt_in_ref[...]

    dils = tuple(x_ref.shape[0] // r.shape[-2] for r in (o0, o1, o2))
    pairs = [stage_refs[j:j + STAGE_HALVES] for j in range(0, len(stage_refs), STAGE_HALVES)]
    os_ = [_from_streams(r, d, pairs[2 * g]) for g, (r, d) in enumerate(zip((o0, o1, o2), dils))]
    ls = [_from_streams(r, d, pairs[2 * g + 1]) for g, (r, d) in enumerate(zip((l0, l1, l2), dils))]
    m = jnp.maximum(jnp.maximum(ls[0], ls[1]), ls[2])
    ws = [jnp.exp(l - m) for l in ls]
    den = ws[0] + ws[1] + ws[2]
    o = (ws[0] * os_[0] + ws[1] * os_[1] + ws[2] * os_[2]) / den
    b = jnp.dot(o.astype(BF16), wao_ref[...], preferred_element_type=F32)
    mixed = ga_ref[...].astype(F32) + sgb_ref[...].astype(F32) * b
    x2 = x_ref[...] + g1_ref[0] * jnp.dot(mixed.astype(BF16), wo_ref[...], preferred_element_type=F32)
    x2_ref[...] = x2
    h2 = _rms_modulate(x2, gf_ref[...], sc2_ref[0], sh2_ref[0])
    h2_ref[...] = h2
    lt = lax.dot_general(wr_ref[...], h2, (((1,), (1,)), ((), ())), preferred_element_type=F32,
                         precision=lax.Precision.HIGHEST) + br_ref[...]
    tm = h2.shape[0]
    gl = lt[0:8, :]
    gmax = jnp.max(gl, axis=0, keepdims=True)
    r8 = lax.broadcasted_iota(jnp.int32, (8, tm), 0)
    grp = jnp.min(jnp.where(gl == gmax, r8, 8), axis=0, keepdims=True)
    p_grp = 1.0 / jnp.sum(jnp.exp(gl - gmax), axis=0, keepdims=True)
    es = jnp.zeros((EXPERTS_PER_GROUP, tm), F32)
    for g in range(N_EXPERT_GROUPS):
        es = jnp.where(grp == g, lt[8 + 8 * g:16 + 8 * g, :], es)
    v1 = jnp.max(es, axis=0, keepdims=True)
    i1 = jnp.min(jnp.where(es == v1, r8, 8), axis=0, keepdims=True)
    rest = jnp.where(r8 == i1, -jnp.inf, es)
    v2 = jnp.max(rest, axis=0, keepdims=True)
    i2 = jnp.min(jnp.where(rest == v2, r8, 8), axis=0, keepdims=True)
    e21 = jnp.exp(v2 - v1)
    w1 = p_grp / (1.0 + e21)
    e1 = grp * EXPERTS_PER_GROUP + i1
    e2 = grp * EXPERTS_PER_GROUP + i2
    eid_ref[0, 0:1, :] = e1
    eid_ref[0, 1:2, :] = e2
    wts_ref[0, 0:1, :] = w1
    wts_ref[0, 1:2, :] = w1 * e21
    r_e = lax.broadcasted_iota(jnp.int32, (N_EXPERTS, tm), 0)
    hit1 = r_e == e1
    hit2 = r_e == e2
    both = jnp.where(hit1 | hit2, 1.0, 0.0)
    s_idx = lax.broadcasted_iota(jnp.int32, (tm, tm), 0)
    t_idx = lax.broadcasted_iota(jnp.int32, (tm, tm), 1)
    before = jnp.where(s_idx < t_idx, 1.0, 0.0).astype(BF16)
    base = carry[...] + jnp.dot(both.astype(BF16), before, preferred_element_type=F32)
    rank_ref[0, 0:1, :] = jnp.sum(jnp.where(hit1, base, 0.0), axis=0, keepdims=True).astype(jnp.int32)
    rank_ref[0, 1:2, :] = jnp.sum(jnp.where(hit2, base, 0.0), axis=0, keepdims=True).astype(jnp.int32)
    total = carry[...] + jnp.sum(both, axis=1, keepdims=True)
    carry[...] = total
    cnt_ref[...] = total


def _k3(x, o_l, g1, sh2, sc2, ga, sgb, norm_ffn_g, w_ao_b, w_o_b, w_r, b_r, cnt_in, tiles_per_mod):
    T = x.shape[0]
    tm = min(TOKEN_TILE, T)
    nt = T // tm
    tok = lambda w: pl.BlockSpec((tm, w), lambda t: (t, 0))
    mod = pl.BlockSpec((1,) + g1.shape[1:], lambda t: (t // tiles_per_mod, 0, 0))
    small = pl.BlockSpec((1, 2, tm), lambda t: (t, 0, 0))
    small_i = jax.ShapeDtypeStruct((nt, 2, tm), jnp.int32)

    def attn_spec(a):
        if a.ndim == 2:
            return tok(GROUP_WIDTH)
        dil = a.shape[2] // GROUP_WIDTH
        return pl.BlockSpec((1, tm // dil, a.shape[2]), lambda t: (t // tiles_per_mod, t % tiles_per_mod, 0))

    attn_in = [o_l[0][0], o_l[1][0], o_l[2][0], o_l[0][1], o_l[1][1], o_l[2][1]]
    return pl.pallas_call(
        _k3_kernel,
        grid=(nt,),
        in_specs=[tok(D_MODEL)] + [attn_spec(a) for a in attn_in] + [tok(D_MODEL)] * 2 + [mod] * 3
                 + [_const_spec((1, D_MODEL)), _weight_spec(w_ao_b.shape), _weight_spec(w_o_b.shape),
                    _const_spec(w_r.shape), _const_spec(b_r.shape), _const_spec((N_EXPERTS, 1))],
        out_specs=[tok(D_MODEL), tok(D_MODEL), small, small, small, _const_spec((N_EXPERTS, 1))],
        out_shape=[jax.ShapeDtypeStruct((T, D_MODEL), F32), jax.ShapeDtypeStruct((T, D_MODEL), F32),
                   small_i, jax.ShapeDtypeStruct((nt, 2, tm), F32), small_i,
                   jax.ShapeDtypeStruct((N_EXPERTS, 1), F32)],
        scratch_shapes=([pltpu.VMEM((N_EXPERTS, 1), F32)]
                        + [pltpu.VMEM((tm, LANES), F32)] * (2 * N_GROUPS * STAGE_HALVES)),
        compiler_params=_params(("arbitrary",)),
        name="merge_router",
    )(x, *attn_in, ga, sgb, g1, sh2, sc2, norm_ffn_g, w_ao_b, w_o_b, w_r, b_r, cnt_in)


def _rows_wait(n_rows, hbm, vmem, sem):
    pltpu.make_async_copy(hbm.at[pl.ds(0, n_rows)], vmem, sem).wait()


def _dispatch_kernel(n_tiles, n_first, dest_ref, fill_ref, ha_ref, hb_ref, xs_ref, buf, sem, zbuf, zsem):
    i = pl.program_id(0)
    slot = i % 2 if n_tiles > 1 else 0
    tm = ha_ref.shape[0]

    def drain(s):
        for _ in range(2):
            _rows_wait(tm, xs_ref, buf.at[s], sem.at[s])

    @pl.when(i == 0)
    def _():
        zbuf[...] = jnp.zeros(zbuf.shape, F32)

        def fill(row):
            return pltpu.make_async_copy(zbuf, xs_ref.at[pl.ds(pl.multiple_of(row, MOE_ROWS), MOE_ROWS)], zsem)

        n_blocks = xs_ref.shape[0] // MOE_ROWS
        n_used = fill_ref[N_EXPERTS]

        def start_block(j, carry):
            fill(j * MOE_ROWS).start()
            return carry

        def wait_block(j, carry):
            fill(j * MOE_ROWS).wait()
            return carry

        for e in range(N_EXPERTS):
            @pl.when(fill_ref[e] >= 0)
            def _(e=e):
                fill(fill_ref[e]).start()
        lax.fori_loop(n_used, n_blocks, start_block, 0)
        for e in range(N_EXPERTS):
            @pl.when(fill_ref[e] >= 0)
            def _(e=e):
                fill(fill_ref[e]).wait()
        lax.fori_loop(n_used, n_blocks, wait_block, 0)

    if n_tiles > 2:
        @pl.when(i >= 2)
        def _():
            drain(slot)

    @pl.when(i < n_first)
    def _():
        buf[slot] = ha_ref[...]

    @pl.when(i >= n_first)
    def _():
        buf[slot] = hb_ref[...]

    for r in range(tm):
        for k in range(2):
            pltpu.make_async_copy(buf.at[slot, pl.ds(r, 1)], xs_ref.at[pl.ds(dest_ref[0, k, r], 1)], sem.at[slot]).start()

    if n_tiles == 1:
        drain(0)
    else:
        @pl.when(i == n_tiles - 1)
        def _():
            drain(slot)
            drain(1 - slot)


def _dispatch(dest, fill_rows, h_a, h_b, n_slots):
    nt, _, tm = dest.shape
    n_first = h_a.shape[0] // tm
    assert h_a.shape[0] % tm == 0 and h_b.shape[0] == (nt - n_first) * tm
    return pl.pallas_call(
        functools.partial(_dispatch_kernel, nt, n_first),
        grid=(nt,),
        in_specs=[pl.BlockSpec((1, 2, tm), lambda t: (t, 0, 0), memory_space=pltpu.SMEM),
                  pl.BlockSpec(memory_space=pltpu.SMEM),
                  pl.BlockSpec((tm, D_MODEL), lambda t: (jnp.minimum(t, n_first - 1), 0)),
                  pl.BlockSpec((tm, D_MODEL), lambda t: (jnp.maximum(t - n_first, 0), 0))],
        out_specs=pl.BlockSpec(memory_space=pl.ANY),
        out_shape=jax.ShapeDtypeStruct((n_slots, D_MODEL), F32),
        scratch_shapes=[pltpu.VMEM((2, tm, D_MODEL), F32), pltpu.SemaphoreType.DMA((2,)),
                        pltpu.VMEM((MOE_ROWS, D_MODEL), F32), pltpu.SemaphoreType.DMA(())],
        compiler_params=_params(("arbitrary",)),
        name="dispatch_rows",
    )(dest, fill_rows, h_a, h_b)


def _k4_kernel(be_ref, nused_ref, x_ref, wg_ref, wu_ref, wd_ref, y_ref, wg_b, wu_b, wd_b):
    i = pl.program_id(0)

    @pl.when((i == 0) | (be_ref[i] != be_ref[jnp.maximum(i - 1, 0)]))
    def _():
        wg_b[...] = wg_ref[0].astype(BF16)
        wu_b[...] = wu_ref[0].astype(BF16)
        wd_b[...] = wd_ref[0].astype(BF16)

    @pl.when(i < nused_ref[0])
    def _():
        xb = x_ref[...].astype(BF16)
        gate = jnp.dot(xb, wg_b[...], preferred_element_type=F32)
        up = jnp.dot(xb, wu_b[...], preferred_element_type=F32)
        mid = (_silu(gate) * up).astype(BF16)
        y_ref[...] = jnp.dot(mid, wd_b[...], preferred_element_type=F32)

    @pl.when(i >= nused_ref[0])
    def _():
        y_ref[...] = jnp.zeros(y_ref.shape, F32)


def _k4(block_expert, n_used, xs, w_gate, w_up, w_down):
    n_blocks = block_expert.shape[0]
    rows = MOE_ROWS
    grid_spec = pltpu.PrefetchScalarGridSpec(
        num_scalar_prefetch=2,
        grid=(n_blocks,),
        in_specs=[pl.BlockSpec((rows, D_MODEL), lambda i, be, nu: (jnp.minimum(i, nu[0] - 1), 0)),
                  pl.BlockSpec((1, D_MODEL, D_FF_EXPERT), lambda i, be, nu: (be[i], 0, 0)),
                  pl.BlockSpec((1, D_MODEL, D_FF_EXPERT), lambda i, be, nu: (be[i], 0, 0)),
                  pl.BlockSpec((1, D_FF_EXPERT, D_MODEL), lambda i, be, nu: (be[i], 0, 0))],
        out_specs=pl.BlockSpec((rows, D_MODEL), lambda i, be, nu: (i, 0)),
        scratch_shapes=[pltpu.VMEM((D_MODEL, D_FF_EXPERT), BF16), pltpu.VMEM((D_MODEL, D_FF_EXPERT), BF16),
                        pltpu.VMEM((D_FF_EXPERT, D_MODEL), BF16)],
    )
    return pl.pallas_call(
        _k4_kernel,
        grid_spec=grid_spec,
        out_shape=jax.ShapeDtypeStruct((n_blocks * rows, D_MODEL), F32),
        compiler_params=_params(("arbitrary",)),
        name="expert_blocks",
    )(block_expert, n_used, xs, w_gate, w_up, w_down)


def _k5_kernel(n_tiles, idx_ref, x2_ref, w_ref, g2_ref, gfin_ref, ys_hbm, y_ref, ybuf, sem):
    i = pl.program_id(0)
    tm = x2_ref.shape[0]

    @pl.when(i < n_tiles)
    def _():
        slot = i % 2
        for r in range(tm):
            for k in range(2):
                pltpu.make_async_copy(ys_hbm.at[pl.ds(idx_ref[0, k, r], 1)], ybuf.at[slot, k, pl.ds(r, 1)],
                                      sem.at[slot]).start()

    @pl.when(i >= 1)
    def _():
        slot = (i - 1) % 2
        for k in range(2):
            _rows_wait(tm, ys_hbm, ybuf.at[slot, k], sem.at[slot])
        w = w_ref[...]
        f = ybuf[slot, 0] * w[:, 0:1] + ybuf[slot, 1] * w[:, 1:2]
        y = x2_ref[...] + g2_ref[0] * f
        r = lax.rsqrt(jnp.mean(y * y, axis=-1, keepdims=True) + EPS)
        y_ref[...] = (y * r) * gfin_ref[...]


def _k5(dest, x2, wts, g2, norm_final_g, ys, tiles_per_mod):
    nt, _, tm = dest.shape
    T = x2.shape[0]
    prev = lambda t: jnp.maximum(t - 1, 0)
    return pl.pallas_call(
        functools.partial(_k5_kernel, nt),
        grid=(nt + 1,),
        in_specs=[pl.BlockSpec((1, 2, tm), lambda t: (jnp.minimum(t, nt - 1), 0, 0), memory_space=pltpu.SMEM),
                  pl.BlockSpec((tm, D_MODEL), lambda t: (prev(t), 0)),
                  pl.BlockSpec((tm, 2), lambda t: (prev(t), 0)),
                  pl.BlockSpec((1,) + g2.shape[1:], lambda t: (prev(t) // tiles_per_mod, 0, 0)),
                  _const_spec((1, D_MODEL)), pl.BlockSpec(memory_space=pl.ANY)],
        out_specs=pl.BlockSpec((tm, D_MODEL), lambda t: (prev(t), 0)),
        out_shape=jax.ShapeDtypeStruct((T, D_MODEL), F32),
        scratch_shapes=[pltpu.VMEM((2, 2, tm, D_MODEL), F32), pltpu.SemaphoreType.DMA((2,))],
        compiler_params=_params(("arbitrary",)),
        name="combine_norm",
    )(dest, x2, wts, g2, norm_final_g, ys)


def _slot_tables(counts, n_blocks):
    padded = (counts + MOE_ROWS - 1) // MOE_ROWS * MOE_ROWS
    pend = jnp.cumsum(padded)
    pstart = pend - padded
    block_lo = jnp.arange(n_blocks, dtype=jnp.int32) * MOE_ROWS
    block_expert = jnp.minimum(jnp.sum(pend[None, :] <= block_lo[:, None], axis=1), N_EXPERTS - 1).astype(jnp.int32)
    n_used = (pend[-1] // MOE_ROWS).astype(jnp.int32).reshape(1)
    fill_rows = jnp.concatenate([jnp.where(padded > 0, pend - MOE_ROWS, -1).astype(jnp.int32), n_used])
    return pstart.astype(jnp.int32), block_expert, n_used, fill_rows


def _slots(eid, rank, pstart):
    sel = eid[..., None] == jnp.arange(N_EXPERTS, dtype=jnp.int32)
    return jnp.sum(jnp.where(sel, pstart, 0), axis=-1).astype(jnp.int32) + rank


def _cache_view(cache):
    bd, lc = cache.shape[:2]
    return cache.transpose(0, 2, 3, 4, 1).reshape(bd, 2 * GROUP_WIDTH, lc)


def _cache_unview(view):
    bd, _, lc = view.shape
    return view.reshape(bd, 2, HEADS_PER_GROUP, HEAD_DIM, lc).transpose(0, 4, 1, 2, 3)


def kernel(x_prompt, x_sample, c_prompt, c_sample, cache_kv_w128, cache_kv_w512, cache_kv_w2048, state_conv, rel_bias, norm_mix_g, norm_ffn_g, w_mod, b_mod, w_in, dw_w, dw_b, ln_conv_g, ln_conv_b, w_conv_out, w_attn_out, w_out, w_router_group, b_router_group, w_router_expert, b_router_expert, w_exp_gate, w_exp_up, w_exp_down, norm_final_g):
    assert norm_mix_g.shape[0] == 1, "single layer"
    B, S, D = x_prompt.shape
    Bd, Td, _ = x_sample.shape
    caches_in = (cache_kv_w128[0], cache_kv_w512[0], cache_kv_w2048[0])
    for (win, dil), c in zip(DILATED_GROUPS, caches_in):
        assert c.shape[1] >= (win // dil) * dil and c.shape[1] % 128 == 0 and S % (dil * ATTN_BLOCK) == 0

    wi = w_in[0]
    q_cols = wi[:, :ATTN_WIDTH] * (HEAD_DIM ** -0.5)
    kv_cols = []
    for g in range(N_GROUPS):
        kv_cols += [wi[:, ATTN_WIDTH + g * GROUP_WIDTH:ATTN_WIDTH + (g + 1) * GROUP_WIDTH],
                    wi[:, 2 * ATTN_WIDTH + g * GROUP_WIDTH:2 * ATTN_WIDTH + (g + 1) * GROUP_WIDTH]]
    w_in_b = jnp.concatenate([q_cols] + kv_cols + [wi[:, 3 * ATTN_WIDTH:]], axis=1).astype(BF16)
    w_co_b = w_conv_out[0].astype(BF16)
    w_ao_b = w_attn_out[0].astype(BF16)
    w_o_b = w_out[0].astype(BF16)
    w_r = jnp.zeros((ROUTER_ROWS, D), F32)
    w_r = w_r.at[0:N_EXPERT_GROUPS].set(w_router_group[0].T)
    w_r = w_r.at[8:].set(w_router_expert[0].reshape(D, N_EXPERTS).T)
    b_r = jnp.full((ROUTER_ROWS, 1), NEG_INF, F32)
    b_r = b_r.at[0:N_EXPERT_GROUPS, 0].set(b_router_group[0])
    b_r = b_r.at[8:, 0].set(b_router_expert[0].reshape(N_EXPERTS))
    row = lambda v: v.reshape(1, -1)

    n_seq = B + Bd
    n_seq_pad = -(-n_seq // 16) * 16
    c_all = jnp.concatenate([c_prompt, c_sample, jnp.zeros((n_seq_pad - n_seq, D), F32)], axis=0)
    mod = _modulation(c_all, w_mod[0], row(b_mod[0]))
    mod_p = [mod[:B, j * D:(j + 1) * D].reshape(B, 1, D) for j in range(6)]
    mod_s = [jnp.tile(mod[B:B + Bd, j * D:(j + 1) * D], (Td, 1)) for j in range(6)]

    dw_w_rows = jnp.broadcast_to(dw_w[0][:, None, :], (CONV_WIDTH, SUBLANES, CONV_CH))
    dw_b_rows = jnp.broadcast_to(dw_b[0][None, :], (SUBLANES, CONV_CH))
    (q0, q1, q2, k0, v0, k1, v1, k2, v2, kvt0, kvt1, kvt2, ga, sgb, convp) = _k1_prompt(
        x_prompt, mod_p[0], mod_p[1], row(norm_mix_g[0]), w_in_b, dw_w_rows, dw_b_rows,
        row(ln_conv_g[0]), row(ln_conv_b[0]), w_co_b)
    o_l = []
    for g, ((win, dil), qg, kg, vg) in enumerate(zip(DILATED_GROUPS, (q0, q1, q2), (k0, k1, k2), (v0, v1, v2))):
        bias = _prompt_bias(rel_bias[:, g * HEADS_PER_GROUP:(g + 1) * HEADS_PER_GROUP], dil, win // dil)
        o_l.append(_k2_prompt(qg, kg, vg, bias, dil))
    xp = x_prompt.reshape(B * S, D)
    tm3 = min(TOKEN_TILE, B * S)
    x2, h2, eid, wts, rank, cnt_p = _k3(xp, o_l, mod_p[2], mod_p[3], mod_p[4], ga, sgb, row(norm_ffn_g[0]),
                                        w_ao_b, w_o_b, w_r, b_r, jnp.zeros((N_EXPERTS, 1), F32), S // tm3)
    kv_p =[kvt.reshape(1, B, kvt.shape[1], 2, HEADS_PER_GROUP, HEAD_DIM) for kvt in (kvt0, kvt1, kvt2)]
    conv_p = convp[:, CONV_HIST - (CONV_WIDTH - 1):, :][None]

    Ts = Td * Bd
    xs = x_sample.transpose(1, 0, 2).reshape(Ts, D)
    state_tm = state_conv[0].transpose(1, 0, 2)
    (sq0, sq1, sq2, sk0, sv0, sk1, sv1, sk2, sv2, ga_s, sgb_s, conv_tm) = _k1_sample(
        xs, mod_s[0], mod_s[1], row(norm_mix_g[0]), w_in_b, state_tm, dw_w[0], row(dw_b[0]),
        row(ln_conv_g[0]), row(ln_conv_b[0]), w_co_b, Td)
    q_rows = 16
    q_b = jnp.concatenate([sq0, sq1, sq2], axis=1).reshape(Td, Bd, ATTN_WIDTH).transpose(1, 0, 2)
    q_b = jnp.pad(q_b, ((0, 0), (0, q_rows - Td), (0, 0)))
    views, tails, biases_c, biases_n = [], [], [], []
    for g, ((win, dil), c, sk, sv) in enumerate(zip(DILATED_GROUPS, caches_in, (sk0, sk1, sk2), (sv0, sv1, sv2))):
        lc = c.shape[1]
        views.append(_cache_view(c))
        new_kv = jnp.concatenate([sk, sv], axis=1).reshape(Td, Bd, 2 * GROUP_WIDTH).transpose(1, 2, 0)
        tails.append(jnp.pad(new_kv, ((0, 0), (0, 0), (128 - Td, 0))))
        bc, bn = _sample_bias(rel_bias[:, g * HEADS_PER_GROUP:(g + 1) * HEADS_PER_GROUP], dil, win // dil, lc, Td, q_rows)
        biases_c.append(bc)
        biases_n.append(bn)
    (nc0, nc1, nc2, so0, so1, so2, sl0, sl1, sl2) = _k2_sample(q_b, views, tails, biases_c, biases_n, Td)
    to_tm = lambda a: a[:, :Td, :].transpose(1, 0, 2).reshape(Ts, GROUP_WIDTH)
    o_l_s = [(to_tm(so0), to_tm(sl0)), (to_tm(so1), to_tm(sl1)), (to_tm(so2), to_tm(sl2))]
    tm3s = min(TOKEN_TILE, Ts)
    tiled = lambda m: m.reshape(Ts // tm3s, tm3s, D)
    x2s, h2s, eid_s, wts_s, rank_s, cnt_all = _k3(xs, o_l_s, tiled(mod_s[2]), tiled(mod_s[3]), tiled(mod_s[4]),
                                                  ga_s, sgb_s, row(norm_ffn_g[0]), w_ao_b, w_o_b, w_r, b_r, cnt_p, 1)

    n_assign = 2 * (B * S + Ts)
    n_blocks = -(-n_assign // MOE_ROWS) + N_EXPERTS
    pstart, block_expert, n_used, fill_rows = _slot_tables(cnt_all[:, 0].astype(jnp.int32), n_blocks)
    dest_p = _slots(eid, rank, pstart)
    dest_s = _slots(eid_s, rank_s, pstart)
    n_slots = n_blocks * MOE_ROWS
    assert tm3s == tm3, "prompt and sample token tiles must match to share the dispatch"
    slots = _dispatch(jnp.concatenate([dest_p, dest_s], axis=0), fill_rows, h2, h2s, n_slots)
    ys = _k4(block_expert, n_used, slots, w_exp_gate[0], w_exp_up[0], w_exp_down[0])
    per_token = lambda a: a.transpose(0, 2, 1).reshape(-1, 2)
    y_prompt = _k5(dest_p, x2, per_token(wts), mod_p[5], row(norm_final_g), ys, S // tm3).reshape(B, S, D)
    y_s = _k5(dest_s, x2s, per_token(wts_s), tiled(mod_s[5]), row(norm_final_g), ys, 1)
    y_sample = y_s.reshape(Td, Bd, D).transpose(1, 0, 2)
    kv_s = [_cache_unview(nc)[None] for nc in (nc0, nc1, nc2)]
    conv_s = conv_tm.transpose(1, 0, 2)[None]

    return (y_prompt, y_sample, kv_p[0], kv_p[1], kv_p[2], conv_p,
            kv_s[0], kv_s[1], kv_s[2], conv_s)
```

```python
import functools
import math

import jax
import jax.numpy as jnp
from jax import lax
from jax.experimental import pallas as pl
from jax.experimental.pallas import tpu as pltpu

F32 = jnp.float32
BF16 = jnp.bfloat16

D_MODEL = 1024
HEAD_DIM = 64
HEADS_PER_GROUP = 4
GROUP_WIDTH = HEADS_PER_GROUP * HEAD_DIM
LANES = 128
SUBLANES = 8
CONV_CHUNK = 32
STAGE_HALVES = GROUP_WIDTH // LANES
DILATED_GROUPS = ((128, 1), (512, 4), (2048, 16))
N_GROUPS = len(DILATED_GROUPS)
ATTN_WIDTH = N_GROUPS * GROUP_WIDTH
CONV_CH = D_MODEL // 2
CONV_WIDTH = 31
CONV_HIST = 32
N_BUCKETS = 32
MAX_DISTANCE = 2048
N_EXPERT_GROUPS = 4
EXPERTS_PER_GROUP = 8
N_EXPERTS = N_EXPERT_GROUPS * EXPERTS_PER_GROUP
D_FF_EXPERT = D_MODEL // 2
EPS = 1e-6
NEG_INF = -1e30

COL_KV = ATTN_WIDTH
COL_ULIN = COL_KV + 2 * ATTN_WIDTH
COL_UGATE = COL_ULIN + CONV_CH
COL_GA = COL_UGATE + CONV_CH
COL_GB = COL_GA + D_MODEL
IN_COLS = COL_GB + D_MODEL

ROUTER_ROWS = 8 + N_EXPERTS

V7X_VMEM_LIMIT = 56 * 1024 * 1024
TOKEN_TILE = 512
ATTN_ROWS = 512
ATTN_BLOCK = 128
MOE_ROWS = 512


def _sigmoid(x):
    return 1.0 / (1.0 + jnp.exp(-x))


def _silu(x):
    return x * _sigmoid(x)


def _rms_modulate(x, g, sc, sh):
    r = lax.rsqrt(jnp.mean(x * x, axis=-1, keepdims=True) + EPS)
    return ((x * r) * g) * (1.0 + sc) + sh


def _conv_tail(y, g, b):
    mu = jnp.mean(y, axis=-1, keepdims=True)
    yc = y - mu
    var = jnp.mean(yc * yc, axis=-1, keepdims=True)
    return _silu((yc * lax.rsqrt(var + EPS)) * g + b)


def _params(semantics):
    return pltpu.CompilerParams(dimension_semantics=semantics, vmem_limit_bytes=V7X_VMEM_LIMIT)


def _const_spec(shape):
    nd = len(shape)
    return pl.BlockSpec(shape, lambda *_: (0,) * nd)


def _weight_spec(shape):
    nd = len(shape)
    return pl.BlockSpec(shape, lambda *_: (0,) * nd, pipeline_mode=pl.Buffered(1))


def _mod_kernel(c_ref, w_ref, b_ref, o_ref):
    s = _silu(c_ref[...]).astype(BF16)
    o_ref[...] = jnp.dot(s, w_ref[...].astype(BF16), preferred_element_type=F32) + b_ref[...]


def _modulation(c_all, w_mod, b_mod):
    rows = c_all.shape[0]
    n_out = w_mod.shape[1]
    chunk = D_MODEL
    return pl.pallas_call(
        _mod_kernel,
        grid=(n_out // chunk,),
        in_specs=[_const_spec((rows, D_MODEL)),
                  pl.BlockSpec((D_MODEL, chunk), lambda j: (0, j)),
                  pl.BlockSpec((1, chunk), lambda j: (0, j))],
        out_specs=pl.BlockSpec((rows, chunk), lambda j: (0, j)),
        out_shape=jax.ShapeDtypeStruct((rows, n_out), F32),
        compiler_params=_params(("arbitrary",)),
        name="modulation",
    )(c_all, w_mod, b_mod)


def _to_streams(ref, val, dil, stage):
    if dil == 1:
        ref[...] = val.astype(ref.dtype).reshape(ref.shape)
        return
    n = val.shape[0] // dil
    for c, half in enumerate(stage):
        half[...] = val[:, c * LANES:(c + 1) * LANES]
        for r in range(dil):
            lo = r * GROUP_WIDTH + c * LANES
            ref[0, :, lo:lo + LANES] = half[pl.ds(r, n, stride=dil), :].astype(ref.dtype)


def _from_streams(ref, dil, stage):
    if dil == 1:
        return ref[...].astype(F32).reshape(ref.shape[-2:])
    n = ref.shape[1]
    for c, half in enumerate(stage):
        for r in range(dil):
            lo = r * GROUP_WIDTH + c * LANES
            half[pl.ds(r, n, stride=dil), :] = ref[0, :, lo:lo + LANES].astype(F32)
    return jnp.concatenate([half[...] for half in stage], axis=1)


def _project_common(hb, win_ref, outs, dils, stages):
    (q_refs, k_refs, v_refs) = outs

    def proj(lo, width):
        return jnp.dot(hb, win_ref[:, lo:lo + width], preferred_element_type=F32)

    zq = proj(0, ATTN_WIDTH)
    for g in range(N_GROUPS):
        _to_streams(q_refs[g], zq[:, g * GROUP_WIDTH:(g + 1) * GROUP_WIDTH], dils[g], stages[g][0])
    zkvs = []
    for g in range(N_GROUPS):
        zkv = proj(COL_KV + 2 * GROUP_WIDTH * g, 2 * GROUP_WIDTH)
        _to_streams(k_refs[g], zkv[:, :GROUP_WIDTH], dils[g], stages[g][1])
        _to_streams(v_refs[g], zkv[:, GROUP_WIDTH:], dils[g], stages[g][2])
        zkvs.append(zkv)
    u = proj(COL_ULIN, CONV_CH) * _sigmoid(proj(COL_UGATE, CONV_CH))
    return zkvs, u, proj


def _k1_kernel(tail_rows, n_tiles,
               x_ref, sh_ref, sc_ref, g_ref, win_ref, dww_ref, dwb_ref, lng_ref, lnb_ref, wco_ref,
               q0, q1, q2, k0, v0, k1, v1, k2, v2, kvt0, kvt1, kvt2, ga_ref, sgb_ref, convp_ref, uext, sbuf, ushift, sga, *stage_refs):
    i = pl.program_id(1)
    tm = x_ref.shape[1]

    @pl.when(i == 0)
    def _():
        uext[0:CONV_HIST, :] = jnp.zeros((CONV_HIST, CONV_CH), F32)

    h = _rms_modulate(x_ref[0], g_ref[...], sc_ref[0], sh_ref[0])
    hb = h.astype(BF16)

    def proj(lo, width):
        return jnp.dot(hb, win_ref[:, lo:lo + width], preferred_element_type=F32)

    uext[CONV_HIST:CONV_HIST + tm, :] = proj(COL_ULIN, CONV_CH) * _sigmoid(proj(COL_UGATE, CONV_CH))
    base = CONV_HIST - (CONV_WIDTH - 1)
    span = tm + CONV_HIST - SUBLANES
    for b in range(1, SUBLANES):
        ushift[b - 1, 0:span, :] = uext[b:b + span, :]

    def conv_chunk(r0):
        accs = [dwb_ref[...]] * (CONV_CHUNK // SUBLANES)
        for k in range(CONV_WIDTH):
            b = (base + k) % SUBLANES
            lo = r0 + base + k - b
            w8 = dww_ref[k]
            for q in range(len(accs)):
                rows = slice(lo + q * SUBLANES, lo + (q + 1) * SUBLANES)
                src = uext[rows, :] if b == 0 else ushift[b - 1, rows, :]
                accs[q] = accs[q] + w8 * src
        s = _conv_tail(jnp.concatenate(accs, axis=0), lng_ref[...], lnb_ref[...])
        sbuf[r0:r0 + CONV_CHUNK, :] = s.astype(BF16)

    dils = tuple(dil for _, dil in DILATED_GROUPS)
    pairs = [stage_refs[j:j + STAGE_HALVES] for j in range(0, len(stage_refs), STAGE_HALVES)]
    stages = [(None,) * 3] + [pairs[3 * (g - 1):3 * g] for g in range(1, N_GROUPS)]
    pieces = []
    for g, q_ref in enumerate((q0, q1, q2)):
        def q_piece(g=g, q_ref=q_ref):
            _to_streams(q_ref, proj(g * GROUP_WIDTH, GROUP_WIDTH), dils[g], stages[g][0])
        pieces.append(q_piece)
    for g, (k_ref, v_ref, kvt) in enumerate(((k0, v0, kvt0), (k1, v1, kvt1), (k2, v2, kvt2))):
        for half, ref in enumerate((k_ref, v_ref)):
            def kv_piece(g=g, half=half, ref=ref, kvt=kvt):
                z = proj(COL_KV + (2 * g + half) * GROUP_WIDTH, GROUP_WIDTH)
                _to_streams(ref, z, dils[g], stages[g][1 + half])
                tr = kvt.shape[2]

                @pl.when(i >= n_tiles - tail_rows[g] // tr)
                def _():
                    kvt[0, half * GROUP_WIDTH:(half + 1) * GROUP_WIDTH, :] = z[tm - tr:, :].T
            pieces.append(kv_piece)
    for c in range(D_MODEL // GROUP_WIDTH):
        cols = slice(c * GROUP_WIDTH, (c + 1) * GROUP_WIDTH)

        def gb_piece(c=c, cols=cols):
            sgb_ref[:, cols] = _sigmoid(proj(COL_GB + c * GROUP_WIDTH, GROUP_WIDTH)).astype(BF16)

        def ga_piece(c=c, cols=cols):
            sga[:, cols] = _sigmoid(proj(COL_GA + c * GROUP_WIDTH, GROUP_WIDTH))
        pieces += [gb_piece, ga_piece]
    n_qkv = 3 * N_GROUPS
    for piece in pieces[:n_qkv]:
        piece()
    for r0 in range(0, tm, CONV_CHUNK):
        conv_chunk(r0)
    for piece in pieces[n_qkv:]:
        piece()
    last = uext[tm:tm + CONV_HIST, :]
    uext[0:CONV_HIST, :] = last
    convp_ref[0] = last
    a = jnp.dot(sbuf[...], wco_ref[...], preferred_element_type=F32)
    ga_ref[...] = (sga[...] * a).astype(BF16)


def _k1_prompt(x, sh1, sc1, norm_g, w_in_b, dw_w, dw_b, ln_g, ln_b, w_co_b):
    B, S, _ = x.shape
    tm = min(TOKEN_TILE, S)
    nt = S // tm
    tail_rows = tuple(min(win, S) for win, _ in DILATED_GROUPS)
    tail_blk = tuple(min(t, tm) for t in tail_rows)

    def tok_spec(width):
        return pl.BlockSpec((tm, width), lambda b, i: (b * nt + i, 0))

    def tail_spec(g):
        first = nt - tail_rows[g] // tail_blk[g]
        return pl.BlockSpec((1, 2 * GROUP_WIDTH, tail_blk[g]), lambda b, i: (b, 0, jnp.maximum(i - first, 0)))

    def stream_shape(dil):
        return jax.ShapeDtypeStruct((B, S // dil, dil * GROUP_WIDTH), BF16)

    def stream_spec(dil):
        return pl.BlockSpec((1, tm // dil, dil * GROUP_WIDTH), lambda b, i: (b, i, 0))

    dils = [dil for _, dil in DILATED_GROUPS]
    qkv_order = [dils[0], dils[1], dils[2]] + [d for d in dils for _ in range(2)]
    mod_spec = pl.BlockSpec((1, 1, D_MODEL), lambda b, i: (b, 0, 0))
    out_shape = ([stream_shape(d) for d in qkv_order]
                 + [jax.ShapeDtypeStruct((B, 2 * GROUP_WIDTH, tail_rows[g]), F32) for g in range(N_GROUPS)]
                 + [jax.ShapeDtypeStruct((B * S, D_MODEL), BF16)] * 2
                 + [jax.ShapeDtypeStruct((B, CONV_HIST, CONV_CH), F32)])
    out_specs = ([stream_spec(d) for d in qkv_order] + [tail_spec(g) for g in range(N_GROUPS)]
                 + [tok_spec(D_MODEL)] * 2 + [pl.BlockSpec((1, CONV_HIST, CONV_CH), lambda b, i: (b, 0, 0))])
    n_stage = 3 * sum(1 for d in dils if d > 1)
    return pl.pallas_call(
        functools.partial(_k1_kernel, tail_rows, nt),
        grid=(B, nt),
        in_specs=[pl.BlockSpec((1, tm, D_MODEL), lambda b, i: (b, i, 0)), mod_spec, mod_spec,
                  _const_spec((1, D_MODEL)), _weight_spec((D_MODEL, IN_COLS)),
                  _const_spec((CONV_WIDTH, SUBLANES, CONV_CH)), _const_spec((SUBLANES, CONV_CH)),
                  _const_spec((1, CONV_CH)), _const_spec((1, CONV_CH)), _weight_spec((CONV_CH, D_MODEL))],
        out_specs=out_specs,
        out_shape=out_shape,
        scratch_shapes=([pltpu.VMEM((CONV_HIST + tm, CONV_CH), F32), pltpu.VMEM((tm, CONV_CH), BF16),
                         pltpu.VMEM((SUBLANES - 1, CONV_HIST + tm - SUBLANES, CONV_CH), F32),
                         pltpu.VMEM((tm, D_MODEL), F32)]
                        + [pltpu.VMEM((tm, LANES), F32)] * (n_stage * STAGE_HALVES)),
        compiler_params=_params(("arbitrary", "arbitrary")),
        name="inproj_prompt",
    )(x, sh1, sc1, norm_g, w_in_b, dw_w, dw_b, ln_g, ln_b, w_co_b)


def _k1s_kernel(n_steps, x_ref, sh_ref, sc_ref, g_ref, win_ref, st_ref, dww_ref, dwb_ref, lng_ref, lnb_ref, wco_ref,
                q0, q1, q2, k0, v0, k1, v1, k2, v2, ga_ref, sgb_ref, conv_ref):
    bd = st_ref.shape[1]
    hist = st_ref.shape[0]
    h = _rms_modulate(x_ref[...], g_ref[...], sc_ref[...], sh_ref[...])
    hb = h.astype(BF16)
    _, u, proj = _project_common(hb, win_ref, ((q0, q1, q2), (k0, k1, k2), (v0, v1, v2)),
                                 (1,) * N_GROUPS, [(None,) * 3] * N_GROUPS)

    def ext(j):
        return st_ref[j] if j < hist else u[(j - hist) * bd:(j - hist + 1) * bd, :]

    outs = []
    for t in range(n_steps):
        acc = jnp.zeros((bd, CONV_CH), F32) + dwb_ref[...]
        for k in range(CONV_WIDTH):
            acc = acc + dww_ref[k:k + 1, :] * ext(t + k + hist - (CONV_WIDTH - 1))
        outs.append(acc)
    for j in range(hist):
        conv_ref[j] = ext(j + n_steps)
    s = _conv_tail(jnp.concatenate(outs, axis=0), lng_ref[...], lnb_ref[...])
    a = jnp.dot(s.astype(BF16), wco_ref[...], preferred_element_type=F32)
    ga_ref[...] = (_sigmoid(proj(COL_GA, D_MODEL)) * a).astype(BF16)
    sgb_ref[...] = _sigmoid(proj(COL_GB, D_MODEL)).astype(BF16)


def _k1_sample(x_tm, sh1, sc1, norm_g, w_in_b, state_tm, dw_w, dw_b, ln_g, ln_b, w_co_b, n_steps):
    T = x_tm.shape[0]
    hist, bd, _ = state_tm.shape
    out_shape = ([jax.ShapeDtypeStruct((T, GROUP_WIDTH), F32)] * 9
                 + [jax.ShapeDtypeStruct((T, D_MODEL), BF16)] * 2
                 + [jax.ShapeDtypeStruct((hist, bd, CONV_CH), F32)])
    out_specs = ([_const_spec((T, GROUP_WIDTH))] * 9 + [_const_spec((T, D_MODEL))] * 2
                 + [_const_spec((hist, bd, CONV_CH))])
    return pl.pallas_call(
        functools.partial(_k1s_kernel, n_steps),
        grid=(1,),
        in_specs=[_const_spec((T, D_MODEL)), _const_spec((T, D_MODEL)), _const_spec((T, D_MODEL)),
                  _const_spec((1, D_MODEL)), _const_spec((D_MODEL, IN_COLS)), _const_spec((hist, bd, CONV_CH)),
                  _const_spec((CONV_WIDTH, CONV_CH)), _const_spec((1, CONV_CH)),
                  _const_spec((1, CONV_CH)), _const_spec((1, CONV_CH)), _const_spec((CONV_CH, D_MODEL))],
        out_specs=out_specs,
        out_shape=out_shape,
        compiler_params=_params(("arbitrary",)),
        name="inproj_sample",
    )(x_tm, sh1, sc1, norm_g, w_in_b, state_tm, dw_w, dw_b, ln_g, ln_b, w_co_b)


def _t5_bucket(dist):
    max_exact = N_BUCKETS // 2
    d_f = jnp.maximum(dist, 1).astype(F32)
    large = max_exact + (jnp.log(d_f / max_exact) / math.log(MAX_DISTANCE / max_exact)
                         * (N_BUCKETS - max_exact)).astype(jnp.int32)
    large = jnp.minimum(large, N_BUCKETS - 1)
    return jnp.where(dist < max_exact, dist, large)


def _bucket_lookup(rel_bias_g, dist):
    bucket = _t5_bucket(dist)
    out = jnp.zeros((rel_bias_g.shape[1],) + dist.shape, F32)
    for b in range(N_BUCKETS):
        out = jnp.where(bucket[None] == b, rel_bias_g[b].reshape((-1,) + (1,) * dist.ndim), out)
    return out


def _prompt_bias(rel_bias_g, dil, n_keys):
    blk = n_keys
    i = jnp.arange(blk)[:, None]
    j = jnp.arange(2 * blk)[None, :]
    rel = i - j + blk
    valid = (rel >= 0) & (rel <= n_keys)
    bias = _bucket_lookup(rel_bias_g, jnp.clip(rel, 0, n_keys) * dil)
    bias = jnp.where(valid[None], bias, NEG_INF)
    return bias.reshape(HEADS_PER_GROUP * blk, 2 * blk).astype(F32)


def _sample_bias(rel_bias_g, dil, n_keys, lc, n_steps, q_rows):
    t = jnp.arange(q_rows)[:, None]
    pos = jnp.arange(lc)[None, :]
    dist_c = lc + t - pos
    dist_n = t - (jnp.arange(128)[None, :] - (128 - n_steps))

    def table(dist, extra):
        ok = (dist >= 0) & (dist % dil == 0) & (dist // dil <= n_keys) & extra & (t < n_steps)
        b = _bucket_lookup(rel_bias_g, jnp.clip(dist, 0, None))
        b = jnp.where(ok[None], b, NEG_INF)
        return jnp.where((t >= n_steps)[None], 0.0, b).astype(F32)

    lane_ok = jnp.arange(128)[None, :] >= 128 - n_steps
    return table(dist_c, True), table(dist_n, lane_ok)


def _k2_kernel(q_ref, k_ref, kh_ref, v_ref, vh_ref, bias_ref, o_ref, lse_ref, kbuf, vbuf):
    i = pl.program_id(2)
    rows = q_ref.shape[1]
    blk = ATTN_BLOCK
    kbuf[0:blk, :] = kh_ref[0]
    kbuf[blk:blk + rows, :] = k_ref[0]
    vbuf[0:blk, :] = vh_ref[0]
    vbuf[blk:blk + rows, :] = v_ref[0]
    lane_head = lax.broadcasted_iota(jnp.int32, (blk, GROUP_WIDTH), 1) // HEAD_DIM
    col = lax.broadcasted_iota(jnp.int32, (HEADS_PER_GROUP * blk, 2 * blk), 1)
    first_mask = jnp.where((col < blk) & (i == 0), NEG_INF, 0.0).astype(F32)
    bias = bias_ref[...]
    for j in range(rows // blk):
        qb = q_ref[0, j * blk:(j + 1) * blk, :]
        q4 = jnp.concatenate([jnp.where(lane_head == h, qb, jnp.zeros_like(qb)) for h in range(HEADS_PER_GROUP)], axis=0)
        kc = kbuf[j * blk:(j + 2) * blk, :]
        vc = vbuf[j * blk:(j + 2) * blk, :]
        s = lax.dot_general(q4, kc, (((1,), (1,)), ((), ())), preferred_element_type=F32) + bias
        if j == 0:
            s = s + first_mask
        m = jnp.max(s, axis=-1, keepdims=True)
        p = jnp.exp(s - m)
        l = jnp.sum(p, axis=-1, keepdims=True)
        o4 = jnp.dot(p.astype(BF16), vc, preferred_element_type=F32) * (1.0 / l)
        lse4 = m + jnp.log(l)
        o = jnp.zeros((blk, GROUP_WIDTH), F32)
        lse = jnp.zeros((blk, GROUP_WIDTH), F32)
        for h in range(HEADS_PER_GROUP):
            sel = lane_head == h
            o = jnp.where(sel, o4[h * blk:(h + 1) * blk, :], o)
            lse = jnp.where(sel, lse4[h * blk:(h + 1) * blk, :], lse)
        o_ref[0, j * blk:(j + 1) * blk, :] = o.astype(o_ref.dtype)
        lse_ref[0, j * blk:(j + 1) * blk, :] = lse


def _k2_prompt(q, k, v, bias, dil):
    B, L, _ = q.shape
    rows = min(ATTN_ROWS, L)
    per = rows // ATTN_BLOCK
    main = pl.BlockSpec((1, rows, GROUP_WIDTH), lambda b, r, i: (b, i, r))
    halo = pl.BlockSpec((1, ATTN_BLOCK, GROUP_WIDTH), lambda b, r, i: (b, jnp.maximum(i * per - 1, 0), r))
    return pl.pallas_call(
        _k2_kernel,
        grid=(B, dil, L // rows),
        in_specs=[main, main, halo, main, halo, _const_spec(bias.shape)],
        out_specs=[main, main],
        out_shape=[jax.ShapeDtypeStruct(q.shape, BF16), jax.ShapeDtypeStruct(q.shape, F32)],
        scratch_shapes=[pltpu.VMEM((ATTN_BLOCK + rows, GROUP_WIDTH), BF16)] * 2,
        compiler_params=_params(("arbitrary", "arbitrary", "arbitrary")),
        name=f"attn_prompt_d{dil}",
    )(q, k, k, v, v, bias)


def _k2s_kernel(n_steps, q_ref, c0, c1, c2, t0, t1, t2, bc0, bc1, bc2, bn0, bn1, bn2,
                nc0, nc1, nc2, o0, o1, o2, l0, l1, l2):
    lane = lax.broadcasted_iota(jnp.int32, (2 * GROUP_WIDTH, 128), 1)
    keep = 128 - n_steps
    groups = ((c0, t0, bc0, bn0, nc0, o0, l0), (c1, t1, bc1, bn1, nc1, o1, l1), (c2, t2, bc2, bn2, nc2, o2, l2))
    for g, (c_ref, t_ref, bc_ref, bn_ref, nc_ref, o_ref, l_ref) in enumerate(groups):
        lc = c_ref.shape[2]
        n_tiles = lc // 128
        new_rows = t_ref[0]
        tail = jnp.concatenate([jnp.zeros((128 - new_rows.shape[0], new_rows.shape[1]), F32), new_rows], axis=0).T
        cur = pltpu.roll(c_ref[0, :, 0:128], keep, 1)
        for c in range(n_tiles):
            nxt = pltpu.roll(c_ref[0, :, (c + 1) * 128:(c + 2) * 128], keep, 1) if c + 1 < n_tiles else tail
            nc_ref[0, :, c * 128:(c + 1) * 128] = jnp.where(lane < keep, cur, nxt)
            cur = nxt
        for h in range(HEADS_PER_GROUP):
            lo = g * GROUP_WIDTH + h * HEAD_DIM
            qh = q_ref[0, :, lo:lo + HEAD_DIM].astype(BF16)
            kh = c_ref[0, h * HEAD_DIM:(h + 1) * HEAD_DIM, :].astype(BF16)
            vh = c_ref[0, GROUP_WIDTH + h * HEAD_DIM:GROUP_WIDTH + (h + 1) * HEAD_DIM, :].astype(BF16)
            kt = tail[h * HEAD_DIM:(h + 1) * HEAD_DIM, :].astype(BF16)
            vt = tail[GROUP_WIDTH + h * HEAD_DIM:GROUP_WIDTH + (h + 1) * HEAD_DIM, :].astype(BF16)
            sc = jnp.dot(qh, kh, preferred_element_type=F32) + bc_ref[h]
            sn = jnp.dot(qh, kt, preferred_element_type=F32) + bn_ref[h]
            m = jnp.maximum(jnp.max(sc, axis=-1, keepdims=True), jnp.max(sn, axis=-1, keepdims=True))
            pc = jnp.exp(sc - m)
            pn = jnp.exp(sn - m)
            l = jnp.sum(pc, axis=-1, keepdims=True) + jnp.sum(pn, axis=-1, keepdims=True)
            nt_dims = (((1,), (1,)), ((), ()))
            o = (lax.dot_general(pc.astype(BF16), vh, nt_dims, preferred_element_type=F32)
                 + lax.dot_general(pn.astype(BF16), vt, nt_dims, preferred_element_type=F32)) * (1.0 / l)
            o_ref[0, :, h * HEAD_DIM:(h + 1) * HEAD_DIM] = o
            l_ref[0, :, h * HEAD_DIM:(h + 1) * HEAD_DIM] = jnp.broadcast_to(m + jnp.log(l), o.shape)


def _k2_sample(q_b, caches, tails, biases_c, biases_n, n_steps):
    bd, q_rows, _ = q_b.shape
    per_b = lambda shape: pl.BlockSpec((1,) + shape[1:], lambda b: (b,) + (0,) * (len(shape) - 1))
    ins = [q_b] + list(caches) + list(tails) + list(biases_c) + list(biases_n)
    in_specs = ([per_b(q_b.shape)] + [per_b(c.shape) for c in caches] + [per_b(t.shape) for t in tails]
                + [_const_spec(b.shape) for b in biases_c] + [_const_spec(b.shape) for b in biases_n])
    o_shape = jax.ShapeDtypeStruct((bd, q_rows, GROUP_WIDTH), F32)
    out_shape = [jax.ShapeDtypeStruct(c.shape, F32) for c in caches] + [o_shape] * 6
    out_specs = [per_b(c.shape) for c in caches] + [per_b(o_shape.shape)] * 6
    return pl.pallas_call(
        functools.partial(_k2s_kernel, n_steps),
        grid=(bd,),
        in_specs=in_specs,
        out_specs=out_specs,
        out_shape=out_shape,
        compiler_params=_params(("arbitrary",)),
        name="attn_sample",
    )(*ins)


def _k3_kernel(x_ref, o0, o1, o2, l0, l1, l2, ga_ref, sgb_ref, g1_ref, sh2_ref, sc2_ref, gf_ref,
               wao_ref, wo_ref, wr_ref, br_ref, cnt_in_ref,
               x2_ref, h2_ref, eid_ref, wts_ref, rank_ref, cnt_ref, carry, *stage_refs):
    @pl.when(pl.program_id(0) == 0)
    def _():
        carry[...] = cnt_in_ref[...]

    dils = tuple(x_ref.shape[0] // r.shape[-2] for r in (o0, o1, o2))
    pairs = [stage_refs[j:j + STAGE_HALVES] for j in range(0, len(stage_refs), STAGE_HALVES)]
    os_ = [_from_streams(r, d, pairs[2 * g]) for g, (r, d) in enumerate(zip((o0, o1, o2), dils))]
    ls = [_from_streams(r, d, pairs[2 * g + 1]) for g, (r, d) in enumerate(zip((l0, l1, l2), dils))]
    m = jnp.maximum(jnp.maximum(ls[0], ls[1]), ls[2])
    ws = [jnp.exp(l - m) for l in ls]
    den = ws[0] + ws[1] + ws[2]
    o = (ws[0] * os_[0] + ws[1] * os_[1] + ws[2] * os_[2]) / den
    b = jnp.dot(o.astype(BF16), wao_ref[...], preferred_element_type=F32)
    mixed = ga_ref[...].astype(F32) + sgb_ref[...].astype(F32) * b
    x2 = x_ref[...] + g1_ref[0] * jnp.dot(mixed.astype(BF16), wo_ref[...], preferred_element_type=F32)
    x2_ref[...] = x2
    h2 = _rms_modulate(x2, gf_ref[...], sc2_ref[0], sh2_ref[0])
    h2_ref[...] = h2
    lt = lax.dot_general(wr_ref[...], h2, (((1,), (1,)), ((), ())), preferred_element_type=F32,
                         precision=lax.Precision.HIGHEST) + br_ref[...]
    tm = h2.shape[0]
    gl = lt[0:8, :]
    gmax = jnp.max(gl, axis=0, keepdims=True)
    r8 = lax.broadcasted_iota(jnp.int32, (8, tm), 0)
    grp = jnp.min(jnp.where(gl == gmax, r8, 8), axis=0, keepdims=True)
    p_grp = 1.0 / jnp.sum(jnp.exp(gl - gmax), axis=0, keepdims=True)
    es = jnp.zeros((EXPERTS_PER_GROUP, tm), F32)
    for g in range(N_EXPERT_GROUPS):
        es = jnp.where(grp == g, lt[8 + 8 * g:16 + 8 * g, :], es)
    v1 = jnp.max(es, axis=0, keepdims=True)
    i1 = jnp.min(jnp.where(es == v1, r8, 8), axis=0, keepdims=True)
    rest = jnp.where(r8 == i1, -jnp.inf, es)
    v2 = jnp.max(rest, axis=0, keepdims=True)
    i2 = jnp.min(jnp.where(rest == v2, r8, 8), axis=0, keepdims=True)
    e21 = jnp.exp(v2 - v1)
    w1 = p_grp / (1.0 + e21)
    e1 = grp * EXPERTS_PER_GROUP + i1
    e2 = grp * EXPERTS_PER_GROUP + i2
    eid_ref[0, 0:1, :] = e1
    eid_ref[0, 1:2, :] = e2
    wts_ref[0, 0:1, :] = w1
    wts_ref[0, 1:2, :] = w1 * e21
    r_e = lax.broadcasted_iota(jnp.int32, (N_EXPERTS, tm), 0)
    hit1 = r_e == e1
    hit2 = r_e == e2
    both = jnp.where(hit1 | hit2, 1.0, 0.0)
    s_idx = lax.broadcasted_iota(jnp.int32, (tm, tm), 0)
    t_idx = lax.broadcasted_iota(jnp.int32, (tm, tm), 1)
    before = jnp.where(s_idx < t_idx, 1.0, 0.0).astype(BF16)
    base = carry[...] + jnp.dot(both.astype(BF16), before, preferred_element_type=F32)
    rank_ref[0, 0:1, :] = jnp.sum(jnp.where(hit1, base, 0.0), axis=0, keepdims=True).astype(jnp.int32)
    rank_ref[0, 1:2, :] = jnp.sum(jnp.where(hit2, base, 0.0), axis=0, keepdims=True).astype(jnp.int32)
    total = carry[...] + jnp.sum(both, axis=1, keepdims=True)
    carry[...] = total
    cnt_ref[...] = total


def _k3(x, o_l, g1, sh2, sc2, ga, sgb, norm_ffn_g, w_ao_b, w_o_b, w_r, b_r, cnt_in, tiles_per_mod):
    T = x.shape[0]
    tm = min(TOKEN_TILE, T)
    nt = T // tm
    tok = lambda w: pl.BlockSpec((tm, w), lambda t: (t, 0))
    mod = pl.BlockSpec((1,) + g1.shape[1:], lambda t: (t // tiles_per_mod, 0, 0))
    small = pl.BlockSpec((1, 2, tm), lambda t: (t, 0, 0))
    small_i = jax.ShapeDtypeStruct((nt, 2, tm), jnp.int32)

    def attn_spec(a):
        if a.ndim == 2:
            return tok(GROUP_WIDTH)
        dil = a.shape[2] // GROUP_WIDTH
        return pl.BlockSpec((1, tm // dil, a.shape[2]), lambda t: (t // tiles_per_mod, t % tiles_per_mod, 0))

    attn_in = [o_l[0][0], o_l[1][0], o_l[2][0], o_l[0][1], o_l[1][1], o_l[2][1]]
    return pl.pallas_call(
        _k3_kernel,
        grid=(nt,),
        in_specs=[tok(D_MODEL)] + [attn_spec(a) for a in attn_in] + [tok(D_MODEL)] * 2 + [mod] * 3
                 + [_const_spec((1, D_MODEL)), _weight_spec(w_ao_b.shape), _weight_spec(w_o_b.shape),
                    _const_spec(w_r.shape), _const_spec(b_r.shape), _const_spec((N_EXPERTS, 1))],
        out_specs=[tok(D_MODEL), tok(D_MODEL), small, small, small, _const_spec((N_EXPERTS, 1))],
        out_shape=[jax.ShapeDtypeStruct((T, D_MODEL), F32), jax.ShapeDtypeStruct((T, D_MODEL), F32),
                   small_i, jax.ShapeDtypeStruct((nt, 2, tm), F32), small_i,
                   jax.ShapeDtypeStruct((N_EXPERTS, 1), F32)],
        scratch_shapes=([pltpu.VMEM((N_EXPERTS, 1), F32)]
                        + [pltpu.VMEM((tm, LANES), F32)] * (2 * N_GROUPS * STAGE_HALVES)),
        compiler_params=_params(("arbitrary",)),
        name="merge_router",
    )(x, *attn_in, ga, sgb, g1, sh2, sc2, norm_ffn_g, w_ao_b, w_o_b, w_r, b_r, cnt_in)


def _rows_wait(n_rows, hbm, vmem, sem):
    pltpu.make_async_copy(hbm.at[pl.ds(0, n_rows)], vmem, sem).wait()


def _dispatch_kernel(n_tiles, n_first, dest_ref, fill_ref, ha_ref, hb_ref, xs_ref, buf, sem, zbuf, zsem):
    i = pl.program_id(0)
    slot = i % 2 if n_tiles > 1 else 0
    tm = ha_ref.shape[0]

    def drain(s):
        for _ in range(2):
            _rows_wait(tm, xs_ref, buf.at[s], sem.at[s])

    @pl.when(i == 0)
    def _():
        zbuf[...] = jnp.zeros(zbuf.shape, F32)

        def fill(row):
            return pltpu.make_async_copy(zbuf, xs_ref.at[pl.ds(pl.multiple_of(row, MOE_ROWS), MOE_ROWS)], zsem)

        n_blocks = xs_ref.shape[0] // MOE_ROWS
        n_used = fill_ref[N_EXPERTS]

        def start_block(j, carry):
            fill(j * MOE_ROWS).start()
            return carry

        def wait_block(j, carry):
            fill(j * MOE_ROWS).wait()
            return carry

        for e in range(N_EXPERTS):
            @pl.when(fill_ref[e] >= 0)
            def _(e=e):
                fill(fill_ref[e]).start()
        lax.fori_loop(n_used, n_blocks, start_block, 0)
        for e in range(N_EXPERTS):
            @pl.when(fill_ref[e] >= 0)
            def _(e=e):
                fill(fill_ref[e]).wait()
        lax.fori_loop(n_used, n_blocks, wait_block, 0)

    if n_tiles > 2:
        @pl.when(i >= 2)
        def _():
            drain(slot)

    @pl.when(i < n_first)
    def _():
        buf[slot] = ha_ref[...]

    @pl.when(i >= n_first)
    def _():
        buf[slot] = hb_ref[...]

    for r in range(tm):
        for k in range(2):
            pltpu.make_async_copy(buf.at[slot, pl.ds(r, 1)], xs_ref.at[pl.ds(dest_ref[0, k, r], 1)], sem.at[slot]).start()

    if n_tiles == 1:
        drain(0)
    else:
        @pl.when(i == n_tiles - 1)
        def _():
            drain(slot)
            drain(1 - slot)


def _dispatch(dest, fill_rows, h_a, h_b, n_slots):
    nt, _, tm = dest.shape
    n_first = h_a.shape[0] // tm
    assert h_a.shape[0] % tm == 0 and h_b.shape[0] == (nt - n_first) * tm
    return pl.pallas_call(
        functools.partial(_dispatch_kernel, nt, n_first),
        grid=(nt,),
        in_specs=[pl.BlockSpec((1, 2, tm), lambda t: (t, 0, 0), memory_space=pltpu.SMEM),
                  pl.BlockSpec(memory_space=pltpu.SMEM),
                  pl.BlockSpec((tm, D_MODEL), lambda t: (jnp.minimum(t, n_first - 1), 0)),
                  pl.BlockSpec((tm, D_MODEL), lambda t: (jnp.maximum(t - n_first, 0), 0))],
        out_specs=pl.BlockSpec(memory_space=pl.ANY),
        out_shape=jax.ShapeDtypeStruct((n_slots, D_MODEL), F32),
        scratch_shapes=[pltpu.VMEM((2, tm, D_MODEL), F32), pltpu.SemaphoreType.DMA((2,)),
                        pltpu.VMEM((MOE_ROWS, D_MODEL), F32), pltpu.SemaphoreType.DMA(())],
        compiler_params=_params(("arbitrary",)),
        name="dispatch_rows",
    )(dest, fill_rows, h_a, h_b)


def _k4_kernel(be_ref, nused_ref, x_ref, wg_ref, wu_ref, wd_ref, y_ref, wg_b, wu_b, wd_b):
    i = pl.program_id(0)

    @pl.when((i == 0) | (be_ref[i] != be_ref[jnp.maximum(i - 1, 0)]))
    def _():
        wg_b[...] = wg_ref[0].astype(BF16)
        wu_b[...] = wu_ref[0].astype(BF16)
        wd_b[...] = wd_ref[0].astype(BF16)

    @pl.when(i < nused_ref[0])
    def _():
        xb = x_ref[...].astype(BF16)
        gate = jnp.dot(xb, wg_b[...], preferred_element_type=F32)
        up = jnp.dot(xb, wu_b[...], preferred_element_type=F32)
        mid = (_silu(gate) * up).astype(BF16)
        y_ref[...] = jnp.dot(mid, wd_b[...], preferred_element_type=F32)

    @pl.when(i >= nused_ref[0])
    def _():
        y_ref[...] = jnp.zeros(y_ref.shape, F32)


def _k4(block_expert, n_used, xs, w_gate, w_up, w_down):
    n_blocks = block_expert.shape[0]
    rows = MOE_ROWS
    grid_spec = pltpu.PrefetchScalarGridSpec(
        num_scalar_prefetch=2,
        grid=(n_blocks,),
        in_specs=[pl.BlockSpec((rows, D_MODEL), lambda i, be, nu: (jnp.minimum(i, nu[0] - 1), 0)),
                  pl.BlockSpec((1, D_MODEL, D_FF_EXPERT), lambda i, be, nu: (be[i], 0, 0)),
                  pl.BlockSpec((1, D_MODEL, D_FF_EXPERT), lambda i, be, nu: (be[i], 0, 0)),
                  pl.BlockSpec((1, D_FF_EXPERT, D_MODEL), lambda i, be, nu: (be[i], 0, 0))],
        out_specs=pl.BlockSpec((rows, D_MODEL), lambda i, be, nu: (i, 0)),
        scratch_shapes=[pltpu.VMEM((D_MODEL, D_FF_EXPERT), BF16), pltpu.VMEM((D_MODEL, D_FF_EXPERT), BF16),
                        pltpu.VMEM((D_FF_EXPERT, D_MODEL), BF16)],
    )
    return pl.pallas_call(
        _k4_kernel,
        grid_spec=grid_spec,
        out_shape=jax.ShapeDtypeStruct((n_blocks * rows, D_MODEL), F32),
        compiler_params=_params(("arbitrary",)),
        name="expert_blocks",
    )(block_expert, n_used, xs, w_gate, w_up, w_down)


def _k5_kernel(n_tiles, idx_ref, x2_ref, w_ref, g2_ref, gfin_ref, ys_hbm, y_ref, ybuf, sem):
    i = pl.program_id(0)
    tm = x2_ref.shape[0]

    @pl.when(i < n_tiles)
    def _():
        slot = i % 2
        for r in range(tm):
            for k in range(2):
                pltpu.make_async_copy(ys_hbm.at[pl.ds(idx_ref[0, k, r], 1)], ybuf.at[slot, k, pl.ds(r, 1)],
                                      sem.at[slot]).start()

    @pl.when(i >= 1)
    def _():
        slot = (i - 1) % 2
        for k in range(2):
            _rows_wait(tm, ys_hbm, ybuf.at[slot, k], sem.at[slot])
        w = w_ref[...]
        f = ybuf[slot, 0] * w[:, 0:1] + ybuf[slot, 1] * w[:, 1:2]
        y = x2_ref[...] + g2_ref[0] * f
        r = lax.rsqrt(jnp.mean(y * y, axis=-1, keepdims=True) + EPS)
        y_ref[...] = (y * r) * gfin_ref[...]


def _k5(dest, x2, wts, g2, norm_final_g, ys, tiles_per_mod):
    nt, _, tm = dest.shape
    T = x2.shape[0]
    prev = lambda t: jnp.maximum(t - 1, 0)
    return pl.pallas_call(
        functools.partial(_k5_kernel, nt),
        grid=(nt + 1,),
        in_specs=[pl.BlockSpec((1, 2, tm), lambda t: (jnp.minimum(t, nt - 1), 0, 0), memory_space=pltpu.SMEM),
                  pl.BlockSpec((tm, D_MODEL), lambda t: (prev(t), 0)),
                  pl.BlockSpec((tm, 2), lambda t: (prev(t), 0)),
                  pl.BlockSpec((1,) + g2.shape[1:], lambda t: (prev(t) // tiles_per_mod, 0, 0)),
                  _const_spec((1, D_MODEL)), pl.BlockSpec(memory_space=pl.ANY)],
        out_specs=pl.BlockSpec((tm, D_MODEL), lambda t: (prev(t), 0)),
        out_shape=jax.ShapeDtypeStruct((T, D_MODEL), F32),
        scratch_shapes=[pltpu.VMEM((2, 2, tm, D_MODEL), F32), pltpu.SemaphoreType.DMA((2,))],
        compiler_params=_params(("arbitrary",)),
        name="combine_norm",
    )(dest, x2, wts, g2, norm_final_g, ys)


def _slot_tables(counts, n_blocks):
    padded = (counts + MOE_ROWS - 1) // MOE_ROWS * MOE_ROWS
    pend = jnp.cumsum(padded)
    pstart = pend - padded
    block_lo = jnp.arange(n_blocks, dtype=jnp.int32) * MOE_ROWS
    block_expert = jnp.minimum(jnp.sum(pend[None, :] <= block_lo[:, None], axis=1), N_EXPERTS - 1).astype(jnp.int32)
    n_used = (pend[-1] // MOE_ROWS).astype(jnp.int32).reshape(1)
    fill_rows = jnp.concatenate([jnp.where(padded > 0, pend - MOE_ROWS, -1).astype(jnp.int32), n_used])
    return pstart.astype(jnp.int32), block_expert, n_used, fill_rows


def _slots(eid, rank, pstart):
    sel = eid[..., None] == jnp.arange(N_EXPERTS, dtype=jnp.int32)
    return jnp.sum(jnp.where(sel, pstart, 0), axis=-1).astype(jnp.int32) + rank


def _cache_view(cache):
    bd, lc = cache.shape[:2]
    return cache.transpose(0, 2, 3, 4, 1).reshape(bd, 2 * GROUP_WIDTH, lc)


def _cache_unview(view):
    bd, _, lc = view.shape
    return view.reshape(bd, 2, HEADS_PER_GROUP, HEAD_DIM, lc).transpose(0, 4, 1, 2, 3)


def kernel(x_prompt, x_sample, c_prompt, c_sample, cache_kv_w128, cache_kv_w512, cache_kv_w2048, state_conv, rel_bias, norm_mix_g, norm_ffn_g, w_mod, b_mod, w_in, dw_w, dw_b, ln_conv_g, ln_conv_b, w_conv_out, w_attn_out, w_out, w_router_group, b_router_group, w_router_expert, b_router_expert, w_exp_gate, w_exp_up, w_exp_down, norm_final_g):
    assert norm_mix_g.shape[0] == 1, "single layer"
    B, S, D = x_prompt.shape
    Bd, Td, _ = x_sample.shape
    caches_in = (cache_kv_w128[0], cache_kv_w512[0], cache_kv_w2048[0])
    for (win, dil), c in zip(DILATED_GROUPS, caches_in):
        assert c.shape[1] >= (win // dil) * dil and c.shape[1] % 128 == 0 and S % (dil * ATTN_BLOCK) == 0

    wi = w_in[0]
    q_cols = wi[:, :ATTN_WIDTH] * (HEAD_DIM ** -0.5)
    kv_cols = []
    for g in range(N_GROUPS):
        kv_cols += [wi[:, ATTN_WIDTH + g * GROUP_WIDTH:ATTN_WIDTH + (g + 1) * GROUP_WIDTH],
                    wi[:, 2 * ATTN_WIDTH + g * GROUP_WIDTH:2 * ATTN_WIDTH + (g + 1) * GROUP_WIDTH]]
    w_in_b = jnp.concatenate([q_cols] + kv_cols + [wi[:, 3 * ATTN_WIDTH:]], axis=1).astype(BF16)
    w_co_b = w_conv_out[0].astype(BF16)
    w_ao_b = w_attn_out[0].astype(BF16)
    w_o_b = w_out[0].astype(BF16)
    w_r = jnp.zeros((ROUTER_ROWS, D), F32)
    w_r = w_r.at[0:N_EXPERT_GROUPS].set(w_router_group[0].T)
    w_r = w_r.at[8:].set(w_router_expert[0].reshape(D, N_EXPERTS).T)
    b_r = jnp.full((ROUTER_ROWS, 1), NEG_INF, F32)
    b_r = b_r.at[0:N_EXPERT_GROUPS, 0].set(b_router_group[0])
    b_r = b_r.at[8:, 0].set(b_router_expert[0].reshape(N_EXPERTS))
    row = lambda v: v.reshape(1, -1)

    n_seq = B + Bd
    n_seq_pad = -(-n_seq // 16) * 16
    c_all = jnp.concatenate([c_prompt, c_sample, jnp.zeros((n_seq_pad - n_seq, D), F32)], axis=0)
    mod = _modulation(c_all, w_mod[0], row(b_mod[0]))
    mod_p = [mod[:B, j * D:(j + 1) * D].reshape(B, 1, D) for j in range(6)]
    mod_s = [jnp.tile(mod[B:B + Bd, j * D:(j + 1) * D], (Td, 1)) for j in range(6)]

    dw_w_rows = jnp.broadcast_to(dw_w[0][:, None, :], (CONV_WIDTH, SUBLANES, CONV_CH))
    dw_b_rows = jnp.broadcast_to(dw_b[0][None, :], (SUBLANES, CONV_CH))
    (q0, q1, q2, k0, v0, k1, v1, k2, v2, kvt0, kvt1, kvt2, ga, sgb, convp) = _k1_prompt(
        x_prompt, mod_p[0], mod_p[1], row(norm_mix_g[0]), w_in_b, dw_w_rows, dw_b_rows,
        row(ln_conv_g[0]), row(ln_conv_b[0]), w_co_b)
    o_l = []
    for g, ((win, dil), qg, kg, vg) in enumerate(zip(DILATED_GROUPS, (q0, q1, q2), (k0, k1, k2), (v0, v1, v2))):
        bias = _prompt_bias(rel_bias[:, g * HEADS_PER_GROUP:(g + 1) * HEADS_PER_GROUP], dil, win // dil)
        o_l.append(_k2_prompt(qg, kg, vg, bias, dil))
    xp = x_prompt.reshape(B * S, D)
    tm3 = min(TOKEN_TILE, B * S)
    x2, h2, eid, wts, rank, cnt_p = _k3(xp, o_l, mod_p[2], mod_p[3], mod_p[4], ga, sgb, row(norm_ffn_g[0]),
                                        w_ao_b, w_o_b, w_r, b_r, jnp.zeros((N_EXPERTS, 1), F32), S // tm3)
    kv_p = [_cache_unview(kvt)[None] for kvt in (kvt0, kvt1, kvt2)]
    conv_p = convp[:, CONV_HIST - (CONV_WIDTH - 1):, :][None]

    Ts = Td * Bd
    xs = x_sample.transpose(1, 0, 2).reshape(Ts, D)
    state_tm = state_conv[0].transpose(1, 0, 2)
    (sq0, sq1, sq2, sk0, sv0, sk1, sv1, sk2, sv2, ga_s, sgb_s, conv_tm) = _k1_sample(
        xs, mod_s[0], mod_s[1], row(norm_mix_g[0]), w_in_b, state_tm, dw_w[0], row(dw_b[0]),
        row(ln_conv_g[0]), row(ln_conv_b[0]), w_co_b, Td)
    q_rows = 16
    q_b = jnp.concatenate([sq0, sq1, sq2], axis=1).reshape(Td, Bd, ATTN_WIDTH).transpose(1, 0, 2)
    q_b = jnp.pad(q_b, ((0, 0), (0, q_rows - Td), (0, 0)))
    views, tails, biases_c, biases_n = [], [], [], []
    for g, ((win, dil), c, sk, sv) in enumerate(zip(DILATED_GROUPS, caches_in, (sk0, sk1, sk2), (sv0, sv1, sv2))):
        lc = c.shape[1]
        views.append(_cache_view(c))
        new_kv = jnp.concatenate([sk, sv], axis=1).reshape(Td, Bd, 2 * GROUP_WIDTH).transpose(1, 0, 2)
        tails.append(jnp.pad(new_kv, ((0, 0), (SUBLANES - Td, 0), (0, 0))))
        bc, bn = _sample_bias(rel_bias[:, g * HEADS_PER_GROUP:(g + 1) * HEADS_PER_GROUP], dil, win // dil, lc, Td, q_rows)
        biases_c.append(bc)
        biases_n.append(bn)
    (nc0, nc1, nc2, so0, so1, so2, sl0, sl1, sl2) = _k2_sample(q_b, views, tails, biases_c, biases_n, Td)
    to_tm = lambda a: a[:, :Td, :].transpose(1, 0, 2).reshape(Ts, GROUP_WIDTH)
    o_l_s = [(to_tm(so0), to_tm(sl0)), (to_tm(so1), to_tm(sl1)), (to_tm(so2), to_tm(sl2))]
    tm3s = min(TOKEN_TILE, Ts)
    tiled = lambda m: m.reshape(Ts // tm3s, tm3s, D)
    x2s, h2s, eid_s, wts_s, rank_s, cnt_all = _k3(xs, o_l_s, tiled(mod_s[2]), tiled(mod_s[3]), tiled(mod_s[4]),
                                                  ga_s, sgb_s, row(norm_ffn_g[0]), w_ao_b, w_o_b, w_r, b_r, cnt_p, 1)

    n_assign = 2 * (B * S + Ts)
    n_blocks = -(-n_assign // MOE_ROWS) + N_EXPERTS
    pstart, block_expert, n_used, fill_rows = _slot_tables(cnt_all[:, 0].astype(jnp.int32), n_blocks)
    dest_p = _slots(eid, rank, pstart)
    dest_s = _slots(eid_s, rank_s, pstart)
    n_slots = n_blocks * MOE_ROWS
    assert tm3s == tm3, "prompt and sample token tiles must match to share the dispatch"
    slots = _dispatch(jnp.concatenate([dest_p, dest_s], axis=0), fill_rows, h2, h2s, n_slots)
    ys = _k4(block_expert, n_used, slots, w_exp_gate[0], w_exp_up[0], w_exp_down[0])
    per_token = lambda a: a.transpose(0, 2, 1).reshape(-1, 2)
    y_prompt = _k5(dest_p, x2, per_token(wts), mod_p[5], row(norm_final_g), ys, S // tm3).reshape(B, S, D)
    y_s = _k5(dest_s, x2s, per_token(wts_s), tiled(mod_s[5]), row(norm_final_g), ys, 1)
    y_sample = y_s.reshape(Td, Bd, D).transpose(1, 0, 2)
    kv_s = [_cache_unview(nc)[None] for nc in (nc0, nc1, nc2)]
    conv_s = conv_tm.transpose(1, 0, 2)[None]

    return (y_prompt, y_sample, kv_p[0], kv_p[1], kv_p[2], conv_p,
            kv_s[0], kv_s[1], kv_s[2], conv_s)
```

```python
import functools
import math

import jax
import jax.numpy as jnp
from jax import lax
from jax.experimental import pallas as pl
from jax.experimental.pallas import tpu as pltpu

F32 = jnp.float32
BF16 = jnp.bfloat16

D_MODEL = 1024
HEAD_DIM = 64
HEADS_PER_GROUP = 4
GROUP_WIDTH = HEADS_PER_GROUP * HEAD_DIM
LANES = 128
SUBLANES = 8
CONV_CHUNK = 32
STAGE_HALVES = GROUP_WIDTH // LANES
DILATED_GROUPS = ((128, 1), (512, 4), (2048, 16))
N_GROUPS = len(DILATED_GROUPS)
ATTN_WIDTH = N_GROUPS * GROUP_WIDTH
CONV_CH = D_MODEL // 2
CONV_WIDTH = 31
CONV_HIST = 32
N_BUCKETS = 32
MAX_DISTANCE = 2048
N_EXPERT_GROUPS = 4
EXPERTS_PER_GROUP = 8
N_EXPERTS = N_EXPERT_GROUPS * EXPERTS_PER_GROUP
D_FF_EXPERT = D_MODEL // 2
EPS = 1e-6
NEG_INF = -1e30

COL_KV = ATTN_WIDTH
COL_ULIN = COL_KV + 2 * ATTN_WIDTH
COL_UGATE = COL_ULIN + CONV_CH
COL_GA = COL_UGATE + CONV_CH
COL_GB = COL_GA + D_MODEL
IN_COLS = COL_GB + D_MODEL

ROUTER_ROWS = 8 + N_EXPERTS

V7X_VMEM_LIMIT = 56 * 1024 * 1024
TOKEN_TILE = 512
ATTN_ROWS = 512
ATTN_BLOCK = 128
MOE_ROWS = 512


def _sigmoid(x):
    return 1.0 / (1.0 + jnp.exp(-x))


def _silu(x):
    return x * _sigmoid(x)


def _rms_modulate(x, g, sc, sh):
    r = lax.rsqrt(jnp.mean(x * x, axis=-1, keepdims=True) + EPS)
    return ((x * r) * g) * (1.0 + sc) + sh


def _conv_tail(y, g, b):
    mu = jnp.mean(y, axis=-1, keepdims=True)
    yc = y - mu
    var = jnp.mean(yc * yc, axis=-1, keepdims=True)
    return _silu((yc * lax.rsqrt(var + EPS)) * g + b)


def _params(semantics):
    return pltpu.CompilerParams(dimension_semantics=semantics, vmem_limit_bytes=V7X_VMEM_LIMIT)


def _const_spec(shape):
    nd = len(shape)
    return pl.BlockSpec(shape, lambda *_: (0,) * nd)


def _weight_spec(shape):
    nd = len(shape)
    return pl.BlockSpec(shape, lambda *_: (0,) * nd, pipeline_mode=pl.Buffered(1))


def _mod_kernel(c_ref, w_ref, b_ref, o_ref):
    s = _silu(c_ref[...]).astype(BF16)
    o_ref[...] = jnp.dot(s, w_ref[...].astype(BF16), preferred_element_type=F32) + b_ref[...]


def _modulation(c_all, w_mod, b_mod):
    rows = c_all.shape[0]
    n_out = w_mod.shape[1]
    chunk = D_MODEL
    return pl.pallas_call(
        _mod_kernel,
        grid=(n_out // chunk,),
        in_specs=[_const_spec((rows, D_MODEL)),
                  pl.BlockSpec((D_MODEL, chunk), lambda j: (0, j)),
                  pl.BlockSpec((1, chunk), lambda j: (0, j))],
        out_specs=pl.BlockSpec((rows, chunk), lambda j: (0, j)),
        out_shape=jax.ShapeDtypeStruct((rows, n_out), F32),
        compiler_params=_params(("arbitrary",)),
        name="modulation",
    )(c_all, w_mod, b_mod)


def _to_streams(ref, val, dil, stage):
    if dil == 1:
        ref[...] = val.astype(ref.dtype).reshape(ref.shape)
        return
    n = val.shape[0] // dil
    for c, half in enumerate(stage):
        half[...] = val[:, c * LANES:(c + 1) * LANES]
        for r in range(dil):
            lo = r * GROUP_WIDTH + c * LANES
            ref[0, :, lo:lo + LANES] = half[pl.ds(r, n, stride=dil), :].astype(ref.dtype)


def _from_streams(ref, dil, stage):
    if dil == 1:
        return ref[...].astype(F32).reshape(ref.shape[-2:])
    n = ref.shape[1]
    for c, half in enumerate(stage):
        for r in range(dil):
            lo = r * GROUP_WIDTH + c * LANES
            half[pl.ds(r, n, stride=dil), :] = ref[0, :, lo:lo + LANES].astype(F32)
    return jnp.concatenate([half[...] for half in stage], axis=1)


def _project_common(hb, win_ref, outs, dils, stages):
    (q_refs, k_refs, v_refs) = outs

    def proj(lo, width):
        return jnp.dot(hb, win_ref[:, lo:lo + width], preferred_element_type=F32)

    zq = proj(0, ATTN_WIDTH)
    for g in range(N_GROUPS):
        _to_streams(q_refs[g], zq[:, g * GROUP_WIDTH:(g + 1) * GROUP_WIDTH], dils[g], stages[g][0])
    zkvs = []
    for g in range(N_GROUPS):
        zkv = proj(COL_KV + 2 * GROUP_WIDTH * g, 2 * GROUP_WIDTH)
        _to_streams(k_refs[g], zkv[:, :GROUP_WIDTH], dils[g], stages[g][1])
        _to_streams(v_refs[g], zkv[:, GROUP_WIDTH:], dils[g], stages[g][2])
        zkvs.append(zkv)
    u = proj(COL_ULIN, CONV_CH) * _sigmoid(proj(COL_UGATE, CONV_CH))
    return zkvs, u, proj


def _k1_kernel(tail_rows, n_tiles,
               x_ref, sh_ref, sc_ref, g_ref, win_ref, dww_ref, dwb_ref, lng_ref, lnb_ref, wco_ref,
               q0, q1, q2, k0, v0, k1, v1, k2, v2, kvt0, kvt1, kvt2, ga_ref, sgb_ref, convp_ref, uext, sbuf, ushift, sga, *stage_refs):
    i = pl.program_id(1)
    tm = x_ref.shape[1]

    @pl.when(i == 0)
    def _():
        uext[0:CONV_HIST, :] = jnp.zeros((CONV_HIST, CONV_CH), F32)

    h = _rms_modulate(x_ref[0], g_ref[...], sc_ref[0], sh_ref[0])
    hb = h.astype(BF16)

    def proj(lo, width):
        return jnp.dot(hb, win_ref[:, lo:lo + width], preferred_element_type=F32)

    uext[CONV_HIST:CONV_HIST + tm, :] = proj(COL_ULIN, CONV_CH) * _sigmoid(proj(COL_UGATE, CONV_CH))
    base = CONV_HIST - (CONV_WIDTH - 1)
    span = tm + CONV_HIST - SUBLANES
    for b in range(1, SUBLANES):
        ushift[b - 1, 0:span, :] = uext[b:b + span, :]

    def conv_chunk(r0):
        accs = [dwb_ref[...]] * (CONV_CHUNK // SUBLANES)
        for k in range(CONV_WIDTH):
            b = (base + k) % SUBLANES
            lo = r0 + base + k - b
            w8 = dww_ref[k]
            for q in range(len(accs)):
                rows = slice(lo + q * SUBLANES, lo + (q + 1) * SUBLANES)
                src = uext[rows, :] if b == 0 else ushift[b - 1, rows, :]
                accs[q] = accs[q] + w8 * src
        s = _conv_tail(jnp.concatenate(accs, axis=0), lng_ref[...], lnb_ref[...])
        sbuf[r0:r0 + CONV_CHUNK, :] = s.astype(BF16)

    dils = tuple(dil for _, dil in DILATED_GROUPS)
    pairs = [stage_refs[j:j + STAGE_HALVES] for j in range(0, len(stage_refs), STAGE_HALVES)]
    stages = [(None,) * 3] + [pairs[3 * (g - 1):3 * g] for g in range(1, N_GROUPS)]
    segments = []
    for g, q_ref in enumerate((q0, q1, q2)):
        def put_q(z, g=g, q_ref=q_ref):
            _to_streams(q_ref, z, dils[g], stages[g][0])
        segments.append((g * GROUP_WIDTH, put_q))
    for g, (k_ref, v_ref, kvt) in enumerate(((k0, v0, kvt0), (k1, v1, kvt1), (k2, v2, kvt2))):
        for half, ref in enumerate((k_ref, v_ref)):
            def put_kv(z, g=g, half=half, ref=ref, kvt=kvt):
                _to_streams(ref, z, dils[g], stages[g][1 + half])
                tr = kvt.shape[2]

                @pl.when(i >= n_tiles - tail_rows[g] // tr)
                def _():
                    kvt[0, half * GROUP_WIDTH:(half + 1) * GROUP_WIDTH, :] = z[tm - tr:, :].T
            segments.append((COL_KV + (2 * g + half) * GROUP_WIDTH, put_kv))
    for c in range(D_MODEL // GROUP_WIDTH):
        def put_ga(z, c=c):
            sga[:, c * GROUP_WIDTH:(c + 1) * GROUP_WIDTH] = _sigmoid(z)
        segments.append((COL_GA + c * GROUP_WIDTH, put_ga))
    for c in range(D_MODEL // GROUP_WIDTH):
        def put_gb(z, c=c):
            sgb_ref[:, c * GROUP_WIDTH:(c + 1) * GROUP_WIDTH] = _sigmoid(z).astype(BF16)
        segments.append((COL_GB + c * GROUP_WIDTH, put_gb))

    n_qkv = 3 * N_GROUPS
    for lo, consume in segments[:n_qkv]:
        consume(proj(lo, GROUP_WIDTH))
    for r0 in range(0, tm, CONV_CHUNK):
        conv_chunk(r0)
    for lo, consume in segments[n_qkv:]:
        consume(proj(lo, GROUP_WIDTH))
    last = uext[tm:tm + CONV_HIST, :]
    uext[0:CONV_HIST, :] = last
    convp_ref[0] = last
    a = jnp.dot(sbuf[...], wco_ref[...], preferred_element_type=F32)
    ga_ref[...] = (sga[...] * a).astype(BF16)


def _k1_prompt(x, sh1, sc1, norm_g, w_in_b, dw_w, dw_b, ln_g, ln_b, w_co_b):
    B, S, _ = x.shape
    tm = min(TOKEN_TILE, S)
    nt = S // tm
    tail_rows = tuple(min(win, S) for win, _ in DILATED_GROUPS)
    tail_blk = tuple(min(t, tm) for t in tail_rows)

    def tok_spec(width):
        return pl.BlockSpec((tm, width), lambda b, i: (b * nt + i, 0))

    def tail_spec(g):
        first = nt - tail_rows[g] // tail_blk[g]
        return pl.BlockSpec((1, 2 * GROUP_WIDTH, tail_blk[g]), lambda b, i: (b, 0, jnp.maximum(i - first, 0)))

    def stream_shape(dil):
        return jax.ShapeDtypeStruct((B, S // dil, dil * GROUP_WIDTH), BF16)

    def stream_spec(dil):
        return pl.BlockSpec((1, tm // dil, dil * GROUP_WIDTH), lambda b, i: (b, i, 0))

    dils = [dil for _, dil in DILATED_GROUPS]
    qkv_order = [dils[0], dils[1], dils[2]] + [d for d in dils for _ in range(2)]
    mod_spec = pl.BlockSpec((1, 1, D_MODEL), lambda b, i: (b, 0, 0))
    out_shape = ([stream_shape(d) for d in qkv_order]
                 + [jax.ShapeDtypeStruct((B, 2 * GROUP_WIDTH, tail_rows[g]), F32) for g in range(N_GROUPS)]
                 + [jax.ShapeDtypeStruct((B * S, D_MODEL), BF16)] * 2
                 + [jax.ShapeDtypeStruct((B, CONV_HIST, CONV_CH), F32)])
    out_specs = ([stream_spec(d) for d in qkv_order] + [tail_spec(g) for g in range(N_GROUPS)]
                 + [tok_spec(D_MODEL)] * 2 + [pl.BlockSpec((1, CONV_HIST, CONV_CH), lambda b, i: (b, 0, 0))])
    n_stage = 3 * sum(1 for d in dils if d > 1)
    return pl.pallas_call(
        functools.partial(_k1_kernel, tail_rows, nt),
        grid=(B, nt),
        in_specs=[pl.BlockSpec((1, tm, D_MODEL), lambda b, i: (b, i, 0)), mod_spec, mod_spec,
                  _const_spec((1, D_MODEL)), _weight_spec((D_MODEL, IN_COLS)),
                  _const_spec((CONV_WIDTH, SUBLANES, CONV_CH)), _const_spec((SUBLANES, CONV_CH)),
                  _const_spec((1, CONV_CH)), _const_spec((1, CONV_CH)), _weight_spec((CONV_CH, D_MODEL))],
        out_specs=out_specs,
        out_shape=out_shape,
        scratch_shapes=([pltpu.VMEM((CONV_HIST + tm, CONV_CH), F32), pltpu.VMEM((tm, CONV_CH), BF16),
                         pltpu.VMEM((SUBLANES - 1, CONV_HIST + tm - SUBLANES, CONV_CH), F32),
                         pltpu.VMEM((tm, D_MODEL), F32)]
                        + [pltpu.VMEM((tm, LANES), F32)] * (n_stage * STAGE_HALVES)),
        compiler_params=_params(("arbitrary", "arbitrary")),
        name="inproj_prompt",
    )(x, sh1, sc1, norm_g, w_in_b, dw_w, dw_b, ln_g, ln_b, w_co_b)


def _k1s_kernel(n_steps, x_ref, sh_ref, sc_ref, g_ref, win_ref, st_ref, dww_ref, dwb_ref, lng_ref, lnb_ref, wco_ref,
                q0, q1, q2, k0, v0, k1, v1, k2, v2, ga_ref, sgb_ref, conv_ref):
    bd = st_ref.shape[1]
    hist = st_ref.shape[0]
    h = _rms_modulate(x_ref[...], g_ref[...], sc_ref[...], sh_ref[...])
    hb = h.astype(BF16)
    _, u, proj = _project_common(hb, win_ref, ((q0, q1, q2), (k0, k1, k2), (v0, v1, v2)),
                                 (1,) * N_GROUPS, [(None,) * 3] * N_GROUPS)

    def ext(j):
        return st_ref[j] if j < hist else u[(j - hist) * bd:(j - hist + 1) * bd, :]

    outs = []
    for t in range(n_steps):
        acc = jnp.zeros((bd, CONV_CH), F32) + dwb_ref[...]
        for k in range(CONV_WIDTH):
            acc = acc + dww_ref[k:k + 1, :] * ext(t + k + hist - (CONV_WIDTH - 1))
        outs.append(acc)
    for j in range(hist):
        conv_ref[j] = ext(j + n_steps)
    s = _conv_tail(jnp.concatenate(outs, axis=0), lng_ref[...], lnb_ref[...])
    a = jnp.dot(s.astype(BF16), wco_ref[...], preferred_element_type=F32)
    ga_ref[...] = (_sigmoid(proj(COL_GA, D_MODEL)) * a).astype(BF16)
    sgb_ref[...] = _sigmoid(proj(COL_GB, D_MODEL)).astype(BF16)


def _k1_sample(x_tm, sh1, sc1, norm_g, w_in_b, state_tm, dw_w, dw_b, ln_g, ln_b, w_co_b, n_steps):
    T = x_tm.shape[0]
    hist, bd, _ = state_tm.shape
    out_shape = ([jax.ShapeDtypeStruct((T, GROUP_WIDTH), F32)] * 9
                 + [jax.ShapeDtypeStruct((T, D_MODEL), BF16)] * 2
                 + [jax.ShapeDtypeStruct((hist, bd, CONV_CH), F32)])
    out_specs = ([_const_spec((T, GROUP_WIDTH))] * 9 + [_const_spec((T, D_MODEL))] * 2
                 + [_const_spec((hist, bd, CONV_CH))])
    return pl.pallas_call(
        functools.partial(_k1s_kernel, n_steps),
        grid=(1,),
        in_specs=[_const_spec((T, D_MODEL)), _const_spec((T, D_MODEL)), _const_spec((T, D_MODEL)),
                  _const_spec((1, D_MODEL)), _const_spec((D_MODEL, IN_COLS)), _const_spec((hist, bd, CONV_CH)),
                  _const_spec((CONV_WIDTH, CONV_CH)), _const_spec((1, CONV_CH)),
                  _const_spec((1, CONV_CH)), _const_spec((1, CONV_CH)), _const_spec((CONV_CH, D_MODEL))],
        out_specs=out_specs,
        out_shape=out_shape,
        compiler_params=_params(("arbitrary",)),
        name="inproj_sample",
    )(x_tm, sh1, sc1, norm_g, w_in_b, state_tm, dw_w, dw_b, ln_g, ln_b, w_co_b)


def _t5_bucket(dist):
    max_exact = N_BUCKETS // 2
    d_f = jnp.maximum(dist, 1).astype(F32)
    large = max_exact + (jnp.log(d_f / max_exact) / math.log(MAX_DISTANCE / max_exact)
                         * (N_BUCKETS - max_exact)).astype(jnp.int32)
    large = jnp.minimum(large, N_BUCKETS - 1)
    return jnp.where(dist < max_exact, dist, large)


def _bucket_lookup(rel_bias_g, dist):
    bucket = _t5_bucket(dist)
    out = jnp.zeros((rel_bias_g.shape[1],) + dist.shape, F32)
    for b in range(N_BUCKETS):
        out = jnp.where(bucket[None] == b, rel_bias_g[b].reshape((-1,) + (1,) * dist.ndim), out)
    return out


def _prompt_bias(rel_bias_g, dil, n_keys):
    blk = n_keys
    i = jnp.arange(blk)[:, None]
    j = jnp.arange(2 * blk)[None, :]
    rel = i - j + blk
    valid = (rel >= 0) & (rel <= n_keys)
    bias = _bucket_lookup(rel_bias_g, jnp.clip(rel, 0, n_keys) * dil)
    bias = jnp.where(valid[None], bias, NEG_INF)
    return bias.reshape(HEADS_PER_GROUP * blk, 2 * blk).astype(F32)


def _sample_bias(rel_bias_g, dil, n_keys, lc, n_steps, q_rows):
    t = jnp.arange(q_rows)[:, None]
    pos = jnp.arange(lc)[None, :]
    dist_c = lc + t - pos
    dist_n = t - (jnp.arange(128)[None, :] - (128 - n_steps))

    def table(dist, extra):
        ok = (dist >= 0) & (dist % dil == 0) & (dist // dil <= n_keys) & extra & (t < n_steps)
        b = _bucket_lookup(rel_bias_g, jnp.clip(dist, 0, None))
        b = jnp.where(ok[None], b, NEG_INF)
        return jnp.where((t >= n_steps)[None], 0.0, b).astype(F32)

    lane_ok = jnp.arange(128)[None, :] >= 128 - n_steps
    return table(dist_c, True), table(dist_n, lane_ok)


def _k2_kernel(q_ref, k_ref, kh_ref, v_ref, vh_ref, bias_ref, o_ref, lse_ref, kbuf, vbuf):
    i = pl.program_id(2)
    rows = q_ref.shape[1]
    blk = ATTN_BLOCK
    kbuf[0:blk, :] = kh_ref[0]
    kbuf[blk:blk + rows, :] = k_ref[0]
    vbuf[0:blk, :] = vh_ref[0]
    vbuf[blk:blk + rows, :] = v_ref[0]
    lane_head = lax.broadcasted_iota(jnp.int32, (blk, GROUP_WIDTH), 1) // HEAD_DIM
    col = lax.broadcasted_iota(jnp.int32, (HEADS_PER_GROUP * blk, 2 * blk), 1)
    first_mask = jnp.where((col < blk) & (i == 0), NEG_INF, 0.0).astype(F32)
    bias = bias_ref[...]
    for j in range(rows // blk):
        qb = q_ref[0, j * blk:(j + 1) * blk, :]
        q4 = jnp.concatenate([jnp.where(lane_head == h, qb, jnp.zeros_like(qb)) for h in range(HEADS_PER_GROUP)], axis=0)
        kc = kbuf[j * blk:(j + 2) * blk, :]
        vc = vbuf[j * blk:(j + 2) * blk, :]
        s = lax.dot_general(q4, kc, (((1,), (1,)), ((), ())), preferred_element_type=F32) + bias
        if j == 0:
            s = s + first_mask
        m = jnp.max(s, axis=-1, keepdims=True)
        p = jnp.exp(s - m)
        l = jnp.sum(p, axis=-1, keepdims=True)
        o4 = jnp.dot(p.astype(BF16), vc, preferred_element_type=F32) * (1.0 / l)
        lse4 = m + jnp.log(l)
        o = jnp.zeros((blk, GROUP_WIDTH), F32)
        lse = jnp.zeros((blk, GROUP_WIDTH), F32)
        for h in range(HEADS_PER_GROUP):
            sel = lane_head == h
            o = jnp.where(sel, o4[h * blk:(h + 1) * blk, :], o)
            lse = jnp.where(sel, lse4[h * blk:(h + 1) * blk, :], lse)
        o_ref[0, j * blk:(j + 1) * blk, :] = o.astype(o_ref.dtype)
        lse_ref[0, j * blk:(j + 1) * blk, :] = lse


def _k2_prompt(q, k, v, bias, dil):
    B, L, _ = q.shape
    rows = min(ATTN_ROWS, L)
    per = rows // ATTN_BLOCK
    main = pl.BlockSpec((1, rows, GROUP_WIDTH), lambda b, r, i: (b, i, r))
    halo = pl.BlockSpec((1, ATTN_BLOCK, GROUP_WIDTH), lambda b, r, i: (b, jnp.maximum(i * per - 1, 0), r))
    return pl.pallas_call(
        _k2_kernel,
        grid=(B, dil, L // rows),
        in_specs=[main, main, halo, main, halo, _const_spec(bias.shape)],
        out_specs=[main, main],
        out_shape=[jax.ShapeDtypeStruct(q.shape, BF16), jax.ShapeDtypeStruct(q.shape, F32)],
        scratch_shapes=[pltpu.VMEM((ATTN_BLOCK + rows, GROUP_WIDTH), BF16)] * 2,
        compiler_params=_params(("arbitrary", "arbitrary", "arbitrary")),
        name=f"attn_prompt_d{dil}",
    )(q, k, k, v, v, bias)


def _k2s_kernel(n_steps, q_ref, c0, c1, c2, t0, t1, t2, bc0, bc1, bc2, bn0, bn1, bn2,
                nc0, nc1, nc2, o0, o1, o2, l0, l1, l2):
    lane = lax.broadcasted_iota(jnp.int32, (2 * GROUP_WIDTH, 128), 1)
    keep = 128 - n_steps
    groups = ((c0, t0, bc0, bn0, nc0, o0, l0), (c1, t1, bc1, bn1, nc1, o1, l1), (c2, t2, bc2, bn2, nc2, o2, l2))
    for g, (c_ref, t_ref, bc_ref, bn_ref, nc_ref, o_ref, l_ref) in enumerate(groups):
        lc = c_ref.shape[2]
        n_tiles = lc // 128
        new_rows = t_ref[0]
        tail = jnp.concatenate([jnp.zeros((128 - new_rows.shape[0], new_rows.shape[1]), F32), new_rows], axis=0).T
        cur = pltpu.roll(c_ref[0, :, 0:128], keep, 1)
        for c in range(n_tiles):
            nxt = pltpu.roll(c_ref[0, :, (c + 1) * 128:(c + 2) * 128], keep, 1) if c + 1 < n_tiles else tail
            nc_ref[0, :, c * 128:(c + 1) * 128] = jnp.where(lane < keep, cur, nxt)
            cur = nxt
        for h in range(HEADS_PER_GROUP):
            lo = g * GROUP_WIDTH + h * HEAD_DIM
            qh = q_ref[0, :, lo:lo + HEAD_DIM].astype(BF16)
            kh = c_ref[0, h * HEAD_DIM:(h + 1) * HEAD_DIM, :].astype(BF16)
            vh = c_ref[0, GROUP_WIDTH + h * HEAD_DIM:GROUP_WIDTH + (h + 1) * HEAD_DIM, :].astype(BF16)
            kt = tail[h * HEAD_DIM:(h + 1) * HEAD_DIM, :].astype(BF16)
            vt = tail[GROUP_WIDTH + h * HEAD_DIM:GROUP_WIDTH + (h + 1) * HEAD_DIM, :].astype(BF16)
            sc = jnp.dot(qh, kh, preferred_element_type=F32) + bc_ref[h]
            sn = jnp.dot(qh, kt, preferred_element_type=F32) + bn_ref[h]
            m = jnp.maximum(jnp.max(sc, axis=-1, keepdims=True), jnp.max(sn, axis=-1, keepdims=True))
            pc = jnp.exp(sc - m)
            pn = jnp.exp(sn - m)
            l = jnp.sum(pc, axis=-1, keepdims=True) + jnp.sum(pn, axis=-1, keepdims=True)
            nt_dims = (((1,), (1,)), ((), ()))
            o = (lax.dot_general(pc.astype(BF16), vh, nt_dims, preferred_element_type=F32)
                 + lax.dot_general(pn.astype(BF16), vt, nt_dims, preferred_element_type=F32)) * (1.0 / l)
            o_ref[0, :, h * HEAD_DIM:(h + 1) * HEAD_DIM] = o
            l_ref[0, :, h * HEAD_DIM:(h + 1) * HEAD_DIM] = jnp.broadcast_to(m + jnp.log(l), o.shape)


def _k2_sample(q_b, caches, tails, biases_c, biases_n, n_steps):
    bd, q_rows, _ = q_b.shape
    per_b = lambda shape: pl.BlockSpec((1,) + shape[1:], lambda b: (b,) + (0,) * (len(shape) - 1))
    ins = [q_b] + list(caches) + list(tails) + list(biases_c) + list(biases_n)
    in_specs = ([per_b(q_b.shape)] + [per_b(c.shape) for c in caches] + [per_b(t.shape) for t in tails]
                + [_const_spec(b.shape) for b in biases_c] + [_const_spec(b.shape) for b in biases_n])
    o_shape = jax.ShapeDtypeStruct((bd, q_rows, GROUP_WIDTH), F32)
    out_shape = [jax.ShapeDtypeStruct(c.shape, F32) for c in caches] + [o_shape] * 6
    out_specs = [per_b(c.shape) for c in caches] + [per_b(o_shape.shape)] * 6
    return pl.pallas_call(
        functools.partial(_k2s_kernel, n_steps),
        grid=(bd,),
        in_specs=in_specs,
        out_specs=out_specs,
        out_shape=out_shape,
        compiler_params=_params(("arbitrary",)),
        name="attn_sample",
    )(*ins)


def _k3_kernel(n_real, has_alias, *refs):
    if has_alias:
        refs = refs[:18] + refs[19:]
    h2_ref = refs[19]
    t = pl.program_id(0)

    @pl.when(t < n_real)
    def _():
        _k3_body(*refs)

    @pl.when(t >= n_real)
    def _():
        h2_ref[...] = jnp.zeros(h2_ref.shape, F32)


def _k3_body(x_ref, o0, o1, o2, l0, l1, l2, ga_ref, sgb_ref, g1_ref, sh2_ref, sc2_ref, gf_ref,
             wao_ref, wo_ref, wr_ref, br_ref, cnt_in_ref,
             x2_ref, h2_ref, eid_ref, wts_ref, rank_ref, cnt_ref, carry, *stage_refs):
    @pl.when(pl.program_id(0) == 0)
    def _():
        carry[...] = cnt_in_ref[...]

    dils = tuple(x_ref.shape[0] // r.shape[-2] for r in (o0, o1, o2))
    pairs = [stage_refs[j:j + STAGE_HALVES] for j in range(0, len(stage_refs), STAGE_HALVES)]
    os_ = [_from_streams(r, d, pairs[2 * g]) for g, (r, d) in enumerate(zip((o0, o1, o2), dils))]
    ls = [_from_streams(r, d, pairs[2 * g + 1]) for g, (r, d) in enumerate(zip((l0, l1, l2), dils))]
    m = jnp.maximum(jnp.maximum(ls[0], ls[1]), ls[2])
    ws = [jnp.exp(l - m) for l in ls]
    den = ws[0] + ws[1] + ws[2]
    o = (ws[0] * os_[0] + ws[1] * os_[1] + ws[2] * os_[2]) / den
    b = jnp.dot(o.astype(BF16), wao_ref[...], preferred_element_type=F32)
    mixed = ga_ref[...].astype(F32) + sgb_ref[...].astype(F32) * b
    x2 = x_ref[...] + g1_ref[0] * jnp.dot(mixed.astype(BF16), wo_ref[...], preferred_element_type=F32)
    x2_ref[...] = x2
    h2 = _rms_modulate(x2, gf_ref[...], sc2_ref[0], sh2_ref[0])
    h2_ref[...] = h2
    lt = lax.dot_general(wr_ref[...], h2, (((1,), (1,)), ((), ())), preferred_element_type=F32,
                         precision=lax.Precision.HIGHEST) + br_ref[...]
    tm = h2.shape[0]
    gl = lt[0:8, :]
    gmax = jnp.max(gl, axis=0, keepdims=True)
    r8 = lax.broadcasted_iota(jnp.int32, (8, tm), 0)
    grp = jnp.min(jnp.where(gl == gmax, r8, 8), axis=0, keepdims=True)
    p_grp = 1.0 / jnp.sum(jnp.exp(gl - gmax), axis=0, keepdims=True)
    es = jnp.zeros((EXPERTS_PER_GROUP, tm), F32)
    for g in range(N_EXPERT_GROUPS):
        es = jnp.where(grp == g, lt[8 + 8 * g:16 + 8 * g, :], es)
    v1 = jnp.max(es, axis=0, keepdims=True)
    i1 = jnp.min(jnp.where(es == v1, r8, 8), axis=0, keepdims=True)
    rest = jnp.where(r8 == i1, -jnp.inf, es)
    v2 = jnp.max(rest, axis=0, keepdims=True)
    i2 = jnp.min(jnp.where(rest == v2, r8, 8), axis=0, keepdims=True)
    e21 = jnp.exp(v2 - v1)
    w1 = p_grp / (1.0 + e21)
    e1 = grp * EXPERTS_PER_GROUP + i1
    e2 = grp * EXPERTS_PER_GROUP + i2
    eid_ref[0, 0:1, :] = e1
    eid_ref[0, 1:2, :] = e2
    wts_ref[0, 0:1, :] = w1
    wts_ref[0, 1:2, :] = w1 * e21
    r_e = lax.broadcasted_iota(jnp.int32, (N_EXPERTS, tm), 0)
    hit1 = r_e == e1
    hit2 = r_e == e2
    both = jnp.where(hit1 | hit2, 1.0, 0.0)
    s_idx = lax.broadcasted_iota(jnp.int32, (tm, tm), 0)
    t_idx = lax.broadcasted_iota(jnp.int32, (tm, tm), 1)
    before = jnp.where(s_idx < t_idx, 1.0, 0.0).astype(BF16)
    base = carry[...] + jnp.dot(both.astype(BF16), before, preferred_element_type=F32)
    rank_ref[0, 0:1, :] = jnp.sum(jnp.where(hit1, base, 0.0), axis=0, keepdims=True).astype(jnp.int32)
    rank_ref[0, 1:2, :] = jnp.sum(jnp.where(hit2, base, 0.0), axis=0, keepdims=True).astype(jnp.int32)
    total = carry[...] + jnp.sum(both, axis=1, keepdims=True)
    carry[...] = total
    cnt_ref[...] = total


def _k3(x, o_l, g1, sh2, sc2, ga, sgb, norm_ffn_g, w_ao_b, w_o_b, w_r, b_r, cnt_in, tiles_per_mod,
        h2_tiles, h2_tile0, h2_buf=None):
    T = x.shape[0]
    tm = min(TOKEN_TILE, T)
    nt = T // tm
    n_steps = nt if h2_buf is not None else h2_tiles - h2_tile0
    last = nt - 1
    tok = lambda w: pl.BlockSpec((tm, w), lambda t: (jnp.minimum(t, last), 0))
    mod = pl.BlockSpec((1,) + g1.shape[1:], lambda t: (jnp.minimum(t, last) // tiles_per_mod, 0, 0))
    small = pl.BlockSpec((1, 2, tm), lambda t: (jnp.minimum(t, last), 0, 0))
    small_i = jax.ShapeDtypeStruct((nt, 2, tm), jnp.int32)

    def attn_spec(a):
        if a.ndim == 2:
            return tok(GROUP_WIDTH)
        dil = a.shape[2] // GROUP_WIDTH
        return pl.BlockSpec((1, tm // dil, a.shape[2]),
                            lambda t: (jnp.minimum(t, last) // tiles_per_mod, jnp.minimum(t, last) % tiles_per_mod, 0))

    attn_in = [o_l[0][0], o_l[1][0], o_l[2][0], o_l[0][1], o_l[1][1], o_l[2][1]]
    in_specs = ([tok(D_MODEL)] + [attn_spec(a) for a in attn_in] + [tok(D_MODEL)] * 2 + [mod] * 3
                + [_const_spec((1, D_MODEL)), _weight_spec(w_ao_b.shape), _weight_spec(w_o_b.shape),
                   _const_spec(w_r.shape), _const_spec(b_r.shape), _const_spec((N_EXPERTS, 1))])
    args = [x, *attn_in, ga, sgb, g1, sh2, sc2, norm_ffn_g, w_ao_b, w_o_b, w_r, b_r, cnt_in]
    aliases = {}
    if h2_buf is not None:
        in_specs.append(pl.BlockSpec(memory_space=pl.ANY))
        args.append(h2_buf)
        aliases = {len(args) - 1: 1}
    return pl.pallas_call(
        functools.partial(_k3_kernel, nt, h2_buf is not None),
        grid=(n_steps,),
        in_specs=in_specs,
        out_specs=[tok(D_MODEL), pl.BlockSpec((tm, D_MODEL), lambda t: (h2_tile0 + t, 0)), small, small, small,
                   _const_spec((N_EXPERTS, 1))],
        out_shape=[jax.ShapeDtypeStruct((T, D_MODEL), F32), jax.ShapeDtypeStruct((h2_tiles * tm, D_MODEL), F32),
                   small_i, jax.ShapeDtypeStruct((nt, 2, tm), F32), small_i,
                   jax.ShapeDtypeStruct((N_EXPERTS, 1), F32)],
        input_output_aliases=aliases,
        scratch_shapes=([pltpu.VMEM((N_EXPERTS, 1), F32)]
                        + [pltpu.VMEM((tm, LANES), F32)] * (2 * N_GROUPS * STAGE_HALVES)),
        compiler_params=_params(("arbitrary",)),
        name="merge_router",
    )(*args)


def _rows_wait(n_rows, hbm, vmem, sem):
    pltpu.make_async_copy(hbm.at[pl.ds(0, n_rows)], vmem, sem).wait()


DUMP_ROWS = 2 * MOE_ROWS


def _invert_kernel(n_tokens, dest_ref, pad_ref, inv_ref):
    t = pl.program_id(0)
    tm = dest_ref.shape[2]

    @pl.when(t == 0)
    def _():
        def fill(lo, hi):
            def body(s, carry):
                inv_ref[s] = 2 * n_tokens + s % DUMP_ROWS
                return carry
            lax.fori_loop(lo, hi, body, 0)

        for e in range(N_EXPERTS + 1):
            fill(pad_ref[2 * e], pad_ref[2 * e + 1])

    for r in range(tm):
        for k in range(2):
            inv_ref[dest_ref[0, k, r]] = k * n_tokens + t * tm + r


def _invert(dest, pad_ranges, n_slots, n_tokens):
    nt, _, tm = dest.shape
    return pl.pallas_call(
        functools.partial(_invert_kernel, n_tokens),
        grid=(nt,),
        in_specs=[pl.BlockSpec((1, 2, tm), lambda t: (t, 0, 0), memory_space=pltpu.SMEM),
                  pl.BlockSpec(memory_space=pltpu.SMEM)],
        out_specs=pl.BlockSpec(memory_space=pltpu.SMEM),
        out_shape=jax.ShapeDtypeStruct((n_slots,), jnp.int32),
        compiler_params=_params(("arbitrary",)),
        name="slot_map",
    )(dest, pad_ranges)


def _moe_kernel(be_ref, nused_ref, src0_ref, src_next_ref, dst_prev_ref, h_hbm, wg_ref, wu_ref, wd_ref, y_hbm,
                xbuf, ybuf, gsem, ssem, wg_b, wu_b, wd_b):
    i = pl.program_id(0)
    n_used = nused_ref[0]
    rows = xbuf.shape[1]
    n_real_rows = y_hbm.shape[0] - DUMP_ROWS

    @pl.when(i == 0)
    def _():
        ybuf[...] = jnp.zeros(ybuf.shape, F32)
        for half in range(DUMP_ROWS // rows):
            fill = pltpu.make_async_copy(ybuf.at[0], y_hbm.at[pl.ds(n_real_rows + half * rows, rows)], ssem.at[0])
            fill.start()
            fill.wait()

        def first(r, carry):
            pltpu.make_async_copy(h_hbm.at[pl.ds(src0_ref[0, 0, r], 1)], xbuf.at[0, pl.ds(r, 1)], gsem.at[0]).start()
            return carry
        lax.fori_loop(0, rows, first, 0)

    @pl.when(i <= n_used)
    def _():
        slot = i % 2

        @pl.when(i >= 1)
        def _():
            _rows_wait(rows, y_hbm, ybuf.at[slot], ssem.at[slot])

        @pl.when((i == 0) | (be_ref[i] != be_ref[jnp.maximum(i - 1, 0)]))
        def _():
            wg_b[...] = wg_ref[0].astype(BF16)
            wu_b[...] = wu_ref[0].astype(BF16)
            wd_b[...] = wd_ref[0].astype(BF16)

        _rows_wait(rows, h_hbm, xbuf.at[slot], gsem.at[slot])
        for r in range(rows):
            pltpu.make_async_copy(h_hbm.at[pl.ds(src_next_ref[0, 0, r], 1)], xbuf.at[1 - slot, pl.ds(r, 1)],
                                  gsem.at[1 - slot]).start()
        xb = xbuf[slot].astype(BF16)
        gate = jnp.dot(xb, wg_b[...], preferred_element_type=F32)
        up = jnp.dot(xb, wu_b[...], preferred_element_type=F32)
        mid = (_silu(gate) * up).astype(BF16)
        ybuf[slot] = jnp.dot(mid, wd_b[...], preferred_element_type=F32)
        for r in range(rows):
            pltpu.make_async_copy(ybuf.at[1 - slot, pl.ds(r, 1)], y_hbm.at[pl.ds(dst_prev_ref[0, 0, r], 1)],
                                  ssem.at[1 - slot]).start()

        @pl.when(i == n_used)
        def _():
            _rows_wait(rows, h_hbm, xbuf.at[1 - slot], gsem.at[1 - slot])
            _rows_wait(rows, y_hbm, ybuf.at[1 - slot], ssem.at[1 - slot])


def _moe(block_expert, n_used, src, inv, h2, w_gate, w_up, w_down, n_tokens):
    n_blocks = block_expert.shape[0]
    rows = MOE_ROWS
    last = n_blocks - 1
    src3 = src.reshape(n_blocks, 1, rows)
    inv3 = inv.reshape(n_blocks, 1, rows)
    be_ext = jnp.concatenate([block_expert, block_expert[-1:]])
    smem = lambda f: pl.BlockSpec((1, 1, rows), f, memory_space=pltpu.SMEM)
    weight = lambda shape: pl.BlockSpec((1,) + shape, lambda i, be, nu: (be[i], 0, 0))
    grid_spec = pltpu.PrefetchScalarGridSpec(
        num_scalar_prefetch=2,
        grid=(n_blocks + 1,),
        in_specs=[smem(lambda i, be, nu: (0, 0, 0)),
                  smem(lambda i, be, nu: (jnp.minimum(i + 1, last), 0, 0)),
                  smem(lambda i, be, nu: (jnp.clip(i - 1, 0, last), 0, 0)),
                  pl.BlockSpec(memory_space=pl.ANY),
                  weight((D_MODEL, D_FF_EXPERT)), weight((D_MODEL, D_FF_EXPERT)), weight((D_FF_EXPERT, D_MODEL))],
        out_specs=pl.BlockSpec(memory_space=pl.ANY),
        scratch_shapes=[pltpu.VMEM((2, rows, D_MODEL), F32), pltpu.VMEM((2, rows, D_MODEL), F32),
                        pltpu.SemaphoreType.DMA((2,)), pltpu.SemaphoreType.DMA((2,)),
                        pltpu.VMEM((D_MODEL, D_FF_EXPERT), BF16), pltpu.VMEM((D_MODEL, D_FF_EXPERT), BF16),
                        pltpu.VMEM((D_FF_EXPERT, D_MODEL), BF16)],
    )
    return pl.pallas_call(
        _moe_kernel,
        grid_spec=grid_spec,
        out_shape=jax.ShapeDtypeStruct((2 * n_tokens + DUMP_ROWS, D_MODEL), F32),
        compiler_params=_params(("arbitrary",)),
        name="expert_blocks",
    )(be_ext, n_used, src3, src3, inv3, h2, w_gate, w_up, w_down)


def _combine_kernel(x2_ref, ya_ref, yb_ref, w_ref, g2_ref, gfin_ref, y_ref):
    w = w_ref[...]
    f = ya_ref[...] * w[:, 0:1] + yb_ref[...] * w[:, 1:2]
    y = x2_ref[...] + g2_ref[0] * f
    r = lax.rsqrt(jnp.mean(y * y, axis=-1, keepdims=True) + EPS)
    y_ref[...] = (y * r) * gfin_ref[...]


def _combine(x2, y2, wts, g2, norm_final_g, tiles_per_mod, tile0, n_tiles_all):
    T = x2.shape[0]
    tm = min(TOKEN_TILE, T)
    return pl.pallas_call(
        _combine_kernel,
        grid=(T // tm,),
        in_specs=[pl.BlockSpec((tm, D_MODEL), lambda t: (t, 0)),
                  pl.BlockSpec((tm, D_MODEL), lambda t: (tile0 + t, 0)),
                  pl.BlockSpec((tm, D_MODEL), lambda t: (n_tiles_all + tile0 + t, 0)),
                  pl.BlockSpec((tm, 2), lambda t: (t, 0)),
                  pl.BlockSpec((1,) + g2.shape[1:], lambda t: (t // tiles_per_mod, 0, 0)),
                  _const_spec((1, D_MODEL))],
        out_specs=pl.BlockSpec((tm, D_MODEL), lambda t: (t, 0)),
        out_shape=jax.ShapeDtypeStruct((T, D_MODEL), F32),
        compiler_params=_params(("arbitrary",)),
        name="combine_norm",
    )(x2, y2, y2, wts, g2, norm_final_g)


def _slot_tables(counts, n_blocks):
    padded = (counts + MOE_ROWS - 1) // MOE_ROWS * MOE_ROWS
    pend = jnp.cumsum(padded)
    pstart = pend - padded
    block_lo = jnp.arange(n_blocks, dtype=jnp.int32) * MOE_ROWS
    block_expert = jnp.minimum(jnp.sum(pend[None, :] <= block_lo[:, None], axis=1), N_EXPERTS - 1).astype(jnp.int32)
    n_used = (pend[-1] // MOE_ROWS).astype(jnp.int32).reshape(1)
    lo = jnp.concatenate([pstart + counts, pend[-1:]])
    hi = jnp.concatenate([pend, jnp.full((1,), n_blocks * MOE_ROWS, pend.dtype)])
    pad_ranges = jnp.stack([lo, hi], axis=1).reshape(-1).astype(jnp.int32)
    return pstart.astype(jnp.int32), block_expert, n_used, pad_ranges


def _slots(eid, rank, pstart):
    sel = eid[..., None] == jnp.arange(N_EXPERTS, dtype=jnp.int32)
    return jnp.sum(jnp.where(sel, pstart, 0), axis=-1).astype(jnp.int32) + rank


def _cache_view(cache):
    bd, lc = cache.shape[:2]
    return cache.transpose(0, 2, 3, 4, 1).reshape(bd, 2 * GROUP_WIDTH, lc)


def _cache_unview(view):
    bd, _, lc = view.shape
    return view.reshape(bd, 2, HEADS_PER_GROUP, HEAD_DIM, lc).transpose(0, 4, 1, 2, 3)


def kernel(x_prompt, x_sample, c_prompt, c_sample, cache_kv_w128, cache_kv_w512, cache_kv_w2048, state_conv, rel_bias, norm_mix_g, norm_ffn_g, w_mod, b_mod, w_in, dw_w, dw_b, ln_conv_g, ln_conv_b, w_conv_out, w_attn_out, w_out, w_router_group, b_router_group, w_router_expert, b_router_expert, w_exp_gate, w_exp_up, w_exp_down, norm_final_g):
    assert norm_mix_g.shape[0] == 1, "single layer"
    B, S, D = x_prompt.shape
    Bd, Td, _ = x_sample.shape
    caches_in = (cache_kv_w128[0], cache_kv_w512[0], cache_kv_w2048[0])
    for (win, dil), c in zip(DILATED_GROUPS, caches_in):
        assert c.shape[1] >= (win // dil) * dil and c.shape[1] % 128 == 0 and S % (dil * ATTN_BLOCK) == 0

    wi = w_in[0]
    q_cols = wi[:, :ATTN_WIDTH] * (HEAD_DIM ** -0.5)
    kv_cols = []
    for g in range(N_GROUPS):
        kv_cols += [wi[:, ATTN_WIDTH + g * GROUP_WIDTH:ATTN_WIDTH + (g + 1) * GROUP_WIDTH],
                    wi[:, 2 * ATTN_WIDTH + g * GROUP_WIDTH:2 * ATTN_WIDTH + (g + 1) * GROUP_WIDTH]]
    w_in_b = jnp.concatenate([q_cols] + kv_cols + [wi[:, 3 * ATTN_WIDTH:]], axis=1).astype(BF16)
    w_co_b = w_conv_out[0].astype(BF16)
    w_ao_b = w_attn_out[0].astype(BF16)
    w_o_b = w_out[0].astype(BF16)
    w_r = jnp.zeros((ROUTER_ROWS, D), F32)
    w_r = w_r.at[0:N_EXPERT_GROUPS].set(w_router_group[0].T)
    w_r = w_r.at[8:].set(w_router_expert[0].reshape(D, N_EXPERTS).T)
    b_r = jnp.full((ROUTER_ROWS, 1), NEG_INF, F32)
    b_r = b_r.at[0:N_EXPERT_GROUPS, 0].set(b_router_group[0])
    b_r = b_r.at[8:, 0].set(b_router_expert[0].reshape(N_EXPERTS))
    row = lambda v: v.reshape(1, -1)

    n_seq = B + Bd
    n_seq_pad = -(-n_seq // 16) * 16
    c_all = jnp.concatenate([c_prompt, c_sample, jnp.zeros((n_seq_pad - n_seq, D), F32)], axis=0)
    mod = _modulation(c_all, w_mod[0], row(b_mod[0]))
    mod_p = [mod[:B, j * D:(j + 1) * D].reshape(B, 1, D) for j in range(6)]
    mod_s = [jnp.tile(mod[B:B + Bd, j * D:(j + 1) * D], (Td, 1)) for j in range(6)]

    dw_w_rows = jnp.broadcast_to(dw_w[0][:, None, :], (CONV_WIDTH, SUBLANES, CONV_CH))
    dw_b_rows = jnp.broadcast_to(dw_b[0][None, :], (SUBLANES, CONV_CH))
    (q0, q1, q2, k0, v0, k1, v1, k2, v2, kvt0, kvt1, kvt2, ga, sgb, convp) = _k1_prompt(
        x_prompt, mod_p[0], mod_p[1], row(norm_mix_g[0]), w_in_b, dw_w_rows, dw_b_rows,
        row(ln_conv_g[0]), row(ln_conv_b[0]), w_co_b)
    o_l = []
    for g, ((win, dil), qg, kg, vg) in enumerate(zip(DILATED_GROUPS, (q0, q1, q2), (k0, k1, k2), (v0, v1, v2))):
        bias = _prompt_bias(rel_bias[:, g * HEADS_PER_GROUP:(g + 1) * HEADS_PER_GROUP], dil, win // dil)
        o_l.append(_k2_prompt(qg, kg, vg, bias, dil))
    xp = x_prompt.reshape(B * S, D)
    tm3 = min(TOKEN_TILE, B * S)
    n_tokens = B * S + Td * Bd
    h2_tiles = n_tokens // tm3
    x2, h2_all, eid, wts, rank, cnt_p = _k3(xp, o_l, mod_p[2], mod_p[3], mod_p[4], ga, sgb, row(norm_ffn_g[0]),
                                            w_ao_b, w_o_b, w_r, b_r, jnp.zeros((N_EXPERTS, 1), F32), S // tm3,
                                            h2_tiles, 0)
    kv_p = [_cache_unview(kvt)[None] for kvt in (kvt0, kvt1, kvt2)]
    conv_p = convp[:, CONV_HIST - (CONV_WIDTH - 1):, :][None]

    Ts = Td * Bd
    xs = x_sample.transpose(1, 0, 2).reshape(Ts, D)
    state_tm = state_conv[0].transpose(1, 0, 2)
    (sq0, sq1, sq2, sk0, sv0, sk1, sv1, sk2, sv2, ga_s, sgb_s, conv_tm) = _k1_sample(
        xs, mod_s[0], mod_s[1], row(norm_mix_g[0]), w_in_b, state_tm, dw_w[0], row(dw_b[0]),
        row(ln_conv_g[0]), row(ln_conv_b[0]), w_co_b, Td)
    q_rows = 16
    q_b = jnp.concatenate([sq0, sq1, sq2], axis=1).reshape(Td, Bd, ATTN_WIDTH).transpose(1, 0, 2)
    q_b = jnp.pad(q_b, ((0, 0), (0, q_rows - Td), (0, 0)))
    views, tails, biases_c, biases_n = [], [], [], []
    for g, ((win, dil), c, sk, sv) in enumerate(zip(DILATED_GROUPS, caches_in, (sk0, sk1, sk2), (sv0, sv1, sv2))):
        lc = c.shape[1]
        views.append(_cache_view(c))
        new_kv = jnp.concatenate([sk, sv], axis=1).reshape(Td, Bd, 2 * GROUP_WIDTH).transpose(1, 0, 2)
        tails.append(jnp.pad(new_kv, ((0, 0), (SUBLANES - Td, 0), (0, 0))))
        bc, bn = _sample_bias(rel_bias[:, g * HEADS_PER_GROUP:(g + 1) * HEADS_PER_GROUP], dil, win // dil, lc, Td, q_rows)
        biases_c.append(bc)
        biases_n.append(bn)
    (nc0, nc1, nc2, so0, so1, so2, sl0, sl1, sl2) = _k2_sample(q_b, views, tails, biases_c, biases_n, Td)
    to_tm = lambda a: a[:, :Td, :].transpose(1, 0, 2).reshape(Ts, GROUP_WIDTH)
    o_l_s = [(to_tm(so0), to_tm(sl0)), (to_tm(so1), to_tm(sl1)), (to_tm(so2), to_tm(sl2))]
    tm3s = min(TOKEN_TILE, Ts)
    tiled = lambda m: m.reshape(Ts // tm3s, tm3s, D)
    assert tm3s == tm3, "prompt and sample token tiles must match to share the expert pass"
    x2s, h2_all, eid_s, wts_s, rank_s, cnt_all = _k3(
        xs, o_l_s, tiled(mod_s[2]), tiled(mod_s[3]), tiled(mod_s[4]), ga_s, sgb_s, row(norm_ffn_g[0]),
        w_ao_b, w_o_b, w_r, b_r, cnt_p, 1, h2_tiles, (B * S) // tm3, h2_buf=h2_all)

    n_assign = 2 * n_tokens
    n_blocks = -(-n_assign // MOE_ROWS) + N_EXPERTS
    pstart, block_expert, n_used, pad_ranges = _slot_tables(cnt_all[:, 0].astype(jnp.int32), n_blocks)
    dest = jnp.concatenate([_slots(eid, rank, pstart), _slots(eid_s, rank_s, pstart)], axis=0)
    inv = _invert(dest, pad_ranges, n_blocks * MOE_ROWS, n_tokens)
    y2 = _moe(block_expert, n_used, inv % n_tokens, inv, h2_all, w_exp_gate[0], w_exp_up[0], w_exp_down[0], n_tokens)
    per_token = lambda a: a.transpose(0, 2, 1).reshape(-1, 2)
    y_prompt = _combine(x2, y2, per_token(wts), mod_p[5], row(norm_final_g), S // tm3, 0, h2_tiles).reshape(B, S, D)
    y_s = _combine(x2s, y2, per_token(wts_s), tiled(mod_s[5]), row(norm_final_g), 1, (B * S) // tm3, h2_tiles)
    y_sample = y_s.reshape(Td, Bd, D).transpose(1, 0, 2)
    kv_s = [_cache_unview(nc)[None] for nc in (nc0, nc1, nc2)]
    conv_s = conv_tm.transpose(1, 0, 2)[None]

    return (y_prompt, y_sample, kv_p[0], kv_p[1], kv_p[2], conv_p,
            kv_s[0], kv_s[1], kv_s[2], conv_s)
```

```python
import functools
import math

import jax
import jax.numpy as jnp
from jax import lax
from jax.experimental import pallas as pl
from jax.experimental.pallas import tpu as pltpu

F32 = jnp.float32
BF16 = jnp.bfloat16

D_MODEL = 1024
HEAD_DIM = 64
HEADS_PER_GROUP = 4
GROUP_WIDTH = HEADS_PER_GROUP * HEAD_DIM
LANES = 128
SUBLANES = 8
CONV_CHUNK = 32
STAGE_HALVES = GROUP_WIDTH // LANES
DILATED_GROUPS = ((128, 1), (512, 4), (2048, 16))
N_GROUPS = len(DILATED_GROUPS)
ATTN_WIDTH = N_GROUPS * GROUP_WIDTH
CONV_CH = D_MODEL // 2
CONV_WIDTH = 31
CONV_HIST = 32
N_BUCKETS = 32
MAX_DISTANCE = 2048
N_EXPERT_GROUPS = 4
EXPERTS_PER_GROUP = 8
N_EXPERTS = N_EXPERT_GROUPS * EXPERTS_PER_GROUP
D_FF_EXPERT = D_MODEL // 2
EPS = 1e-6
NEG_INF = -1e30

COL_KV = ATTN_WIDTH
COL_ULIN = COL_KV + 2 * ATTN_WIDTH
COL_UGATE = COL_ULIN + CONV_CH
COL_GA = COL_UGATE + CONV_CH
COL_GB = COL_GA + D_MODEL
IN_COLS = COL_GB + D_MODEL

ROUTER_ROWS = 8 + N_EXPERTS

V7X_VMEM_LIMIT = 56 * 1024 * 1024
TOKEN_TILE = 512
ATTN_ROWS = 1024
ATTN_BLOCK = 128
MOE_ROWS = 512


def _sigmoid(x):
    return 1.0 / (1.0 + jnp.exp(-x))


def _silu(x):
    return x * _sigmoid(x)


def _rms_modulate(x, g, sc, sh):
    r = lax.rsqrt(jnp.mean(x * x, axis=-1, keepdims=True) + EPS)
    return ((x * r) * g) * (1.0 + sc) + sh


def _conv_tail(y, g, b):
    mu = jnp.mean(y, axis=-1, keepdims=True)
    yc = y - mu
    var = jnp.mean(yc * yc, axis=-1, keepdims=True)
    return _silu((yc * lax.rsqrt(var + EPS)) * g + b)


def _params(semantics):
    return pltpu.CompilerParams(dimension_semantics=semantics, vmem_limit_bytes=V7X_VMEM_LIMIT)


def _const_spec(shape):
    nd = len(shape)
    return pl.BlockSpec(shape, lambda *_: (0,) * nd)


def _weight_spec(shape):
    nd = len(shape)
    return pl.BlockSpec(shape, lambda *_: (0,) * nd, pipeline_mode=pl.Buffered(1))


def _mod_kernel(c_ref, w_ref, b_ref, o_ref):
    s = _silu(c_ref[...]).astype(BF16)
    o_ref[...] = jnp.dot(s, w_ref[...].astype(BF16), preferred_element_type=F32) + b_ref[...]


def _modulation(c_all, w_mod, b_mod):
    rows = c_all.shape[0]
    n_out = w_mod.shape[1]
    chunk = D_MODEL
    return pl.pallas_call(
        _mod_kernel,
        grid=(n_out // chunk,),
        in_specs=[_const_spec((rows, D_MODEL)),
                  pl.BlockSpec((D_MODEL, chunk), lambda j: (0, j)),
                  pl.BlockSpec((1, chunk), lambda j: (0, j))],
        out_specs=pl.BlockSpec((rows, chunk), lambda j: (0, j)),
        out_shape=jax.ShapeDtypeStruct((rows, n_out), F32),
        compiler_params=_params(("arbitrary",)),
        name="modulation",
    )(c_all, w_mod, b_mod)


def _to_streams(ref, val, dil, stage):
    if dil == 1:
        ref[...] = val.astype(ref.dtype).reshape(ref.shape)
        return
    n = val.shape[0] // dil
    for c, half in enumerate(stage):
        half[...] = val[:, c * LANES:(c + 1) * LANES]
        for r in range(dil):
            lo = r * GROUP_WIDTH + c * LANES
            ref[0, :, lo:lo + LANES] = half[pl.ds(r, n, stride=dil), :].astype(ref.dtype)


def _from_streams(ref, dil, stage):
    if dil == 1:
        return ref[...].astype(F32).reshape(ref.shape[-2:])
    n = ref.shape[1]
    for c, half in enumerate(stage):
        for r in range(dil):
            lo = r * GROUP_WIDTH + c * LANES
            half[pl.ds(r, n, stride=dil), :] = ref[0, :, lo:lo + LANES].astype(F32)
    return jnp.concatenate([half[...] for half in stage], axis=1)


def _project_common(hb, win_ref, outs, dils, stages):
    (q_refs, k_refs, v_refs) = outs

    def proj(lo, width):
        return jnp.dot(hb, win_ref[:, lo:lo + width], preferred_element_type=F32)

    zq = proj(0, ATTN_WIDTH)
    for g in range(N_GROUPS):
        _to_streams(q_refs[g], zq[:, g * GROUP_WIDTH:(g + 1) * GROUP_WIDTH], dils[g], stages[g][0])
    zkvs = []
    for g in range(N_GROUPS):
        zkv = proj(COL_KV + 2 * GROUP_WIDTH * g, 2 * GROUP_WIDTH)
        _to_streams(k_refs[g], zkv[:, :GROUP_WIDTH], dils[g], stages[g][1])
        _to_streams(v_refs[g], zkv[:, GROUP_WIDTH:], dils[g], stages[g][2])
        zkvs.append(zkv)
    u = proj(COL_ULIN, CONV_CH) * _sigmoid(proj(COL_UGATE, CONV_CH))
    return zkvs, u, proj


def _k1_kernel(tail_rows, n_tiles,
               x_ref, sh_ref, sc_ref, g_ref, win_ref, dww_ref, dwb_ref, lng_ref, lnb_ref, wco_ref,
               q0, q1, q2, k0, v0, k1, v1, k2, v2, kvt0, kvt1, kvt2, ga_ref, sgb_ref, convp_ref, uext, sbuf, ushift, sga, *stage_refs):
    i = pl.program_id(1)
    tm = x_ref.shape[1]

    @pl.when(i == 0)
    def _():
        uext[0:CONV_HIST, :] = jnp.zeros((CONV_HIST, CONV_CH), F32)

    h = _rms_modulate(x_ref[0], g_ref[...], sc_ref[0], sh_ref[0])
    hb = h.astype(BF16)

    def proj(lo, width):
        return jnp.dot(hb, win_ref[:, lo:lo + width], preferred_element_type=F32)

    uext[CONV_HIST:CONV_HIST + tm, :] = proj(COL_ULIN, CONV_CH) * _sigmoid(proj(COL_UGATE, CONV_CH))
    base = CONV_HIST - (CONV_WIDTH - 1)
    span = tm + CONV_HIST - SUBLANES
    for b in range(1, SUBLANES):
        ushift[b - 1, 0:span, :] = uext[b:b + span, :]

    def conv_chunk(r0):
        accs = [dwb_ref[...]] * (CONV_CHUNK // SUBLANES)
        for k in range(CONV_WIDTH):
            b = (base + k) % SUBLANES
            lo = r0 + base + k - b
            w8 = dww_ref[k]
            for q in range(len(accs)):
                rows = slice(lo + q * SUBLANES, lo + (q + 1) * SUBLANES)
                src = uext[rows, :] if b == 0 else ushift[b - 1, rows, :]
                accs[q] = accs[q] + w8 * src
        s = _conv_tail(jnp.concatenate(accs, axis=0), lng_ref[...], lnb_ref[...])
        sbuf[r0:r0 + CONV_CHUNK, :] = s.astype(BF16)

    dils = tuple(dil for _, dil in DILATED_GROUPS)
    pairs = [stage_refs[j:j + STAGE_HALVES] for j in range(0, len(stage_refs), STAGE_HALVES)]
    stages = [(None,) * 3] + [pairs[3 * (g - 1):3 * g] for g in range(1, N_GROUPS)]
    pieces = []
    for g, q_ref in enumerate((q0, q1, q2)):
        def q_piece(g=g, q_ref=q_ref):
            _to_streams(q_ref, proj(g * GROUP_WIDTH, GROUP_WIDTH), dils[g], stages[g][0])
        pieces.append(q_piece)
    for g, (k_ref, v_ref, kvt) in enumerate(((k0, v0, kvt0), (k1, v1, kvt1), (k2, v2, kvt2))):
        for half, ref in enumerate((k_ref, v_ref)):
            def kv_piece(g=g, half=half, ref=ref, kvt=kvt):
                z = proj(COL_KV + (2 * g + half) * GROUP_WIDTH, GROUP_WIDTH)
                _to_streams(ref, z, dils[g], stages[g][1 + half])
                tr = kvt.shape[2]

                @pl.when(i >= n_tiles - tail_rows[g] // tr)
                def _():
                    kvt[0, half * GROUP_WIDTH:(half + 1) * GROUP_WIDTH, :] = z[tm - tr:, :].T
            pieces.append(kv_piece)
    for c in range(D_MODEL // GROUP_WIDTH):
        cols = slice(c * GROUP_WIDTH, (c + 1) * GROUP_WIDTH)

        def gb_piece(c=c, cols=cols):
            sgb_ref[:, cols] = _sigmoid(proj(COL_GB + c * GROUP_WIDTH, GROUP_WIDTH)).astype(BF16)

        def ga_piece(c=c, cols=cols):
            sga[:, cols] = _sigmoid(proj(COL_GA + c * GROUP_WIDTH, GROUP_WIDTH))
        pieces += [gb_piece, ga_piece]
    n_qkv = 3 * N_GROUPS
    for piece in pieces[:n_qkv]:
        piece()
    for r0 in range(0, tm, CONV_CHUNK):
        conv_chunk(r0)
    for piece in pieces[n_qkv:]:
        piece()
    last = uext[tm:tm + CONV_HIST, :]
    uext[0:CONV_HIST, :] = last
    convp_ref[0] = last
    a = jnp.dot(sbuf[...], wco_ref[...], preferred_element_type=F32)
    ga_ref[...] = (sga[...] * a).astype(BF16)


def _k1_prompt(x, sh1, sc1, norm_g, w_in_b, dw_w, dw_b, ln_g, ln_b, w_co_b):
    B, S, _ = x.shape
    tm = min(TOKEN_TILE, S)
    nt = S // tm
    tail_rows = tuple(min(win, S) for win, _ in DILATED_GROUPS)
    tail_blk = tuple(min(t, tm) for t in tail_rows)

    def tok_spec(width):
        return pl.BlockSpec((tm, width), lambda b, i: (b * nt + i, 0))

    def tail_spec(g):
        first = nt - tail_rows[g] // tail_blk[g]
        return pl.BlockSpec((1, 2 * GROUP_WIDTH, tail_blk[g]), lambda b, i: (b, 0, jnp.maximum(i - first, 0)))

    def stream_shape(dil):
        return jax.ShapeDtypeStruct((B, S // dil, dil * GROUP_WIDTH), BF16)

    def stream_spec(dil):
        return pl.BlockSpec((1, tm // dil, dil * GROUP_WIDTH), lambda b, i: (b, i, 0))

    dils = [dil for _, dil in DILATED_GROUPS]
    qkv_order = [dils[0], dils[1], dils[2]] + [d for d in dils for _ in range(2)]
    mod_spec = pl.BlockSpec((1, 1, D_MODEL), lambda b, i: (b, 0, 0))
    out_shape = ([stream_shape(d) for d in qkv_order]
                 + [jax.ShapeDtypeStruct((B, 2 * GROUP_WIDTH, tail_rows[g]), F32) for g in range(N_GROUPS)]
                 + [jax.ShapeDtypeStruct((B * S, D_MODEL), BF16)] * 2
                 + [jax.ShapeDtypeStruct((B, CONV_HIST, CONV_CH), F32)])
    out_specs = ([stream_spec(d) for d in qkv_order] + [tail_spec(g) for g in range(N_GROUPS)]
                 + [tok_spec(D_MODEL)] * 2 + [pl.BlockSpec((1, CONV_HIST, CONV_CH), lambda b, i: (b, 0, 0))])
    n_stage = 3 * sum(1 for d in dils if d > 1)
    return pl.pallas_call(
        functools.partial(_k1_kernel, tail_rows, nt),
        grid=(B, nt),
        in_specs=[pl.BlockSpec((1, tm, D_MODEL), lambda b, i: (b, i, 0)), mod_spec, mod_spec,
                  _const_spec((1, D_MODEL)), _weight_spec((D_MODEL, IN_COLS)),
                  _const_spec((CONV_WIDTH, SUBLANES, CONV_CH)), _const_spec((SUBLANES, CONV_CH)),
                  _const_spec((1, CONV_CH)), _const_spec((1, CONV_CH)), _weight_spec((CONV_CH, D_MODEL))],
        out_specs=out_specs,
        out_shape=out_shape,
        scratch_shapes=([pltpu.VMEM((CONV_HIST + tm, CONV_CH), F32), pltpu.VMEM((tm, CONV_CH), BF16),
                         pltpu.VMEM((SUBLANES - 1, CONV_HIST + tm - SUBLANES, CONV_CH), F32),
                         pltpu.VMEM((tm, D_MODEL), F32)]
                        + [pltpu.VMEM((tm, LANES), F32)] * (n_stage * STAGE_HALVES)),
        compiler_params=_params(("arbitrary", "arbitrary")),
        name="inproj_prompt",
    )(x, sh1, sc1, norm_g, w_in_b, dw_w, dw_b, ln_g, ln_b, w_co_b)


def _k1s_kernel(n_steps, x_ref, sh_ref, sc_ref, g_ref, win_ref, st_ref, dww_ref, dwb_ref, lng_ref, lnb_ref, wco_ref,
                q0, q1, q2, k0, v0, k1, v1, k2, v2, ga_ref, sgb_ref, conv_ref):
    bd = st_ref.shape[1]
    hist = st_ref.shape[0]
    h = _rms_modulate(x_ref[...], g_ref[...], sc_ref[...], sh_ref[...])
    hb = h.astype(BF16)
    _, u, proj = _project_common(hb, win_ref, ((q0, q1, q2), (k0, k1, k2), (v0, v1, v2)),
                                 (1,) * N_GROUPS, [(None,) * 3] * N_GROUPS)

    def ext(j):
        return st_ref[j] if j < hist else u[(j - hist) * bd:(j - hist + 1) * bd, :]

    outs = []
    for t in range(n_steps):
        acc = jnp.zeros((bd, CONV_CH), F32) + dwb_ref[...]
        for k in range(CONV_WIDTH):
            acc = acc + dww_ref[k:k + 1, :] * ext(t + k + hist - (CONV_WIDTH - 1))
        outs.append(acc)
    for j in range(hist):
        conv_ref[j] = ext(j + n_steps)
    s = _conv_tail(jnp.concatenate(outs, axis=0), lng_ref[...], lnb_ref[...])
    a = jnp.dot(s.astype(BF16), wco_ref[...], preferred_element_type=F32)
    ga_ref[...] = (_sigmoid(proj(COL_GA, D_MODEL)) * a).astype(BF16)
    sgb_ref[...] = _sigmoid(proj(COL_GB, D_MODEL)).astype(BF16)


def _k1_sample(x_tm, sh1, sc1, norm_g, w_in_b, state_tm, dw_w, dw_b, ln_g, ln_b, w_co_b, n_steps):
    T = x_tm.shape[0]
    hist, bd, _ = state_tm.shape
    out_shape = ([jax.ShapeDtypeStruct((T, GROUP_WIDTH), F32)] * 9
                 + [jax.ShapeDtypeStruct((T, D_MODEL), BF16)] * 2
                 + [jax.ShapeDtypeStruct((hist, bd, CONV_CH), F32)])
    out_specs = ([_const_spec((T, GROUP_WIDTH))] * 9 + [_const_spec((T, D_MODEL))] * 2
                 + [_const_spec((hist, bd, CONV_CH))])
    return pl.pallas_call(
        functools.partial(_k1s_kernel, n_steps),
        grid=(1,),
        in_specs=[_const_spec((T, D_MODEL)), _const_spec((T, D_MODEL)), _const_spec((T, D_MODEL)),
                  _const_spec((1, D_MODEL)), _const_spec((D_MODEL, IN_COLS)), _const_spec((hist, bd, CONV_CH)),
                  _const_spec((CONV_WIDTH, CONV_CH)), _const_spec((1, CONV_CH)),
                  _const_spec((1, CONV_CH)), _const_spec((1, CONV_CH)), _const_spec((CONV_CH, D_MODEL))],
        out_specs=out_specs,
        out_shape=out_shape,
        compiler_params=_params(("arbitrary",)),
        name="inproj_sample",
    )(x_tm, sh1, sc1, norm_g, w_in_b, state_tm, dw_w, dw_b, ln_g, ln_b, w_co_b)


def _t5_bucket(dist):
    max_exact = N_BUCKETS // 2
    d_f = jnp.maximum(dist, 1).astype(F32)
    large = max_exact + (jnp.log(d_f / max_exact) / math.log(MAX_DISTANCE / max_exact)
                         * (N_BUCKETS - max_exact)).astype(jnp.int32)
    large = jnp.minimum(large, N_BUCKETS - 1)
    return jnp.where(dist < max_exact, dist, large)


def _bucket_lookup(rel_bias_g, dist):
    bucket = _t5_bucket(dist)
    out = jnp.zeros((rel_bias_g.shape[1],) + dist.shape, F32)
    for b in range(N_BUCKETS):
        out = jnp.where(bucket[None] == b, rel_bias_g[b].reshape((-1,) + (1,) * dist.ndim), out)
    return out


def _prompt_bias(rel_bias_g, dil, n_keys):
    blk = n_keys
    i = jnp.arange(blk)[:, None]
    j = jnp.arange(2 * blk)[None, :]
    rel = i - j + blk
    valid = (rel >= 0) & (rel <= n_keys)
    bias = _bucket_lookup(rel_bias_g, jnp.clip(rel, 0, n_keys) * dil)
    bias = jnp.where(valid[None], bias, NEG_INF)
    return bias.reshape(HEADS_PER_GROUP * blk, 2 * blk).astype(F32)


def _sample_bias(rel_bias_g, dil, n_keys, lc, n_steps, q_rows):
    t = jnp.arange(q_rows)[:, None]
    pos = jnp.arange(lc)[None, :]
    dist_c = lc + t - pos
    dist_n = t - (jnp.arange(128)[None, :] - (128 - n_steps))

    def table(dist, extra):
        ok = (dist >= 0) & (dist % dil == 0) & (dist // dil <= n_keys) & extra & (t < n_steps)
        b = _bucket_lookup(rel_bias_g, jnp.clip(dist, 0, None))
        b = jnp.where(ok[None], b, NEG_INF)
        return jnp.where((t >= n_steps)[None], 0.0, b).astype(F32)

    lane_ok = jnp.arange(128)[None, :] >= 128 - n_steps
    return table(dist_c, True), table(dist_n, lane_ok)


def _k2_kernel(q_ref, k_ref, kh_ref, v_ref, vh_ref, bias_ref, o_ref, lse_ref, kbuf, vbuf):
    i = pl.program_id(2)
    rows = q_ref.shape[1]
    blk = ATTN_BLOCK
    kbuf[0:blk, :] = kh_ref[0]
    kbuf[blk:blk + rows, :] = k_ref[0]
    vbuf[0:blk, :] = vh_ref[0]
    vbuf[blk:blk + rows, :] = v_ref[0]
    lane_head = lax.broadcasted_iota(jnp.int32, (blk, GROUP_WIDTH), 1) // HEAD_DIM
    col = lax.broadcasted_iota(jnp.int32, (HEADS_PER_GROUP * blk, 2 * blk), 1)
    first_mask = jnp.where((col < blk) & (i == 0), NEG_INF, 0.0).astype(F32)
    bias = bias_ref[...]
    for j in range(rows // blk):
        qb = q_ref[0, j * blk:(j + 1) * blk, :]
        q4 = jnp.concatenate([jnp.where(lane_head == h, qb, jnp.zeros_like(qb)) for h in range(HEADS_PER_GROUP)], axis=0)
        kc = kbuf[j * blk:(j + 2) * blk, :]
        vc = vbuf[j * blk:(j + 2) * blk, :]
        s = lax.dot_general(q4, kc, (((1,), (1,)), ((), ())), preferred_element_type=F32) + bias
        if j == 0:
            s = s + first_mask
        m = jnp.max(s, axis=-1, keepdims=True)
        p = jnp.exp(s - m)
        l = jnp.sum(p, axis=-1, keepdims=True)
        o4 = jnp.dot(p.astype(BF16), vc, preferred_element_type=F32) * (1.0 / l)
        lse4 = m + jnp.log(l)
        o = jnp.zeros((blk, GROUP_WIDTH), F32)
        lse = jnp.zeros((blk, GROUP_WIDTH), F32)
        for h in range(HEADS_PER_GROUP):
            sel = lane_head == h
            o = jnp.where(sel, o4[h * blk:(h + 1) * blk, :], o)
            lse = jnp.where(sel, lse4[h * blk:(h + 1) * blk, :], lse)
        o_ref[0, j * blk:(j + 1) * blk, :] = o.astype(o_ref.dtype)
        lse_ref[0, j * blk:(j + 1) * blk, :] = lse


def _k2_prompt(q, k, v, bias, dil):
    B, L, _ = q.shape
    rows = min(ATTN_ROWS, L)
    per = rows // ATTN_BLOCK
    main = pl.BlockSpec((1, rows, GROUP_WIDTH), lambda b, r, i: (b, i, r))
    halo = pl.BlockSpec((1, ATTN_BLOCK, GROUP_WIDTH), lambda b, r, i: (b, jnp.maximum(i * per - 1, 0), r))
    return pl.pallas_call(
        _k2_kernel,
        grid=(B, dil, L // rows),
        in_specs=[main, main, halo, main, halo, _const_spec(bias.shape)],
        out_specs=[main, main],
        out_shape=[jax.ShapeDtypeStruct(q.shape, BF16), jax.ShapeDtypeStruct(q.shape, F32)],
        scratch_shapes=[pltpu.VMEM((ATTN_BLOCK + rows, GROUP_WIDTH), BF16)] * 2,
        compiler_params=_params(("arbitrary", "arbitrary", "arbitrary")),
        name=f"attn_prompt_d{dil}",
    )(q, k, k, v, v, bias)


def _k2s_kernel(n_steps, q_ref, c0, c1, c2, t0, t1, t2, bc0, bc1, bc2, bn0, bn1, bn2,
                nc0, nc1, nc2, o0, o1, o2, l0, l1, l2):
    lane = lax.broadcasted_iota(jnp.int32, (2 * GROUP_WIDTH, 128), 1)
    keep = 128 - n_steps
    groups = ((c0, t0, bc0, bn0, nc0, o0, l0), (c1, t1, bc1, bn1, nc1, o1, l1), (c2, t2, bc2, bn2, nc2, o2, l2))
    for g, (c_ref, t_ref, bc_ref, bn_ref, nc_ref, o_ref, l_ref) in enumerate(groups):
        lc = c_ref.shape[2]
        n_tiles = lc // 128
        new_rows = t_ref[0]
        tail = jnp.concatenate([jnp.zeros((128 - new_rows.shape[0], new_rows.shape[1]), F32), new_rows], axis=0).T
        cur = pltpu.roll(c_ref[0, :, 0:128], keep, 1)
        for c in range(n_tiles):
            nxt = pltpu.roll(c_ref[0, :, (c + 1) * 128:(c + 2) * 128], keep, 1) if c + 1 < n_tiles else tail
            nc_ref[0, :, c * 128:(c + 1) * 128] = jnp.where(lane < keep, cur, nxt)
            cur = nxt
        for h in range(HEADS_PER_GROUP):
            lo = g * GROUP_WIDTH + h * HEAD_DIM
            qh = q_ref[0, :, lo:lo + HEAD_DIM].astype(BF16)
            kh = c_ref[0, h * HEAD_DIM:(h + 1) * HEAD_DIM, :].astype(BF16)
            vh = c_ref[0, GROUP_WIDTH + h * HEAD_DIM:GROUP_WIDTH + (h + 1) * HEAD_DIM, :].astype(BF16)
            kt = tail[h * HEAD_DIM:(h + 1) * HEAD_DIM, :].astype(BF16)
            vt = tail[GROUP_WIDTH + h * HEAD_DIM:GROUP_WIDTH + (h + 1) * HEAD_DIM, :].astype(BF16)
            sc = jnp.dot(qh, kh, preferred_element_type=F32) + bc_ref[h]
            sn = jnp.dot(qh, kt, preferred_element_type=F32) + bn_ref[h]
            m = jnp.maximum(jnp.max(sc, axis=-1, keepdims=True), jnp.max(sn, axis=-1, keepdims=True))
            pc = jnp.exp(sc - m)
            pn = jnp.exp(sn - m)
            l = jnp.sum(pc, axis=-1, keepdims=True) + jnp.sum(pn, axis=-1, keepdims=True)
            nt_dims = (((1,), (1,)), ((), ()))
            o = (lax.dot_general(pc.astype(BF16), vh, nt_dims, preferred_element_type=F32)
                 + lax.dot_general(pn.astype(BF16), vt, nt_dims, preferred_element_type=F32)) * (1.0 / l)
            o_ref[0, :, h * HEAD_DIM:(h + 1) * HEAD_DIM] = o
            l_ref[0, :, h * HEAD_DIM:(h + 1) * HEAD_DIM] = jnp.broadcast_to(m + jnp.log(l), o.shape)


def _k2_sample(q_b, caches, tails, biases_c, biases_n, n_steps):
    bd, q_rows, _ = q_b.shape
    per_b = lambda shape: pl.BlockSpec((1,) + shape[1:], lambda b: (b,) + (0,) * (len(shape) - 1))
    ins = [q_b] + list(caches) + list(tails) + list(biases_c) + list(biases_n)
    in_specs = ([per_b(q_b.shape)] + [per_b(c.shape) for c in caches] + [per_b(t.shape) for t in tails]
                + [_const_spec(b.shape) for b in biases_c] + [_const_spec(b.shape) for b in biases_n])
    o_shape = jax.ShapeDtypeStruct((bd, q_rows, GROUP_WIDTH), F32)
    out_shape = [jax.ShapeDtypeStruct(c.shape, F32) for c in caches] + [o_shape] * 6
    out_specs = [per_b(c.shape) for c in caches] + [per_b(o_shape.shape)] * 6
    return pl.pallas_call(
        functools.partial(_k2s_kernel, n_steps),
        grid=(bd,),
        in_specs=in_specs,
        out_specs=out_specs,
        out_shape=out_shape,
        compiler_params=_params(("arbitrary",)),
        name="attn_sample",
    )(*ins)


def _k3_kernel(x_ref, o0, o1, o2, l0, l1, l2, ga_ref, sgb_ref, g1_ref, sh2_ref, sc2_ref, gf_ref,
               wao_ref, wo_ref, wr_ref, br_ref, before_ref, cnt_in_ref,
               x2_ref, h2_ref, eid_ref, wts_ref, rank_ref, cnt_ref, carry, *stage_refs):
    @pl.when(pl.program_id(0) == 0)
    def _():
        carry[...] = cnt_in_ref[...]

    dils = tuple(x_ref.shape[0] // r.shape[-2] for r in (o0, o1, o2))
    pairs = [stage_refs[j:j + STAGE_HALVES] for j in range(0, len(stage_refs), STAGE_HALVES)]
    os_ = [_from_streams(r, d, pairs[2 * g]) for g, (r, d) in enumerate(zip((o0, o1, o2), dils))]
    ls = [_from_streams(r, d, pairs[2 * g + 1]) for g, (r, d) in enumerate(zip((l0, l1, l2), dils))]
    m = jnp.maximum(jnp.maximum(ls[0], ls[1]), ls[2])
    ws = [jnp.exp(l - m) for l in ls]
    den = ws[0] + ws[1] + ws[2]
    o = (ws[0] * os_[0] + ws[1] * os_[1] + ws[2] * os_[2]) / den
    b = jnp.dot(o.astype(BF16), wao_ref[...], preferred_element_type=F32)
    mixed = ga_ref[...].astype(F32) + sgb_ref[...].astype(F32) * b
    x2 = x_ref[...] + g1_ref[0] * jnp.dot(mixed.astype(BF16), wo_ref[...], preferred_element_type=F32)
    x2_ref[...] = x2
    h2 = _rms_modulate(x2, gf_ref[...], sc2_ref[0], sh2_ref[0])
    h2_ref[...] = h2
    lt = lax.dot_general(wr_ref[...], h2, (((1,), (1,)), ((), ())), preferred_element_type=F32,
                         precision=lax.Precision.HIGHEST) + br_ref[...]
    tm = h2.shape[0]
    gl = lt[0:8, :]
    gmax = jnp.max(gl, axis=0, keepdims=True)
    r8 = lax.broadcasted_iota(jnp.int32, (8, tm), 0)
    grp = jnp.min(jnp.where(gl == gmax, r8, 8), axis=0, keepdims=True)
    p_grp = 1.0 / jnp.sum(jnp.exp(gl - gmax), axis=0, keepdims=True)
    es = jnp.zeros((EXPERTS_PER_GROUP, tm), F32)
    for g in range(N_EXPERT_GROUPS):
        es = jnp.where(grp == g, lt[8 + 8 * g:16 + 8 * g, :], es)
    v1 = jnp.max(es, axis=0, keepdims=True)
    i1 = jnp.min(jnp.where(es == v1, r8, 8), axis=0, keepdims=True)
    rest = jnp.where(r8 == i1, -jnp.inf, es)
    v2 = jnp.max(rest, axis=0, keepdims=True)
    i2 = jnp.min(jnp.where(rest == v2, r8, 8), axis=0, keepdims=True)
    e21 = jnp.exp(v2 - v1)
    w1 = p_grp / (1.0 + e21)
    e1 = grp * EXPERTS_PER_GROUP + i1
    e2 = grp * EXPERTS_PER_GROUP + i2
    eid_ref[0, 0:1, :] = e1
    eid_ref[0, 1:2, :] = e2
    wts_ref[0, 0:1, :] = w1
    wts_ref[0, 1:2, :] = w1 * e21
    r_e = lax.broadcasted_iota(jnp.int32, (N_EXPERTS, tm), 0)
    hit1 = r_e == e1
    hit2 = r_e == e2
    both = jnp.where(hit1 | hit2, 1.0, 0.0)
    base = carry[...] + jnp.dot(both.astype(BF16), before_ref[...], preferred_element_type=F32)
    rank_ref[0, 0:1, :] = jnp.sum(jnp.where(hit1, base, 0.0), axis=0, keepdims=True).astype(jnp.int32)
    rank_ref[0, 1:2, :] = jnp.sum(jnp.where(hit2, base, 0.0), axis=0, keepdims=True).astype(jnp.int32)
    total = carry[...] + jnp.sum(both, axis=1, keepdims=True)
    carry[...] = total
    cnt_ref[...] = total


def _k3(x, o_l, g1, sh2, sc2, ga, sgb, norm_ffn_g, w_ao_b, w_o_b, w_r, b_r, cnt_in, tiles_per_mod):
    T = x.shape[0]
    tm = min(TOKEN_TILE, T)
    nt = T // tm
    tok = lambda w: pl.BlockSpec((tm, w), lambda t: (t, 0))
    mod = pl.BlockSpec((1,) + g1.shape[1:], lambda t: (t // tiles_per_mod, 0, 0))
    small = pl.BlockSpec((1, 2, tm), lambda t: (t, 0, 0))
    small_i = jax.ShapeDtypeStruct((nt, 2, tm), jnp.int32)
    before = (jnp.arange(tm)[:, None] < jnp.arange(tm)[None, :]).astype(BF16)

    def attn_spec(a):
        if a.ndim == 2:
            return tok(GROUP_WIDTH)
        dil = a.shape[2] // GROUP_WIDTH
        return pl.BlockSpec((1, tm // dil, a.shape[2]), lambda t: (t // tiles_per_mod, t % tiles_per_mod, 0))

    attn_in = [o_l[0][0], o_l[1][0], o_l[2][0], o_l[0][1], o_l[1][1], o_l[2][1]]
    return pl.pallas_call(
        _k3_kernel,
        grid=(nt,),
        in_specs=[tok(D_MODEL)] + [attn_spec(a) for a in attn_in] + [tok(D_MODEL)] * 2 + [mod] * 3
                 + [_const_spec((1, D_MODEL)), _weight_spec(w_ao_b.shape), _weight_spec(w_o_b.shape),
                    _const_spec(w_r.shape), _const_spec(b_r.shape), _weight_spec((tm, tm)),
                    _const_spec((N_EXPERTS, 1))],
        out_specs=[tok(D_MODEL), tok(D_MODEL), small, small, small, _const_spec((N_EXPERTS, 1))],
        out_shape=[jax.ShapeDtypeStruct((T, D_MODEL), F32), jax.ShapeDtypeStruct((T, D_MODEL), F32),
                   small_i, jax.ShapeDtypeStruct((nt, 2, tm), F32), small_i,
                   jax.ShapeDtypeStruct((N_EXPERTS, 1), F32)],
        scratch_shapes=([pltpu.VMEM((N_EXPERTS, 1), F32)]
                        + [pltpu.VMEM((tm, LANES), F32)] * (2 * N_GROUPS * STAGE_HALVES)),
        compiler_params=_params(("arbitrary",)),
        name="merge_router",
    )(x, *attn_in, ga, sgb, g1, sh2, sc2, norm_ffn_g, w_ao_b, w_o_b, w_r, b_r, before, cnt_in)


def _rows_wait(n_rows, hbm, vmem, sem):
    pltpu.make_async_copy(hbm.at[pl.ds(0, n_rows)], vmem, sem).wait()


def _dispatch_kernel(n_tiles, n_first, dest_ref, fill_ref, ha_ref, hb_ref, xs_ref, buf, sem, zbuf, zsem):
    i = pl.program_id(0)
    slot = i % 2 if n_tiles > 1 else 0
    tm = ha_ref.shape[0]

    def drain(s):
        for _ in range(2):
            _rows_wait(tm, xs_ref, buf.at[s], sem.at[s])

    @pl.when(i == 0)
    def _():
        zbuf[...] = jnp.zeros(zbuf.shape, F32)

        def fill(row):
            return pltpu.make_async_copy(zbuf, xs_ref.at[pl.ds(pl.multiple_of(row, MOE_ROWS), MOE_ROWS)], zsem)

        n_blocks = xs_ref.shape[0] // MOE_ROWS
        n_used = fill_ref[N_EXPERTS]

        def start_block(j, carry):
            fill(j * MOE_ROWS).start()
            return carry

        def wait_block(j, carry):
            fill(j * MOE_ROWS).wait()
            return carry

        for e in range(N_EXPERTS):
            @pl.when(fill_ref[e] >= 0)
            def _(e=e):
                fill(fill_ref[e]).start()
        lax.fori_loop(n_used, n_blocks, start_block, 0)
        for e in range(N_EXPERTS):
            @pl.when(fill_ref[e] >= 0)
            def _(e=e):
                fill(fill_ref[e]).wait()
        lax.fori_loop(n_used, n_blocks, wait_block, 0)

    if n_tiles > 2:
        @pl.when(i >= 2)
        def _():
            drain(slot)

    @pl.when(i < n_first)
    def _():
        buf[slot] = ha_ref[...]

    @pl.when(i >= n_first)
    def _():
        buf[slot] = hb_ref[...]

    for r in range(tm):
        for k in range(2):
            pltpu.make_async_copy(buf.at[slot, pl.ds(r, 1)], xs_ref.at[pl.ds(dest_ref[0, k, r], 1)], sem.at[slot]).start()

    if n_tiles == 1:
        drain(0)
    else:
        @pl.when(i == n_tiles - 1)
        def _():
            drain(slot)
            drain(1 - slot)


def _dispatch(dest, fill_rows, h_a, h_b, n_slots):
    nt, _, tm = dest.shape
    n_first = h_a.shape[0] // tm
    assert h_a.shape[0] % tm == 0 and h_b.shape[0] == (nt - n_first) * tm
    return pl.pallas_call(
        functools.partial(_dispatch_kernel, nt, n_first),
        grid=(nt,),
        in_specs=[pl.BlockSpec((1, 2, tm), lambda t: (t, 0, 0), memory_space=pltpu.SMEM),
                  pl.BlockSpec(memory_space=pltpu.SMEM),
                  pl.BlockSpec((tm, D_MODEL), lambda t: (jnp.minimum(t, n_first - 1), 0)),
                  pl.BlockSpec((tm, D_MODEL), lambda t: (jnp.maximum(t - n_first, 0), 0))],
        out_specs=pl.BlockSpec(memory_space=pl.ANY),
        out_shape=jax.ShapeDtypeStruct((n_slots, D_MODEL), F32),
        scratch_shapes=[pltpu.VMEM((2, tm, D_MODEL), F32), pltpu.SemaphoreType.DMA((2,)),
                        pltpu.VMEM((MOE_ROWS, D_MODEL), F32), pltpu.SemaphoreType.DMA(())],
        compiler_params=_params(("arbitrary",)),
        name="dispatch_rows",
    )(dest, fill_rows, h_a, h_b)


def _k4_kernel(be_ref, nused_ref, x_ref, wg_ref, wu_ref, wd_ref, y_ref, wg_b, wu_b, wd_b):
    i = pl.program_id(0)

    @pl.when((i == 0) | (be_ref[i] != be_ref[jnp.maximum(i - 1, 0)]))
    def _():
        wg_b[...] = wg_ref[0].astype(BF16)
        wu_b[...] = wu_ref[0].astype(BF16)
        wd_b[...] = wd_ref[0].astype(BF16)

    @pl.when(i < nused_ref[0])
    def _():
        xb = x_ref[...].astype(BF16)
        gate = jnp.dot(xb, wg_b[...], preferred_element_type=F32)
        up = jnp.dot(xb, wu_b[...], preferred_element_type=F32)
        mid = (_silu(gate) * up).astype(BF16)
        y_ref[...] = jnp.dot(mid, wd_b[...], preferred_element_type=F32)

    @pl.when(i >= nused_ref[0])
    def _():
        y_ref[...] = jnp.zeros(y_ref.shape, F32)


def _k4(block_expert, n_used, xs, w_gate, w_up, w_down):
    n_blocks = block_expert.shape[0]
    rows = MOE_ROWS
    weight = lambda shape: pl.BlockSpec((1,) + shape, lambda i, be, nu: (be[i], 0, 0))
    grid_spec = pltpu.PrefetchScalarGridSpec(
        num_scalar_prefetch=2,
        grid=(n_blocks,),
        in_specs=[pl.BlockSpec((rows, D_MODEL), lambda i, be, nu: (jnp.minimum(i, nu[0] - 1), 0)),
                  weight((D_MODEL, D_FF_EXPERT)), weight((D_MODEL, D_FF_EXPERT)), weight((D_FF_EXPERT, D_MODEL))],
        out_specs=pl.BlockSpec((rows, D_MODEL), lambda i, be, nu: (i, 0)),
        scratch_shapes=[pltpu.VMEM((D_MODEL, D_FF_EXPERT), BF16), pltpu.VMEM((D_MODEL, D_FF_EXPERT), BF16),
                        pltpu.VMEM((D_FF_EXPERT, D_MODEL), BF16)],
    )
    return pl.pallas_call(
        _k4_kernel,
        grid_spec=grid_spec,
        out_shape=jax.ShapeDtypeStruct((n_blocks * rows, D_MODEL), F32),
        compiler_params=_params(("arbitrary",)),
        name="expert_blocks",
    )(block_expert, n_used, xs, w_gate, w_up, w_down)


def _k5_kernel(n_tiles, idx_ref, x2_ref, w_ref, g2_ref, gfin_ref, ys_hbm, y_ref, ybuf, sem):
    i = pl.program_id(0)
    tm = x2_ref.shape[0]

    @pl.when(i < n_tiles)
    def _():
        slot = i % 2
        for r in range(tm):
            for k in range(2):
                pltpu.make_async_copy(ys_hbm.at[pl.ds(idx_ref[0, k, r], 1)], ybuf.at[slot, k, pl.ds(r, 1)],
                                      sem.at[slot]).start()

    @pl.when(i >= 1)
    def _():
        slot = (i - 1) % 2
        for k in range(2):
            _rows_wait(tm, ys_hbm, ybuf.at[slot, k], sem.at[slot])
        w = w_ref[...]
        f = ybuf[slot, 0] * w[:, 0:1] + ybuf[slot, 1] * w[:, 1:2]
        y = x2_ref[...] + g2_ref[0] * f
        r = lax.rsqrt(jnp.mean(y * y, axis=-1, keepdims=True) + EPS)
        y_ref[...] = (y * r) * gfin_ref[...]


def _k5(dest, x2, wts, g2, norm_final_g, ys, tiles_per_mod):
    nt, _, tm = dest.shape
    T = x2.shape[0]
    prev = lambda t: jnp.maximum(t - 1, 0)
    return pl.pallas_call(
        functools.partial(_k5_kernel, nt),
        grid=(nt + 1,),
        in_specs=[pl.BlockSpec((1, 2, tm), lambda t: (jnp.minimum(t, nt - 1), 0, 0), memory_space=pltpu.SMEM),
                  pl.BlockSpec((tm, D_MODEL), lambda t: (prev(t), 0)),
                  pl.BlockSpec((tm, 2), lambda t: (prev(t), 0)),
                  pl.BlockSpec((1,) + g2.shape[1:], lambda t: (prev(t) // tiles_per_mod, 0, 0)),
                  _const_spec((1, D_MODEL)), pl.BlockSpec(memory_space=pl.ANY)],
        out_specs=pl.BlockSpec((tm, D_MODEL), lambda t: (prev(t), 0)),
        out_shape=jax.ShapeDtypeStruct((T, D_MODEL), F32),
        scratch_shapes=[pltpu.VMEM((2, 2, tm, D_MODEL), F32), pltpu.SemaphoreType.DMA((2,))],
        compiler_params=_params(("arbitrary",)),
        name="combine_norm",
    )(dest, x2, wts, g2, norm_final_g, ys)


def _slot_tables(counts, n_blocks):
    padded = (counts + MOE_ROWS - 1) // MOE_ROWS * MOE_ROWS
    pend = jnp.cumsum(padded)
    pstart = pend - padded
    block_lo = jnp.arange(n_blocks, dtype=jnp.int32) * MOE_ROWS
    block_expert = jnp.minimum(jnp.sum(pend[None, :] <= block_lo[:, None], axis=1), N_EXPERTS - 1).astype(jnp.int32)
    n_used = (pend[-1] // MOE_ROWS).astype(jnp.int32).reshape(1)
    fill_rows = jnp.concatenate([jnp.where(padded > 0, pend - MOE_ROWS, -1).astype(jnp.int32), n_used])
    return pstart.astype(jnp.int32), block_expert, n_used, fill_rows


def _slots(eid, rank, pstart):
    sel = eid[..., None] == jnp.arange(N_EXPERTS, dtype=jnp.int32)
    return jnp.sum(jnp.where(sel, pstart, 0), axis=-1).astype(jnp.int32) + rank


def _cache_view(cache):
    bd, lc = cache.shape[:2]
    return cache.transpose(0, 2, 3, 4, 1).reshape(bd, 2 * GROUP_WIDTH, lc)


def _cache_unview(view):
    bd, _, lc = view.shape
    return view.reshape(bd, 2, HEADS_PER_GROUP, HEAD_DIM, lc).transpose(0, 4, 1, 2, 3)


def kernel(x_prompt, x_sample, c_prompt, c_sample, cache_kv_w128, cache_kv_w512, cache_kv_w2048, state_conv, rel_bias, norm_mix_g, norm_ffn_g, w_mod, b_mod, w_in, dw_w, dw_b, ln_conv_g, ln_conv_b, w_conv_out, w_attn_out, w_out, w_router_group, b_router_group, w_router_expert, b_router_expert, w_exp_gate, w_exp_up, w_exp_down, norm_final_g):
    assert norm_mix_g.shape[0] == 1, "single layer"
    B, S, D = x_prompt.shape
    Bd, Td, _ = x_sample.shape
    caches_in = (cache_kv_w128[0], cache_kv_w512[0], cache_kv_w2048[0])
    for (win, dil), c in zip(DILATED_GROUPS, caches_in):
        assert c.shape[1] >= (win // dil) * dil and c.shape[1] % 128 == 0 and S % (dil * ATTN_BLOCK) == 0

    wi = w_in[0]
    q_cols = wi[:, :ATTN_WIDTH] * (HEAD_DIM ** -0.5)
    kv_cols = []
    for g in range(N_GROUPS):
        kv_cols += [wi[:, ATTN_WIDTH + g * GROUP_WIDTH:ATTN_WIDTH + (g + 1) * GROUP_WIDTH],
                    wi[:, 2 * ATTN_WIDTH + g * GROUP_WIDTH:2 * ATTN_WIDTH + (g + 1) * GROUP_WIDTH]]
    w_in_b = jnp.concatenate([q_cols] + kv_cols + [wi[:, 3 * ATTN_WIDTH:]], axis=1).astype(BF16)
    w_co_b = w_conv_out[0].astype(BF16)
    w_ao_b = w_attn_out[0].astype(BF16)
    w_o_b = w_out[0].astype(BF16)
    w_r = jnp.zeros((ROUTER_ROWS, D), F32)
    w_r = w_r.at[0:N_EXPERT_GROUPS].set(w_router_group[0].T)
    w_r = w_r.at[8:].set(w_router_expert[0].reshape(D, N_EXPERTS).T)
    b_r = jnp.full((ROUTER_ROWS, 1), NEG_INF, F32)
    b_r = b_r.at[0:N_EXPERT_GROUPS, 0].set(b_router_group[0])
    b_r = b_r.at[8:, 0].set(b_router_expert[0].reshape(N_EXPERTS))
    row = lambda v: v.reshape(1, -1)

    n_seq = B + Bd
    n_seq_pad = -(-n_seq // 16) * 16
    c_all = jnp.concatenate([c_prompt, c_sample, jnp.zeros((n_seq_pad - n_seq, D), F32)], axis=0)
    mod = _modulation(c_all, w_mod[0], row(b_mod[0]))
    mod_p = [mod[:B, j * D:(j + 1) * D].reshape(B, 1, D) for j in range(6)]
    mod_s = [jnp.tile(mod[B:B + Bd, j * D:(j + 1) * D], (Td, 1)) for j in range(6)]

    dw_w_rows = jnp.broadcast_to(dw_w[0][:, None, :], (CONV_WIDTH, SUBLANES, CONV_CH))
    dw_b_rows = jnp.broadcast_to(dw_b[0][None, :], (SUBLANES, CONV_CH))
    (q0, q1, q2, k0, v0, k1, v1, k2, v2, kvt0, kvt1, kvt2, ga, sgb, convp) = _k1_prompt(
        x_prompt, mod_p[0], mod_p[1], row(norm_mix_g[0]), w_in_b, dw_w_rows, dw_b_rows,
        row(ln_conv_g[0]), row(ln_conv_b[0]), w_co_b)
    o_l = []
    for g, ((win, dil), qg, kg, vg) in enumerate(zip(DILATED_GROUPS, (q0, q1, q2), (k0, k1, k2), (v0, v1, v2))):
        bias = _prompt_bias(rel_bias[:, g * HEADS_PER_GROUP:(g + 1) * HEADS_PER_GROUP], dil, win // dil)
        o_l.append(_k2_prompt(qg, kg, vg, bias, dil))
    xp = x_prompt.reshape(B * S, D)
    tm3 = min(TOKEN_TILE, B * S)
    x2, h2, eid, wts, rank, cnt_p = _k3(xp, o_l, mod_p[2], mod_p[3], mod_p[4], ga, sgb, row(norm_ffn_g[0]),
                                        w_ao_b, w_o_b, w_r, b_r, jnp.zeros((N_EXPERTS, 1), F32), S // tm3)
    kv_p = [_cache_unview(kvt)[None] for kvt in (kvt0, kvt1, kvt2)]
    conv_p = convp[:, CONV_HIST - (CONV_WIDTH - 1):, :][None]

    Ts = Td * Bd
    xs = x_sample.transpose(1, 0, 2).reshape(Ts, D)
    state_tm = state_conv[0].transpose(1, 0, 2)
    (sq0, sq1, sq2, sk0, sv0, sk1, sv1, sk2, sv2, ga_s, sgb_s, conv_tm) = _k1_sample(
        xs, mod_s[0], mod_s[1], row(norm_mix_g[0]), w_in_b, state_tm, dw_w[0], row(dw_b[0]),
        row(ln_conv_g[0]), row(ln_conv_b[0]), w_co_b, Td)
    q_rows = 16
    q_b = jnp.concatenate([sq0, sq1, sq2], axis=1).reshape(Td, Bd, ATTN_WIDTH).transpose(1, 0, 2)
    q_b = jnp.pad(q_b, ((0, 0), (0, q_rows - Td), (0, 0)))
    views, tails, biases_c, biases_n = [], [], [], []
    for g, ((win, dil), c, sk, sv) in enumerate(zip(DILATED_GROUPS, caches_in, (sk0, sk1, sk2), (sv0, sv1, sv2))):
        lc = c.shape[1]
        views.append(_cache_view(c))
        new_kv = jnp.concatenate([sk, sv], axis=1).reshape(Td, Bd, 2 * GROUP_WIDTH).transpose(1, 0, 2)
        tails.append(jnp.pad(new_kv, ((0, 0), (SUBLANES - Td, 0), (0, 0))))
        bc, bn = _sample_bias(rel_bias[:, g * HEADS_PER_GROUP:(g + 1) * HEADS_PER_GROUP], dil, win // dil, lc, Td, q_rows)
        biases_c.append(bc)
        biases_n.append(bn)
    (nc0, nc1, nc2, so0, so1, so2, sl0, sl1, sl2) = _k2_sample(q_b, views, tails, biases_c, biases_n, Td)
    to_tm = lambda a: a[:, :Td, :].transpose(1, 0, 2).reshape(Ts, GROUP_WIDTH)
    o_l_s = [(to_tm(so0), to_tm(sl0)), (to_tm(so1), to_tm(sl1)), (to_tm(so2), to_tm(sl2))]
    tm3s = min(TOKEN_TILE, Ts)
    tiled = lambda m: m.reshape(Ts // tm3s, tm3s, D)
    x2s, h2s, eid_s, wts_s, rank_s, cnt_all = _k3(xs, o_l_s, tiled(mod_s[2]), tiled(mod_s[3]), tiled(mod_s[4]),
                                                  ga_s, sgb_s, row(norm_ffn_g[0]), w_ao_b, w_o_b, w_r, b_r, cnt_p, 1)

    n_assign = 2 * (B * S + Ts)
    n_blocks = -(-n_assign // MOE_ROWS) + N_EXPERTS
    pstart, block_expert, n_used, fill_rows = _slot_tables(cnt_all[:, 0].astype(jnp.int32), n_blocks)
    dest_p = _slots(eid, rank, pstart)
    dest_s = _slots(eid_s, rank_s, pstart)
    n_slots = n_blocks * MOE_ROWS
    assert tm3s == tm3, "prompt and sample token tiles must match to share the dispatch"
    slots = _dispatch(jnp.concatenate([dest_p, dest_s], axis=0), fill_rows, h2, h2s, n_slots)
    ys = _k4(block_expert, n_used, slots, w_exp_gate[0], w_exp_up[0], w_exp_down[0])
    per_token = lambda a: a.transpose(0, 2, 1).reshape(-1, 2)
    y_prompt = _k5(dest_p, x2, per_token(wts), mod_p[5], row(norm_final_g), ys, S // tm3).reshape(B, S, D)
    y_s = _k5(dest_s, x2s, per_token(wts_s), tiled(mod_s[5]), row(norm_final_g), ys, 1)
    y_sample = y_s.reshape(Td, Bd, D).transpose(1, 0, 2)
    kv_s = [_cache_unview(nc)[None] for nc in (nc0, nc1, nc2)]
    conv_s = conv_tm.transpose(1, 0, 2)[None]

    return (y_prompt, y_sample, kv_p[0], kv_p[1], kv_p[2], conv_p,
            kv_s[0], kv_s[1], kv_s[2], conv_s)
```

```python
import functools
import math

import jax
import jax.numpy as jnp
from jax import lax
from jax.experimental import pallas as pl
from jax.experimental.pallas import tpu as pltpu

F32 = jnp.float32
BF16 = jnp.bfloat16

D_MODEL = 1024
HEAD_DIM = 64
HEADS_PER_GROUP = 4
GROUP_WIDTH = HEADS_PER_GROUP * HEAD_DIM
LANES = 128
SUBLANES = 8
CONV_CHUNK = 32
STAGE_HALVES = GROUP_WIDTH // LANES
DILATED_GROUPS = ((128, 1), (512, 4), (2048, 16))
N_GROUPS = len(DILATED_GROUPS)
ATTN_WIDTH = N_GROUPS * GROUP_WIDTH
CONV_CH = D_MODEL // 2
CONV_WIDTH = 31
CONV_HIST = 32
N_BUCKETS = 32
MAX_DISTANCE = 2048
N_EXPERT_GROUPS = 4
EXPERTS_PER_GROUP = 8
N_EXPERTS = N_EXPERT_GROUPS * EXPERTS_PER_GROUP
D_FF_EXPERT = D_MODEL // 2
EPS = 1e-6
NEG_INF = -1e30

COL_KV = ATTN_WIDTH
COL_ULIN = COL_KV + 2 * ATTN_WIDTH
COL_UGATE = COL_ULIN + CONV_CH
COL_GA = COL_UGATE + CONV_CH
COL_GB = COL_GA + D_MODEL
IN_COLS = COL_GB + D_MODEL

ROUTER_ROWS = 8 + N_EXPERTS

V7X_VMEM_LIMIT = 56 * 1024 * 1024
TOKEN_TILE = 512
ATTN_ROWS = 2048
ATTN_BLOCK = 128
MOE_ROWS = 512


def _sigmoid(x):
    return 1.0 / (1.0 + jnp.exp(-x))


def _silu(x):
    return x * _sigmoid(x)


def _rms_modulate(x, g, sc, sh):
    r = lax.rsqrt(jnp.mean(x * x, axis=-1, keepdims=True) + EPS)
    return ((x * r) * g) * (1.0 + sc) + sh


def _conv_tail(y, g, b):
    mu = jnp.mean(y, axis=-1, keepdims=True)
    yc = y - mu
    var = jnp.mean(yc * yc, axis=-1, keepdims=True)
    return _silu((yc * lax.rsqrt(var + EPS)) * g + b)


def _params(semantics):
    return pltpu.CompilerParams(dimension_semantics=semantics, vmem_limit_bytes=V7X_VMEM_LIMIT)


def _const_spec(shape):
    nd = len(shape)
    return pl.BlockSpec(shape, lambda *_: (0,) * nd)


def _weight_spec(shape):
    nd = len(shape)
    return pl.BlockSpec(shape, lambda *_: (0,) * nd, pipeline_mode=pl.Buffered(1))


def _mod_kernel(c_ref, w_ref, b_ref, o_ref):
    s = _silu(c_ref[...]).astype(BF16)
    o_ref[...] = jnp.dot(s, w_ref[...].astype(BF16), preferred_element_type=F32) + b_ref[...]


def _modulation(c_all, w_mod, b_mod):
    rows = c_all.shape[0]
    n_out = w_mod.shape[1]
    chunk = D_MODEL
    return pl.pallas_call(
        _mod_kernel,
        grid=(n_out // chunk,),
        in_specs=[_const_spec((rows, D_MODEL)),
                  pl.BlockSpec((D_MODEL, chunk), lambda j: (0, j)),
                  pl.BlockSpec((1, chunk), lambda j: (0, j))],
        out_specs=pl.BlockSpec((rows, chunk), lambda j: (0, j)),
        out_shape=jax.ShapeDtypeStruct((rows, n_out), F32),
        compiler_params=_params(("arbitrary",)),
        name="modulation",
    )(c_all, w_mod, b_mod)


def _to_streams(ref, val, dil, stage):
    if dil == 1:
        ref[...] = val.astype(ref.dtype).reshape(ref.shape)
        return
    n = val.shape[0] // dil
    for c, half in enumerate(stage):
        half[...] = val[:, c * LANES:(c + 1) * LANES]
        for r in range(dil):
            lo = r * GROUP_WIDTH + c * LANES
            ref[0, :, lo:lo + LANES] = half[pl.ds(r, n, stride=dil), :].astype(ref.dtype)


def _from_streams(ref, dil, stage):
    if dil == 1:
        return ref[...].astype(F32).reshape(ref.shape[-2:])
    n = ref.shape[1]
    for c, half in enumerate(stage):
        for r in range(dil):
            lo = r * GROUP_WIDTH + c * LANES
            half[pl.ds(r, n, stride=dil), :] = ref[0, :, lo:lo + LANES].astype(F32)
    return jnp.concatenate([half[...] for half in stage], axis=1)


def _project_common(hb, win_ref, outs, dils, stages):
    (q_refs, k_refs, v_refs) = outs

    def proj(lo, width):
        return jnp.dot(hb, win_ref[:, lo:lo + width], preferred_element_type=F32)

    zq = proj(0, ATTN_WIDTH)
    for g in range(N_GROUPS):
        _to_streams(q_refs[g], zq[:, g * GROUP_WIDTH:(g + 1) * GROUP_WIDTH], dils[g], stages[g][0])
    zkvs = []
    for g in range(N_GROUPS):
        zkv = proj(COL_KV + 2 * GROUP_WIDTH * g, 2 * GROUP_WIDTH)
        _to_streams(k_refs[g], zkv[:, :GROUP_WIDTH], dils[g], stages[g][1])
        _to_streams(v_refs[g], zkv[:, GROUP_WIDTH:], dils[g], stages[g][2])
        zkvs.append(zkv)
    u = proj(COL_ULIN, CONV_CH) * _sigmoid(proj(COL_UGATE, CONV_CH))
    return zkvs, u, proj


def _k1_kernel(tail_rows, n_tiles,
               x_ref, sh_ref, sc_ref, g_ref, win_ref, dww_ref, dwb_ref, lng_ref, lnb_ref, wco_ref,
               q0, q1, q2, k0, v0, k1, v1, k2, v2, kvt0, kvt1, kvt2, ga_ref, sgb_ref, convp_ref, uext, sbuf, ushift, sga, *stage_refs):
    i = pl.program_id(1)
    tm = x_ref.shape[1]

    @pl.when(i == 0)
    def _():
        uext[0:CONV_HIST, :] = jnp.zeros((CONV_HIST, CONV_CH), F32)

    h = _rms_modulate(x_ref[0], g_ref[...], sc_ref[0], sh_ref[0])
    hb = h.astype(BF16)

    def proj(lo, width):
        return jnp.dot(hb, win_ref[:, lo:lo + width], preferred_element_type=F32)

    uext[CONV_HIST:CONV_HIST + tm, :] = proj(COL_ULIN, CONV_CH) * _sigmoid(proj(COL_UGATE, CONV_CH))
    base = CONV_HIST - (CONV_WIDTH - 1)
    span = tm + CONV_HIST - SUBLANES
    for b in range(1, SUBLANES):
        ushift[b - 1, 0:span, :] = uext[b:b + span, :]

    def conv_chunk(r0):
        accs = [dwb_ref[...]] * (CONV_CHUNK // SUBLANES)
        for k in range(CONV_WIDTH):
            b = (base + k) % SUBLANES
            lo = r0 + base + k - b
            w8 = dww_ref[k]
            for q in range(len(accs)):
                rows = slice(lo + q * SUBLANES, lo + (q + 1) * SUBLANES)
                src = uext[rows, :] if b == 0 else ushift[b - 1, rows, :]
                accs[q] = accs[q] + w8 * src
        s = _conv_tail(jnp.concatenate(accs, axis=0), lng_ref[...], lnb_ref[...])
        sbuf[r0:r0 + CONV_CHUNK, :] = s.astype(BF16)

    dils = tuple(dil for _, dil in DILATED_GROUPS)
    pairs = [stage_refs[j:j + STAGE_HALVES] for j in range(0, len(stage_refs), STAGE_HALVES)]
    stages = [(None,) * 3] + [pairs[3 * (g - 1):3 * g] for g in range(1, N_GROUPS)]
    pieces = []
    for g, q_ref in enumerate((q0, q1, q2)):
        def q_piece(g=g, q_ref=q_ref):
            _to_streams(q_ref, proj(g * GROUP_WIDTH, GROUP_WIDTH), dils[g], stages[g][0])
        pieces.append(q_piece)
    for g, (k_ref, v_ref, kvt) in enumerate(((k0, v0, kvt0), (k1, v1, kvt1), (k2, v2, kvt2))):
        for half, ref in enumerate((k_ref, v_ref)):
            def kv_piece(g=g, half=half, ref=ref, kvt=kvt):
                z = proj(COL_KV + (2 * g + half) * GROUP_WIDTH, GROUP_WIDTH)
                _to_streams(ref, z, dils[g], stages[g][1 + half])
                tr = kvt.shape[2]

                @pl.when(i >= n_tiles - tail_rows[g] // tr)
                def _():
                    kvt[0, half * GROUP_WIDTH:(half + 1) * GROUP_WIDTH, :] = z[tm - tr:, :].T
            pieces.append(kv_piece)
    for c in range(D_MODEL // GROUP_WIDTH):
        cols = slice(c * GROUP_WIDTH, (c + 1) * GROUP_WIDTH)

        def gb_piece(c=c, cols=cols):
            sgb_ref[:, cols] = _sigmoid(proj(COL_GB + c * GROUP_WIDTH, GROUP_WIDTH)).astype(BF16)

        def ga_piece(c=c, cols=cols):
            sga[:, cols] = _sigmoid(proj(COL_GA + c * GROUP_WIDTH, GROUP_WIDTH))
        pieces += [gb_piece, ga_piece]
    n_qkv = 3 * N_GROUPS
    for piece in pieces[:n_qkv]:
        piece()
    for r0 in range(0, tm, CONV_CHUNK):
        conv_chunk(r0)
    for piece in pieces[n_qkv:]:
        piece()
    last = uext[tm:tm + CONV_HIST, :]
    uext[0:CONV_HIST, :] = last
    convp_ref[0] = last
    a = jnp.dot(sbuf[...], wco_ref[...], preferred_element_type=F32)
    ga_ref[...] = (sga[...] * a).astype(BF16)


def _k1_prompt(x, sh1, sc1, norm_g, w_in_b, dw_w, dw_b, ln_g, ln_b, w_co_b):
    B, S, _ = x.shape
    tm = min(TOKEN_TILE, S)
    nt = S // tm
    tail_rows = tuple(min(win, S) for win, _ in DILATED_GROUPS)
    tail_blk = tuple(min(t, tm) for t in tail_rows)

    def tok_spec(width):
        return pl.BlockSpec((tm, width), lambda b, i: (b * nt + i, 0))

    def tail_spec(g):
        first = nt - tail_rows[g] // tail_blk[g]
        return pl.BlockSpec((1, 2 * GROUP_WIDTH, tail_blk[g]), lambda b, i: (b, 0, jnp.maximum(i - first, 0)))

    def stream_shape(dil):
        return jax.ShapeDtypeStruct((B, S // dil, dil * GROUP_WIDTH), BF16)

    def stream_spec(dil):
        return pl.BlockSpec((1, tm // dil, dil * GROUP_WIDTH), lambda b, i: (b, i, 0))

    dils = [dil for _, dil in DILATED_GROUPS]
    qkv_order = [dils[0], dils[1], dils[2]] + [d for d in dils for _ in range(2)]
    mod_spec = pl.BlockSpec((1, 1, D_MODEL), lambda b, i: (b, 0, 0))
    out_shape = ([stream_shape(d) for d in qkv_order]
                 + [jax.ShapeDtypeStruct((B, 2 * GROUP_WIDTH, tail_rows[g]), F32) for g in range(N_GROUPS)]
                 + [jax.ShapeDtypeStruct((B * S, D_MODEL), BF16)] * 2
                 + [jax.ShapeDtypeStruct((B, CONV_HIST, CONV_CH), F32)])
    out_specs = ([stream_spec(d) for d in qkv_order] + [tail_spec(g) for g in range(N_GROUPS)]
                 + [tok_spec(D_MODEL)] * 2 + [pl.BlockSpec((1, CONV_HIST, CONV_CH), lambda b, i: (b, 0, 0))])
    n_stage = 3 * sum(1 for d in dils if d > 1)
    return pl.pallas_call(
        functools.partial(_k1_kernel, tail_rows, nt),
        grid=(B, nt),
        in_specs=[pl.BlockSpec((1, tm, D_MODEL), lambda b, i: (b, i, 0)), mod_spec, mod_spec,
                  _const_spec((1, D_MODEL)), _weight_spec((D_MODEL, IN_COLS)),
                  _const_spec((CONV_WIDTH, SUBLANES, CONV_CH)), _const_spec((SUBLANES, CONV_CH)),
                  _const_spec((1, CONV_CH)), _const_spec((1, CONV_CH)), _weight_spec((CONV_CH, D_MODEL))],
        out_specs=out_specs,
        out_shape=out_shape,
        scratch_shapes=([pltpu.VMEM((CONV_HIST + tm, CONV_CH), F32), pltpu.VMEM((tm, CONV_CH), BF16),
                         pltpu.VMEM((SUBLANES - 1, CONV_HIST + tm - SUBLANES, CONV_CH), F32),
                         pltpu.VMEM((tm, D_MODEL), F32)]
                        + [pltpu.VMEM((tm, LANES), F32)] * (n_stage * STAGE_HALVES)),
        compiler_params=_params(("arbitrary", "arbitrary")),
        name="inproj_prompt",
    )(x, sh1, sc1, norm_g, w_in_b, dw_w, dw_b, ln_g, ln_b, w_co_b)


def _k1s_kernel(n_steps, x_ref, sh_ref, sc_ref, g_ref, win_ref, st_ref, dww_ref, dwb_ref, lng_ref, lnb_ref, wco_ref,
                q0, q1, q2, k0, v0, k1, v1, k2, v2, ga_ref, sgb_ref, conv_ref):
    bd = st_ref.shape[1]
    hist = st_ref.shape[0]
    h = _rms_modulate(x_ref[...], g_ref[...], sc_ref[...], sh_ref[...])
    hb = h.astype(BF16)
    _, u, proj = _project_common(hb, win_ref, ((q0, q1, q2), (k0, k1, k2), (v0, v1, v2)),
                                 (1,) * N_GROUPS, [(None,) * 3] * N_GROUPS)

    def ext(j):
        return st_ref[j] if j < hist else u[(j - hist) * bd:(j - hist + 1) * bd, :]

    outs = []
    for t in range(n_steps):
        acc = jnp.zeros((bd, CONV_CH), F32) + dwb_ref[...]
        for k in range(CONV_WIDTH):
            acc = acc + dww_ref[k:k + 1, :] * ext(t + k + hist - (CONV_WIDTH - 1))
        outs.append(acc)
    for j in range(hist):
        conv_ref[j] = ext(j + n_steps)
    s = _conv_tail(jnp.concatenate(outs, axis=0), lng_ref[...], lnb_ref[...])
    a = jnp.dot(s.astype(BF16), wco_ref[...], preferred_element_type=F32)
    ga_ref[...] = (_sigmoid(proj(COL_GA, D_MODEL)) * a).astype(BF16)
    sgb_ref[...] = _sigmoid(proj(COL_GB, D_MODEL)).astype(BF16)


def _k1_sample(x_tm, sh1, sc1, norm_g, w_in_b, state_tm, dw_w, dw_b, ln_g, ln_b, w_co_b, n_steps):
    T = x_tm.shape[0]
    hist, bd, _ = state_tm.shape
    out_shape = ([jax.ShapeDtypeStruct((T, GROUP_WIDTH), F32)] * 9
                 + [jax.ShapeDtypeStruct((T, D_MODEL), BF16)] * 2
                 + [jax.ShapeDtypeStruct((hist, bd, CONV_CH), F32)])
    out_specs = ([_const_spec((T, GROUP_WIDTH))] * 9 + [_const_spec((T, D_MODEL))] * 2
                 + [_const_spec((hist, bd, CONV_CH))])
    return pl.pallas_call(
        functools.partial(_k1s_kernel, n_steps),
        grid=(1,),
        in_specs=[_const_spec((T, D_MODEL)), _const_spec((T, D_MODEL)), _const_spec((T, D_MODEL)),
                  _const_spec((1, D_MODEL)), _const_spec((D_MODEL, IN_COLS)), _const_spec((hist, bd, CONV_CH)),
                  _const_spec((CONV_WIDTH, CONV_CH)), _const_spec((1, CONV_CH)),
                  _const_spec((1, CONV_CH)), _const_spec((1, CONV_CH)), _const_spec((CONV_CH, D_MODEL))],
        out_specs=out_specs,
        out_shape=out_shape,
        compiler_params=_params(("arbitrary",)),
        name="inproj_sample",
    )(x_tm, sh1, sc1, norm_g, w_in_b, state_tm, dw_w, dw_b, ln_g, ln_b, w_co_b)


def _t5_bucket(dist):
    max_exact = N_BUCKETS // 2
    d_f = jnp.maximum(dist, 1).astype(F32)
    large = max_exact + (jnp.log(d_f / max_exact) / math.log(MAX_DISTANCE / max_exact)
                         * (N_BUCKETS - max_exact)).astype(jnp.int32)
    large = jnp.minimum(large, N_BUCKETS - 1)
    return jnp.where(dist < max_exact, dist, large)


def _bucket_lookup(rel_bias_g, dist):
    bucket = _t5_bucket(dist)
    out = jnp.zeros((rel_bias_g.shape[1],) + dist.shape, F32)
    for b in range(N_BUCKETS):
        out = jnp.where(bucket[None] == b, rel_bias_g[b].reshape((-1,) + (1,) * dist.ndim), out)
    return out


def _prompt_bias(rel_bias_g, dil, n_keys):
    blk = n_keys
    i = jnp.arange(blk)[:, None]
    j = jnp.arange(2 * blk)[None, :]
    rel = i - j + blk
    valid = (rel >= 0) & (rel <= n_keys)
    bias = _bucket_lookup(rel_bias_g, jnp.clip(rel, 0, n_keys) * dil)
    bias = jnp.where(valid[None], bias, NEG_INF)
    return bias.reshape(HEADS_PER_GROUP * blk, 2 * blk).astype(F32)


def _sample_bias(rel_bias_g, dil, n_keys, lc, n_steps, q_rows):
    t = jnp.arange(q_rows)[:, None]
    pos = jnp.arange(lc)[None, :]
    dist_c = lc + t - pos
    dist_n = t - (jnp.arange(128)[None, :] - (128 - n_steps))

    def table(dist, extra):
        ok = (dist >= 0) & (dist % dil == 0) & (dist // dil <= n_keys) & extra & (t < n_steps)
        b = _bucket_lookup(rel_bias_g, jnp.clip(dist, 0, None))
        b = jnp.where(ok[None], b, NEG_INF)
        return jnp.where((t >= n_steps)[None], 0.0, b).astype(F32)

    lane_ok = jnp.arange(128)[None, :] >= 128 - n_steps
    return table(dist_c, True), table(dist_n, lane_ok)


def _k2_kernel(q_ref, k_ref, kh_ref, v_ref, vh_ref, bias_ref, o_ref, lse_ref, kbuf, vbuf):
    i = pl.program_id(2)
    rows = q_ref.shape[1]
    n_streams = q_ref.shape[2] // GROUP_WIDTH
    blk = ATTN_BLOCK
    kbuf[0:blk, :] = kh_ref[0]
    kbuf[blk:blk + rows, :] = k_ref[0]
    vbuf[0:blk, :] = vh_ref[0]
    vbuf[blk:blk + rows, :] = v_ref[0]
    lane_head = lax.broadcasted_iota(jnp.int32, (blk, GROUP_WIDTH), 1) // HEAD_DIM
    col = lax.broadcasted_iota(jnp.int32, (HEADS_PER_GROUP * blk, 2 * blk), 1)
    first_mask = jnp.where((col < blk) & (i == 0), NEG_INF, 0.0).astype(F32)
    bias = bias_ref[...]
    for st in range(n_streams):
        cols = slice(st * GROUP_WIDTH, (st + 1) * GROUP_WIDTH)
        for j in range(rows // blk):
            qb = q_ref[0, j * blk:(j + 1) * blk, cols]
            q4 = jnp.concatenate([jnp.where(lane_head == h, qb, jnp.zeros_like(qb)) for h in range(HEADS_PER_GROUP)],
                                 axis=0)
            kc = kbuf[j * blk:(j + 2) * blk, cols]
            vc = vbuf[j * blk:(j + 2) * blk, cols]
            s = lax.dot_general(q4, kc, (((1,), (1,)), ((), ())), preferred_element_type=F32) + bias
            if j == 0:
                s = s + first_mask
            m = jnp.max(s, axis=-1, keepdims=True)
            p = jnp.exp(s - m)
            l = jnp.sum(p, axis=-1, keepdims=True)
            o4 = jnp.dot(p.astype(BF16), vc, preferred_element_type=F32) * (1.0 / l)
            lse4 = m + jnp.log(l)
            o = jnp.zeros((blk, GROUP_WIDTH), F32)
            lse = jnp.zeros((blk, GROUP_WIDTH), F32)
            for h in range(HEADS_PER_GROUP):
                sel = lane_head == h
                o = jnp.where(sel, o4[h * blk:(h + 1) * blk, :], o)
                lse = jnp.where(sel, lse4[h * blk:(h + 1) * blk, :], lse)
            o_ref[0, j * blk:(j + 1) * blk, cols] = o.astype(o_ref.dtype)
            lse_ref[0, j * blk:(j + 1) * blk, cols] = lse


def _k2_prompt(q, k, v, bias, dil):
    B, L, _ = q.shape
    rows = min(ATTN_ROWS, L)
    per = rows // ATTN_BLOCK
    n_streams = min(dil, max(1, ATTN_ROWS // rows))
    width = n_streams * GROUP_WIDTH
    main = pl.BlockSpec((1, rows, width), lambda b, r, i: (b, i, r))
    halo = pl.BlockSpec((1, ATTN_BLOCK, width), lambda b, r, i: (b, jnp.maximum(i * per - 1, 0), r))
    return pl.pallas_call(
        _k2_kernel,
        grid=(B, dil // n_streams, L // rows),
        in_specs=[main, main, halo, main, halo, _const_spec(bias.shape)],
        out_specs=[main, main],
        out_shape=[jax.ShapeDtypeStruct(q.shape, BF16), jax.ShapeDtypeStruct(q.shape, F32)],
        scratch_shapes=[pltpu.VMEM((ATTN_BLOCK + rows, width), BF16)] * 2,
        compiler_params=_params(("arbitrary", "arbitrary", "arbitrary")),
        name=f"attn_prompt_d{dil}",
    )(q, k, k, v, v, bias)


def _k2s_kernel(n_steps, q_ref, c0, c1, c2, t0, t1, t2, bc0, bc1, bc2, bn0, bn1, bn2,
                nc0, nc1, nc2, o0, o1, o2, l0, l1, l2):
    lane = lax.broadcasted_iota(jnp.int32, (2 * GROUP_WIDTH, 128), 1)
    keep = 128 - n_steps
    groups = ((c0, t0, bc0, bn0, nc0, o0, l0), (c1, t1, bc1, bn1, nc1, o1, l1), (c2, t2, bc2, bn2, nc2, o2, l2))
    for g, (c_ref, t_ref, bc_ref, bn_ref, nc_ref, o_ref, l_ref) in enumerate(groups):
        lc = c_ref.shape[2]
        n_tiles = lc // 128
        new_rows = t_ref[0]
        tail = jnp.concatenate([jnp.zeros((128 - new_rows.shape[0], new_rows.shape[1]), F32), new_rows], axis=0).T
        cur = pltpu.roll(c_ref[0, :, 0:128], keep, 1)
        for c in range(n_tiles):
            nxt = pltpu.roll(c_ref[0, :, (c + 1) * 128:(c + 2) * 128], keep, 1) if c + 1 < n_tiles else tail
            nc_ref[0, :, c * 128:(c + 1) * 128] = jnp.where(lane < keep, cur, nxt)
            cur = nxt
        for h in range(HEADS_PER_GROUP):
            lo = g * GROUP_WIDTH + h * HEAD_DIM
            qh = q_ref[0, :, lo:lo + HEAD_DIM].astype(BF16)
            kh = c_ref[0, h * HEAD_DIM:(h + 1) * HEAD_DIM, :].astype(BF16)
            vh = c_ref[0, GROUP_WIDTH + h * HEAD_DIM:GROUP_WIDTH + (h + 1) * HEAD_DIM, :].astype(BF16)
            kt = tail[h * HEAD_DIM:(h + 1) * HEAD_DIM, :].astype(BF16)
            vt = tail[GROUP_WIDTH + h * HEAD_DIM:GROUP_WIDTH + (h + 1) * HEAD_DIM, :].astype(BF16)
            sc = jnp.dot(qh, kh, preferred_element_type=F32) + bc_ref[h]
            sn = jnp.dot(qh, kt, preferred_element_type=F32) + bn_ref[h]
            m = jnp.maximum(jnp.max(sc, axis=-1, keepdims=True), jnp.max(sn, axis=-1, keepdims=True))
            pc = jnp.exp(sc - m)
            pn = jnp.exp(sn - m)
            l = jnp.sum(pc, axis=-1, keepdims=True) + jnp.sum(pn, axis=-1, keepdims=True)
            nt_dims = (((1,), (1,)), ((), ()))
            o = (lax.dot_general(pc.astype(BF16), vh, nt_dims, preferred_element_type=F32)
                 + lax.dot_general(pn.astype(BF16), vt, nt_dims, preferred_element_type=F32)) * (1.0 / l)
            o_ref[0, :, h * HEAD_DIM:(h + 1) * HEAD_DIM] = o
            l_ref[0, :, h * HEAD_DIM:(h + 1) * HEAD_DIM] = jnp.broadcast_to(m + jnp.log(l), o.shape)


def _k2_sample(q_b, caches, tails, biases_c, biases_n, n_steps):
    bd, q_rows, _ = q_b.shape
    per_b = lambda shape: pl.BlockSpec((1,) + shape[1:], lambda b: (b,) + (0,) * (len(shape) - 1))
    ins = [q_b] + list(caches) + list(tails) + list(biases_c) + list(biases_n)
    in_specs = ([per_b(q_b.shape)] + [per_b(c.shape) for c in caches] + [per_b(t.shape) for t in tails]
                + [_const_spec(b.shape) for b in biases_c] + [_const_spec(b.shape) for b in biases_n])
    o_shape = jax.ShapeDtypeStruct((bd, q_rows, GROUP_WIDTH), F32)
    out_shape = [jax.ShapeDtypeStruct(c.shape, F32) for c in caches] + [o_shape] * 6
    out_specs = [per_b(c.shape) for c in caches] + [per_b(o_shape.shape)] * 6
    return pl.pallas_call(
        functools.partial(_k2s_kernel, n_steps),
        grid=(bd,),
        in_specs=in_specs,
        out_specs=out_specs,
        out_shape=out_shape,
        compiler_params=_params(("arbitrary",)),
        name="attn_sample",
    )(*ins)


def _k3_kernel(x_ref, o0, o1, o2, l0, l1, l2, ga_ref, sgb_ref, g1_ref, sh2_ref, sc2_ref, gf_ref,
               wao_ref, wo_ref, wr_ref, br_ref, before_ref, cnt_in_ref,
               x2_ref, h2_ref, eid_ref, wts_ref, rank_ref, cnt_ref, carry, *stage_refs):
    @pl.when(pl.program_id(0) == 0)
    def _():
        carry[...] = cnt_in_ref[...]

    dils = tuple(x_ref.shape[0] // r.shape[-2] for r in (o0, o1, o2))
    pairs = [stage_refs[j:j + STAGE_HALVES] for j in range(0, len(stage_refs), STAGE_HALVES)]
    os_ = [_from_streams(r, d, pairs[2 * g]) for g, (r, d) in enumerate(zip((o0, o1, o2), dils))]
    ls = [_from_streams(r, d, pairs[2 * g + 1]) for g, (r, d) in enumerate(zip((l0, l1, l2), dils))]
    m = jnp.maximum(jnp.maximum(ls[0], ls[1]), ls[2])
    ws = [jnp.exp(l - m) for l in ls]
    den = ws[0] + ws[1] + ws[2]
    o = (ws[0] * os_[0] + ws[1] * os_[1] + ws[2] * os_[2]) / den
    b = jnp.dot(o.astype(BF16), wao_ref[...], preferred_element_type=F32)
    mixed = ga_ref[...].astype(F32) + sgb_ref[...].astype(F32) * b
    x2 = x_ref[...] + g1_ref[0] * jnp.dot(mixed.astype(BF16), wo_ref[...], preferred_element_type=F32)
    x2_ref[...] = x2
    h2 = _rms_modulate(x2, gf_ref[...], sc2_ref[0], sh2_ref[0])
    h2_ref[...] = h2
    lt = lax.dot_general(wr_ref[...], h2, (((1,), (1,)), ((), ())), preferred_element_type=F32,
                         precision=lax.Precision.HIGHEST) + br_ref[...]
    tm = h2.shape[0]
    gl = lt[0:8, :]
    gmax = jnp.max(gl, axis=0, keepdims=True)
    r8 = lax.broadcasted_iota(jnp.int32, (8, tm), 0)
    grp = jnp.min(jnp.where(gl == gmax, r8, 8), axis=0, keepdims=True)
    p_grp = 1.0 / jnp.sum(jnp.exp(gl - gmax), axis=0, keepdims=True)
    es = jnp.zeros((EXPERTS_PER_GROUP, tm), F32)
    for g in range(N_EXPERT_GROUPS):
        es = jnp.where(grp == g, lt[8 + 8 * g:16 + 8 * g, :], es)
    v1 = jnp.max(es, axis=0, keepdims=True)
    i1 = jnp.min(jnp.where(es == v1, r8, 8), axis=0, keepdims=True)
    rest = jnp.where(r8 == i1, -jnp.inf, es)
    v2 = jnp.max(rest, axis=0, keepdims=True)
    i2 = jnp.min(jnp.where(rest == v2, r8, 8), axis=0, keepdims=True)
    e21 = jnp.exp(v2 - v1)
    w1 = p_grp / (1.0 + e21)
    e1 = grp * EXPERTS_PER_GROUP + i1
    e2 = grp * EXPERTS_PER_GROUP + i2
    eid_ref[0, 0:1, :] = e1
    eid_ref[0, 1:2, :] = e2
    wts_ref[0, 0:1, :] = w1
    wts_ref[0, 1:2, :] = w1 * e21
    r_e = lax.broadcasted_iota(jnp.int32, (N_EXPERTS, tm), 0)
    hit1 = r_e == e1
    hit2 = r_e == e2
    both = jnp.where(hit1 | hit2, 1.0, 0.0)
    base = carry[...] + jnp.dot(both.astype(BF16), before_ref[...], preferred_element_type=F32)
    rank_ref[0, 0:1, :] = jnp.sum(jnp.where(hit1, base, 0.0), axis=0, keepdims=True).astype(jnp.int32)
    rank_ref[0, 1:2, :] = jnp.sum(jnp.where(hit2, base, 0.0), axis=0, keepdims=True).astype(jnp.int32)
    total = carry[...] + jnp.sum(both, axis=1, keepdims=True)
    carry[...] = total
    cnt_ref[...] = total


def _k3(x, o_l, g1, sh2, sc2, ga, sgb, norm_ffn_g, w_ao_b, w_o_b, w_r, b_r, cnt_in, tiles_per_mod):
    T = x.shape[0]
    tm = min(TOKEN_TILE, T)
    nt = T // tm
    tok = lambda w: pl.BlockSpec((tm, w), lambda t: (t, 0))
    mod = pl.BlockSpec((1,) + g1.shape[1:], lambda t: (t // tiles_per_mod, 0, 0))
    small = pl.BlockSpec((1, 2, tm), lambda t: (t, 0, 0))
    small_i = jax.ShapeDtypeStruct((nt, 2, tm), jnp.int32)
    before = (jnp.arange(tm)[:, None] < jnp.arange(tm)[None, :]).astype(BF16)

    def attn_spec(a):
        if a.ndim == 2:
            return tok(GROUP_WIDTH)
        dil = a.shape[2] // GROUP_WIDTH
        return pl.BlockSpec((1, tm // dil, a.shape[2]), lambda t: (t // tiles_per_mod, t % tiles_per_mod, 0))

    attn_in = [o_l[0][0], o_l[1][0], o_l[2][0], o_l[0][1], o_l[1][1], o_l[2][1]]
    return pl.pallas_call(
        _k3_kernel,
        grid=(nt,),
        in_specs=[tok(D_MODEL)] + [attn_spec(a) for a in attn_in] + [tok(D_MODEL)] * 2 + [mod] * 3
                 + [_const_spec((1, D_MODEL)), _weight_spec(w_ao_b.shape), _weight_spec(w_o_b.shape),
                    _const_spec(w_r.shape), _const_spec(b_r.shape), _weight_spec((tm, tm)),
                    _const_spec((N_EXPERTS, 1))],
        out_specs=[tok(D_MODEL), tok(D_MODEL), small, small, small, _const_spec((N_EXPERTS, 1))],
        out_shape=[jax.ShapeDtypeStruct((T, D_MODEL), F32), jax.ShapeDtypeStruct((T, D_MODEL), F32),
                   small_i, jax.ShapeDtypeStruct((nt, 2, tm), F32), small_i,
                   jax.ShapeDtypeStruct((N_EXPERTS, 1), F32)],
        scratch_shapes=([pltpu.VMEM((N_EXPERTS, 1), F32)]
                        + [pltpu.VMEM((tm, LANES), F32)] * (2 * N_GROUPS * STAGE_HALVES)),
        compiler_params=_params(("arbitrary",)),
        name="merge_router",
    )(x, *attn_in, ga, sgb, g1, sh2, sc2, norm_ffn_g, w_ao_b, w_o_b, w_r, b_r, before, cnt_in)


def _rows_wait(n_rows, hbm, vmem, sem):
    pltpu.make_async_copy(hbm.at[pl.ds(0, n_rows)], vmem, sem).wait()


def _dispatch_kernel(n_tiles, n_first, dest_ref, fill_ref, ha_ref, hb_ref, xs_ref, buf, sem, zbuf, zsem):
    i = pl.program_id(0)
    slot = i % 2 if n_tiles > 1 else 0
    tm = ha_ref.shape[0]

    def drain(s):
        for _ in range(2):
            _rows_wait(tm, xs_ref, buf.at[s], sem.at[s])

    @pl.when(i == 0)
    def _():
        zbuf[...] = jnp.zeros(zbuf.shape, F32)

        def fill(row):
            return pltpu.make_async_copy(zbuf, xs_ref.at[pl.ds(pl.multiple_of(row, MOE_ROWS), MOE_ROWS)], zsem)

        n_blocks = xs_ref.shape[0] // MOE_ROWS
        n_used = fill_ref[N_EXPERTS]

        def start_block(j, carry):
            fill(j * MOE_ROWS).start()
            return carry

        def wait_block(j, carry):
            fill(j * MOE_ROWS).wait()
            return carry

        for e in range(N_EXPERTS):
            @pl.when(fill_ref[e] >= 0)
            def _(e=e):
                fill(fill_ref[e]).start()
        lax.fori_loop(n_used, n_blocks, start_block, 0)
        for e in range(N_EXPERTS):
            @pl.when(fill_ref[e] >= 0)
            def _(e=e):
                fill(fill_ref[e]).wait()
        lax.fori_loop(n_used, n_blocks, wait_block, 0)

    if n_tiles > 2:
        @pl.when(i >= 2)
        def _():
            drain(slot)

    @pl.when(i < n_first)
    def _():
        buf[slot] = ha_ref[...]

    @pl.when(i >= n_first)
    def _():
        buf[slot] = hb_ref[...]

    for r in range(tm):
        for k in range(2):
            pltpu.make_async_copy(buf.at[slot, pl.ds(r, 1)], xs_ref.at[pl.ds(dest_ref[0, k, r], 1)], sem.at[slot]).start()

    if n_tiles == 1:
        drain(0)
    else:
        @pl.when(i == n_tiles - 1)
        def _():
            drain(slot)
            drain(1 - slot)


def _dispatch(dest, fill_rows, h_a, h_b, n_slots):
    nt, _, tm = dest.shape
    n_first = h_a.shape[0] // tm
    assert h_a.shape[0] % tm == 0 and h_b.shape[0] == (nt - n_first) * tm
    return pl.pallas_call(
        functools.partial(_dispatch_kernel, nt, n_first),
        grid=(nt,),
        in_specs=[pl.BlockSpec((1, 2, tm), lambda t: (t, 0, 0), memory_space=pltpu.SMEM),
                  pl.BlockSpec(memory_space=pltpu.SMEM),
                  pl.BlockSpec((tm, D_MODEL), lambda t: (jnp.minimum(t, n_first - 1), 0)),
                  pl.BlockSpec((tm, D_MODEL), lambda t: (jnp.maximum(t - n_first, 0), 0))],
        out_specs=pl.BlockSpec(memory_space=pl.ANY),
        out_shape=jax.ShapeDtypeStruct((n_slots, D_MODEL), F32),
        scratch_shapes=[pltpu.VMEM((2, tm, D_MODEL), F32), pltpu.SemaphoreType.DMA((2,)),
                        pltpu.VMEM((MOE_ROWS, D_MODEL), F32), pltpu.SemaphoreType.DMA(())],
        compiler_params=_params(("arbitrary",)),
        name="dispatch_rows",
    )(dest, fill_rows, h_a, h_b)


def _k4_kernel(be_ref, nused_ref, x_ref, wg_ref, wu_ref, wd_ref, y_ref, wg_b, wu_b, wd_b):
    i = pl.program_id(0)

    @pl.when((i == 0) | (be_ref[i] != be_ref[jnp.maximum(i - 1, 0)]))
    def _():
        wg_b[...] = wg_ref[0].astype(BF16)
        wu_b[...] = wu_ref[0].astype(BF16)
        wd_b[...] = wd_ref[0].astype(BF16)

    @pl.when(i < nused_ref[0])
    def _():
        xb = x_ref[...].astype(BF16)
        gate = jnp.dot(xb, wg_b[...], preferred_element_type=F32)
        up = jnp.dot(xb, wu_b[...], preferred_element_type=F32)
        mid = (_silu(gate) * up).astype(BF16)
        y_ref[...] = jnp.dot(mid, wd_b[...], preferred_element_type=F32)

    @pl.when(i >= nused_ref[0])
    def _():
        y_ref[...] = jnp.zeros(y_ref.shape, F32)


def _k4(block_expert, n_used, xs, w_gate, w_up, w_down):
    n_blocks = block_expert.shape[0]
    rows = MOE_ROWS
    weight = lambda shape: pl.BlockSpec((1,) + shape, lambda i, be, nu: (be[i], 0, 0))
    grid_spec = pltpu.PrefetchScalarGridSpec(
        num_scalar_prefetch=2,
        grid=(n_blocks,),
        in_specs=[pl.BlockSpec((rows, D_MODEL), lambda i, be, nu: (jnp.minimum(i, nu[0] - 1), 0)),
                  weight((D_MODEL, D_FF_EXPERT)), weight((D_MODEL, D_FF_EXPERT)), weight((D_FF_EXPERT, D_MODEL))],
        out_specs=pl.BlockSpec((rows, D_MODEL), lambda i, be, nu: (i, 0)),
        scratch_shapes=[pltpu.VMEM((D_MODEL, D_FF_EXPERT), BF16), pltpu.VMEM((D_MODEL, D_FF_EXPERT), BF16),
                        pltpu.VMEM((D_FF_EXPERT, D_MODEL), BF16)],
    )
    return pl.pallas_call(
        _k4_kernel,
        grid_spec=grid_spec,
        out_shape=jax.ShapeDtypeStruct((n_blocks * rows, D_MODEL), F32),
        compiler_params=_params(("arbitrary",)),
        name="expert_blocks",
    )(block_expert, n_used, xs, w_gate, w_up, w_down)


def _k5_kernel(n_tiles, idx_ref, x2_ref, w_ref, g2_ref, gfin_ref, ys_hbm, y_ref, ybuf, sem):
    i = pl.program_id(0)
    tm = x2_ref.shape[0]

    @pl.when(i < n_tiles)
    def _():
        slot = i % 2
        for r in range(tm):
            for k in range(2):
                pltpu.make_async_copy(ys_hbm.at[pl.ds(idx_ref[0, k, r], 1)], ybuf.at[slot, k, pl.ds(r, 1)],
                                      sem.at[slot]).start()

    @pl.when(i >= 1)
    def _():
        slot = (i - 1) % 2
        for k in range(2):
            _rows_wait(tm, ys_hbm, ybuf.at[slot, k], sem.at[slot])
        w = w_ref[...]
        f = ybuf[slot, 0] * w[:, 0:1] + ybuf[slot, 1] * w[:, 1:2]
        y = x2_ref[...] + g2_ref[0] * f
        r = lax.rsqrt(jnp.mean(y * y, axis=-1, keepdims=True) + EPS)
        y_ref[...] = (y * r) * gfin_ref[...]


def _k5(dest, x2, wts, g2, norm_final_g, ys, tiles_per_mod):
    nt, _, tm = dest.shape
    T = x2.shape[0]
    prev = lambda t: jnp.maximum(t - 1, 0)
    return pl.pallas_call(
        functools.partial(_k5_kernel, nt),
        grid=(nt + 1,),
        in_specs=[pl.BlockSpec((1, 2, tm), lambda t: (jnp.minimum(t, nt - 1), 0, 0), memory_space=pltpu.SMEM),
                  pl.BlockSpec((tm, D_MODEL), lambda t: (prev(t), 0)),
                  pl.BlockSpec((tm, 2), lambda t: (prev(t), 0)),
                  pl.BlockSpec((1,) + g2.shape[1:], lambda t: (prev(t) // tiles_per_mod, 0, 0)),
                  _const_spec((1, D_MODEL)), pl.BlockSpec(memory_space=pl.ANY)],
        out_specs=pl.BlockSpec((tm, D_MODEL), lambda t: (prev(t), 0)),
        out_shape=jax.ShapeDtypeStruct((T, D_MODEL), F32),
        scratch_shapes=[pltpu.VMEM((2, 2, tm, D_MODEL), F32), pltpu.SemaphoreType.DMA((2,))],
        compiler_params=_params(("arbitrary",)),
        name="combine_norm",
    )(dest, x2, wts, g2, norm_final_g, ys)


def _slot_tables(counts, n_blocks):
    padded = (counts + MOE_ROWS - 1) // MOE_ROWS * MOE_ROWS
    pend = jnp.cumsum(padded)
    pstart = pend - padded
    block_lo = jnp.arange(n_blocks, dtype=jnp.int32) * MOE_ROWS
    block_expert = jnp.minimum(jnp.sum(pend[None, :] <= block_lo[:, None], axis=1), N_EXPERTS - 1).astype(jnp.int32)
    n_used = (pend[-1] // MOE_ROWS).astype(jnp.int32).reshape(1)
    fill_rows = jnp.concatenate([jnp.where(padded > 0, pend - MOE_ROWS, -1).astype(jnp.int32), n_used])
    return pstart.astype(jnp.int32), block_expert, n_used, fill_rows


def _slots(eid, rank, pstart):
    sel = eid[..., None] == jnp.arange(N_EXPERTS, dtype=jnp.int32)
    return jnp.sum(jnp.where(sel, pstart, 0), axis=-1).astype(jnp.int32) + rank


def _cache_view(cache):
    bd, lc = cache.shape[:2]
    return cache.transpose(0, 2, 3, 4, 1).reshape(bd, 2 * GROUP_WIDTH, lc)


def _cache_unview(view):
    bd, _, lc = view.shape
    return view.reshape(bd, 2, HEADS_PER_GROUP, HEAD_DIM, lc).transpose(0, 4, 1, 2, 3)


def kernel(x_prompt, x_sample, c_prompt, c_sample, cache_kv_w128, cache_kv_w512, cache_kv_w2048, state_conv, rel_bias, norm_mix_g, norm_ffn_g, w_mod, b_mod, w_in, dw_w, dw_b, ln_conv_g, ln_conv_b, w_conv_out, w_attn_out, w_out, w_router_group, b_router_group, w_router_expert, b_router_expert, w_exp_gate, w_exp_up, w_exp_down, norm_final_g):
    assert norm_mix_g.shape[0] == 1, "single layer"
    B, S, D = x_prompt.shape
    Bd, Td, _ = x_sample.shape
    caches_in = (cache_kv_w128[0], cache_kv_w512[0], cache_kv_w2048[0])
    for (win, dil), c in zip(DILATED_GROUPS, caches_in):
        assert c.shape[1] >= (win // dil) * dil and c.shape[1] % 128 == 0 and S % (dil * ATTN_BLOCK) == 0

    wi = w_in[0]
    q_cols = wi[:, :ATTN_WIDTH] * (HEAD_DIM ** -0.5)
    kv_cols = []
    for g in range(N_GROUPS):
        kv_cols += [wi[:, ATTN_WIDTH + g * GROUP_WIDTH:ATTN_WIDTH + (g + 1) * GROUP_WIDTH],
                    wi[:, 2 * ATTN_WIDTH + g * GROUP_WIDTH:2 * ATTN_WIDTH + (g + 1) * GROUP_WIDTH]]
    w_in_b = jnp.concatenate([q_cols] + kv_cols + [wi[:, 3 * ATTN_WIDTH:]], axis=1).astype(BF16)
    w_co_b = w_conv_out[0].astype(BF16)
    w_ao_b = w_attn_out[0].astype(BF16)
    w_o_b = w_out[0].astype(BF16)
    w_r = jnp.zeros((ROUTER_ROWS, D), F32)
    w_r = w_r.at[0:N_EXPERT_GROUPS].set(w_router_group[0].T)
    w_r = w_r.at[8:].set(w_router_expert[0].reshape(D, N_EXPERTS).T)
    b_r = jnp.full((ROUTER_ROWS, 1), NEG_INF, F32)
    b_r = b_r.at[0:N_EXPERT_GROUPS, 0].set(b_router_group[0])
    b_r = b_r.at[8:, 0].set(b_router_expert[0].reshape(N_EXPERTS))
    row = lambda v: v.reshape(1, -1)

    n_seq = B + Bd
    n_seq_pad = -(-n_seq // 16) * 16
    c_all = jnp.concatenate([c_prompt, c_sample, jnp.zeros((n_seq_pad - n_seq, D), F32)], axis=0)
    mod = _modulation(c_all, w_mod[0], row(b_mod[0]))
    mod_p = [mod[:B, j * D:(j + 1) * D].reshape(B, 1, D) for j in range(6)]
    mod_s = [jnp.tile(mod[B:B + Bd, j * D:(j + 1) * D], (Td, 1)) for j in range(6)]

    dw_w_rows = jnp.broadcast_to(dw_w[0][:, None, :], (CONV_WIDTH, SUBLANES, CONV_CH))
    dw_b_rows = jnp.broadcast_to(dw_b[0][None, :], (SUBLANES, CONV_CH))
    (q0, q1, q2, k0, v0, k1, v1, k2, v2, kvt0, kvt1, kvt2, ga, sgb, convp) = _k1_prompt(
        x_prompt, mod_p[0], mod_p[1], row(norm_mix_g[0]), w_in_b, dw_w_rows, dw_b_rows,
        row(ln_conv_g[0]), row(ln_conv_b[0]), w_co_b)
    o_l = []
    for g, ((win, dil), qg, kg, vg) in enumerate(zip(DILATED_GROUPS, (q0, q1, q2), (k0, k1, k2), (v0, v1, v2))):
        bias = _prompt_bias(rel_bias[:, g * HEADS_PER_GROUP:(g + 1) * HEADS_PER_GROUP], dil, win // dil)
        o_l.append(_k2_prompt(qg, kg, vg, bias, dil))
    xp = x_prompt.reshape(B * S, D)
    tm3 = min(TOKEN_TILE, B * S)
    x2, h2, eid, wts, rank, cnt_p = _k3(xp, o_l, mod_p[2], mod_p[3], mod_p[4], ga, sgb, row(norm_ffn_g[0]),
                                        w_ao_b, w_o_b, w_r, b_r, jnp.zeros((N_EXPERTS, 1), F32), S // tm3)
    kv_p = [_cache_unview(kvt)[None] for kvt in (kvt0, kvt1, kvt2)]
    conv_p = convp[:, CONV_HIST - (CONV_WIDTH - 1):, :][None]

    Ts = Td * Bd
    xs = x_sample.transpose(1, 0, 2).reshape(Ts, D)
    state_tm = state_conv[0].transpose(1, 0, 2)
    (sq0, sq1, sq2, sk0, sv0, sk1, sv1, sk2, sv2, ga_s, sgb_s, conv_tm) = _k1_sample(
        xs, mod_s[0], mod_s[1], row(norm_mix_g[0]), w_in_b, state_tm, dw_w[0], row(dw_b[0]),
        row(ln_conv_g[0]), row(ln_conv_b[0]), w_co_b, Td)
    q_rows = 16
    q_b = jnp.concatenate([sq0, sq1, sq2], axis=1).reshape(Td, Bd, ATTN_WIDTH).transpose(1, 0, 2)
    q_b = jnp.pad(q_b, ((0, 0), (0, q_rows - Td), (0, 0)))
    views, tails, biases_c, biases_n = [], [], [], []
    for g, ((win, dil), c, sk, sv) in enumerate(zip(DILATED_GROUPS, caches_in, (sk0, sk1, sk2), (sv0, sv1, sv2))):
        lc = c.shape[1]
        views.append(_cache_view(c))
        new_kv = jnp.concatenate([sk, sv], axis=1).reshape(Td, Bd, 2 * GROUP_WIDTH).transpose(1, 0, 2)
        tails.append(jnp.pad(new_kv, ((0, 0), (SUBLANES - Td, 0), (0, 0))))
        bc, bn = _sample_bias(rel_bias[:, g * HEADS_PER_GROUP:(g + 1) * HEADS_PER_GROUP], dil, win // dil, lc, Td, q_rows)
        biases_c.append(bc)
        biases_n.append(bn)
    (nc0, nc1, nc2, so0, so1, so2, sl0, sl1, sl2) = _k2_sample(q_b, views, tails, biases_c, biases_n, Td)
    to_tm = lambda a: a[:, :Td, :].transpose(1, 0, 2).reshape(Ts, GROUP_WIDTH)
    o_l_s = [(to_tm(so0), to_tm(sl0)), (to_tm(so1), to_tm(sl1)), (to_tm(so2), to_tm(sl2))]
    tm3s = min(TOKEN_TILE, Ts)
    tiled = lambda m: m.reshape(Ts // tm3s, tm3s, D)
    x2s, h2s, eid_s, wts_s, rank_s, cnt_all = _k3(xs, o_l_s, tiled(mod_s[2]), tiled(mod_s[3]), tiled(mod_s[4]),
                                                  ga_s, sgb_s, row(norm_ffn_g[0]), w_ao_b, w_o_b, w_r, b_r, cnt_p, 1)

    n_assign = 2 * (B * S + Ts)
    n_blocks = -(-n_assign // MOE_ROWS) + N_EXPERTS
    pstart, block_expert, n_used, fill_rows = _slot_tables(cnt_all[:, 0].astype(jnp.int32), n_blocks)
    dest_p = _slots(eid, rank, pstart)
    dest_s = _slots(eid_s, rank_s, pstart)
    n_slots = n_blocks * MOE_ROWS
    assert tm3s == tm3, "prompt and sample token tiles must match to share the dispatch"
    slots = _dispatch(jnp.concatenate([dest_p, dest_s], axis=0), fill_rows, h2, h2s, n_slots)
    ys = _k4(block_expert, n_used, slots, w_exp_gate[0], w_exp_up[0], w_exp_down[0])
    per_token = lambda a: a.transpose(0, 2, 1).reshape(-1, 2)
    y_prompt = _k5(dest_p, x2, per_token(wts), mod_p[5], row(norm_final_g), ys, S // tm3).reshape(B, S, D)
    y_s = _k5(dest_s, x2s, per_token(wts_s), tiled(mod_s[5]), row(norm_final_g), ys, 1)
    y_sample = y_s.reshape(Td, Bd, D).transpose(1, 0, 2)
    kv_s = [_cache_unview(nc)[None] for nc in (nc0, nc1, nc2)]
    conv_s = conv_tm.transpose(1, 0, 2)[None]

    return (y_prompt, y_sample, kv_p[0], kv_p[1], kv_p[2], conv_p,
            kv_s[0], kv_s[1], kv_s[2], conv_s)
```

```python
import functools
import math

import jax
import jax.numpy as jnp
from jax import lax
from jax.experimental import pallas as pl
from jax.experimental.pallas import tpu as pltpu

F32 = jnp.float32
BF16 = jnp.bfloat16

D_MODEL = 1024
HEAD_DIM = 64
HEADS_PER_GROUP = 4
GROUP_WIDTH = HEADS_PER_GROUP * HEAD_DIM
LANES = 128
SUBLANES = 8
CONV_CHUNK = 32
STAGE_HALVES = GROUP_WIDTH // LANES
DILATED_GROUPS = ((128, 1), (512, 4), (2048, 16))
N_GROUPS = len(DILATED_GROUPS)
ATTN_WIDTH = N_GROUPS * GROUP_WIDTH
CONV_CH = D_MODEL // 2
CONV_WIDTH = 31
CONV_HIST = 32
N_BUCKETS = 32
MAX_DISTANCE = 2048
N_EXPERT_GROUPS = 4
EXPERTS_PER_GROUP = 8
N_EXPERTS = N_EXPERT_GROUPS * EXPERTS_PER_GROUP
D_FF_EXPERT = D_MODEL // 2
EPS = 1e-6
NEG_INF = -1e30

COL_KV = ATTN_WIDTH
COL_ULIN = COL_KV + 2 * ATTN_WIDTH
COL_UGATE = COL_ULIN + CONV_CH
COL_GA = COL_UGATE + CONV_CH
COL_GB = COL_GA + D_MODEL
IN_COLS = COL_GB + D_MODEL

ROUTER_ROWS = 8 + N_EXPERTS

V7X_VMEM_LIMIT = 56 * 1024 * 1024
TOKEN_TILE = 512
ATTN_ROWS = 2048
ATTN_BLOCK = 128
MOE_ROWS = 512
SAMPLE_BATCH = 2


def _sigmoid(x):
    return 1.0 / (1.0 + jnp.exp(-x))


def _silu(x):
    return x * _sigmoid(x)


def _rms_modulate(x, g, sc, sh):
    r = lax.rsqrt(jnp.mean(x * x, axis=-1, keepdims=True) + EPS)
    return ((x * r) * g) * (1.0 + sc) + sh


def _conv_tail(y, g, b):
    mu = jnp.mean(y, axis=-1, keepdims=True)
    yc = y - mu
    var = jnp.mean(yc * yc, axis=-1, keepdims=True)
    return _silu((yc * lax.rsqrt(var + EPS)) * g + b)


def _params(semantics):
    return pltpu.CompilerParams(dimension_semantics=semantics, vmem_limit_bytes=V7X_VMEM_LIMIT)


def _const_spec(shape):
    nd = len(shape)
    return pl.BlockSpec(shape, lambda *_: (0,) * nd)


def _weight_spec(shape):
    nd = len(shape)
    return pl.BlockSpec(shape, lambda *_: (0,) * nd, pipeline_mode=pl.Buffered(1))


def _mod_kernel(c_ref, w_ref, b_ref, o_ref):
    s = _silu(c_ref[...]).astype(BF16)
    o_ref[...] = jnp.dot(s, w_ref[...].astype(BF16), preferred_element_type=F32) + b_ref[...]


def _modulation(c_all, w_mod, b_mod):
    rows = c_all.shape[0]
    n_out = w_mod.shape[1]
    chunk = D_MODEL
    return pl.pallas_call(
        _mod_kernel,
        grid=(n_out // chunk,),
        in_specs=[_const_spec((rows, D_MODEL)),
                  pl.BlockSpec((D_MODEL, chunk), lambda j: (0, j)),
                  pl.BlockSpec((1, chunk), lambda j: (0, j))],
        out_specs=pl.BlockSpec((rows, chunk), lambda j: (0, j)),
        out_shape=jax.ShapeDtypeStruct((rows, n_out), F32),
        compiler_params=_params(("arbitrary",)),
        name="modulation",
    )(c_all, w_mod, b_mod)


def _to_streams(ref, val, dil, stage):
    if dil == 1:
        ref[...] = val.astype(ref.dtype).reshape(ref.shape)
        return
    n = val.shape[0] // dil
    for c, half in enumerate(stage):
        half[...] = val[:, c * LANES:(c + 1) * LANES]
        for r in range(dil):
            lo = r * GROUP_WIDTH + c * LANES
            ref[0, :, lo:lo + LANES] = half[pl.ds(r, n, stride=dil), :].astype(ref.dtype)


def _from_streams(ref, dil, stage):
    if dil == 1:
        return ref[...].astype(F32).reshape(ref.shape[-2:])
    n = ref.shape[1]
    for c, half in enumerate(stage):
        for r in range(dil):
            lo = r * GROUP_WIDTH + c * LANES
            half[pl.ds(r, n, stride=dil), :] = ref[0, :, lo:lo + LANES].astype(F32)
    return jnp.concatenate([half[...] for half in stage], axis=1)


def _project_common(hb, win_ref, outs, dils, stages):
    (q_refs, k_refs, v_refs) = outs

    def proj(lo, width):
        return jnp.dot(hb, win_ref[:, lo:lo + width], preferred_element_type=F32)

    zq = proj(0, ATTN_WIDTH)
    for g in range(N_GROUPS):
        _to_streams(q_refs[g], zq[:, g * GROUP_WIDTH:(g + 1) * GROUP_WIDTH], dils[g], stages[g][0])
    zkvs = []
    for g in range(N_GROUPS):
        zkv = proj(COL_KV + 2 * GROUP_WIDTH * g, 2 * GROUP_WIDTH)
        _to_streams(k_refs[g], zkv[:, :GROUP_WIDTH], dils[g], stages[g][1])
        _to_streams(v_refs[g], zkv[:, GROUP_WIDTH:], dils[g], stages[g][2])
        zkvs.append(zkv)
    u = proj(COL_ULIN, CONV_CH) * _sigmoid(proj(COL_UGATE, CONV_CH))
    return zkvs, u, proj


def _k1_kernel(tail_rows, n_tiles,
               x_ref, sh_ref, sc_ref, g_ref, win_ref, dww_ref, dwb_ref, lng_ref, lnb_ref, wco_ref,
               q0, q1, q2, k0, v0, k1, v1, k2, v2, kvt0, kvt1, kvt2, ga_ref, sgb_ref, convp_ref, uext, sbuf, ushift, sga, *stage_refs):
    i = pl.program_id(1)
    tm = x_ref.shape[1]

    @pl.when(i == 0)
    def _():
        uext[0:CONV_HIST, :] = jnp.zeros((CONV_HIST, CONV_CH), F32)

    h = _rms_modulate(x_ref[0], g_ref[...], sc_ref[0], sh_ref[0])
    hb = h.astype(BF16)

    def proj(lo, width):
        return jnp.dot(hb, win_ref[:, lo:lo + width], preferred_element_type=F32)

    uext[CONV_HIST:CONV_HIST + tm, :] = proj(COL_ULIN, CONV_CH) * _sigmoid(proj(COL_UGATE, CONV_CH))
    base = CONV_HIST - (CONV_WIDTH - 1)
    span = tm + CONV_HIST - SUBLANES
    for b in range(1, SUBLANES):
        ushift[b - 1, 0:span, :] = uext[b:b + span, :]

    def conv_chunk(r0):
        accs = [dwb_ref[...]] * (CONV_CHUNK // SUBLANES)
        for k in range(CONV_WIDTH):
            b = (base + k) % SUBLANES
            lo = r0 + base + k - b
            w8 = dww_ref[k]
            for q in range(len(accs)):
                rows = slice(lo + q * SUBLANES, lo + (q + 1) * SUBLANES)
                src = uext[rows, :] if b == 0 else ushift[b - 1, rows, :]
                accs[q] = accs[q] + w8 * src
        s = _conv_tail(jnp.concatenate(accs, axis=0), lng_ref[...], lnb_ref[...])
        sbuf[r0:r0 + CONV_CHUNK, :] = s.astype(BF16)

    dils = tuple(dil for _, dil in DILATED_GROUPS)
    pairs = [stage_refs[j:j + STAGE_HALVES] for j in range(0, len(stage_refs), STAGE_HALVES)]
    stages = [(None,) * 3] + [pairs[3 * (g - 1):3 * g] for g in range(1, N_GROUPS)]
    pieces = []
    for g, q_ref in enumerate((q0, q1, q2)):
        def q_piece(g=g, q_ref=q_ref):
            _to_streams(q_ref, proj(g * GROUP_WIDTH, GROUP_WIDTH), dils[g], stages[g][0])
        pieces.append(q_piece)
    for g, (k_ref, v_ref, kvt) in enumerate(((k0, v0, kvt0), (k1, v1, kvt1), (k2, v2, kvt2))):
        for half, ref in enumerate((k_ref, v_ref)):
            def kv_piece(g=g, half=half, ref=ref, kvt=kvt):
                z = proj(COL_KV + (2 * g + half) * GROUP_WIDTH, GROUP_WIDTH)
                _to_streams(ref, z, dils[g], stages[g][1 + half])
                tr = kvt.shape[2]

                @pl.when(i >= n_tiles - tail_rows[g] // tr)
                def _():
                    kvt[0, half * GROUP_WIDTH:(half + 1) * GROUP_WIDTH, :] = z[tm - tr:, :].T
            pieces.append(kv_piece)
    for c in range(D_MODEL // GROUP_WIDTH):
        cols = slice(c * GROUP_WIDTH, (c + 1) * GROUP_WIDTH)

        def gb_piece(c=c, cols=cols):
            sgb_ref[:, cols] = _sigmoid(proj(COL_GB + c * GROUP_WIDTH, GROUP_WIDTH)).astype(BF16)

        def ga_piece(c=c, cols=cols):
            sga[:, cols] = _sigmoid(proj(COL_GA + c * GROUP_WIDTH, GROUP_WIDTH))
        pieces += [gb_piece, ga_piece]
    n_qkv = 3 * N_GROUPS
    for piece in pieces[:n_qkv]:
        piece()
    for r0 in range(0, tm, CONV_CHUNK):
        conv_chunk(r0)
    for piece in pieces[n_qkv:]:
        piece()
    last = uext[tm:tm + CONV_HIST, :]
    uext[0:CONV_HIST, :] = last
    convp_ref[0] = last
    a = jnp.dot(sbuf[...], wco_ref[...], preferred_element_type=F32)
    ga_ref[...] = (sga[...] * a).astype(BF16)


def _k1_prompt(x, sh1, sc1, norm_g, w_in_b, dw_w, dw_b, ln_g, ln_b, w_co_b):
    B, S, _ = x.shape
    tm = min(TOKEN_TILE, S)
    nt = S // tm
    tail_rows = tuple(min(win, S) for win, _ in DILATED_GROUPS)
    tail_blk = tuple(min(t, tm) for t in tail_rows)

    def tok_spec(width):
        return pl.BlockSpec((tm, width), lambda b, i: (b * nt + i, 0))

    def tail_spec(g):
        first = nt - tail_rows[g] // tail_blk[g]
        return pl.BlockSpec((1, 2 * GROUP_WIDTH, tail_blk[g]), lambda b, i: (b, 0, jnp.maximum(i - first, 0)))

    def stream_shape(dil):
        return jax.ShapeDtypeStruct((B, S // dil, dil * GROUP_WIDTH), BF16)

    def stream_spec(dil):
        return pl.BlockSpec((1, tm // dil, dil * GROUP_WIDTH), lambda b, i: (b, i, 0))

    dils = [dil for _, dil in DILATED_GROUPS]
    qkv_order = [dils[0], dils[1], dils[2]] + [d for d in dils for _ in range(2)]
    mod_spec = pl.BlockSpec((1, 1, D_MODEL), lambda b, i: (b, 0, 0))
    out_shape = ([stream_shape(d) for d in qkv_order]
                 + [jax.ShapeDtypeStruct((B, 2 * GROUP_WIDTH, tail_rows[g]), F32) for g in range(N_GROUPS)]
                 + [jax.ShapeDtypeStruct((B * S, D_MODEL), BF16)] * 2
                 + [jax.ShapeDtypeStruct((B, CONV_HIST, CONV_CH), F32)])
    out_specs = ([stream_spec(d) for d in qkv_order] + [tail_spec(g) for g in range(N_GROUPS)]
                 + [tok_spec(D_MODEL)] * 2 + [pl.BlockSpec((1, CONV_HIST, CONV_CH), lambda b, i: (b, 0, 0))])
    n_stage = 3 * sum(1 for d in dils if d > 1)
    return pl.pallas_call(
        functools.partial(_k1_kernel, tail_rows, nt),
        grid=(B, nt),
        in_specs=[pl.BlockSpec((1, tm, D_MODEL), lambda b, i: (b, i, 0)), mod_spec, mod_spec,
                  _const_spec((1, D_MODEL)), _weight_spec((D_MODEL, IN_COLS)),
                  _const_spec((CONV_WIDTH, SUBLANES, CONV_CH)), _const_spec((SUBLANES, CONV_CH)),
                  _const_spec((1, CONV_CH)), _const_spec((1, CONV_CH)), _weight_spec((CONV_CH, D_MODEL))],
        out_specs=out_specs,
        out_shape=out_shape,
        scratch_shapes=([pltpu.VMEM((CONV_HIST + tm, CONV_CH), F32), pltpu.VMEM((tm, CONV_CH), BF16),
                         pltpu.VMEM((SUBLANES - 1, CONV_HIST + tm - SUBLANES, CONV_CH), F32),
                         pltpu.VMEM((tm, D_MODEL), F32)]
                        + [pltpu.VMEM((tm, LANES), F32)] * (n_stage * STAGE_HALVES)),
        compiler_params=_params(("arbitrary", "arbitrary")),
        name="inproj_prompt",
    )(x, sh1, sc1, norm_g, w_in_b, dw_w, dw_b, ln_g, ln_b, w_co_b)


def _k1s_kernel(n_steps, x_ref, sh_ref, sc_ref, g_ref, win_ref, st_ref, dww_ref, dwb_ref, lng_ref, lnb_ref, wco_ref,
                q0, q1, q2, k0, v0, k1, v1, k2, v2, ga_ref, sgb_ref, conv_ref):
    bd = st_ref.shape[1]
    hist = st_ref.shape[0]
    h = _rms_modulate(x_ref[...], g_ref[...], sc_ref[...], sh_ref[...])
    hb = h.astype(BF16)
    _, u, proj = _project_common(hb, win_ref, ((q0, q1, q2), (k0, k1, k2), (v0, v1, v2)),
                                 (1,) * N_GROUPS, [(None,) * 3] * N_GROUPS)

    def ext(j):
        return st_ref[j] if j < hist else u[(j - hist) * bd:(j - hist + 1) * bd, :]

    outs = []
    for t in range(n_steps):
        acc = jnp.zeros((bd, CONV_CH), F32) + dwb_ref[...]
        for k in range(CONV_WIDTH):
            acc = acc + dww_ref[k:k + 1, :] * ext(t + k + hist - (CONV_WIDTH - 1))
        outs.append(acc)
    for j in range(hist):
        conv_ref[j] = ext(j + n_steps)
    s = _conv_tail(jnp.concatenate(outs, axis=0), lng_ref[...], lnb_ref[...])
    a = jnp.dot(s.astype(BF16), wco_ref[...], preferred_element_type=F32)
    ga_ref[...] = (_sigmoid(proj(COL_GA, D_MODEL)) * a).astype(BF16)
    sgb_ref[...] = _sigmoid(proj(COL_GB, D_MODEL)).astype(BF16)


def _k1_sample(x_tm, sh1, sc1, norm_g, w_in_b, state_tm, dw_w, dw_b, ln_g, ln_b, w_co_b, n_steps):
    T = x_tm.shape[0]
    hist, bd, _ = state_tm.shape
    out_shape = ([jax.ShapeDtypeStruct((T, GROUP_WIDTH), F32)] * 9
                 + [jax.ShapeDtypeStruct((T, D_MODEL), BF16)] * 2
                 + [jax.ShapeDtypeStruct((hist, bd, CONV_CH), F32)])
    out_specs = ([_const_spec((T, GROUP_WIDTH))] * 9 + [_const_spec((T, D_MODEL))] * 2
                 + [_const_spec((hist, bd, CONV_CH))])
    return pl.pallas_call(
        functools.partial(_k1s_kernel, n_steps),
        grid=(1,),
        in_specs=[_const_spec((T, D_MODEL)), _const_spec((T, D_MODEL)), _const_spec((T, D_MODEL)),
                  _const_spec((1, D_MODEL)), _const_spec((D_MODEL, IN_COLS)), _const_spec((hist, bd, CONV_CH)),
                  _const_spec((CONV_WIDTH, CONV_CH)), _const_spec((1, CONV_CH)),
                  _const_spec((1, CONV_CH)), _const_spec((1, CONV_CH)), _const_spec((CONV_CH, D_MODEL))],
        out_specs=out_specs,
        out_shape=out_shape,
        compiler_params=_params(("arbitrary",)),
        name="inproj_sample",
    )(x_tm, sh1, sc1, norm_g, w_in_b, state_tm, dw_w, dw_b, ln_g, ln_b, w_co_b)


def _t5_bucket(dist):
    max_exact = N_BUCKETS // 2
    d_f = jnp.maximum(dist, 1).astype(F32)
    large = max_exact + (jnp.log(d_f / max_exact) / math.log(MAX_DISTANCE / max_exact)
                         * (N_BUCKETS - max_exact)).astype(jnp.int32)
    large = jnp.minimum(large, N_BUCKETS - 1)
    return jnp.where(dist < max_exact, dist, large)


def _bucket_lookup(rel_bias_g, dist):
    bucket = _t5_bucket(dist)
    out = jnp.zeros((rel_bias_g.shape[1],) + dist.shape, F32)
    for b in range(N_BUCKETS):
        out = jnp.where(bucket[None] == b, rel_bias_g[b].reshape((-1,) + (1,) * dist.ndim), out)
    return out


def _prompt_bias(rel_bias_g, dil, n_keys):
    blk = n_keys
    i = jnp.arange(blk)[:, None]
    j = jnp.arange(2 * blk)[None, :]
    rel = i - j + blk
    valid = (rel >= 0) & (rel <= n_keys)
    bias = _bucket_lookup(rel_bias_g, jnp.clip(rel, 0, n_keys) * dil)
    bias = jnp.where(valid[None], bias, NEG_INF)
    return bias.reshape(HEADS_PER_GROUP * blk, 2 * blk).astype(F32)


def _sample_bias(rel_bias_g, dil, n_keys, lc, n_steps, q_rows):
    t = jnp.arange(q_rows)[:, None]
    pos = jnp.arange(lc)[None, :]
    dist_c = lc + t - pos
    dist_n = t - (jnp.arange(128)[None, :] - (128 - n_steps))

    def table(dist, extra):
        ok = (dist >= 0) & (dist % dil == 0) & (dist // dil <= n_keys) & extra & (t < n_steps)
        b = _bucket_lookup(rel_bias_g, jnp.clip(dist, 0, None))
        b = jnp.where(ok[None], b, NEG_INF)
        return jnp.where((t >= n_steps)[None], 0.0, b).astype(F32)

    lane_ok = jnp.arange(128)[None, :] >= 128 - n_steps
    return table(dist_c, True), table(dist_n, lane_ok)


def _k2_kernel(q_ref, k_ref, kh_ref, v_ref, vh_ref, bias_ref, o_ref, lse_ref, kbuf, vbuf):
    i = pl.program_id(2)
    rows = q_ref.shape[1]
    n_streams = q_ref.shape[2] // GROUP_WIDTH
    blk = ATTN_BLOCK
    kbuf[0:blk, :] = kh_ref[0]
    kbuf[blk:blk + rows, :] = k_ref[0]
    vbuf[0:blk, :] = vh_ref[0]
    vbuf[blk:blk + rows, :] = v_ref[0]
    lane_head = lax.broadcasted_iota(jnp.int32, (blk, GROUP_WIDTH), 1) // HEAD_DIM
    col = lax.broadcasted_iota(jnp.int32, (HEADS_PER_GROUP * blk, 2 * blk), 1)
    first_mask = jnp.where((col < blk) & (i == 0), NEG_INF, 0.0).astype(F32)
    bias = bias_ref[...]
    for st in range(n_streams):
        cols = slice(st * GROUP_WIDTH, (st + 1) * GROUP_WIDTH)
        for j in range(rows // blk):
            qb = q_ref[0, j * blk:(j + 1) * blk, cols]
            q4 = jnp.concatenate([jnp.where(lane_head == h, qb, jnp.zeros_like(qb)) for h in range(HEADS_PER_GROUP)],
                                 axis=0)
            kc = kbuf[j * blk:(j + 2) * blk, cols]
            vc = vbuf[j * blk:(j + 2) * blk, cols]
            s = lax.dot_general(q4, kc, (((1,), (1,)), ((), ())), preferred_element_type=F32) + bias
            if j == 0:
                s = s + first_mask
            m = jnp.max(s, axis=-1, keepdims=True)
            p = jnp.exp(s - m)
            l = jnp.sum(p, axis=-1, keepdims=True)
            o4 = jnp.dot(p.astype(BF16), vc, preferred_element_type=F32) * (1.0 / l)
            lse4 = m + jnp.log(l)
            o = jnp.zeros((blk, GROUP_WIDTH), F32)
            lse = jnp.zeros((blk, GROUP_WIDTH), F32)
            for h in range(HEADS_PER_GROUP):
                sel = lane_head == h
                o = jnp.where(sel, o4[h * blk:(h + 1) * blk, :], o)
                lse = jnp.where(sel, lse4[h * blk:(h + 1) * blk, :], lse)
            o_ref[0, j * blk:(j + 1) * blk, cols] = o.astype(o_ref.dtype)
            lse_ref[0, j * blk:(j + 1) * blk, cols] = lse


def _k2_prompt(q, k, v, bias, dil):
    B, L, _ = q.shape
    rows = min(ATTN_ROWS, L)
    per = rows // ATTN_BLOCK
    n_streams = min(dil, max(1, ATTN_ROWS // rows))
    width = n_streams * GROUP_WIDTH
    main = pl.BlockSpec((1, rows, width), lambda b, r, i: (b, i, r))
    halo = pl.BlockSpec((1, ATTN_BLOCK, width), lambda b, r, i: (b, jnp.maximum(i * per - 1, 0), r))
    return pl.pallas_call(
        _k2_kernel,
        grid=(B, dil // n_streams, L // rows),
        in_specs=[main, main, halo, main, halo, _const_spec(bias.shape)],
        out_specs=[main, main],
        out_shape=[jax.ShapeDtypeStruct(q.shape, BF16), jax.ShapeDtypeStruct(q.shape, F32)],
        scratch_shapes=[pltpu.VMEM((ATTN_BLOCK + rows, width), BF16)] * 2,
        compiler_params=_params(("arbitrary", "arbitrary", "arbitrary")),
        name=f"attn_prompt_d{dil}",
    )(q, k, k, v, v, bias)


def _k2s_kernel(n_steps, q_ref, c0, c1, c2, t0, t1, t2, bc0, bc1, bc2, bn0, bn1, bn2,
                nc0, nc1, nc2, o0, o1, o2, l0, l1, l2):
    lane = lax.broadcasted_iota(jnp.int32, (2 * GROUP_WIDTH, 128), 1)
    keep = 128 - n_steps
    groups = ((c0, t0, bc0, bn0, nc0, o0, l0), (c1, t1, bc1, bn1, nc1, o1, l1), (c2, t2, bc2, bn2, nc2, o2, l2))
    for bi in range(q_ref.shape[0]):
        for g, (c_ref, t_ref, bc_ref, bn_ref, nc_ref, o_ref, l_ref) in enumerate(groups):
            _sample_group(bi, g, lane, keep, q_ref, c_ref, t_ref, bc_ref, bn_ref, nc_ref, o_ref, l_ref)


def _sample_group(bi, g, lane, keep, q_ref, c_ref, t_ref, bc_ref, bn_ref, nc_ref, o_ref, l_ref):
    lc = c_ref.shape[2]
    n_tiles = lc // 128
    new_rows = t_ref[bi]
    tail = jnp.concatenate([jnp.zeros((128 - new_rows.shape[0], new_rows.shape[1]), F32), new_rows], axis=0).T
    cur = pltpu.roll(c_ref[bi, :, 0:128], keep, 1)
    for c in range(n_tiles):
        nxt = pltpu.roll(c_ref[bi, :, (c + 1) * 128:(c + 2) * 128], keep, 1) if c + 1 < n_tiles else tail
        nc_ref[bi, :, c * 128:(c + 1) * 128] = jnp.where(lane < keep, cur, nxt)
        cur = nxt
    for h in range(HEADS_PER_GROUP):
        lo = g * GROUP_WIDTH + h * HEAD_DIM
        qh = q_ref[bi, :, lo:lo + HEAD_DIM].astype(BF16)
        kh = c_ref[bi, h * HEAD_DIM:(h + 1) * HEAD_DIM, :].astype(BF16)
        vh = c_ref[bi, GROUP_WIDTH + h * HEAD_DIM:GROUP_WIDTH + (h + 1) * HEAD_DIM, :].astype(BF16)
        kt = tail[h * HEAD_DIM:(h + 1) * HEAD_DIM, :].astype(BF16)
        vt = tail[GROUP_WIDTH + h * HEAD_DIM:GROUP_WIDTH + (h + 1) * HEAD_DIM, :].astype(BF16)
        sc = jnp.dot(qh, kh, preferred_element_type=F32) + bc_ref[h]
        sn = jnp.dot(qh, kt, preferred_element_type=F32) + bn_ref[h]
        m = jnp.maximum(jnp.max(sc, axis=-1, keepdims=True), jnp.max(sn, axis=-1, keepdims=True))
        pc = jnp.exp(sc - m)
        pn = jnp.exp(sn - m)
        l = jnp.sum(pc, axis=-1, keepdims=True) + jnp.sum(pn, axis=-1, keepdims=True)
        nt_dims = (((1,), (1,)), ((), ()))
        o = (lax.dot_general(pc.astype(BF16), vh, nt_dims, preferred_element_type=F32)
             + lax.dot_general(pn.astype(BF16), vt, nt_dims, preferred_element_type=F32)) * (1.0 / l)
        o_ref[bi, :, h * HEAD_DIM:(h + 1) * HEAD_DIM] = o
        l_ref[bi, :, h * HEAD_DIM:(h + 1) * HEAD_DIM] = jnp.broadcast_to(m + jnp.log(l), o.shape)


def _k2_sample(q_b, caches, tails, biases_c, biases_n, n_steps):
    bd, q_rows, _ = q_b.shape
    bb = SAMPLE_BATCH if bd % SAMPLE_BATCH == 0 else 1
    per_b = lambda shape: pl.BlockSpec((bb,) + shape[1:], lambda b: (b,) + (0,) * (len(shape) - 1))
    ins = [q_b] + list(caches) + list(tails) + list(biases_c) + list(biases_n)
    in_specs = ([per_b(q_b.shape)] + [per_b(c.shape) for c in caches] + [per_b(t.shape) for t in tails]
                + [_const_spec(b.shape) for b in biases_c] + [_const_spec(b.shape) for b in biases_n])
    o_shape = jax.ShapeDtypeStruct((bd, q_rows, GROUP_WIDTH), F32)
    out_shape = [jax.ShapeDtypeStruct(c.shape, F32) for c in caches] + [o_shape] * 6
    out_specs = [per_b(c.shape) for c in caches] + [per_b(o_shape.shape)] * 6
    return pl.pallas_call(
        functools.partial(_k2s_kernel, n_steps),
        grid=(bd // bb,),
        in_specs=in_specs,
        out_specs=out_specs,
        out_shape=out_shape,
        compiler_params=_params(("arbitrary",)),
        name="attn_sample",
    )(*ins)


def _k3_kernel(x_ref, o0, o1, o2, l0, l1, l2, ga_ref, sgb_ref, g1_ref, sh2_ref, sc2_ref, gf_ref,
               wao_ref, wo_ref, wr_ref, br_ref, before_ref, cnt_in_ref,
               x2_ref, h2_ref, eid_ref, wts_ref, rank_ref, cnt_ref, carry, *stage_refs):
    @pl.when(pl.program_id(0) == 0)
    def _():
        carry[...] = cnt_in_ref[...]

    dils = tuple(x_ref.shape[0] // r.shape[-2] for r in (o0, o1, o2))
    pairs = [stage_refs[j:j + STAGE_HALVES] for j in range(0, len(stage_refs), STAGE_HALVES)]
    os_ = [_from_streams(r, d, pairs[2 * g]) for g, (r, d) in enumerate(zip((o0, o1, o2), dils))]
    ls = [_from_streams(r, d, pairs[2 * g + 1]) for g, (r, d) in enumerate(zip((l0, l1, l2), dils))]
    m = jnp.maximum(jnp.maximum(ls[0], ls[1]), ls[2])
    ws = [jnp.exp(l - m) for l in ls]
    den = ws[0] + ws[1] + ws[2]
    o = (ws[0] * os_[0] + ws[1] * os_[1] + ws[2] * os_[2]) / den
    b = jnp.dot(o.astype(BF16), wao_ref[...], preferred_element_type=F32)
    mixed = ga_ref[...].astype(F32) + sgb_ref[...].astype(F32) * b
    x2 = x_ref[...] + g1_ref[0] * jnp.dot(mixed.astype(BF16), wo_ref[...], preferred_element_type=F32)
    x2_ref[...] = x2
    h2 = _rms_modulate(x2, gf_ref[...], sc2_ref[0], sh2_ref[0])
    h2_ref[...] = h2
    lt = lax.dot_general(wr_ref[...], h2, (((1,), (1,)), ((), ())), preferred_element_type=F32,
                         precision=lax.Precision.HIGHEST) + br_ref[...]
    tm = h2.shape[0]
    gl = lt[0:8, :]
    gmax = jnp.max(gl, axis=0, keepdims=True)
    r8 = lax.broadcasted_iota(jnp.int32, (8, tm), 0)
    grp = jnp.min(jnp.where(gl == gmax, r8, 8), axis=0, keepdims=True)
    p_grp = 1.0 / jnp.sum(jnp.exp(gl - gmax), axis=0, keepdims=True)
    es = jnp.zeros((EXPERTS_PER_GROUP, tm), F32)
    for g in range(N_EXPERT_GROUPS):
        es = jnp.where(grp == g, lt[8 + 8 * g:16 + 8 * g, :], es)
    v1 = jnp.max(es, axis=0, keepdims=True)
    i1 = jnp.min(jnp.where(es == v1, r8, 8), axis=0, keepdims=True)
    rest = jnp.where(r8 == i1, -jnp.inf, es)
    v2 = jnp.max(rest, axis=0, keepdims=True)
    i2 = jnp.min(jnp.where(rest == v2, r8, 8), axis=0, keepdims=True)
    e21 = jnp.exp(v2 - v1)
    w1 = p_grp / (1.0 + e21)
    e1 = grp * EXPERTS_PER_GROUP + i1
    e2 = grp * EXPERTS_PER_GROUP + i2
    eid_ref[0, 0:1, :] = e1
    eid_ref[0, 1:2, :] = e2
    wts_ref[0, 0:1, :] = w1
    wts_ref[0, 1:2, :] = w1 * e21
    r_e = lax.broadcasted_iota(jnp.int32, (N_EXPERTS, tm), 0)
    hit1 = r_e == e1
    hit2 = r_e == e2
    both = jnp.where(hit1 | hit2, 1.0, 0.0)
    base = carry[...] + jnp.dot(both.astype(BF16), before_ref[...], preferred_element_type=F32)
    rank_ref[0, 0:1, :] = jnp.sum(jnp.where(hit1, base, 0.0), axis=0, keepdims=True).astype(jnp.int32)
    rank_ref[0, 1:2, :] = jnp.sum(jnp.where(hit2, base, 0.0), axis=0, keepdims=True).astype(jnp.int32)
    total = carry[...] + jnp.sum(both, axis=1, keepdims=True)
    carry[...] = total
    cnt_ref[...] = total


def _k3(x, o_l, g1, sh2, sc2, ga, sgb, norm_ffn_g, w_ao_b, w_o_b, w_r, b_r, cnt_in, tiles_per_mod):
    T = x.shape[0]
    tm = min(TOKEN_TILE, T)
    nt = T // tm
    tok = lambda w: pl.BlockSpec((tm, w), lambda t: (t, 0))
    mod = pl.BlockSpec((1,) + g1.shape[1:], lambda t: (t // tiles_per_mod, 0, 0))
    small = pl.BlockSpec((1, 2, tm), lambda t: (t, 0, 0))
    small_i = jax.ShapeDtypeStruct((nt, 2, tm), jnp.int32)
    before = (jnp.arange(tm)[:, None] < jnp.arange(tm)[None, :]).astype(BF16)

    def attn_spec(a):
        if a.ndim == 2:
            return tok(GROUP_WIDTH)
        dil = a.shape[2] // GROUP_WIDTH
        return pl.BlockSpec((1, tm // dil, a.shape[2]), lambda t: (t // tiles_per_mod, t % tiles_per_mod, 0))

    attn_in = [o_l[0][0], o_l[1][0], o_l[2][0], o_l[0][1], o_l[1][1], o_l[2][1]]
    return pl.pallas_call(
        _k3_kernel,
        grid=(nt,),
        in_specs=[tok(D_MODEL)] + [attn_spec(a) for a in attn_in] + [tok(D_MODEL)] * 2 + [mod] * 3
                 + [_const_spec((1, D_MODEL)), _weight_spec(w_ao_b.shape), _weight_spec(w_o_b.shape),
                    _const_spec(w_r.shape), _const_spec(b_r.shape), _weight_spec((tm, tm)),
                    _const_spec((N_EXPERTS, 1))],
        out_specs=[tok(D_MODEL), tok(D_MODEL), small, small, small, _const_spec((N_EXPERTS, 1))],
        out_shape=[jax.ShapeDtypeStruct((T, D_MODEL), F32), jax.ShapeDtypeStruct((T, D_MODEL), F32),
                   small_i, jax.ShapeDtypeStruct((nt, 2, tm), F32), small_i,
                   jax.ShapeDtypeStruct((N_EXPERTS, 1), F32)],
        scratch_shapes=([pltpu.VMEM((N_EXPERTS, 1), F32)]
                        + [pltpu.VMEM((tm, LANES), F32)] * (2 * N_GROUPS * STAGE_HALVES)),
        compiler_params=_params(("arbitrary",)),
        name="merge_router",
    )(x, *attn_in, ga, sgb, g1, sh2, sc2, norm_ffn_g, w_ao_b, w_o_b, w_r, b_r, before, cnt_in)


def _rows_wait(n_rows, hbm, vmem, sem):
    pltpu.make_async_copy(hbm.at[pl.ds(0, n_rows)], vmem, sem).wait()


def _dispatch_kernel(n_tiles, n_first, dest_ref, fill_ref, ha_ref, hb_ref, xs_ref, buf, sem, zbuf, zsem):
    i = pl.program_id(0)
    slot = i % 2 if n_tiles > 1 else 0
    tm = ha_ref.shape[0]

    def drain(s):
        for _ in range(2):
            _rows_wait(tm, xs_ref, buf.at[s], sem.at[s])

    @pl.when(i == 0)
    def _():
        zbuf[...] = jnp.zeros(zbuf.shape, F32)

        def fill(row):
            return pltpu.make_async_copy(zbuf, xs_ref.at[pl.ds(pl.multiple_of(row, MOE_ROWS), MOE_ROWS)], zsem)

        n_blocks = xs_ref.shape[0] // MOE_ROWS
        n_used = fill_ref[N_EXPERTS]

        def start_block(j, carry):
            fill(j * MOE_ROWS).start()
            return carry

        def wait_block(j, carry):
            fill(j * MOE_ROWS).wait()
            return carry

        for e in range(N_EXPERTS):
            @pl.when(fill_ref[e] >= 0)
            def _(e=e):
                fill(fill_ref[e]).start()
        lax.fori_loop(n_used, n_blocks, start_block, 0)
        for e in range(N_EXPERTS):
            @pl.when(fill_ref[e] >= 0)
            def _(e=e):
                fill(fill_ref[e]).wait()
        lax.fori_loop(n_used, n_blocks, wait_block, 0)

    if n_tiles > 2:
        @pl.when(i >= 2)
        def _():
            drain(slot)

    @pl.when(i < n_first)
    def _():
        buf[slot] = ha_ref[...]

    @pl.when(i >= n_first)
    def _():
        buf[slot] = hb_ref[...]

    for r in range(tm):
        for k in range(2):
            pltpu.make_async_copy(buf.at[slot, pl.ds(r, 1)], xs_ref.at[pl.ds(dest_ref[0, k, r], 1)], sem.at[slot]).start()

    if n_tiles == 1:
        drain(0)
    else:
        @pl.when(i == n_tiles - 1)
        def _():
            drain(slot)
            drain(1 - slot)


def _dispatch(dest, fill_rows, h_a, h_b, n_slots):
    nt, _, tm = dest.shape
    n_first = h_a.shape[0] // tm
    assert h_a.shape[0] % tm == 0 and h_b.shape[0] == (nt - n_first) * tm
    return pl.pallas_call(
        functools.partial(_dispatch_kernel, nt, n_first),
        grid=(nt,),
        in_specs=[pl.BlockSpec((1, 2, tm), lambda t: (t, 0, 0), memory_space=pltpu.SMEM),
                  pl.BlockSpec(memory_space=pltpu.SMEM),
                  pl.BlockSpec((tm, D_MODEL), lambda t: (jnp.minimum(t, n_first - 1), 0)),
                  pl.BlockSpec((tm, D_MODEL), lambda t: (jnp.maximum(t - n_first, 0), 0))],
        out_specs=pl.BlockSpec(memory_space=pl.ANY),
        out_shape=jax.ShapeDtypeStruct((n_slots, D_MODEL), F32),
        scratch_shapes=[pltpu.VMEM((2, tm, D_MODEL), F32), pltpu.SemaphoreType.DMA((2,)),
                        pltpu.VMEM((MOE_ROWS, D_MODEL), F32), pltpu.SemaphoreType.DMA(())],
        compiler_params=_params(("arbitrary",)),
        name="dispatch_rows",
    )(dest, fill_rows, h_a, h_b)


def _k4_kernel(be_ref, nused_ref, x_ref, wg_ref, wu_ref, wd_ref, y_ref, wg_b, wu_b, wd_b):
    i = pl.program_id(0)

    @pl.when((i == 0) | (be_ref[i] != be_ref[jnp.maximum(i - 1, 0)]))
    def _():
        wg_b[...] = wg_ref[0].astype(BF16)
        wu_b[...] = wu_ref[0].astype(BF16)
        wd_b[...] = wd_ref[0].astype(BF16)

    @pl.when(i < nused_ref[0])
    def _():
        xb = x_ref[...].astype(BF16)
        gate = jnp.dot(xb, wg_b[...], preferred_element_type=F32)
        up = jnp.dot(xb, wu_b[...], preferred_element_type=F32)
        mid = (_silu(gate) * up).astype(BF16)
        y_ref[...] = jnp.dot(mid, wd_b[...], preferred_element_type=F32)

    @pl.when(i >= nused_ref[0])
    def _():
        y_ref[...] = jnp.zeros(y_ref.shape, F32)


def _k4(block_expert, n_used, xs, w_gate, w_up, w_down):
    n_blocks = block_expert.shape[0]
    rows = MOE_ROWS
    weight = lambda shape: pl.BlockSpec((1,) + shape, lambda i, be, nu: (be[i], 0, 0))
    grid_spec = pltpu.PrefetchScalarGridSpec(
        num_scalar_prefetch=2,
        grid=(n_blocks,),
        in_specs=[pl.BlockSpec((rows, D_MODEL), lambda i, be, nu: (jnp.minimum(i, nu[0] - 1), 0)),
                  weight((D_MODEL, D_FF_EXPERT)), weight((D_MODEL, D_FF_EXPERT)), weight((D_FF_EXPERT, D_MODEL))],
        out_specs=pl.BlockSpec((rows, D_MODEL), lambda i, be, nu: (i, 0)),
        scratch_shapes=[pltpu.VMEM((D_MODEL, D_FF_EXPERT), BF16), pltpu.VMEM((D_MODEL, D_FF_EXPERT), BF16),
                        pltpu.VMEM((D_FF_EXPERT, D_MODEL), BF16)],
    )
    return pl.pallas_call(
        _k4_kernel,
        grid_spec=grid_spec,
        out_shape=jax.ShapeDtypeStruct((n_blocks * rows, D_MODEL), F32),
        compiler_params=_params(("arbitrary",)),
        name="expert_blocks",
    )(block_expert, n_used, xs, w_gate, w_up, w_down)


def _k5_kernel(n_tiles, idx_ref, x2_ref, w_ref, g2_ref, gfin_ref, ys_hbm, y_ref, ybuf, sem):
    i = pl.program_id(0)
    tm = x2_ref.shape[0]

    @pl.when(i < n_tiles)
    def _():
        slot = i % 2
        for r in range(tm):
            for k in range(2):
                pltpu.make_async_copy(ys_hbm.at[pl.ds(idx_ref[0, k, r], 1)], ybuf.at[slot, k, pl.ds(r, 1)],
                                      sem.at[slot]).start()

    @pl.when(i >= 1)
    def _():
        slot = (i - 1) % 2
        for k in range(2):
            _rows_wait(tm, ys_hbm, ybuf.at[slot, k], sem.at[slot])
        w = w_ref[...]
        f = ybuf[slot, 0] * w[:, 0:1] + ybuf[slot, 1] * w[:, 1:2]
        y = x2_ref[...] + g2_ref[0] * f
        r = lax.rsqrt(jnp.mean(y * y, axis=-1, keepdims=True) + EPS)
        y_ref[...] = (y * r) * gfin_ref[...]


def _k5(dest, x2, wts, g2, norm_final_g, ys, tiles_per_mod):
    nt, _, tm = dest.shape
    T = x2.shape[0]
    prev = lambda t: jnp.maximum(t - 1, 0)
    return pl.pallas_call(
        functools.partial(_k5_kernel, nt),
        grid=(nt + 1,),
        in_specs=[pl.BlockSpec((1, 2, tm), lambda t: (jnp.minimum(t, nt - 1), 0, 0), memory_space=pltpu.SMEM),
                  pl.BlockSpec((tm, D_MODEL), lambda t: (prev(t), 0)),
                  pl.BlockSpec((tm, 2), lambda t: (prev(t), 0)),
                  pl.BlockSpec((1,) + g2.shape[1:], lambda t: (prev(t) // tiles_per_mod, 0, 0)),
                  _const_spec((1, D_MODEL)), pl.BlockSpec(memory_space=pl.ANY)],
        out_specs=pl.BlockSpec((tm, D_MODEL), lambda t: (prev(t), 0)),
        out_shape=jax.ShapeDtypeStruct((T, D_MODEL), F32),
        scratch_shapes=[pltpu.VMEM((2, 2, tm, D_MODEL), F32), pltpu.SemaphoreType.DMA((2,))],
        compiler_params=_params(("arbitrary",)),
        name="combine_norm",
    )(dest, x2, wts, g2, norm_final_g, ys)


def _slot_tables(counts, n_blocks):
    padded = (counts + MOE_ROWS - 1) // MOE_ROWS * MOE_ROWS
    pend = jnp.cumsum(padded)
    pstart = pend - padded
    block_lo = jnp.arange(n_blocks, dtype=jnp.int32) * MOE_ROWS
    block_expert = jnp.minimum(jnp.sum(pend[None, :] <= block_lo[:, None], axis=1), N_EXPERTS - 1).astype(jnp.int32)
    n_used = (pend[-1] // MOE_ROWS).astype(jnp.int32).reshape(1)
    fill_rows = jnp.concatenate([jnp.where(padded > 0, pend - MOE_ROWS, -1).astype(jnp.int32), n_used])
    return pstart.astype(jnp.int32), block_expert, n_used, fill_rows


def _slots(eid, rank, pstart):
    sel = eid[..., None] == jnp.arange(N_EXPERTS, dtype=jnp.int32)
    return jnp.sum(jnp.where(sel, pstart, 0), axis=-1).astype(jnp.int32) + rank


def _cache_view(cache):
    bd, lc = cache.shape[:2]
    return cache.transpose(0, 2, 3, 4, 1).reshape(bd, 2 * GROUP_WIDTH, lc)


def _cache_unview(view):
    bd, _, lc = view.shape
    return view.reshape(bd, 2, HEADS_PER_GROUP, HEAD_DIM, lc).transpose(0, 4, 1, 2, 3)


def kernel(x_prompt, x_sample, c_prompt, c_sample, cache_kv_w128, cache_kv_w512, cache_kv_w2048, state_conv, rel_bias, norm_mix_g, norm_ffn_g, w_mod, b_mod, w_in, dw_w, dw_b, ln_conv_g, ln_conv_b, w_conv_out, w_attn_out, w_out, w_router_group, b_router_group, w_router_expert, b_router_expert, w_exp_gate, w_exp_up, w_exp_down, norm_final_g):
    assert norm_mix_g.shape[0] == 1, "single layer"
    B, S, D = x_prompt.shape
    Bd, Td, _ = x_sample.shape
    caches_in = (cache_kv_w128[0], cache_kv_w512[0], cache_kv_w2048[0])
    for (win, dil), c in zip(DILATED_GROUPS, caches_in):
        assert c.shape[1] >= (win // dil) * dil and c.shape[1] % 128 == 0 and S % (dil * ATTN_BLOCK) == 0

    wi = w_in[0]
    q_cols = wi[:, :ATTN_WIDTH] * (HEAD_DIM ** -0.5)
    kv_cols = []
    for g in range(N_GROUPS):
        kv_cols += [wi[:, ATTN_WIDTH + g * GROUP_WIDTH:ATTN_WIDTH + (g + 1) * GROUP_WIDTH],
                    wi[:, 2 * ATTN_WIDTH + g * GROUP_WIDTH:2 * ATTN_WIDTH + (g + 1) * GROUP_WIDTH]]
    w_in_b = jnp.concatenate([q_cols] + kv_cols + [wi[:, 3 * ATTN_WIDTH:]], axis=1).astype(BF16)
    w_co_b = w_conv_out[0].astype(BF16)
    w_ao_b = w_attn_out[0].astype(BF16)
    w_o_b = w_out[0].astype(BF16)
    w_r = jnp.zeros((ROUTER_ROWS, D), F32)
    w_r = w_r.at[0:N_EXPERT_GROUPS].set(w_router_group[0].T)
    w_r = w_r.at[8:].set(w_router_expert[0].reshape(D, N_EXPERTS).T)
    b_r = jnp.full((ROUTER_ROWS, 1), NEG_INF, F32)
    b_r = b_r.at[0:N_EXPERT_GROUPS, 0].set(b_router_group[0])
    b_r = b_r.at[8:, 0].set(b_router_expert[0].reshape(N_EXPERTS))
    row = lambda v: v.reshape(1, -1)

    n_seq = B + Bd
    n_seq_pad = -(-n_seq // 16) * 16
    c_all = jnp.concatenate([c_prompt, c_sample, jnp.zeros((n_seq_pad - n_seq, D), F32)], axis=0)
    mod = _modulation(c_all, w_mod[0], row(b_mod[0]))
    mod_p = [mod[:B, j * D:(j + 1) * D].reshape(B, 1, D) for j in range(6)]
    mod_s = [jnp.tile(mod[B:B + Bd, j * D:(j + 1) * D], (Td, 1)) for j in range(6)]

    dw_w_rows = jnp.broadcast_to(dw_w[0][:, None, :], (CONV_WIDTH, SUBLANES, CONV_CH))
    dw_b_rows = jnp.broadcast_to(dw_b[0][None, :], (SUBLANES, CONV_CH))
    (q0, q1, q2, k0, v0, k1, v1, k2, v2, kvt0, kvt1, kvt2, ga, sgb, convp) = _k1_prompt(
        x_prompt, mod_p[0], mod_p[1], row(norm_mix_g[0]), w_in_b, dw_w_rows, dw_b_rows,
        row(ln_conv_g[0]), row(ln_conv_b[0]), w_co_b)
    o_l = []
    for g, ((win, dil), qg, kg, vg) in enumerate(zip(DILATED_GROUPS, (q0, q1, q2), (k0, k1, k2), (v0, v1, v2))):
        bias = _prompt_bias(rel_bias[:, g * HEADS_PER_GROUP:(g + 1) * HEADS_PER_GROUP], dil, win // dil)
        o_l.append(_k2_prompt(qg, kg, vg, bias, dil))
    xp = x_prompt.reshape(B * S, D)
    tm3 = min(TOKEN_TILE, B * S)
    x2, h2, eid, wts, rank, cnt_p = _k3(xp, o_l, mod_p[2], mod_p[3], mod_p[4], ga, sgb, row(norm_ffn_g[0]),
                                        w_ao_b, w_o_b, w_r, b_r, jnp.zeros((N_EXPERTS, 1), F32), S // tm3)
    kv_p = [_cache_unview(kvt)[None] for kvt in (kvt0, kvt1, kvt2)]
    conv_p = convp[:, CONV_HIST - (CONV_WIDTH - 1):, :][None]

    Ts = Td * Bd
    xs = x_sample.transpose(1, 0, 2).reshape(Ts, D)
    state_tm = state_conv[0].transpose(1, 0, 2)
    (sq0, sq1, sq2, sk0, sv0, sk1, sv1, sk2, sv2, ga_s, sgb_s, conv_tm) = _k1_sample(
        xs, mod_s[0], mod_s[1], row(norm_mix_g[0]), w_in_b, state_tm, dw_w[0], row(dw_b[0]),
        row(ln_conv_g[0]), row(ln_conv_b[0]), w_co_b, Td)
    q_rows = 16
    q_b = jnp.concatenate([sq0, sq1, sq2], axis=1).reshape(Td, Bd, ATTN_WIDTH).transpose(1, 0, 2)
    q_b = jnp.pad(q_b, ((0, 0), (0, q_rows - Td), (0, 0)))
    views, tails, biases_c, biases_n = [], [], [], []
    for g, ((win, dil), c, sk, sv) in enumerate(zip(DILATED_GROUPS, caches_in, (sk0, sk1, sk2), (sv0, sv1, sv2))):
        lc = c.shape[1]
        views.append(_cache_view(c))
        new_kv = jnp.concatenate([sk, sv], axis=1).reshape(Td, Bd, 2 * GROUP_WIDTH).transpose(1, 0, 2)
        tails.append(jnp.pad(new_kv, ((0, 0), (SUBLANES - Td, 0), (0, 0))))
        bc, bn = _sample_bias(rel_bias[:, g * HEADS_PER_GROUP:(g + 1) * HEADS_PER_GROUP], dil, win // dil, lc, Td, q_rows)
        biases_c.append(bc)
        biases_n.append(bn)
    (nc0, nc1, nc2, so0, so1, so2, sl0, sl1, sl2) = _k2_sample(q_b, views, tails, biases_c, biases_n, Td)
    to_tm = lambda a: a[:, :Td, :].transpose(1, 0, 2).reshape(Ts, GROUP_WIDTH)
    o_l_s = [(to_tm(so0), to_tm(sl0)), (to_tm(so1), to_tm(sl1)), (to_tm(so2), to_tm(sl2))]
    tm3s = min(TOKEN_TILE, Ts)
    tiled = lambda m: m.reshape(Ts // tm3s, tm3s, D)
    x2s, h2s, eid_s, wts_s, rank_s, cnt_all = _k3(xs, o_l_s, tiled(mod_s[2]), tiled(mod_s[3]), tiled(mod_s[4]),
                                                  ga_s, sgb_s, row(norm_ffn_g[0]), w_ao_b, w_o_b, w_r, b_r, cnt_p, 1)

    n_assign = 2 * (B * S + Ts)
    n_blocks = -(-n_assign // MOE_ROWS) + N_EXPERTS
    pstart, block_expert, n_used, fill_rows = _slot_tables(cnt_all[:, 0].astype(jnp.int32), n_blocks)
    dest_p = _slots(eid, rank, pstart)
    dest_s = _slots(eid_s, rank_s, pstart)
    n_slots = n_blocks * MOE_ROWS
    assert tm3s == tm3, "prompt and sample token tiles must match to share the dispatch"
    slots = _dispatch(jnp.concatenate([dest_p, dest_s], axis=0), fill_rows, h2, h2s, n_slots)
    ys = _k4(block_expert, n_used, slots, w_exp_gate[0], w_exp_up[0], w_exp_down[0])
    per_token = lambda a: a.transpose(0, 2, 1).reshape(-1, 2)
    y_prompt = _k5(dest_p, x2, per_token(wts), mod_p[5], row(norm_final_g), ys, S // tm3).reshape(B, S, D)
    y_s = _k5(dest_s, x2s, per_token(wts_s), tiled(mod_s[5]), row(norm_final_g), ys, 1)
    y_sample = y_s.reshape(Td, Bd, D).transpose(1, 0, 2)
    kv_s = [_cache_unview(nc)[None] for nc in (nc0, nc1, nc2)]
    conv_s = conv_tm.transpose(1, 0, 2)[None]

    return (y_prompt, y_sample, kv_p[0], kv_p[1], kv_p[2], conv_p,
            kv_s[0], kv_s[1], kv_s[2], conv_s)
```

```python
import functools
import math

import jax
import jax.numpy as jnp
from jax import lax
from jax.experimental import pallas as pl
from jax.experimental.pallas import tpu as pltpu

F32 = jnp.float32
BF16 = jnp.bfloat16

D_MODEL = 1024
HEAD_DIM = 64
HEADS_PER_GROUP = 4
GROUP_WIDTH = HEADS_PER_GROUP * HEAD_DIM
LANES = 128
SUBLANES = 8
CONV_CHUNK = 32
STAGE_HALVES = GROUP_WIDTH // LANES
DILATED_GROUPS = ((128, 1), (512, 4), (2048, 16))
N_GROUPS = len(DILATED_GROUPS)
ATTN_WIDTH = N_GROUPS * GROUP_WIDTH
CONV_CH = D_MODEL // 2
CONV_WIDTH = 31
CONV_HIST = 32
N_BUCKETS = 32
MAX_DISTANCE = 2048
N_EXPERT_GROUPS = 4
EXPERTS_PER_GROUP = 8
N_EXPERTS = N_EXPERT_GROUPS * EXPERTS_PER_GROUP
D_FF_EXPERT = D_MODEL // 2
EPS = 1e-6
NEG_INF = -1e30

COL_KV = ATTN_WIDTH
COL_ULIN = COL_KV + 2 * ATTN_WIDTH
COL_UGATE = COL_ULIN + CONV_CH
COL_GA = COL_UGATE + CONV_CH
COL_GB = COL_GA + D_MODEL
IN_COLS = COL_GB + D_MODEL

ROUTER_ROWS = 8 + N_EXPERTS

V7X_VMEM_LIMIT = 56 * 1024 * 1024
TOKEN_TILE = 512
ATTN_ROWS = 2048
ATTN_BLOCK = 128
MOE_ROWS = 512
SAMPLE_BATCH = 2


def _sigmoid(x):
    return 1.0 / (1.0 + jnp.exp(-x))


def _silu(x):
    return x * _sigmoid(x)


def _rms_modulate(x, g, sc, sh):
    r = lax.rsqrt(jnp.mean(x * x, axis=-1, keepdims=True) + EPS)
    return ((x * r) * g) * (1.0 + sc) + sh


def _conv_tail(y, g, b):
    mu = jnp.mean(y, axis=-1, keepdims=True)
    yc = y - mu
    var = jnp.mean(yc * yc, axis=-1, keepdims=True)
    return _silu((yc * lax.rsqrt(var + EPS)) * g + b)


def _params(semantics):
    return pltpu.CompilerParams(dimension_semantics=semantics, vmem_limit_bytes=V7X_VMEM_LIMIT)


def _const_spec(shape):
    nd = len(shape)
    return pl.BlockSpec(shape, lambda *_: (0,) * nd)


def _weight_spec(shape):
    nd = len(shape)
    return pl.BlockSpec(shape, lambda *_: (0,) * nd, pipeline_mode=pl.Buffered(1))


def _mod_kernel(c_ref, w_ref, b_ref, o_ref):
    s = _silu(c_ref[...]).astype(BF16)
    o_ref[...] = jnp.dot(s, w_ref[...].astype(BF16), preferred_element_type=F32) + b_ref[...]


def _modulation(c_all, w_mod, b_mod):
    rows = c_all.shape[0]
    n_out = w_mod.shape[1]
    chunk = D_MODEL
    return pl.pallas_call(
        _mod_kernel,
        grid=(n_out // chunk,),
        in_specs=[_const_spec((rows, D_MODEL)),
                  pl.BlockSpec((D_MODEL, chunk), lambda j: (0, j)),
                  pl.BlockSpec((1, chunk), lambda j: (0, j))],
        out_specs=pl.BlockSpec((rows, chunk), lambda j: (0, j)),
        out_shape=jax.ShapeDtypeStruct((rows, n_out), F32),
        compiler_params=_params(("arbitrary",)),
        name="modulation",
    )(c_all, w_mod, b_mod)


def _to_streams(ref, val, dil, stage):
    if dil == 1:
        ref[...] = val.astype(ref.dtype).reshape(ref.shape)
        return
    n = val.shape[0] // dil
    for c, half in enumerate(stage):
        half[...] = val[:, c * LANES:(c + 1) * LANES]
        for r in range(dil):
            lo = r * GROUP_WIDTH + c * LANES
            ref[0, :, lo:lo + LANES] = half[pl.ds(r, n, stride=dil), :].astype(ref.dtype)


def _from_streams(ref, dil, stage):
    if dil == 1:
        return ref[...].astype(F32).reshape(ref.shape[-2:])
    n = ref.shape[1]
    for c, half in enumerate(stage):
        for r in range(dil):
            lo = r * GROUP_WIDTH + c * LANES
            half[pl.ds(r, n, stride=dil), :] = ref[0, :, lo:lo + LANES].astype(F32)
    return jnp.concatenate([half[...] for half in stage], axis=1)


def _project_common(hb, win_ref, outs, dils, stages):
    (q_refs, k_refs, v_refs) = outs

    def proj(lo, width):
        return jnp.dot(hb, win_ref[:, lo:lo + width], preferred_element_type=F32)

    zq = proj(0, ATTN_WIDTH)
    for g in range(N_GROUPS):
        _to_streams(q_refs[g], zq[:, g * GROUP_WIDTH:(g + 1) * GROUP_WIDTH], dils[g], stages[g][0])
    zkvs = []
    for g in range(N_GROUPS):
        zkv = proj(COL_KV + 2 * GROUP_WIDTH * g, 2 * GROUP_WIDTH)
        _to_streams(k_refs[g], zkv[:, :GROUP_WIDTH], dils[g], stages[g][1])
        _to_streams(v_refs[g], zkv[:, GROUP_WIDTH:], dils[g], stages[g][2])
        zkvs.append(zkv)
    u = proj(COL_ULIN, CONV_CH) * _sigmoid(proj(COL_UGATE, CONV_CH))
    return zkvs, u, proj


def _k1_kernel(tail_rows, n_tiles,
               x_ref, sh_ref, sc_ref, g_ref, win_ref, dww_ref, dwb_ref, lng_ref, lnb_ref, wco_ref,
               q0, q1, q2, k0, v0, k1, v1, k2, v2, kvt0, kvt1, kvt2, ga_ref, sgb_ref, convp_ref, uext, sbuf, ushift, sga, *stage_refs):
    i = pl.program_id(1)
    tm = x_ref.shape[1]

    @pl.when(i == 0)
    def _():
        uext[0:CONV_HIST, :] = jnp.zeros((CONV_HIST, CONV_CH), F32)

    h = _rms_modulate(x_ref[0], g_ref[...], sc_ref[0], sh_ref[0])
    hb = h.astype(BF16)

    def proj(lo, width):
        return jnp.dot(hb, win_ref[:, lo:lo + width], preferred_element_type=F32)

    uext[CONV_HIST:CONV_HIST + tm, :] = proj(COL_ULIN, CONV_CH) * _sigmoid(proj(COL_UGATE, CONV_CH))
    base = CONV_HIST - (CONV_WIDTH - 1)
    span = tm + CONV_HIST - SUBLANES
    for b in range(1, SUBLANES):
        ushift[b - 1, 0:span, :] = uext[b:b + span, :]

    def conv_chunk(r0):
        accs = [dwb_ref[...]] * (CONV_CHUNK // SUBLANES)
        for k in range(CONV_WIDTH):
            b = (base + k) % SUBLANES
            lo = r0 + base + k - b
            w8 = dww_ref[k]
            for q in range(len(accs)):
                rows = slice(lo + q * SUBLANES, lo + (q + 1) * SUBLANES)
                src = uext[rows, :] if b == 0 else ushift[b - 1, rows, :]
                accs[q] = accs[q] + w8 * src
        s = _conv_tail(jnp.concatenate(accs, axis=0), lng_ref[...], lnb_ref[...])
        sbuf[r0:r0 + CONV_CHUNK, :] = s.astype(BF16)

    dils = tuple(dil for _, dil in DILATED_GROUPS)
    pairs = [stage_refs[j:j + STAGE_HALVES] for j in range(0, len(stage_refs), STAGE_HALVES)]
    stages = [(None,) * 3] + [pairs[3 * (g - 1):3 * g] for g in range(1, N_GROUPS)]
    pieces = []
    for g, q_ref in enumerate((q0, q1, q2)):
        def q_piece(g=g, q_ref=q_ref):
            _to_streams(q_ref, proj(g * GROUP_WIDTH, GROUP_WIDTH), dils[g], stages[g][0])
        pieces.append(q_piece)
    for g, (k_ref, v_ref, kvt) in enumerate(((k0, v0, kvt0), (k1, v1, kvt1), (k2, v2, kvt2))):
        for half, ref in enumerate((k_ref, v_ref)):
            def kv_piece(g=g, half=half, ref=ref, kvt=kvt):
                z = proj(COL_KV + (2 * g + half) * GROUP_WIDTH, GROUP_WIDTH)
                _to_streams(ref, z, dils[g], stages[g][1 + half])
                tr = kvt.shape[2]

                @pl.when(i >= n_tiles - tail_rows[g] // tr)
                def _():
                    kvt[0, half * GROUP_WIDTH:(half + 1) * GROUP_WIDTH, :] = z[tm - tr:, :].T
            pieces.append(kv_piece)
    for c in range(D_MODEL // GROUP_WIDTH):
        cols = slice(c * GROUP_WIDTH, (c + 1) * GROUP_WIDTH)

        def gb_piece(c=c, cols=cols):
            sgb_ref[:, cols] = _sigmoid(proj(COL_GB + c * GROUP_WIDTH, GROUP_WIDTH)).astype(BF16)

        def ga_piece(c=c, cols=cols):
            sga[:, cols] = _sigmoid(proj(COL_GA + c * GROUP_WIDTH, GROUP_WIDTH))
        pieces += [gb_piece, ga_piece]
    n_qkv = 3 * N_GROUPS
    for piece in pieces[:n_qkv]:
        piece()
    for r0 in range(0, tm, CONV_CHUNK):
        conv_chunk(r0)
    for piece in pieces[n_qkv:]:
        piece()
    last = uext[tm:tm + CONV_HIST, :]
    uext[0:CONV_HIST, :] = last
    convp_ref[0] = last
    a = jnp.dot(sbuf[...], wco_ref[...], preferred_element_type=F32)
    ga_ref[...] = (sga[...] * a).astype(BF16)


def _k1_prompt(x, sh1, sc1, norm_g, w_in_b, dw_w, dw_b, ln_g, ln_b, w_co_b):
    B, S, _ = x.shape
    tm = min(TOKEN_TILE, S)
    nt = S // tm
    tail_rows = tuple(min(win, S) for win, _ in DILATED_GROUPS)
    tail_blk = tuple(min(t, tm) for t in tail_rows)

    def tok_spec(width):
        return pl.BlockSpec((tm, width), lambda b, i: (b * nt + i, 0))

    def tail_spec(g):
        first = nt - tail_rows[g] // tail_blk[g]
        return pl.BlockSpec((1, 2 * GROUP_WIDTH, tail_blk[g]), lambda b, i: (b, 0, jnp.maximum(i - first, 0)))

    def stream_shape(dil):
        return jax.ShapeDtypeStruct((B, S // dil, dil * GROUP_WIDTH), BF16)

    def stream_spec(dil):
        return pl.BlockSpec((1, tm // dil, dil * GROUP_WIDTH), lambda b, i: (b, i, 0))

    dils = [dil for _, dil in DILATED_GROUPS]
    qkv_order = [dils[0], dils[1], dils[2]] + [d for d in dils for _ in range(2)]
    mod_spec = pl.BlockSpec((1, 1, D_MODEL), lambda b, i: (b, 0, 0))
    out_shape = ([stream_shape(d) for d in qkv_order]
                 + [jax.ShapeDtypeStruct((B, 2 * GROUP_WIDTH, tail_rows[g]), F32) for g in range(N_GROUPS)]
                 + [jax.ShapeDtypeStruct((B * S, D_MODEL), BF16)] * 2
                 + [jax.ShapeDtypeStruct((B, CONV_HIST, CONV_CH), F32)])
    out_specs = ([stream_spec(d) for d in qkv_order] + [tail_spec(g) for g in range(N_GROUPS)]
                 + [tok_spec(D_MODEL)] * 2 + [pl.BlockSpec((1, CONV_HIST, CONV_CH), lambda b, i: (b, 0, 0))])
    n_stage = 3 * sum(1 for d in dils if d > 1)
    return pl.pallas_call(
        functools.partial(_k1_kernel, tail_rows, nt),
        grid=(B, nt),
        in_specs=[pl.BlockSpec((1, tm, D_MODEL), lambda b, i: (b, i, 0)), mod_spec, mod_spec,
                  _const_spec((1, D_MODEL)), _weight_spec((D_MODEL, IN_COLS)),
                  _const_spec((CONV_WIDTH, SUBLANES, CONV_CH)), _const_spec((SUBLANES, CONV_CH)),
                  _const_spec((1, CONV_CH)), _const_spec((1, CONV_CH)), _weight_spec((CONV_CH, D_MODEL))],
        out_specs=out_specs,
        out_shape=out_shape,
        scratch_shapes=([pltpu.VMEM((CONV_HIST + tm, CONV_CH), F32), pltpu.VMEM((tm, CONV_CH), BF16),
                         pltpu.VMEM((SUBLANES - 1, CONV_HIST + tm - SUBLANES, CONV_CH), F32),
                         pltpu.VMEM((tm, D_MODEL), F32)]
                        + [pltpu.VMEM((tm, LANES), F32)] * (n_stage * STAGE_HALVES)),
        compiler_params=_params(("arbitrary", "arbitrary")),
        name="inproj_prompt",
    )(x, sh1, sc1, norm_g, w_in_b, dw_w, dw_b, ln_g, ln_b, w_co_b)


def _k1s_kernel(n_steps, x_ref, sh_ref, sc_ref, g_ref, win_ref, st_ref, dww_ref, dwb_ref, lng_ref, lnb_ref, wco_ref,
                q0, q1, q2, k0, v0, k1, v1, k2, v2, ga_ref, sgb_ref, conv_ref):
    bd = st_ref.shape[1]
    hist = st_ref.shape[0]
    h = _rms_modulate(x_ref[...], g_ref[...], sc_ref[...], sh_ref[...])
    hb = h.astype(BF16)
    _, u, proj = _project_common(hb, win_ref, ((q0, q1, q2), (k0, k1, k2), (v0, v1, v2)),
                                 (1,) * N_GROUPS, [(None,) * 3] * N_GROUPS)

    def ext(j):
        return st_ref[j] if j < hist else u[(j - hist) * bd:(j - hist + 1) * bd, :]

    outs = []
    for t in range(n_steps):
        acc = jnp.zeros((bd, CONV_CH), F32) + dwb_ref[...]
        for k in range(CONV_WIDTH):
            acc = acc + dww_ref[k:k + 1, :] * ext(t + k + hist - (CONV_WIDTH - 1))
        outs.append(acc)
    for j in range(hist):
        conv_ref[j] = ext(j + n_steps)
    s = _conv_tail(jnp.concatenate(outs, axis=0), lng_ref[...], lnb_ref[...])
    a = jnp.dot(s.astype(BF16), wco_ref[...], preferred_element_type=F32)
    ga_ref[...] = (_sigmoid(proj(COL_GA, D_MODEL)) * a).astype(BF16)
    sgb_ref[...] = _sigmoid(proj(COL_GB, D_MODEL)).astype(BF16)


def _k1_sample(x_tm, sh1, sc1, norm_g, w_in_b, state_tm, dw_w, dw_b, ln_g, ln_b, w_co_b, n_steps):
    T = x_tm.shape[0]
    hist, bd, _ = state_tm.shape
    out_shape = ([jax.ShapeDtypeStruct((T, GROUP_WIDTH), F32)] * 9
                 + [jax.ShapeDtypeStruct((T, D_MODEL), BF16)] * 2
                 + [jax.ShapeDtypeStruct((hist, bd, CONV_CH), F32)])
    out_specs = ([_const_spec((T, GROUP_WIDTH))] * 9 + [_const_spec((T, D_MODEL))] * 2
                 + [_const_spec((hist, bd, CONV_CH))])
    return pl.pallas_call(
        functools.partial(_k1s_kernel, n_steps),
        grid=(1,),
        in_specs=[_const_spec((T, D_MODEL)), _const_spec((T, D_MODEL)), _const_spec((T, D_MODEL)),
                  _const_spec((1, D_MODEL)), _const_spec((D_MODEL, IN_COLS)), _const_spec((hist, bd, CONV_CH)),
                  _const_spec((CONV_WIDTH, CONV_CH)), _const_spec((1, CONV_CH)),
                  _const_spec((1, CONV_CH)), _const_spec((1, CONV_CH)), _const_spec((CONV_CH, D_MODEL))],
        out_specs=out_specs,
        out_shape=out_shape,
        compiler_params=_params(("arbitrary",)),
        name="inproj_sample",
    )(x_tm, sh1, sc1, norm_g, w_in_b, state_tm, dw_w, dw_b, ln_g, ln_b, w_co_b)


def _t5_bucket(dist):
    max_exact = N_BUCKETS // 2
    d_f = jnp.maximum(dist, 1).astype(F32)
    large = max_exact + (jnp.log(d_f / max_exact) / math.log(MAX_DISTANCE / max_exact)
                         * (N_BUCKETS - max_exact)).astype(jnp.int32)
    large = jnp.minimum(large, N_BUCKETS - 1)
    return jnp.where(dist < max_exact, dist, large)


def _bucket_lookup(rel_bias_g, dist):
    onehot = (_t5_bucket(dist)[..., None] == jnp.arange(N_BUCKETS)).astype(F32)
    return jnp.einsum('...b,bh->h...', onehot, rel_bias_g, precision=lax.Precision.HIGHEST)


def _prompt_bias(rel_bias_g, dil, n_keys):
    blk = n_keys
    i = jnp.arange(blk)[:, None]
    j = jnp.arange(2 * blk)[None, :]
    rel = i - j + blk
    valid = (rel >= 0) & (rel <= n_keys)
    bias = _bucket_lookup(rel_bias_g, jnp.clip(rel, 0, n_keys) * dil)
    bias = jnp.where(valid[None], bias, NEG_INF)
    return bias.reshape(HEADS_PER_GROUP * blk, 2 * blk).astype(F32)


def _sample_bias(rel_bias_g, dil, n_keys, lc, n_steps, q_rows):
    t = jnp.arange(q_rows)[:, None]
    pos = jnp.arange(lc)[None, :]
    dist_c = lc + t - pos
    dist_n = t - (jnp.arange(128)[None, :] - (128 - n_steps))

    def table(dist, extra):
        ok = (dist >= 0) & (dist % dil == 0) & (dist // dil <= n_keys) & extra & (t < n_steps)
        b = _bucket_lookup(rel_bias_g, jnp.clip(dist, 0, None))
        b = jnp.where(ok[None], b, NEG_INF)
        return jnp.where((t >= n_steps)[None], 0.0, b).astype(F32)

    lane_ok = jnp.arange(128)[None, :] >= 128 - n_steps
    return table(dist_c, True), table(dist_n, lane_ok)


def _k2_kernel(q_ref, k_ref, kh_ref, v_ref, vh_ref, bias_ref, o_ref, lse_ref, kbuf, vbuf):
    i = pl.program_id(2)
    rows = q_ref.shape[1]
    n_streams = q_ref.shape[2] // GROUP_WIDTH
    blk = ATTN_BLOCK
    kbuf[0:blk, :] = kh_ref[0]
    kbuf[blk:blk + rows, :] = k_ref[0]
    vbuf[0:blk, :] = vh_ref[0]
    vbuf[blk:blk + rows, :] = v_ref[0]
    lane_head = lax.broadcasted_iota(jnp.int32, (blk, GROUP_WIDTH), 1) // HEAD_DIM
    col = lax.broadcasted_iota(jnp.int32, (HEADS_PER_GROUP * blk, 2 * blk), 1)
    first_mask = jnp.where((col < blk) & (i == 0), NEG_INF, 0.0).astype(F32)
    bias = bias_ref[...]
    for st in range(n_streams):
        cols = slice(st * GROUP_WIDTH, (st + 1) * GROUP_WIDTH)
        for j in range(rows // blk):
            qb = q_ref[0, j * blk:(j + 1) * blk, cols]
            q4 = jnp.concatenate([jnp.where(lane_head == h, qb, jnp.zeros_like(qb)) for h in range(HEADS_PER_GROUP)],
                                 axis=0)
            kc = kbuf[j * blk:(j + 2) * blk, cols]
            vc = vbuf[j * blk:(j + 2) * blk, cols]
            s = lax.dot_general(q4, kc, (((1,), (1,)), ((), ())), preferred_element_type=F32) + bias
            if j == 0:
                s = s + first_mask
            m = jnp.max(s, axis=-1, keepdims=True)
            p = jnp.exp(s - m)
            l = jnp.sum(p, axis=-1, keepdims=True)
            o4 = jnp.dot(p.astype(BF16), vc, preferred_element_type=F32) * (1.0 / l)
            lse4 = m + jnp.log(l)
            o = jnp.zeros((blk, GROUP_WIDTH), F32)
            lse = jnp.zeros((blk, GROUP_WIDTH), F32)
            for h in range(HEADS_PER_GROUP):
                sel = lane_head == h
                o = jnp.where(sel, o4[h * blk:(h + 1) * blk, :], o)
                lse = jnp.where(sel, lse4[h * blk:(h + 1) * blk, :], lse)
            o_ref[0, j * blk:(j + 1) * blk, cols] = o.astype(o_ref.dtype)
            lse_ref[0, j * blk:(j + 1) * blk, cols] = lse


def _k2_prompt(q, k, v, bias, dil):
    B, L, _ = q.shape
    rows = min(ATTN_ROWS, L)
    per = rows // ATTN_BLOCK
    n_streams = min(dil, max(1, ATTN_ROWS // rows))
    width = n_streams * GROUP_WIDTH
    main = pl.BlockSpec((1, rows, width), lambda b, r, i: (b, i, r))
    halo = pl.BlockSpec((1, ATTN_BLOCK, width), lambda b, r, i: (b, jnp.maximum(i * per - 1, 0), r))
    return pl.pallas_call(
        _k2_kernel,
        grid=(B, dil // n_streams, L // rows),
        in_specs=[main, main, halo, main, halo, _const_spec(bias.shape)],
        out_specs=[main, main],
        out_shape=[jax.ShapeDtypeStruct(q.shape, BF16), jax.ShapeDtypeStruct(q.shape, F32)],
        scratch_shapes=[pltpu.VMEM((ATTN_BLOCK + rows, width), BF16)] * 2,
        compiler_params=_params(("arbitrary", "arbitrary", "arbitrary")),
        name=f"attn_prompt_d{dil}",
    )(q, k, k, v, v, bias)


def _k2s_kernel(n_steps, q_ref, c0, c1, c2, t0, t1, t2, bc0, bc1, bc2, bn0, bn1, bn2,
                nc0, nc1, nc2, o0, o1, o2, l0, l1, l2):
    lane = lax.broadcasted_iota(jnp.int32, (2 * GROUP_WIDTH, 128), 1)
    keep = 128 - n_steps
    groups = ((c0, t0, bc0, bn0, nc0, o0, l0), (c1, t1, bc1, bn1, nc1, o1, l1), (c2, t2, bc2, bn2, nc2, o2, l2))
    for bi in range(q_ref.shape[0]):
        for g, (c_ref, t_ref, bc_ref, bn_ref, nc_ref, o_ref, l_ref) in enumerate(groups):
            _sample_group(bi, g, lane, keep, q_ref, c_ref, t_ref, bc_ref, bn_ref, nc_ref, o_ref, l_ref)


def _sample_group(bi, g, lane, keep, q_ref, c_ref, t_ref, bc_ref, bn_ref, nc_ref, o_ref, l_ref):
    lc = c_ref.shape[2]
    n_tiles = lc // 128
    new_rows = t_ref[bi]
    tail = jnp.concatenate([jnp.zeros((128 - new_rows.shape[0], new_rows.shape[1]), F32), new_rows], axis=0).T
    cur = pltpu.roll(c_ref[bi, :, 0:128], keep, 1)
    for c in range(n_tiles):
        nxt = pltpu.roll(c_ref[bi, :, (c + 1) * 128:(c + 2) * 128], keep, 1) if c + 1 < n_tiles else tail
        nc_ref[bi, :, c * 128:(c + 1) * 128] = jnp.where(lane < keep, cur, nxt)
        cur = nxt
    for h in range(HEADS_PER_GROUP):
        lo = g * GROUP_WIDTH + h * HEAD_DIM
        qh = q_ref[bi, :, lo:lo + HEAD_DIM].astype(BF16)
        kh = c_ref[bi, h * HEAD_DIM:(h + 1) * HEAD_DIM, :].astype(BF16)
        vh = c_ref[bi, GROUP_WIDTH + h * HEAD_DIM:GROUP_WIDTH + (h + 1) * HEAD_DIM, :].astype(BF16)
        kt = tail[h * HEAD_DIM:(h + 1) * HEAD_DIM, :].astype(BF16)
        vt = tail[GROUP_WIDTH + h * HEAD_DIM:GROUP_WIDTH + (h + 1) * HEAD_DIM, :].astype(BF16)
        sc = jnp.dot(qh, kh, preferred_element_type=F32) + bc_ref[h]
        sn = jnp.dot(qh, kt, preferred_element_type=F32) + bn_ref[h]
        m = jnp.maximum(jnp.max(sc, axis=-1, keepdims=True), jnp.max(sn, axis=-1, keepdims=True))
        pc = jnp.exp(sc - m)
        pn = jnp.exp(sn - m)
        l = jnp.sum(pc, axis=-1, keepdims=True) + jnp.sum(pn, axis=-1, keepdims=True)
        nt_dims = (((1,), (1,)), ((), ()))
        o = (lax.dot_general(pc.astype(BF16), vh, nt_dims, preferred_element_type=F32)
             + lax.dot_general(pn.astype(BF16), vt, nt_dims, preferred_element_type=F32)) * (1.0 / l)
        o_ref[bi, :, h * HEAD_DIM:(h + 1) * HEAD_DIM] = o
        l_ref[bi, :, h * HEAD_DIM:(h + 1) * HEAD_DIM] = jnp.broadcast_to(m + jnp.log(l), o.shape)


def _k2_sample(q_b, caches, tails, biases_c, biases_n, n_steps):
    bd, q_rows, _ = q_b.shape
    bb = SAMPLE_BATCH if bd % SAMPLE_BATCH == 0 else 1
    per_b = lambda shape: pl.BlockSpec((bb,) + shape[1:], lambda b: (b,) + (0,) * (len(shape) - 1))
    ins = [q_b] + list(caches) + list(tails) + list(biases_c) + list(biases_n)
    in_specs = ([per_b(q_b.shape)] + [per_b(c.shape) for c in caches] + [per_b(t.shape) for t in tails]
                + [_const_spec(b.shape) for b in biases_c] + [_const_spec(b.shape) for b in biases_n])
    o_shape = jax.ShapeDtypeStruct((bd, q_rows, GROUP_WIDTH), F32)
    out_shape = [jax.ShapeDtypeStruct(c.shape, F32) for c in caches] + [o_shape] * 6
    out_specs = [per_b(c.shape) for c in caches] + [per_b(o_shape.shape)] * 6
    return pl.pallas_call(
        functools.partial(_k2s_kernel, n_steps),
        grid=(bd // bb,),
        in_specs=in_specs,
        out_specs=out_specs,
        out_shape=out_shape,
        compiler_params=_params(("arbitrary",)),
        name="attn_sample",
    )(*ins)


def _k3_kernel(x_ref, o0, o1, o2, l0, l1, l2, ga_ref, sgb_ref, g1_ref, sh2_ref, sc2_ref, gf_ref,
               wao_ref, wo_ref, wr_ref, br_ref, before_ref, cnt_in_ref,
               x2_ref, h2_ref, eid_ref, wts_ref, rank_ref, cnt_ref, carry, *stage_refs):
    @pl.when(pl.program_id(0) == 0)
    def _():
        carry[...] = cnt_in_ref[...]

    dils = tuple(x_ref.shape[0] // r.shape[-2] for r in (o0, o1, o2))
    pairs = [stage_refs[j:j + STAGE_HALVES] for j in range(0, len(stage_refs), STAGE_HALVES)]
    os_ = [_from_streams(r, d, pairs[2 * g]) for g, (r, d) in enumerate(zip((o0, o1, o2), dils))]
    ls = [_from_streams(r, d, pairs[2 * g + 1]) for g, (r, d) in enumerate(zip((l0, l1, l2), dils))]
    m = jnp.maximum(jnp.maximum(ls[0], ls[1]), ls[2])
    ws = [jnp.exp(l - m) for l in ls]
    den = ws[0] + ws[1] + ws[2]
    o = (ws[0] * os_[0] + ws[1] * os_[1] + ws[2] * os_[2]) / den
    b = jnp.dot(o.astype(BF16), wao_ref[...], preferred_element_type=F32)
    mixed = ga_ref[...].astype(F32) + sgb_ref[...].astype(F32) * b
    x2 = x_ref[...] + g1_ref[0] * jnp.dot(mixed.astype(BF16), wo_ref[...], preferred_element_type=F32)
    x2_ref[...] = x2
    h2 = _rms_modulate(x2, gf_ref[...], sc2_ref[0], sh2_ref[0])
    h2_ref[...] = h2
    lt = lax.dot_general(wr_ref[...], h2, (((1,), (1,)), ((), ())), preferred_element_type=F32,
                         precision=lax.Precision.HIGHEST) + br_ref[...]
    tm = h2.shape[0]
    gl = lt[0:8, :]
    gmax = jnp.max(gl, axis=0, keepdims=True)
    r8 = lax.broadcasted_iota(jnp.int32, (8, tm), 0)
    grp = jnp.min(jnp.where(gl == gmax, r8, 8), axis=0, keepdims=True)
    p_grp = 1.0 / jnp.sum(jnp.exp(gl - gmax), axis=0, keepdims=True)
    es = jnp.zeros((EXPERTS_PER_GROUP, tm), F32)
    for g in range(N_EXPERT_GROUPS):
        es = jnp.where(grp == g, lt[8 + 8 * g:16 + 8 * g, :], es)
    v1 = jnp.max(es, axis=0, keepdims=True)
    i1 = jnp.min(jnp.where(es == v1, r8, 8), axis=0, keepdims=True)
    rest = jnp.where(r8 == i1, -jnp.inf, es)
    v2 = jnp.max(rest, axis=0, keepdims=True)
    i2 = jnp.min(jnp.where(rest == v2, r8, 8), axis=0, keepdims=True)
    e21 = jnp.exp(v2 - v1)
    w1 = p_grp / (1.0 + e21)
    e1 = grp * EXPERTS_PER_GROUP + i1
    e2 = grp * EXPERTS_PER_GROUP + i2
    eid_ref[0, 0:1, :] = e1
    eid_ref[0, 1:2, :] = e2
    wts_ref[0, 0:1, :] = w1
    wts_ref[0, 1:2, :] = w1 * e21
    r_e = lax.broadcasted_iota(jnp.int32, (N_EXPERTS, tm), 0)
    hit1 = r_e == e1
    hit2 = r_e == e2
    both = jnp.where(hit1 | hit2, 1.0, 0.0)
    base = carry[...] + jnp.dot(both.astype(BF16), before_ref[...], preferred_element_type=F32)
    rank_ref[0, 0:1, :] = jnp.sum(jnp.where(hit1, base, 0.0), axis=0, keepdims=True).astype(jnp.int32)
    rank_ref[0, 1:2, :] = jnp.sum(jnp.where(hit2, base, 0.0), axis=0, keepdims=True).astype(jnp.int32)
    total = carry[...] + jnp.sum(both, axis=1, keepdims=True)
    carry[...] = total
    cnt_ref[...] = total


def _k3(x, o_l, g1, sh2, sc2, ga, sgb, norm_ffn_g, w_ao_b, w_o_b, w_r, b_r, cnt_in, tiles_per_mod):
    T = x.shape[0]
    tm = min(TOKEN_TILE, T)
    nt = T // tm
    tok = lambda w: pl.BlockSpec((tm, w), lambda t: (t, 0))
    mod = pl.BlockSpec((1,) + g1.shape[1:], lambda t: (t // tiles_per_mod, 0, 0))
    small = pl.BlockSpec((1, 2, tm), lambda t: (t, 0, 0))
    small_i = jax.ShapeDtypeStruct((nt, 2, tm), jnp.int32)
    before = (jnp.arange(tm)[:, None] < jnp.arange(tm)[None, :]).astype(BF16)

    def attn_spec(a):
        if a.ndim == 2:
            return tok(GROUP_WIDTH)
        dil = a.shape[2] // GROUP_WIDTH
        return pl.BlockSpec((1, tm // dil, a.shape[2]), lambda t: (t // tiles_per_mod, t % tiles_per_mod, 0))

    attn_in = [o_l[0][0], o_l[1][0], o_l[2][0], o_l[0][1], o_l[1][1], o_l[2][1]]
    return pl.pallas_call(
        _k3_kernel,
        grid=(nt,),
        in_specs=[tok(D_MODEL)] + [attn_spec(a) for a in attn_in] + [tok(D_MODEL)] * 2 + [mod] * 3
                 + [_const_spec((1, D_MODEL)), _weight_spec(w_ao_b.shape), _weight_spec(w_o_b.shape),
                    _const_spec(w_r.shape), _const_spec(b_r.shape), _weight_spec((tm, tm)),
                    _const_spec((N_EXPERTS, 1))],
        out_specs=[tok(D_MODEL), tok(D_MODEL), small, small, small, _const_spec((N_EXPERTS, 1))],
        out_shape=[jax.ShapeDtypeStruct((T, D_MODEL), F32), jax.ShapeDtypeStruct((T, D_MODEL), F32),
                   small_i, jax.ShapeDtypeStruct((nt, 2, tm), F32), small_i,
                   jax.ShapeDtypeStruct((N_EXPERTS, 1), F32)],
        scratch_shapes=([pltpu.VMEM((N_EXPERTS, 1), F32)]
                        + [pltpu.VMEM((tm, LANES), F32)] * (2 * N_GROUPS * STAGE_HALVES)),
        compiler_params=_params(("arbitrary",)),
        name="merge_router",
    )(x, *attn_in, ga, sgb, g1, sh2, sc2, norm_ffn_g, w_ao_b, w_o_b, w_r, b_r, before, cnt_in)


def _rows_wait(n_rows, hbm, vmem, sem):
    pltpu.make_async_copy(hbm.at[pl.ds(0, n_rows)], vmem, sem).wait()


def _dispatch_kernel(n_tiles, n_first, dest_ref, fill_ref, ha_ref, hb_ref, xs_ref, buf, sem, zbuf, zsem):
    i = pl.program_id(0)
    slot = i % 2 if n_tiles > 1 else 0
    tm = ha_ref.shape[0]

    def drain(s):
        for _ in range(2):
            _rows_wait(tm, xs_ref, buf.at[s], sem.at[s])

    @pl.when(i == 0)
    def _():
        zbuf[...] = jnp.zeros(zbuf.shape, F32)

        def fill(row):
            return pltpu.make_async_copy(zbuf, xs_ref.at[pl.ds(pl.multiple_of(row, MOE_ROWS), MOE_ROWS)], zsem)

        n_blocks = xs_ref.shape[0] // MOE_ROWS
        n_used = fill_ref[N_EXPERTS]

        def start_block(j, carry):
            fill(j * MOE_ROWS).start()
            return carry

        def wait_block(j, carry):
            fill(j * MOE_ROWS).wait()
            return carry

        for e in range(N_EXPERTS):
            @pl.when(fill_ref[e] >= 0)
            def _(e=e):
                fill(fill_ref[e]).start()
        lax.fori_loop(n_used, n_blocks, start_block, 0)
        for e in range(N_EXPERTS):
            @pl.when(fill_ref[e] >= 0)
            def _(e=e):
                fill(fill_ref[e]).wait()
        lax.fori_loop(n_used, n_blocks, wait_block, 0)

    if n_tiles > 2:
        @pl.when(i >= 2)
        def _():
            drain(slot)

    @pl.when(i < n_first)
    def _():
        buf[slot] = ha_ref[...]

    @pl.when(i >= n_first)
    def _():
        buf[slot] = hb_ref[...]

    for r in range(tm):
        for k in range(2):
            pltpu.make_async_copy(buf.at[slot, pl.ds(r, 1)], xs_ref.at[pl.ds(dest_ref[0, k, r], 1)], sem.at[slot]).start()

    if n_tiles == 1:
        drain(0)
    else:
        @pl.when(i == n_tiles - 1)
        def _():
            drain(slot)
            drain(1 - slot)


def _dispatch(dest, fill_rows, h_a, h_b, n_slots):
    nt, _, tm = dest.shape
    n_first = h_a.shape[0] // tm
    assert h_a.shape[0] % tm == 0 and h_b.shape[0] == (nt - n_first) * tm
    return pl.pallas_call(
        functools.partial(_dispatch_kernel, nt, n_first),
        grid=(nt,),
        in_specs=[pl.BlockSpec((1, 2, tm), lambda t: (t, 0, 0), memory_space=pltpu.SMEM),
                  pl.BlockSpec(memory_space=pltpu.SMEM),
                  pl.BlockSpec((tm, D_MODEL), lambda t: (jnp.minimum(t, n_first - 1), 0)),
                  pl.BlockSpec((tm, D_MODEL), lambda t: (jnp.maximum(t - n_first, 0), 0))],
        out_specs=pl.BlockSpec(memory_space=pl.ANY),
        out_shape=jax.ShapeDtypeStruct((n_slots, D_MODEL), F32),
        scratch_shapes=[pltpu.VMEM((2, tm, D_MODEL), F32), pltpu.SemaphoreType.DMA((2,)),
                        pltpu.VMEM((MOE_ROWS, D_MODEL), F32), pltpu.SemaphoreType.DMA(())],
        compiler_params=_params(("arbitrary",)),
        name="dispatch_rows",
    )(dest, fill_rows, h_a, h_b)


def _k4_kernel(be_ref, nused_ref, x_ref, wg_ref, wu_ref, wd_ref, y_ref, wg_b, wu_b, wd_b):
    i = pl.program_id(0)

    @pl.when((i == 0) | (be_ref[i] != be_ref[jnp.maximum(i - 1, 0)]))
    def _():
        wg_b[...] = wg_ref[0].astype(BF16)
        wu_b[...] = wu_ref[0].astype(BF16)
        wd_b[...] = wd_ref[0].astype(BF16)

    @pl.when(i < nused_ref[0])
    def _():
        xb = x_ref[...].astype(BF16)
        gate = jnp.dot(xb, wg_b[...], preferred_element_type=F32)
        up = jnp.dot(xb, wu_b[...], preferred_element_type=F32)
        mid = (_silu(gate) * up).astype(BF16)
        y_ref[...] = jnp.dot(mid, wd_b[...], preferred_element_type=F32)

    @pl.when(i >= nused_ref[0])
    def _():
        y_ref[...] = jnp.zeros(y_ref.shape, F32)


def _k4(block_expert, n_used, xs, w_gate, w_up, w_down):
    n_blocks = block_expert.shape[0]
    rows = MOE_ROWS
    weight = lambda shape: pl.BlockSpec((1,) + shape, lambda i, be, nu: (be[i], 0, 0))
    grid_spec = pltpu.PrefetchScalarGridSpec(
        num_scalar_prefetch=2,
        grid=(n_blocks,),
        in_specs=[pl.BlockSpec((rows, D_MODEL), lambda i, be, nu: (jnp.minimum(i, nu[0] - 1), 0)),
                  weight((D_MODEL, D_FF_EXPERT)), weight((D_MODEL, D_FF_EXPERT)), weight((D_FF_EXPERT, D_MODEL))],
        out_specs=pl.BlockSpec((rows, D_MODEL), lambda i, be, nu: (i, 0)),
        scratch_shapes=[pltpu.VMEM((D_MODEL, D_FF_EXPERT), BF16), pltpu.VMEM((D_MODEL, D_FF_EXPERT), BF16),
                        pltpu.VMEM((D_FF_EXPERT, D_MODEL), BF16)],
    )
    return pl.pallas_call(
        _k4_kernel,
        grid_spec=grid_spec,
        out_shape=jax.ShapeDtypeStruct((n_blocks * rows, D_MODEL), F32),
        compiler_params=_params(("arbitrary",)),
        name="expert_blocks",
    )(block_expert, n_used, xs, w_gate, w_up, w_down)


def _k5_kernel(n_tiles, idx_ref, x2_ref, w_ref, g2_ref, gfin_ref, ys_hbm, y_ref, ybuf, sem):
    i = pl.program_id(0)
    tm = x2_ref.shape[0]

    @pl.when(i < n_tiles)
    def _():
        slot = i % 2
        for r in range(tm):
            for k in range(2):
                pltpu.make_async_copy(ys_hbm.at[pl.ds(idx_ref[0, k, r], 1)], ybuf.at[slot, k, pl.ds(r, 1)],
                                      sem.at[slot]).start()

    @pl.when(i >= 1)
    def _():
        slot = (i - 1) % 2
        for k in range(2):
            _rows_wait(tm, ys_hbm, ybuf.at[slot, k], sem.at[slot])
        w = w_ref[...]
        f = ybuf[slot, 0] * w[:, 0:1] + ybuf[slot, 1] * w[:, 1:2]
        y = x2_ref[...] + g2_ref[0] * f
        r = lax.rsqrt(jnp.mean(y * y, axis=-1, keepdims=True) + EPS)
        y_ref[...] = (y * r) * gfin_ref[...]


def _k5(dest, x2, wts, g2, norm_final_g, ys, tiles_per_mod):
    nt, _, tm = dest.shape
    T = x2.shape[0]
    prev = lambda t: jnp.maximum(t - 1, 0)
    return pl.pallas_call(
        functools.partial(_k5_kernel, nt),
        grid=(nt + 1,),
        in_specs=[pl.BlockSpec((1, 2, tm), lambda t: (jnp.minimum(t, nt - 1), 0, 0), memory_space=pltpu.SMEM),
                  pl.BlockSpec((tm, D_MODEL), lambda t: (prev(t), 0)),
                  pl.BlockSpec((tm, 2), lambda t: (prev(t), 0)),
                  pl.BlockSpec((1,) + g2.shape[1:], lambda t: (prev(t) // tiles_per_mod, 0, 0)),
                  _const_spec((1, D_MODEL)), pl.BlockSpec(memory_space=pl.ANY)],
        out_specs=pl.BlockSpec((tm, D_MODEL), lambda t: (prev(t), 0)),
        out_shape=jax.ShapeDtypeStruct((T, D_MODEL), F32),
        scratch_shapes=[pltpu.VMEM((2, 2, tm, D_MODEL), F32), pltpu.SemaphoreType.DMA((2,))],
        compiler_params=_params(("arbitrary",)),
        name="combine_norm",
    )(dest, x2, wts, g2, norm_final_g, ys)


def _slot_tables(counts, n_blocks):
    padded = (counts + MOE_ROWS - 1) // MOE_ROWS * MOE_ROWS
    pend = jnp.cumsum(padded)
    pstart = pend - padded
    block_lo = jnp.arange(n_blocks, dtype=jnp.int32) * MOE_ROWS
    block_expert = jnp.minimum(jnp.sum(pend[None, :] <= block_lo[:, None], axis=1), N_EXPERTS - 1).astype(jnp.int32)
    n_used = (pend[-1] // MOE_ROWS).astype(jnp.int32).reshape(1)
    fill_rows = jnp.concatenate([jnp.where(padded > 0, pend - MOE_ROWS, -1).astype(jnp.int32), n_used])
    return pstart.astype(jnp.int32), block_expert, n_used, fill_rows


def _slots(eid, rank, pstart):
    sel = eid[..., None] == jnp.arange(N_EXPERTS, dtype=jnp.int32)
    return jnp.sum(jnp.where(sel, pstart, 0), axis=-1).astype(jnp.int32) + rank


def _cache_view(cache):
    bd, lc = cache.shape[:2]
    return cache.transpose(0, 2, 3, 4, 1).reshape(bd, 2 * GROUP_WIDTH, lc)


def _cache_unview(view):
    bd, _, lc = view.shape
    return view.reshape(bd, 2, HEADS_PER_GROUP, HEAD_DIM, lc).transpose(0, 4, 1, 2, 3)


def kernel(x_prompt, x_sample, c_prompt, c_sample, cache_kv_w128, cache_kv_w512, cache_kv_w2048, state_conv, rel_bias, norm_mix_g, norm_ffn_g, w_mod, b_mod, w_in, dw_w, dw_b, ln_conv_g, ln_conv_b, w_conv_out, w_attn_out, w_out, w_router_group, b_router_group, w_router_expert, b_router_expert, w_exp_gate, w_exp_up, w_exp_down, norm_final_g):
    assert norm_mix_g.shape[0] == 1, "single layer"
    B, S, D = x_prompt.shape
    Bd, Td, _ = x_sample.shape
    caches_in = (cache_kv_w128[0], cache_kv_w512[0], cache_kv_w2048[0])
    for (win, dil), c in zip(DILATED_GROUPS, caches_in):
        assert c.shape[1] >= (win // dil) * dil and c.shape[1] % 128 == 0 and S % (dil * ATTN_BLOCK) == 0

    wi = w_in[0]
    q_cols = wi[:, :ATTN_WIDTH] * (HEAD_DIM ** -0.5)
    kv_cols = []
    for g in range(N_GROUPS):
        kv_cols += [wi[:, ATTN_WIDTH + g * GROUP_WIDTH:ATTN_WIDTH + (g + 1) * GROUP_WIDTH],
                    wi[:, 2 * ATTN_WIDTH + g * GROUP_WIDTH:2 * ATTN_WIDTH + (g + 1) * GROUP_WIDTH]]
    w_in_b = jnp.concatenate([q_cols] + kv_cols + [wi[:, 3 * ATTN_WIDTH:]], axis=1).astype(BF16)
    w_co_b = w_conv_out[0].astype(BF16)
    w_ao_b = w_attn_out[0].astype(BF16)
    w_o_b = w_out[0].astype(BF16)
    w_r = jnp.zeros((ROUTER_ROWS, D), F32)
    w_r = w_r.at[0:N_EXPERT_GROUPS].set(w_router_group[0].T)
    w_r = w_r.at[8:].set(w_router_expert[0].reshape(D, N_EXPERTS).T)
    b_r = jnp.full((ROUTER_ROWS, 1), NEG_INF, F32)
    b_r = b_r.at[0:N_EXPERT_GROUPS, 0].set(b_router_group[0])
    b_r = b_r.at[8:, 0].set(b_router_expert[0].reshape(N_EXPERTS))
    row = lambda v: v.reshape(1, -1)

    n_seq = B + Bd
    n_seq_pad = -(-n_seq // 16) * 16
    c_all = jnp.concatenate([c_prompt, c_sample, jnp.zeros((n_seq_pad - n_seq, D), F32)], axis=0)
    mod = _modulation(c_all, w_mod[0], row(b_mod[0]))
    mod_p = [mod[:B, j * D:(j + 1) * D].reshape(B, 1, D) for j in range(6)]
    mod_s = [jnp.tile(mod[B:B + Bd, j * D:(j + 1) * D], (Td, 1)) for j in range(6)]

    dw_w_rows = jnp.broadcast_to(dw_w[0][:, None, :], (CONV_WIDTH, SUBLANES, CONV_CH))
    dw_b_rows = jnp.broadcast_to(dw_b[0][None, :], (SUBLANES, CONV_CH))
    (q0, q1, q2, k0, v0, k1, v1, k2, v2, kvt0, kvt1, kvt2, ga, sgb, convp) = _k1_prompt(
        x_prompt, mod_p[0], mod_p[1], row(norm_mix_g[0]), w_in_b, dw_w_rows, dw_b_rows,
        row(ln_conv_g[0]), row(ln_conv_b[0]), w_co_b)
    o_l = []
    for g, ((win, dil), qg, kg, vg) in enumerate(zip(DILATED_GROUPS, (q0, q1, q2), (k0, k1, k2), (v0, v1, v2))):
        bias = _prompt_bias(rel_bias[:, g * HEADS_PER_GROUP:(g + 1) * HEADS_PER_GROUP], dil, win // dil)
        o_l.append(_k2_prompt(qg, kg, vg, bias, dil))
    xp = x_prompt.reshape(B * S, D)
    tm3 = min(TOKEN_TILE, B * S)
    x2, h2, eid, wts, rank, cnt_p = _k3(xp, o_l, mod_p[2], mod_p[3], mod_p[4], ga, sgb, row(norm_ffn_g[0]),
                                        w_ao_b, w_o_b, w_r, b_r, jnp.zeros((N_EXPERTS, 1), F32), S // tm3)
    kv_p = [_cache_unview(kvt)[None] for kvt in (kvt0, kvt1, kvt2)]
    conv_p = convp[:, CONV_HIST - (CONV_WIDTH - 1):, :][None]

    Ts = Td * Bd
    xs = x_sample.transpose(1, 0, 2).reshape(Ts, D)
    state_tm = state_conv[0].transpose(1, 0, 2)
    (sq0, sq1, sq2, sk0, sv0, sk1, sv1, sk2, sv2, ga_s, sgb_s, conv_tm) = _k1_sample(
        xs, mod_s[0], mod_s[1], row(norm_mix_g[0]), w_in_b, state_tm, dw_w[0], row(dw_b[0]),
        row(ln_conv_g[0]), row(ln_conv_b[0]), w_co_b, Td)
    q_rows = 16
    q_b = jnp.concatenate([sq0, sq1, sq2], axis=1).reshape(Td, Bd, ATTN_WIDTH).transpose(1, 0, 2)
    q_b = jnp.pad(q_b, ((0, 0), (0, q_rows - Td), (0, 0)))
    views, tails, biases_c, biases_n = [], [], [], []
    for g, ((win, dil), c, sk, sv) in enumerate(zip(DILATED_GROUPS, caches_in, (sk0, sk1, sk2), (sv0, sv1, sv2))):
        lc = c.shape[1]
        views.append(_cache_view(c))
        new_kv = jnp.concatenate([sk, sv], axis=1).reshape(Td, Bd, 2 * GROUP_WIDTH).transpose(1, 0, 2)
        tails.append(jnp.pad(new_kv, ((0, 0), (SUBLANES - Td, 0), (0, 0))))
        bc, bn = _sample_bias(rel_bias[:, g * HEADS_PER_GROUP:(g + 1) * HEADS_PER_GROUP], dil, win // dil, lc, Td, q_rows)
        biases_c.append(bc)
        biases_n.append(bn)
    (nc0, nc1, nc2, so0, so1, so2, sl0, sl1, sl2) = _k2_sample(q_b, views, tails, biases_c, biases_n, Td)
    to_tm = lambda a: a[:, :Td, :].transpose(1, 0, 2).reshape(Ts, GROUP_WIDTH)
    o_l_s = [(to_tm(so0), to_tm(sl0)), (to_tm(so1), to_tm(sl1)), (to_tm(so2), to_tm(sl2))]
    tm3s = min(TOKEN_TILE, Ts)
    tiled = lambda m: m.reshape(Ts // tm3s, tm3s, D)
    x2s, h2s, eid_s, wts_s, rank_s, cnt_all = _k3(xs, o_l_s, tiled(mod_s[2]), tiled(mod_s[3]), tiled(mod_s[4]),
                                                  ga_s, sgb_s, row(norm_ffn_g[0]), w_ao_b, w_o_b, w_r, b_r, cnt_p, 1)

    n_assign = 2 * (B * S + Ts)
    n_blocks = -(-n_assign // MOE_ROWS) + N_EXPERTS
    pstart, block_expert, n_used, fill_rows = _slot_tables(cnt_all[:, 0].astype(jnp.int32), n_blocks)
    dest_p = _slots(eid, rank, pstart)
    dest_s = _slots(eid_s, rank_s, pstart)
    n_slots = n_blocks * MOE_ROWS
    assert tm3s == tm3, "prompt and sample token tiles must match to share the dispatch"
    slots = _dispatch(jnp.concatenate([dest_p, dest_s], axis=0), fill_rows, h2, h2s, n_slots)
    ys = _k4(block_expert, n_used, slots, w_exp_gate[0], w_exp_up[0], w_exp_down[0])
    per_token = lambda a: a.transpose(0, 2, 1).reshape(-1, 2)
    y_prompt = _k5(dest_p, x2, per_token(wts), mod_p[5], row(norm_final_g), ys, S // tm3).reshape(B, S, D)
    y_s = _k5(dest_s, x2s, per_token(wts_s), tiled(mod_s[5]), row(norm_final_g), ys, 1)
    y_sample = y_s.reshape(Td, Bd, D).transpose(1, 0, 2)
    kv_s = [_cache_unview(nc)[None] for nc in (nc0, nc1, nc2)]
    conv_s = conv_tm.transpose(1, 0, 2)[None]

    return (y_prompt, y_sample, kv_p[0], kv_p[1], kv_p[2], conv_p,
            kv_s[0], kv_s[1], kv_s[2], conv_s)
```

```python
import functools
import math

import jax
import jax.numpy as jnp
from jax import lax
from jax.experimental import pallas as pl
from jax.experimental.pallas import tpu as pltpu

F32 = jnp.float32
BF16 = jnp.bfloat16

D_MODEL = 1024
HEAD_DIM = 64
HEADS_PER_GROUP = 4
GROUP_WIDTH = HEADS_PER_GROUP * HEAD_DIM
LANES = 128
SUBLANES = 8
CONV_CHUNK = 32
STAGE_HALVES = GROUP_WIDTH // LANES
DILATED_GROUPS = ((128, 1), (512, 4), (2048, 16))
N_GROUPS = len(DILATED_GROUPS)
ATTN_WIDTH = N_GROUPS * GROUP_WIDTH
CONV_CH = D_MODEL // 2
CONV_WIDTH = 31
CONV_HIST = 32
N_BUCKETS = 32
MAX_DISTANCE = 2048
N_EXPERT_GROUPS = 4
EXPERTS_PER_GROUP = 8
N_EXPERTS = N_EXPERT_GROUPS * EXPERTS_PER_GROUP
D_FF_EXPERT = D_MODEL // 2
EPS = 1e-6
NEG_INF = -1e30

COL_KV = ATTN_WIDTH
COL_ULIN = COL_KV + 2 * ATTN_WIDTH
COL_UGATE = COL_ULIN + CONV_CH
COL_GA = COL_UGATE + CONV_CH
COL_GB = COL_GA + D_MODEL
IN_COLS = COL_GB + D_MODEL

ROUTER_ROWS = 8 + N_EXPERTS

V7X_VMEM_LIMIT = 56 * 1024 * 1024
TOKEN_TILE = 512
ATTN_ROWS = 2048
ATTN_BLOCK = 128
MOE_ROWS = 512
SAMPLE_BATCH = 2
PACKED_ROWS = 16


def _sigmoid(x):
    return 1.0 / (1.0 + jnp.exp(-x))


def _silu(x):
    return x * _sigmoid(x)


def _rms_modulate(x, g, sc, sh):
    r = lax.rsqrt(jnp.mean(x * x, axis=-1, keepdims=True) + EPS)
    return ((x * r) * g) * (1.0 + sc) + sh


def _conv_tail(y, g, b):
    mu = jnp.mean(y, axis=-1, keepdims=True)
    yc = y - mu
    var = jnp.mean(yc * yc, axis=-1, keepdims=True)
    return _silu((yc * lax.rsqrt(var + EPS)) * g + b)


def _params(semantics):
    return pltpu.CompilerParams(dimension_semantics=semantics, vmem_limit_bytes=V7X_VMEM_LIMIT)


def _const_spec(shape):
    nd = len(shape)
    return pl.BlockSpec(shape, lambda *_: (0,) * nd)


def _weight_spec(shape):
    nd = len(shape)
    return pl.BlockSpec(shape, lambda *_: (0,) * nd, pipeline_mode=pl.Buffered(1))


def _mod_kernel(c_ref, w_ref, b_ref, o_ref):
    s = _silu(c_ref[...]).astype(BF16)
    o_ref[...] = jnp.dot(s, w_ref[...].astype(BF16), preferred_element_type=F32) + b_ref[...]


def _modulation(c_all, w_mod, b_mod):
    rows = c_all.shape[0]
    n_out = w_mod.shape[1]
    chunk = D_MODEL
    return pl.pallas_call(
        _mod_kernel,
        grid=(n_out // chunk,),
        in_specs=[_const_spec((rows, D_MODEL)),
                  pl.BlockSpec((D_MODEL, chunk), lambda j: (0, j)),
                  pl.BlockSpec((1, chunk), lambda j: (0, j))],
        out_specs=pl.BlockSpec((rows, chunk), lambda j: (0, j)),
        out_shape=jax.ShapeDtypeStruct((rows, n_out), F32),
        compiler_params=_params(("arbitrary",)),
        name="modulation",
    )(c_all, w_mod, b_mod)


def _to_streams(ref, val, dil, stage):
    if dil == 1:
        ref[...] = val.astype(ref.dtype).reshape(ref.shape)
        return
    n = val.shape[0] // dil
    for c, half in enumerate(stage):
        half[...] = val[:, c * LANES:(c + 1) * LANES]
        for r in range(dil):
            lo = r * GROUP_WIDTH + c * LANES
            ref[0, :, lo:lo + LANES] = half[pl.ds(r, n, stride=dil), :].astype(ref.dtype)


def _from_streams(ref, dil, stage):
    if dil == 1:
        return ref[...].astype(F32).reshape(ref.shape[-2:])
    n = ref.shape[1]
    for c, half in enumerate(stage):
        for r in range(dil):
            lo = r * GROUP_WIDTH + c * LANES
            half[pl.ds(r, n, stride=dil), :] = ref[0, :, lo:lo + LANES].astype(F32)
    return jnp.concatenate([half[...] for half in stage], axis=1)


def _project_common(hb, win_ref, outs, dils, stages):
    (q_refs, k_refs, v_refs) = outs

    def proj(lo, width):
        return jnp.dot(hb, win_ref[:, lo:lo + width], preferred_element_type=F32)

    zq = proj(0, ATTN_WIDTH)
    for g in range(N_GROUPS):
        _to_streams(q_refs[g], zq[:, g * GROUP_WIDTH:(g + 1) * GROUP_WIDTH], dils[g], stages[g][0])
    zkvs = []
    for g in range(N_GROUPS):
        zkv = proj(COL_KV + 2 * GROUP_WIDTH * g, 2 * GROUP_WIDTH)
        _to_streams(k_refs[g], zkv[:, :GROUP_WIDTH], dils[g], stages[g][1])
        _to_streams(v_refs[g], zkv[:, GROUP_WIDTH:], dils[g], stages[g][2])
        zkvs.append(zkv)
    u = proj(COL_ULIN, CONV_CH) * _sigmoid(proj(COL_UGATE, CONV_CH))
    return zkvs, u, proj


def _k1_kernel(tail_rows, n_tiles,
               x_ref, sh_ref, sc_ref, g_ref, win_ref, dww_ref, dwb_ref, lng_ref, lnb_ref, wco_ref,
               q0, q1, q2, k0, v0, k1, v1, k2, v2, kvt0, kvt1, kvt2, ga_ref, sgb_ref, convp_ref, uext, sbuf, ushift, sga, *stage_refs):
    i = pl.program_id(1)
    tm = x_ref.shape[1]

    @pl.when(i == 0)
    def _():
        uext[0:CONV_HIST, :] = jnp.zeros((CONV_HIST, CONV_CH), F32)

    h = _rms_modulate(x_ref[0], g_ref[...], sc_ref[0], sh_ref[0])
    hb = h.astype(BF16)

    def proj(lo, width):
        return jnp.dot(hb, win_ref[:, lo:lo + width], preferred_element_type=F32)

    uext[CONV_HIST:CONV_HIST + tm, :] = proj(COL_ULIN, CONV_CH) * _sigmoid(proj(COL_UGATE, CONV_CH))
    base = CONV_HIST - (CONV_WIDTH - 1)
    span = tm + CONV_HIST - SUBLANES
    for b in range(1, SUBLANES):
        ushift[b - 1, 0:span, :] = uext[b:b + span, :]

    def conv_chunk(r0):
        accs = [dwb_ref[...]] * (CONV_CHUNK // SUBLANES)
        for k in range(CONV_WIDTH):
            b = (base + k) % SUBLANES
            lo = r0 + base + k - b
            w8 = dww_ref[k]
            for q in range(len(accs)):
                rows = slice(lo + q * SUBLANES, lo + (q + 1) * SUBLANES)
                src = uext[rows, :] if b == 0 else ushift[b - 1, rows, :]
                accs[q] = accs[q] + w8 * src
        s = _conv_tail(jnp.concatenate(accs, axis=0), lng_ref[...], lnb_ref[...])
        sbuf[r0:r0 + CONV_CHUNK, :] = s.astype(BF16)

    dils = tuple(dil for _, dil in DILATED_GROUPS)
    pairs = [stage_refs[j:j + STAGE_HALVES] for j in range(0, len(stage_refs), STAGE_HALVES)]
    stages = [(None,) * 3] + [pairs[3 * (g - 1):3 * g] for g in range(1, N_GROUPS)]
    pieces = []
    for g, q_ref in enumerate((q0, q1, q2)):
        def q_piece(g=g, q_ref=q_ref):
            _to_streams(q_ref, proj(g * GROUP_WIDTH, GROUP_WIDTH), dils[g], stages[g][0])
        pieces.append(q_piece)
    for g, (k_ref, v_ref, kvt) in enumerate(((k0, v0, kvt0), (k1, v1, kvt1), (k2, v2, kvt2))):
        for half, ref in enumerate((k_ref, v_ref)):
            def kv_piece(g=g, half=half, ref=ref, kvt=kvt):
                z = proj(COL_KV + (2 * g + half) * GROUP_WIDTH, GROUP_WIDTH)
                _to_streams(ref, z, dils[g], stages[g][1 + half])
                tr = kvt.shape[2]

                @pl.when(i >= n_tiles - tail_rows[g] // tr)
                def _():
                    kvt[0, half * GROUP_WIDTH:(half + 1) * GROUP_WIDTH, :] = z[tm - tr:, :].T
            pieces.append(kv_piece)
    for c in range(D_MODEL // GROUP_WIDTH):
        cols = slice(c * GROUP_WIDTH, (c + 1) * GROUP_WIDTH)

        def gb_piece(c=c, cols=cols):
            sgb_ref[:, cols] = _sigmoid(proj(COL_GB + c * GROUP_WIDTH, GROUP_WIDTH)).astype(BF16)

        def ga_piece(c=c, cols=cols):
            sga[:, cols] = _sigmoid(proj(COL_GA + c * GROUP_WIDTH, GROUP_WIDTH))
        pieces += [gb_piece, ga_piece]
    n_qkv = 3 * N_GROUPS
    for piece in pieces[:n_qkv]:
        piece()
    for r0 in range(0, tm, CONV_CHUNK):
        conv_chunk(r0)
    for piece in pieces[n_qkv:]:
        piece()
    last = uext[tm:tm + CONV_HIST, :]
    uext[0:CONV_HIST, :] = last
    convp_ref[0] = last
    a = jnp.dot(sbuf[...], wco_ref[...], preferred_element_type=F32)
    ga_ref[...] = (sga[...] * a).astype(BF16)


def _k1_prompt(x, sh1, sc1, norm_g, w_in_b, dw_w, dw_b, ln_g, ln_b, w_co_b):
    B, S, _ = x.shape
    tm = min(TOKEN_TILE, S)
    nt = S // tm
    tail_rows = tuple(min(win, S) for win, _ in DILATED_GROUPS)
    tail_blk = tuple(min(t, tm) for t in tail_rows)

    def tok_spec(width):
        return pl.BlockSpec((tm, width), lambda b, i: (b * nt + i, 0))

    def tail_spec(g):
        first = nt - tail_rows[g] // tail_blk[g]
        return pl.BlockSpec((1, 2 * GROUP_WIDTH, tail_blk[g]), lambda b, i: (b, 0, jnp.maximum(i - first, 0)))

    def stream_shape(dil):
        return jax.ShapeDtypeStruct((B, S // dil, dil * GROUP_WIDTH), BF16)

    def stream_spec(dil):
        return pl.BlockSpec((1, tm // dil, dil * GROUP_WIDTH), lambda b, i: (b, i, 0))

    dils = [dil for _, dil in DILATED_GROUPS]
    qkv_order = [dils[0], dils[1], dils[2]] + [d for d in dils for _ in range(2)]
    mod_spec = pl.BlockSpec((1, 1, D_MODEL), lambda b, i: (b, 0, 0))
    out_shape = ([stream_shape(d) for d in qkv_order]
                 + [jax.ShapeDtypeStruct((B, 2 * GROUP_WIDTH, tail_rows[g]), F32) for g in range(N_GROUPS)]
                 + [jax.ShapeDtypeStruct((B * S, D_MODEL), BF16)] * 2
                 + [jax.ShapeDtypeStruct((B, CONV_HIST, CONV_CH), F32)])
    out_specs = ([stream_spec(d) for d in qkv_order] + [tail_spec(g) for g in range(N_GROUPS)]
                 + [tok_spec(D_MODEL)] * 2 + [pl.BlockSpec((1, CONV_HIST, CONV_CH), lambda b, i: (b, 0, 0))])
    n_stage = 3 * sum(1 for d in dils if d > 1)
    return pl.pallas_call(
        functools.partial(_k1_kernel, tail_rows, nt),
        grid=(B, nt),
        in_specs=[pl.BlockSpec((1, tm, D_MODEL), lambda b, i: (b, i, 0)), mod_spec, mod_spec,
                  _const_spec((1, D_MODEL)), _weight_spec((D_MODEL, IN_COLS)),
                  _const_spec((CONV_WIDTH, SUBLANES, CONV_CH)), _const_spec((SUBLANES, CONV_CH)),
                  _const_spec((1, CONV_CH)), _const_spec((1, CONV_CH)), _weight_spec((CONV_CH, D_MODEL))],
        out_specs=out_specs,
        out_shape=out_shape,
        scratch_shapes=([pltpu.VMEM((CONV_HIST + tm, CONV_CH), F32), pltpu.VMEM((tm, CONV_CH), BF16),
                         pltpu.VMEM((SUBLANES - 1, CONV_HIST + tm - SUBLANES, CONV_CH), F32),
                         pltpu.VMEM((tm, D_MODEL), F32)]
                        + [pltpu.VMEM((tm, LANES), F32)] * (n_stage * STAGE_HALVES)),
        compiler_params=_params(("arbitrary", "arbitrary")),
        name="inproj_prompt",
    )(x, sh1, sc1, norm_g, w_in_b, dw_w, dw_b, ln_g, ln_b, w_co_b)


def _k1s_kernel(n_steps, x_ref, sh_ref, sc_ref, g_ref, win_ref, st_ref, dww_ref, dwb_ref, lng_ref, lnb_ref, wco_ref,
                q0, q1, q2, k0, v0, k1, v1, k2, v2, ga_ref, sgb_ref, conv_ref):
    bd = st_ref.shape[1]
    hist = st_ref.shape[0]
    h = _rms_modulate(x_ref[...], g_ref[...], sc_ref[...], sh_ref[...])
    hb = h.astype(BF16)
    _, u, proj = _project_common(hb, win_ref, ((q0, q1, q2), (k0, k1, k2), (v0, v1, v2)),
                                 (1,) * N_GROUPS, [(None,) * 3] * N_GROUPS)

    def ext(j):
        return st_ref[j] if j < hist else u[(j - hist) * bd:(j - hist + 1) * bd, :]

    outs = []
    for t in range(n_steps):
        acc = jnp.zeros((bd, CONV_CH), F32) + dwb_ref[...]
        for k in range(CONV_WIDTH):
            acc = acc + dww_ref[k:k + 1, :] * ext(t + k + hist - (CONV_WIDTH - 1))
        outs.append(acc)
    for j in range(hist):
        conv_ref[j] = ext(j + n_steps)
    s = _conv_tail(jnp.concatenate(outs, axis=0), lng_ref[...], lnb_ref[...])
    a = jnp.dot(s.astype(BF16), wco_ref[...], preferred_element_type=F32)
    ga_ref[...] = (_sigmoid(proj(COL_GA, D_MODEL)) * a).astype(BF16)
    sgb_ref[...] = _sigmoid(proj(COL_GB, D_MODEL)).astype(BF16)


def _k1_sample(x_tm, sh1, sc1, norm_g, w_in_b, state_tm, dw_w, dw_b, ln_g, ln_b, w_co_b, n_steps):
    T = x_tm.shape[0]
    hist, bd, _ = state_tm.shape
    out_shape = ([jax.ShapeDtypeStruct((T, GROUP_WIDTH), F32)] * 9
                 + [jax.ShapeDtypeStruct((T, D_MODEL), BF16)] * 2
                 + [jax.ShapeDtypeStruct((hist, bd, CONV_CH), F32)])
    out_specs = ([_const_spec((T, GROUP_WIDTH))] * 9 + [_const_spec((T, D_MODEL))] * 2
                 + [_const_spec((hist, bd, CONV_CH))])
    return pl.pallas_call(
        functools.partial(_k1s_kernel, n_steps),
        grid=(1,),
        in_specs=[_const_spec((T, D_MODEL)), _const_spec((T, D_MODEL)), _const_spec((T, D_MODEL)),
                  _const_spec((1, D_MODEL)), _const_spec((D_MODEL, IN_COLS)), _const_spec((hist, bd, CONV_CH)),
                  _const_spec((CONV_WIDTH, CONV_CH)), _const_spec((1, CONV_CH)),
                  _const_spec((1, CONV_CH)), _const_spec((1, CONV_CH)), _const_spec((CONV_CH, D_MODEL))],
        out_specs=out_specs,
        out_shape=out_shape,
        compiler_params=_params(("arbitrary",)),
        name="inproj_sample",
    )(x_tm, sh1, sc1, norm_g, w_in_b, state_tm, dw_w, dw_b, ln_g, ln_b, w_co_b)


def _t5_bucket(dist):
    max_exact = N_BUCKETS // 2
    d_f = jnp.maximum(dist, 1).astype(F32)
    large = max_exact + (jnp.log(d_f / max_exact) / math.log(MAX_DISTANCE / max_exact)
                         * (N_BUCKETS - max_exact)).astype(jnp.int32)
    large = jnp.minimum(large, N_BUCKETS - 1)
    return jnp.where(dist < max_exact, dist, large)


def _bucket_lookup(rel_bias_g, dist):
    onehot = (_t5_bucket(dist)[..., None] == jnp.arange(N_BUCKETS)).astype(F32)
    return jnp.einsum('...b,bh->h...', onehot, rel_bias_g, precision=lax.Precision.HIGHEST)


def _prompt_bias(rel_bias_g, dil, n_keys):
    blk = n_keys
    i = jnp.arange(blk)[:, None]
    j = jnp.arange(2 * blk)[None, :]
    rel = i - j + blk
    valid = (rel >= 0) & (rel <= n_keys)
    bias = _bucket_lookup(rel_bias_g, jnp.clip(rel, 0, n_keys) * dil)
    bias = jnp.where(valid[None], bias, NEG_INF)
    return bias.reshape(HEADS_PER_GROUP * blk, 2 * blk).astype(F32)


def _sample_bias(rel_bias_g, dil, n_keys, lc, n_steps, q_rows):
    t = jnp.arange(q_rows)[:, None]
    pos = jnp.arange(lc)[None, :]
    dist_c = lc + t - pos
    dist_n = t - (jnp.arange(LANES)[None, :] - (LANES - n_steps))

    def table(dist, extra):
        ok = (dist >= 0) & (dist % dil == 0) & (dist // dil <= n_keys) & extra & (t < n_steps)
        b = _bucket_lookup(rel_bias_g, jnp.clip(dist, 0, None))
        b = jnp.where(ok[None], b, NEG_INF)
        return jnp.where((t >= n_steps)[None], 0.0, b).astype(F32)

    lane_ok = jnp.arange(LANES)[None, :] >= LANES - n_steps
    return table(dist_c, True), table(dist_n, lane_ok)


def _k2_kernel(q_ref, k_ref, kh_ref, v_ref, vh_ref, bias_ref, o_ref, lse_ref, kbuf, vbuf):
    i = pl.program_id(2)
    rows = q_ref.shape[1]
    n_streams = q_ref.shape[2] // GROUP_WIDTH
    blk = ATTN_BLOCK
    kbuf[0:blk, :] = kh_ref[0]
    kbuf[blk:blk + rows, :] = k_ref[0]
    vbuf[0:blk, :] = vh_ref[0]
    vbuf[blk:blk + rows, :] = v_ref[0]
    lane_head = lax.broadcasted_iota(jnp.int32, (blk, GROUP_WIDTH), 1) // HEAD_DIM
    col = lax.broadcasted_iota(jnp.int32, (HEADS_PER_GROUP * blk, 2 * blk), 1)
    first_mask = jnp.where((col < blk) & (i == 0), NEG_INF, 0.0).astype(F32)
    bias = bias_ref[...]
    for st in range(n_streams):
        cols = slice(st * GROUP_WIDTH, (st + 1) * GROUP_WIDTH)
        for j in range(rows // blk):
            qb = q_ref[0, j * blk:(j + 1) * blk, cols]
            q4 = jnp.concatenate([jnp.where(lane_head == h, qb, jnp.zeros_like(qb)) for h in range(HEADS_PER_GROUP)],
                                 axis=0)
            kc = kbuf[j * blk:(j + 2) * blk, cols]
            vc = vbuf[j * blk:(j + 2) * blk, cols]
            s = lax.dot_general(q4, kc, (((1,), (1,)), ((), ())), preferred_element_type=F32) + bias
            if j == 0:
                s = s + first_mask
            m = jnp.max(s, axis=-1, keepdims=True)
            p = jnp.exp(s - m)
            l = jnp.sum(p, axis=-1, keepdims=True)
            o4 = jnp.dot(p.astype(BF16), vc, preferred_element_type=F32) * (1.0 / l)
            lse4 = m + jnp.log(l)
            o = jnp.zeros((blk, GROUP_WIDTH), F32)
            lse = jnp.zeros((blk, GROUP_WIDTH), F32)
            for h in range(HEADS_PER_GROUP):
                sel = lane_head == h
                o = jnp.where(sel, o4[h * blk:(h + 1) * blk, :], o)
                lse = jnp.where(sel, lse4[h * blk:(h + 1) * blk, :], lse)
            o_ref[0, j * blk:(j + 1) * blk, cols] = o.astype(o_ref.dtype)
            lse_ref[0, j * blk:(j + 1) * blk, cols] = lse


def _k2_prompt(q, k, v, bias, dil):
    B, L, _ = q.shape
    rows = min(ATTN_ROWS, L)
    per = rows // ATTN_BLOCK
    n_streams = min(dil, max(1, ATTN_ROWS // rows))
    width = n_streams * GROUP_WIDTH
    main = pl.BlockSpec((1, rows, width), lambda b, r, i: (b, i, r))
    halo = pl.BlockSpec((1, ATTN_BLOCK, width), lambda b, r, i: (b, jnp.maximum(i * per - 1, 0), r))
    return pl.pallas_call(
        _k2_kernel,
        grid=(B, dil // n_streams, L // rows),
        in_specs=[main, main, halo, main, halo, _const_spec(bias.shape)],
        out_specs=[main, main],
        out_shape=[jax.ShapeDtypeStruct(q.shape, BF16), jax.ShapeDtypeStruct(q.shape, F32)],
        scratch_shapes=[pltpu.VMEM((ATTN_BLOCK + rows, width), BF16)] * 2,
        compiler_params=_params(("arbitrary", "arbitrary", "arbitrary")),
        name=f"attn_prompt_d{dil}",
    )(q, k, k, v, v, bias)


def _k2s_kernel(n_steps, q_ref, c0, c1, c2, t0, t1, t2, bc0, bc1, bc2, bn0, bn1, bn2,
                nc0, nc1, nc2, o0, o1, o2, l0, l1, l2):
    lane = lax.broadcasted_iota(jnp.int32, (2 * GROUP_WIDTH, LANES), 1)
    keep = LANES - n_steps
    groups = ((c0, t0, bc0, bn0, nc0, o0, l0), (c1, t1, bc1, bn1, nc1, o1, l1), (c2, t2, bc2, bn2, nc2, o2, l2))
    for bi in range(q_ref.shape[0]):
        for g, (c_ref, t_ref, bc_ref, bn_ref, nc_ref, o_ref, l_ref) in enumerate(groups):
            _sample_group(bi, g, lane, keep, q_ref, c_ref, t_ref, bc_ref, bn_ref, nc_ref, o_ref, l_ref)


def _sample_group(bi, g, lane, keep, q_ref, c_ref, t_ref, bc_ref, bn_ref, nc_ref, o_ref, l_ref):
    lc = c_ref.shape[2]
    n_tiles = lc // LANES
    new_rows = t_ref[bi]
    tail = jnp.concatenate([jnp.zeros((LANES - new_rows.shape[0], new_rows.shape[1]), F32), new_rows], axis=0).T
    cur = pltpu.roll(c_ref[bi, :, 0:LANES], keep, 1)
    for c in range(n_tiles):
        nxt = pltpu.roll(c_ref[bi, :, (c + 1) * LANES:(c + 2) * LANES], keep, 1) if c + 1 < n_tiles else tail
        nc_ref[bi, :, c * LANES:(c + 1) * LANES] = jnp.where(lane < keep, cur, nxt)
        cur = nxt
    for h in range(HEADS_PER_GROUP):
        lo = g * GROUP_WIDTH + h * HEAD_DIM
        qh = q_ref[bi, :, lo:lo + HEAD_DIM].astype(BF16)
        kh = c_ref[bi, h * HEAD_DIM:(h + 1) * HEAD_DIM, :].astype(BF16)
        vh = c_ref[bi, GROUP_WIDTH + h * HEAD_DIM:GROUP_WIDTH + (h + 1) * HEAD_DIM, :].astype(BF16)
        kt = tail[h * HEAD_DIM:(h + 1) * HEAD_DIM, :].astype(BF16)
        vt = tail[GROUP_WIDTH + h * HEAD_DIM:GROUP_WIDTH + (h + 1) * HEAD_DIM, :].astype(BF16)
        sc = jnp.dot(qh, kh, preferred_element_type=F32) + bc_ref[h]
        sn = jnp.dot(qh, kt, preferred_element_type=F32) + bn_ref[h]
        m = jnp.maximum(jnp.max(sc, axis=-1, keepdims=True), jnp.max(sn, axis=-1, keepdims=True))
        pc = jnp.exp(sc - m)
        pn = jnp.exp(sn - m)
        l = jnp.sum(pc, axis=-1, keepdims=True) + jnp.sum(pn, axis=-1, keepdims=True)
        nt_dims = (((1,), (1,)), ((), ()))
        o = (lax.dot_general(pc.astype(BF16), vh, nt_dims, preferred_element_type=F32)
             + lax.dot_general(pn.astype(BF16), vt, nt_dims, preferred_element_type=F32)) * (1.0 / l)
        o_ref[bi, :, h * HEAD_DIM:(h + 1) * HEAD_DIM] = o
        l_ref[bi, :, h * HEAD_DIM:(h + 1) * HEAD_DIM] = jnp.broadcast_to(m + jnp.log(l), o.shape)


def _k2_sample(q_b, caches, tails, biases_c, biases_n, n_steps):
    bd, q_rows, _ = q_b.shape
    bb = SAMPLE_BATCH if bd % SAMPLE_BATCH == 0 else 1
    per_b = lambda shape: pl.BlockSpec((bb,) + shape[1:], lambda b: (b,) + (0,) * (len(shape) - 1))
    ins = [q_b] + list(caches) + list(tails) + list(biases_c) + list(biases_n)
    in_specs = ([per_b(q_b.shape)] + [per_b(c.shape) for c in caches] + [per_b(t.shape) for t in tails]
                + [_const_spec(b.shape) for b in biases_c] + [_const_spec(b.shape) for b in biases_n])
    o_shape = jax.ShapeDtypeStruct((bd, q_rows, GROUP_WIDTH), F32)
    out_shape = [jax.ShapeDtypeStruct(c.shape, F32) for c in caches] + [o_shape] * 6
    out_specs = [per_b(c.shape) for c in caches] + [per_b(o_shape.shape)] * 6
    return pl.pallas_call(
        functools.partial(_k2s_kernel, n_steps),
        grid=(bd // bb,),
        in_specs=in_specs,
        out_specs=out_specs,
        out_shape=out_shape,
        compiler_params=_params(("arbitrary",)),
        name="attn_sample",
    )(*ins)


def _k3_kernel(x_ref, o0, o1, o2, l0, l1, l2, ga_ref, sgb_ref, g1_ref, sh2_ref, sc2_ref, gf_ref,
               wao_ref, wo_ref, wr_ref, br_ref, before_ref, cnt_in_ref,
               x2_ref, h2_ref, eid_ref, wts_ref, rank_ref, cnt_ref, carry, *stage_refs):
    @pl.when(pl.program_id(0) == 0)
    def _():
        carry[...] = cnt_in_ref[...]

    dils = tuple(x_ref.shape[0] // r.shape[-2] for r in (o0, o1, o2))
    pairs = [stage_refs[j:j + STAGE_HALVES] for j in range(0, len(stage_refs), STAGE_HALVES)]
    os_ = [_from_streams(r, d, pairs[2 * g]) for g, (r, d) in enumerate(zip((o0, o1, o2), dils))]
    ls = [_from_streams(r, d, pairs[2 * g + 1]) for g, (r, d) in enumerate(zip((l0, l1, l2), dils))]
    m = jnp.maximum(jnp.maximum(ls[0], ls[1]), ls[2])
    ws = [jnp.exp(l - m) for l in ls]
    den = ws[0] + ws[1] + ws[2]
    o = (ws[0] * os_[0] + ws[1] * os_[1] + ws[2] * os_[2]) / den
    b = jnp.dot(o.astype(BF16), wao_ref[...], preferred_element_type=F32)
    mixed = ga_ref[...].astype(F32) + sgb_ref[...].astype(F32) * b
    x2 = x_ref[...] + g1_ref[0] * jnp.dot(mixed.astype(BF16), wo_ref[...], preferred_element_type=F32)
    x2_ref[...] = x2
    h2 = _rms_modulate(x2, gf_ref[...], sc2_ref[0], sh2_ref[0])
    h2_ref[...] = h2
    lt = lax.dot_general(wr_ref[...], h2, (((1,), (1,)), ((), ())), preferred_element_type=F32,
                         precision=lax.Precision.HIGHEST) + br_ref[...]
    tm = h2.shape[0]
    gl = lt[0:8, :]
    gmax = jnp.max(gl, axis=0, keepdims=True)
    r8 = lax.broadcasted_iota(jnp.int32, (8, tm), 0)
    grp = jnp.min(jnp.where(gl == gmax, r8, 8), axis=0, keepdims=True)
    p_grp = 1.0 / jnp.sum(jnp.exp(gl - gmax), axis=0, keepdims=True)
    es = jnp.zeros((EXPERTS_PER_GROUP, tm), F32)
    for g in range(N_EXPERT_GROUPS):
        es = jnp.where(grp == g, lt[8 + 8 * g:16 + 8 * g, :], es)
    v1 = jnp.max(es, axis=0, keepdims=True)
    i1 = jnp.min(jnp.where(es == v1, r8, 8), axis=0, keepdims=True)
    rest = jnp.where(r8 == i1, -jnp.inf, es)
    v2 = jnp.max(rest, axis=0, keepdims=True)
    i2 = jnp.min(jnp.where(rest == v2, r8, 8), axis=0, keepdims=True)
    e21 = jnp.exp(v2 - v1)
    w1 = p_grp / (1.0 + e21)
    e1 = grp * EXPERTS_PER_GROUP + i1
    e2 = grp * EXPERTS_PER_GROUP + i2
    eid_ref[0, 0:1, :] = e1
    eid_ref[0, 1:2, :] = e2
    wts_ref[0, 0:1, :] = w1
    wts_ref[0, 1:2, :] = w1 * e21
    r_e = lax.broadcasted_iota(jnp.int32, (N_EXPERTS, tm), 0)
    hit1 = r_e == e1
    hit2 = r_e == e2
    both = jnp.where(hit1 | hit2, 1.0, 0.0)
    base = carry[...] + jnp.dot(both.astype(BF16), before_ref[...], preferred_element_type=F32)
    rank_ref[0, 0:1, :] = jnp.sum(jnp.where(hit1, base, 0.0), axis=0, keepdims=True).astype(jnp.int32)
    rank_ref[0, 1:2, :] = jnp.sum(jnp.where(hit2, base, 0.0), axis=0, keepdims=True).astype(jnp.int32)
    total = carry[...] + jnp.sum(both, axis=1, keepdims=True)
    carry[...] = total
    cnt_ref[...] = total


def _k3(x, o_l, g1, sh2, sc2, ga, sgb, norm_ffn_g, w_ao_b, w_o_b, w_r, b_r, cnt_in, tiles_per_mod):
    T = x.shape[0]
    tm = min(TOKEN_TILE, T)
    nt = T // tm
    tok = lambda w: pl.BlockSpec((tm, w), lambda t: (t, 0))
    mod = pl.BlockSpec((1,) + g1.shape[1:], lambda t: (t // tiles_per_mod, 0, 0))
    small = pl.BlockSpec((1, 2, tm), lambda t: (t, 0, 0))
    small_i = jax.ShapeDtypeStruct((nt, 2, tm), jnp.int32)
    before = (jnp.arange(tm)[:, None] < jnp.arange(tm)[None, :]).astype(BF16)

    def attn_spec(a):
        if a.ndim == 2:
            return tok(GROUP_WIDTH)
        dil = a.shape[2] // GROUP_WIDTH
        return pl.BlockSpec((1, tm // dil, a.shape[2]), lambda t: (t // tiles_per_mod, t % tiles_per_mod, 0))

    attn_in = [o_l[0][0], o_l[1][0], o_l[2][0], o_l[0][1], o_l[1][1], o_l[2][1]]
    return pl.pallas_call(
        _k3_kernel,
        grid=(nt,),
        in_specs=[tok(D_MODEL)] + [attn_spec(a) for a in attn_in] + [tok(D_MODEL)] * 2 + [mod] * 3
                 + [_const_spec((1, D_MODEL)), _weight_spec(w_ao_b.shape), _weight_spec(w_o_b.shape),
                    _const_spec(w_r.shape), _const_spec(b_r.shape), _weight_spec((tm, tm)),
                    _const_spec((N_EXPERTS, 1))],
        out_specs=[tok(D_MODEL), tok(D_MODEL), small, small, small, _const_spec((N_EXPERTS, 1))],
        out_shape=[jax.ShapeDtypeStruct((T, D_MODEL), F32), jax.ShapeDtypeStruct((T, D_MODEL), F32),
                   small_i, jax.ShapeDtypeStruct((nt, 2, tm), F32), small_i,
                   jax.ShapeDtypeStruct((N_EXPERTS, 1), F32)],
        scratch_shapes=([pltpu.VMEM((N_EXPERTS, 1), F32)]
                        + [pltpu.VMEM((tm, LANES), F32)] * (2 * N_GROUPS * STAGE_HALVES)),
        compiler_params=_params(("arbitrary",)),
        name="merge_router",
    )(x, *attn_in, ga, sgb, g1, sh2, sc2, norm_ffn_g, w_ao_b, w_o_b, w_r, b_r, before, cnt_in)


def _rows_wait(n_rows, hbm, vmem, sem):
    pltpu.make_async_copy(hbm.at[pl.ds(0, n_rows)], vmem, sem).wait()


def _dispatch_kernel(n_tiles, n_first, dest_ref, fill_ref, ha_ref, hb_ref, xs_ref, buf, sem, zbuf, zsem):
    i = pl.program_id(0)
    slot = i % 2 if n_tiles > 1 else 0
    tm = ha_ref.shape[0]

    def drain(s):
        for _ in range(2):
            _rows_wait(tm, xs_ref, buf.at[s], sem.at[s])

    @pl.when(i == 0)
    def _():
        zbuf[...] = jnp.zeros(zbuf.shape, F32)

        def fill(row):
            return pltpu.make_async_copy(zbuf, xs_ref.at[pl.ds(pl.multiple_of(row, MOE_ROWS), MOE_ROWS)], zsem)

        n_blocks = xs_ref.shape[0] // MOE_ROWS
        n_used = fill_ref[N_EXPERTS]

        def start_block(j, carry):
            fill(j * MOE_ROWS).start()
            return carry

        def wait_block(j, carry):
            fill(j * MOE_ROWS).wait()
            return carry

        for e in range(N_EXPERTS):
            @pl.when(fill_ref[e] >= 0)
            def _(e=e):
                fill(fill_ref[e]).start()
        lax.fori_loop(n_used, n_blocks, start_block, 0)
        for e in range(N_EXPERTS):
            @pl.when(fill_ref[e] >= 0)
            def _(e=e):
                fill(fill_ref[e]).wait()
        lax.fori_loop(n_used, n_blocks, wait_block, 0)

    if n_tiles > 2:
        @pl.when(i >= 2)
        def _():
            drain(slot)

    @pl.when(i < n_first)
    def _():
        buf[slot] = ha_ref[...]

    @pl.when(i >= n_first)
    def _():
        buf[slot] = hb_ref[...]

    for r in range(tm):
        for k in range(2):
            pltpu.make_async_copy(buf.at[slot, pl.ds(r, 1)], xs_ref.at[pl.ds(dest_ref[0, k, r], 1)], sem.at[slot]).start()

    if n_tiles == 1:
        drain(0)
    else:
        @pl.when(i == n_tiles - 1)
        def _():
            drain(slot)
            drain(1 - slot)


def _dispatch(dest, fill_rows, h_a, h_b, n_slots):
    nt, _, tm = dest.shape
    n_first = h_a.shape[0] // tm
    assert h_a.shape[0] % tm == 0 and h_b.shape[0] == (nt - n_first) * tm
    return pl.pallas_call(
        functools.partial(_dispatch_kernel, nt, n_first),
        grid=(nt,),
        in_specs=[pl.BlockSpec((1, 2, tm), lambda t: (t, 0, 0), memory_space=pltpu.SMEM),
                  pl.BlockSpec(memory_space=pltpu.SMEM),
                  pl.BlockSpec((tm, D_MODEL), lambda t: (jnp.minimum(t, n_first - 1), 0)),
                  pl.BlockSpec((tm, D_MODEL), lambda t: (jnp.maximum(t - n_first, 0), 0))],
        out_specs=pl.BlockSpec(memory_space=pl.ANY),
        out_shape=jax.ShapeDtypeStruct((n_slots, D_MODEL), F32),
        scratch_shapes=[pltpu.VMEM((2, tm, D_MODEL), F32), pltpu.SemaphoreType.DMA((2,)),
                        pltpu.VMEM((MOE_ROWS, D_MODEL), F32), pltpu.SemaphoreType.DMA(())],
        compiler_params=_params(("arbitrary",)),
        name="dispatch_rows",
    )(dest, fill_rows, h_a, h_b)


def _k4_kernel(be_ref, nused_ref, x_ref, wg_ref, wu_ref, wd_ref, y_ref, wg_b, wu_b, wd_b):
    i = pl.program_id(0)

    @pl.when((i == 0) | (be_ref[i] != be_ref[jnp.maximum(i - 1, 0)]))
    def _():
        wg_b[...] = wg_ref[0].astype(BF16)
        wu_b[...] = wu_ref[0].astype(BF16)
        wd_b[...] = wd_ref[0].astype(BF16)

    @pl.when(i < nused_ref[0])
    def _():
        xb = x_ref[...].astype(BF16)
        gate = jnp.dot(xb, wg_b[...], preferred_element_type=F32)
        up = jnp.dot(xb, wu_b[...], preferred_element_type=F32)
        mid = (_silu(gate) * up).astype(BF16)
        y_ref[...] = jnp.dot(mid, wd_b[...], preferred_element_type=F32)

    @pl.when(i >= nused_ref[0])
    def _():
        y_ref[...] = jnp.zeros(y_ref.shape, F32)


def _k4(block_expert, n_used, xs, w_gate, w_up, w_down):
    n_blocks = block_expert.shape[0]
    rows = MOE_ROWS
    weight = lambda shape: pl.BlockSpec((1,) + shape, lambda i, be, nu: (be[i], 0, 0))
    grid_spec = pltpu.PrefetchScalarGridSpec(
        num_scalar_prefetch=2,
        grid=(n_blocks,),
        in_specs=[pl.BlockSpec((rows, D_MODEL), lambda i, be, nu: (jnp.minimum(i, nu[0] - 1), 0)),
                  weight((D_MODEL, D_FF_EXPERT)), weight((D_MODEL, D_FF_EXPERT)), weight((D_FF_EXPERT, D_MODEL))],
        out_specs=pl.BlockSpec((rows, D_MODEL), lambda i, be, nu: (i, 0)),
        scratch_shapes=[pltpu.VMEM((D_MODEL, D_FF_EXPERT), BF16), pltpu.VMEM((D_MODEL, D_FF_EXPERT), BF16),
                        pltpu.VMEM((D_FF_EXPERT, D_MODEL), BF16)],
    )
    return pl.pallas_call(
        _k4_kernel,
        grid_spec=grid_spec,
        out_shape=jax.ShapeDtypeStruct((n_blocks * rows, D_MODEL), F32),
        compiler_params=_params(("arbitrary",)),
        name="expert_blocks",
    )(block_expert, n_used, xs, w_gate, w_up, w_down)


def _k5_kernel(n_tiles, idx_ref, x2_ref, w_ref, g2_ref, gfin_ref, ys_hbm, y_ref, ybuf, sem):
    i = pl.program_id(0)
    tm = x2_ref.shape[0]

    @pl.when(i < n_tiles)
    def _():
        slot = i % 2
        for r in range(tm):
            for k in range(2):
                pltpu.make_async_copy(ys_hbm.at[pl.ds(idx_ref[0, k, r], 1)], ybuf.at[slot, k, pl.ds(r, 1)],
                                      sem.at[slot]).start()

    @pl.when(i >= 1)
    def _():
        slot = (i - 1) % 2
        for k in range(2):
            _rows_wait(tm, ys_hbm, ybuf.at[slot, k], sem.at[slot])
        w = w_ref[...]
        f = ybuf[slot, 0] * w[:, 0:1] + ybuf[slot, 1] * w[:, 1:2]
        y = x2_ref[...] + g2_ref[0] * f
        r = lax.rsqrt(jnp.mean(y * y, axis=-1, keepdims=True) + EPS)
        y_ref[...] = (y * r) * gfin_ref[...]


def _k5(dest, x2, wts, g2, norm_final_g, ys, tiles_per_mod):
    nt, _, tm = dest.shape
    T = x2.shape[0]
    prev = lambda t: jnp.maximum(t - 1, 0)
    return pl.pallas_call(
        functools.partial(_k5_kernel, nt),
        grid=(nt + 1,),
        in_specs=[pl.BlockSpec((1, 2, tm), lambda t: (jnp.minimum(t, nt - 1), 0, 0), memory_space=pltpu.SMEM),
                  pl.BlockSpec((tm, D_MODEL), lambda t: (prev(t), 0)),
                  pl.BlockSpec((tm, 2), lambda t: (prev(t), 0)),
                  pl.BlockSpec((1,) + g2.shape[1:], lambda t: (prev(t) // tiles_per_mod, 0, 0)),
                  _const_spec((1, D_MODEL)), pl.BlockSpec(memory_space=pl.ANY)],
        out_specs=pl.BlockSpec((tm, D_MODEL), lambda t: (prev(t), 0)),
        out_shape=jax.ShapeDtypeStruct((T, D_MODEL), F32),
        scratch_shapes=[pltpu.VMEM((2, 2, tm, D_MODEL), F32), pltpu.SemaphoreType.DMA((2,))],
        compiler_params=_params(("arbitrary",)),
        name="combine_norm",
    )(dest, x2, wts, g2, norm_final_g, ys)


def _slot_tables(counts, n_blocks):
    padded = (counts + MOE_ROWS - 1) // MOE_ROWS * MOE_ROWS
    pend = jnp.cumsum(padded)
    pstart = pend - padded
    block_lo = jnp.arange(n_blocks, dtype=jnp.int32) * MOE_ROWS
    block_expert = jnp.minimum(jnp.sum(pend[None, :] <= block_lo[:, None], axis=1), N_EXPERTS - 1).astype(jnp.int32)
    n_used = (pend[-1] // MOE_ROWS).astype(jnp.int32).reshape(1)
    fill_rows = jnp.concatenate([jnp.where(padded > 0, pend - MOE_ROWS, -1).astype(jnp.int32), n_used])
    return pstart.astype(jnp.int32), block_expert, n_used, fill_rows


def _slots(eid, rank, pstart):
    sel = eid[..., None] == jnp.arange(N_EXPERTS, dtype=jnp.int32)
    return jnp.sum(jnp.where(sel, pstart, 0), axis=-1).astype(jnp.int32) + rank


def _cache_view(cache):
    bd, lc = cache.shape[:2]
    return cache.transpose(0, 2, 3, 4, 1).reshape(bd, 2 * GROUP_WIDTH, lc)


def _cache_unview(view):
    bd, _, lc = view.shape
    return view.reshape(bd, 2, HEADS_PER_GROUP, HEAD_DIM, lc).transpose(0, 4, 1, 2, 3)


def kernel(x_prompt, x_sample, c_prompt, c_sample, cache_kv_w128, cache_kv_w512, cache_kv_w2048, state_conv, rel_bias, norm_mix_g, norm_ffn_g, w_mod, b_mod, w_in, dw_w, dw_b, ln_conv_g, ln_conv_b, w_conv_out, w_attn_out, w_out, w_router_group, b_router_group, w_router_expert, b_router_expert, w_exp_gate, w_exp_up, w_exp_down, norm_final_g):
    assert norm_mix_g.shape[0] == 1, "single layer"
    B, S, D = x_prompt.shape
    Bd, Td, _ = x_sample.shape
    caches_in = (cache_kv_w128[0], cache_kv_w512[0], cache_kv_w2048[0])
    for (win, dil), c in zip(DILATED_GROUPS, caches_in):
        assert c.shape[1] >= (win // dil) * dil and c.shape[1] % LANES == 0 and S % (dil * ATTN_BLOCK) == 0

    wi = w_in[0]
    q_cols = wi[:, :ATTN_WIDTH] * (HEAD_DIM ** -0.5)
    kv_cols = []
    for g in range(N_GROUPS):
        kv_cols += [wi[:, ATTN_WIDTH + g * GROUP_WIDTH:ATTN_WIDTH + (g + 1) * GROUP_WIDTH],
                    wi[:, 2 * ATTN_WIDTH + g * GROUP_WIDTH:2 * ATTN_WIDTH + (g + 1) * GROUP_WIDTH]]
    w_in_b = jnp.concatenate([q_cols] + kv_cols + [wi[:, 3 * ATTN_WIDTH:]], axis=1).astype(BF16)
    w_co_b = w_conv_out[0].astype(BF16)
    w_ao_b = w_attn_out[0].astype(BF16)
    w_o_b = w_out[0].astype(BF16)
    w_r = jnp.zeros((ROUTER_ROWS, D), F32)
    w_r = w_r.at[0:N_EXPERT_GROUPS].set(w_router_group[0].T)
    w_r = w_r.at[8:].set(w_router_expert[0].reshape(D, N_EXPERTS).T)
    b_r = jnp.full((ROUTER_ROWS, 1), NEG_INF, F32)
    b_r = b_r.at[0:N_EXPERT_GROUPS, 0].set(b_router_group[0])
    b_r = b_r.at[8:, 0].set(b_router_expert[0].reshape(N_EXPERTS))
    row = lambda v: v.reshape(1, -1)

    n_seq = B + Bd
    n_seq_pad = -(-n_seq // 16) * 16
    c_all = jnp.concatenate([c_prompt, c_sample, jnp.zeros((n_seq_pad - n_seq, D), F32)], axis=0)
    mod = _modulation(c_all, w_mod[0], row(b_mod[0]))
    mod_p = [mod[:B, j * D:(j + 1) * D].reshape(B, 1, D) for j in range(6)]
    mod_s = [jnp.tile(mod[B:B + Bd, j * D:(j + 1) * D], (Td, 1)) for j in range(6)]

    dw_w_rows = jnp.broadcast_to(dw_w[0][:, None, :], (CONV_WIDTH, SUBLANES, CONV_CH))
    dw_b_rows = jnp.broadcast_to(dw_b[0][None, :], (SUBLANES, CONV_CH))
    (q0, q1, q2, k0, v0, k1, v1, k2, v2, kvt0, kvt1, kvt2, ga, sgb, convp) = _k1_prompt(
        x_prompt, mod_p[0], mod_p[1], row(norm_mix_g[0]), w_in_b, dw_w_rows, dw_b_rows,
        row(ln_conv_g[0]), row(ln_conv_b[0]), w_co_b)
    o_l = []
    for g, ((win, dil), qg, kg, vg) in enumerate(zip(DILATED_GROUPS, (q0, q1, q2), (k0, k1, k2), (v0, v1, v2))):
        bias = _prompt_bias(rel_bias[:, g * HEADS_PER_GROUP:(g + 1) * HEADS_PER_GROUP], dil, win // dil)
        o_l.append(_k2_prompt(qg, kg, vg, bias, dil))
    xp = x_prompt.reshape(B * S, D)
    tm3 = min(TOKEN_TILE, B * S)
    x2, h2, eid, wts, rank, cnt_p = _k3(xp, o_l, mod_p[2], mod_p[3], mod_p[4], ga, sgb, row(norm_ffn_g[0]),
                                        w_ao_b, w_o_b, w_r, b_r, jnp.zeros((N_EXPERTS, 1), F32), S // tm3)
    kv_p = [_cache_unview(kvt)[None] for kvt in (kvt0, kvt1, kvt2)]
    conv_p = convp[:, CONV_HIST - (CONV_WIDTH - 1):, :][None]

    Ts = Td * Bd
    xs = x_sample.transpose(1, 0, 2).reshape(Ts, D)
    state_tm = state_conv[0].transpose(1, 0, 2)
    (sq0, sq1, sq2, sk0, sv0, sk1, sv1, sk2, sv2, ga_s, sgb_s, conv_tm) = _k1_sample(
        xs, mod_s[0], mod_s[1], row(norm_mix_g[0]), w_in_b, state_tm, dw_w[0], row(dw_b[0]),
        row(ln_conv_g[0]), row(ln_conv_b[0]), w_co_b, Td)
    q_rows = PACKED_ROWS
    assert Td <= SUBLANES
    q_b = jnp.concatenate([sq0, sq1, sq2], axis=1).reshape(Td, Bd, ATTN_WIDTH).transpose(1, 0, 2)
    q_b = jnp.pad(q_b, ((0, 0), (0, q_rows - Td), (0, 0)))
    views, tails, biases_c, biases_n = [], [], [], []
    for g, ((win, dil), c, sk, sv) in enumerate(zip(DILATED_GROUPS, caches_in, (sk0, sk1, sk2), (sv0, sv1, sv2))):
        lc = c.shape[1]
        views.append(_cache_view(c))
        new_kv = jnp.concatenate([sk, sv], axis=1).reshape(Td, Bd, 2 * GROUP_WIDTH).transpose(1, 0, 2)
        tails.append(jnp.pad(new_kv, ((0, 0), (SUBLANES - Td, 0), (0, 0))))
        bc, bn = _sample_bias(rel_bias[:, g * HEADS_PER_GROUP:(g + 1) * HEADS_PER_GROUP], dil, win // dil, lc, Td, q_rows)
        biases_c.append(bc)
        biases_n.append(bn)
    (nc0, nc1, nc2, so0, so1, so2, sl0, sl1, sl2) = _k2_sample(q_b, views, tails, biases_c, biases_n, Td)
    to_tm = lambda a: a[:, :Td, :].transpose(1, 0, 2).reshape(Ts, GROUP_WIDTH)
    o_l_s = [(to_tm(so0), to_tm(sl0)), (to_tm(so1), to_tm(sl1)), (to_tm(so2), to_tm(sl2))]
    tm3s = min(TOKEN_TILE, Ts)
    tiled = lambda m: m.reshape(Ts // tm3s, tm3s, D)
    x2s, h2s, eid_s, wts_s, rank_s, cnt_all = _k3(xs, o_l_s, tiled(mod_s[2]), tiled(mod_s[3]), tiled(mod_s[4]),
                                                  ga_s, sgb_s, row(norm_ffn_g[0]), w_ao_b, w_o_b, w_r, b_r, cnt_p, 1)

    n_assign = 2 * (B * S + Ts)
    n_blocks = -(-n_assign // MOE_ROWS) + N_EXPERTS
    pstart, block_expert, n_used, fill_rows = _slot_tables(cnt_all[:, 0].astype(jnp.int32), n_blocks)
    dest_p = _slots(eid, rank, pstart)
    dest_s = _slots(eid_s, rank_s, pstart)
    n_slots = n_blocks * MOE_ROWS
    assert tm3s == tm3, "prompt and sample token tiles must match to share the dispatch"
    slots = _dispatch(jnp.concatenate([dest_p, dest_s], axis=0), fill_rows, h2, h2s, n_slots)
    ys = _k4(block_expert, n_used, slots, w_exp_gate[0], w_exp_up[0], w_exp_down[0])
    per_token = lambda a: a.transpose(0, 2, 1).reshape(-1, 2)
    y_prompt = _k5(dest_p, x2, per_token(wts), mod_p[5], row(norm_final_g), ys, S // tm3).reshape(B, S, D)
    y_s = _k5(dest_s, x2s, per_token(wts_s), tiled(mod_s[5]), row(norm_final_g), ys, 1)
    y_sample = y_s.reshape(Td, Bd, D).transpose(1, 0, 2)
    kv_s = [_cache_unview(nc)[None] for nc in (nc0, nc1, nc2)]
    conv_s = conv_tm.transpose(1, 0, 2)[None]

    return (y_prompt, y_sample, kv_p[0], kv_p[1], kv_p[2], conv_p,
            kv_s[0], kv_s[1], kv_s[2], conv_s)
```

```python
import functools
import math

import jax
import jax.numpy as jnp
from jax import lax
from jax.experimental import pallas as pl
from jax.experimental.pallas import tpu as pltpu

F32 = jnp.float32
BF16 = jnp.bfloat16

D_MODEL = 1024
HEAD_DIM = 64
HEADS_PER_GROUP = 4
GROUP_WIDTH = HEADS_PER_GROUP * HEAD_DIM
LANES = 128
SUBLANES = 8
CONV_CHUNK = 32
STAGE_HALVES = GROUP_WIDTH // LANES
DILATED_GROUPS = ((128, 1), (512, 4), (2048, 16))
N_GROUPS = len(DILATED_GROUPS)
ATTN_WIDTH = N_GROUPS * GROUP_WIDTH
CONV_CH = D_MODEL // 2
CONV_WIDTH = 31
CONV_HIST = 32
N_BUCKETS = 32
MAX_DISTANCE = 2048
N_EXPERT_GROUPS = 4
EXPERTS_PER_GROUP = 8
N_EXPERTS = N_EXPERT_GROUPS * EXPERTS_PER_GROUP
D_FF_EXPERT = D_MODEL // 2
EPS = 1e-6
NEG_INF = -1e30

COL_KV = ATTN_WIDTH
COL_ULIN = COL_KV + 2 * ATTN_WIDTH
COL_UGATE = COL_ULIN + CONV_CH
COL_GA = COL_UGATE + CONV_CH
COL_GB = COL_GA + D_MODEL
IN_COLS = COL_GB + D_MODEL

ROUTER_ROWS = 8 + N_EXPERTS

V7X_VMEM_LIMIT = 56 * 1024 * 1024
TOKEN_TILE = 512
ATTN_ROWS = 2048
ATTN_BLOCK = 128
MOE_ROWS = 512
SAMPLE_BATCH = 2
PACKED_ROWS = 16


def _sigmoid(x):
    return 1.0 / (1.0 + jnp.exp(-x))


def _silu(x):
    return x * _sigmoid(x)


def _rms_modulate(x, g, sc, sh):
    r = lax.rsqrt(jnp.mean(x * x, axis=-1, keepdims=True) + EPS)
    return ((x * r) * g) * (1.0 + sc) + sh


def _conv_tail(y, g, b):
    mu = jnp.mean(y, axis=-1, keepdims=True)
    yc = y - mu
    var = jnp.mean(yc * yc, axis=-1, keepdims=True)
    return _silu((yc * lax.rsqrt(var + EPS)) * g + b)


def _params(semantics):
    return pltpu.CompilerParams(dimension_semantics=semantics, vmem_limit_bytes=V7X_VMEM_LIMIT)


def _const_spec(shape):
    nd = len(shape)
    return pl.BlockSpec(shape, lambda *_: (0,) * nd)


def _weight_spec(shape):
    nd = len(shape)
    return pl.BlockSpec(shape, lambda *_: (0,) * nd, pipeline_mode=pl.Buffered(1))


def _mod_kernel(c_ref, w_ref, b_ref, o_ref):
    s = _silu(c_ref[...]).astype(BF16)
    o_ref[...] = jnp.dot(s, w_ref[...].astype(BF16), preferred_element_type=F32) + b_ref[...]


def _modulation(c_all, w_mod, b_mod):
    rows = c_all.shape[0]
    n_out = w_mod.shape[1]
    chunk = D_MODEL
    return pl.pallas_call(
        _mod_kernel,
        grid=(n_out // chunk,),
        in_specs=[_const_spec((rows, D_MODEL)),
                  pl.BlockSpec((D_MODEL, chunk), lambda j: (0, j)),
                  pl.BlockSpec((1, chunk), lambda j: (0, j))],
        out_specs=pl.BlockSpec((rows, chunk), lambda j: (0, j)),
        out_shape=jax.ShapeDtypeStruct((rows, n_out), F32),
        compiler_params=_params(("arbitrary",)),
        name="modulation",
    )(c_all, w_mod, b_mod)


def _to_streams(ref, val, dil, stage):
    if dil == 1:
        ref[...] = val.astype(ref.dtype).reshape(ref.shape)
        return
    n = val.shape[0] // dil
    for c, half in enumerate(stage):
        half[...] = val[:, c * LANES:(c + 1) * LANES]
        for r in range(dil):
            lo = r * GROUP_WIDTH + c * LANES
            ref[0, :, lo:lo + LANES] = half[pl.ds(r, n, stride=dil), :].astype(ref.dtype)


def _from_streams(ref, dil, stage):
    if dil == 1:
        return ref[...].astype(F32).reshape(ref.shape[-2:])
    n = ref.shape[1]
    for c, half in enumerate(stage):
        for r in range(dil):
            lo = r * GROUP_WIDTH + c * LANES
            half[pl.ds(r, n, stride=dil), :] = ref[0, :, lo:lo + LANES].astype(F32)
    return jnp.concatenate([half[...] for half in stage], axis=1)


def _project_common(hb, win_ref, outs, dils, stages):
    (q_refs, k_refs, v_refs) = outs

    def proj(lo, width):
        return jnp.dot(hb, win_ref[:, lo:lo + width], preferred_element_type=F32)

    zq = proj(0, ATTN_WIDTH)
    for g in range(N_GROUPS):
        _to_streams(q_refs[g], zq[:, g * GROUP_WIDTH:(g + 1) * GROUP_WIDTH], dils[g], stages[g][0])
    zkvs = []
    for g in range(N_GROUPS):
        zkv = proj(COL_KV + 2 * GROUP_WIDTH * g, 2 * GROUP_WIDTH)
        _to_streams(k_refs[g], zkv[:, :GROUP_WIDTH], dils[g], stages[g][1])
        _to_streams(v_refs[g], zkv[:, GROUP_WIDTH:], dils[g], stages[g][2])
        zkvs.append(zkv)
    u = proj(COL_ULIN, CONV_CH) * _sigmoid(proj(COL_UGATE, CONV_CH))
    return zkvs, u, proj


def _k1_kernel(tail_rows, n_tiles,
               x_ref, sh_ref, sc_ref, g_ref, win_ref, dww_ref, dwb_ref, lng_ref, lnb_ref, wco_ref,
               q0, q1, q2, k0, v0, k1, v1, k2, v2, kvt0, kvt1, kvt2, ga_ref, sgb_ref, convp_ref, uext, sbuf, ushift, sga, *stage_refs):
    i = pl.program_id(1)
    tm = x_ref.shape[1]

    @pl.when(i == 0)
    def _():
        uext[0:CONV_HIST, :] = jnp.zeros((CONV_HIST, CONV_CH), F32)

    h = _rms_modulate(x_ref[0], g_ref[...], sc_ref[0], sh_ref[0])
    hb = h.astype(BF16)

    def proj(lo, width):
        return jnp.dot(hb, win_ref[:, lo:lo + width], preferred_element_type=F32)

    uext[CONV_HIST:CONV_HIST + tm, :] = proj(COL_ULIN, CONV_CH) * _sigmoid(proj(COL_UGATE, CONV_CH))
    base = CONV_HIST - (CONV_WIDTH - 1)
    span = tm + CONV_HIST - SUBLANES
    for b in range(1, SUBLANES):
        ushift[b - 1, 0:span, :] = uext[b:b + span, :]

    def conv_chunk(r0):
        accs = [dwb_ref[...]] * (CONV_CHUNK // SUBLANES)
        for k in range(CONV_WIDTH):
            b = (base + k) % SUBLANES
            lo = r0 + base + k - b
            w8 = dww_ref[k]
            for q in range(len(accs)):
                rows = slice(lo + q * SUBLANES, lo + (q + 1) * SUBLANES)
                src = uext[rows, :] if b == 0 else ushift[b - 1, rows, :]
                accs[q] = accs[q] + w8 * src
        s = _conv_tail(jnp.concatenate(accs, axis=0), lng_ref[...], lnb_ref[...])
        sbuf[r0:r0 + CONV_CHUNK, :] = s.astype(BF16)

    dils = tuple(dil for _, dil in DILATED_GROUPS)
    pairs = [stage_refs[j:j + STAGE_HALVES] for j in range(0, len(stage_refs), STAGE_HALVES)]
    stages = [(None,) * 3] + [pairs[3 * (g - 1):3 * g] for g in range(1, N_GROUPS)]
    pieces = []
    for g, q_ref in enumerate((q0, q1, q2)):
        def q_piece(g=g, q_ref=q_ref):
            _to_streams(q_ref, proj(g * GROUP_WIDTH, GROUP_WIDTH), dils[g], stages[g][0])
        pieces.append(q_piece)
    for g, (k_ref, v_ref, kvt) in enumerate(((k0, v0, kvt0), (k1, v1, kvt1), (k2, v2, kvt2))):
        for half, ref in enumerate((k_ref, v_ref)):
            def kv_piece(g=g, half=half, ref=ref, kvt=kvt):
                z = proj(COL_KV + (2 * g + half) * GROUP_WIDTH, GROUP_WIDTH)
                _to_streams(ref, z, dils[g], stages[g][1 + half])
                tr = kvt.shape[2]

                @pl.when(i >= n_tiles - tail_rows[g] // tr)
                def _():
                    kvt[0, half * GROUP_WIDTH:(half + 1) * GROUP_WIDTH, :] = z[tm - tr:, :].T
            pieces.append(kv_piece)
    for c in range(D_MODEL // GROUP_WIDTH):
        cols = slice(c * GROUP_WIDTH, (c + 1) * GROUP_WIDTH)

        def gb_piece(c=c, cols=cols):
            sgb_ref[:, cols] = _sigmoid(proj(COL_GB + c * GROUP_WIDTH, GROUP_WIDTH)).astype(BF16)

        def ga_piece(c=c, cols=cols):
            sga[:, cols] = _sigmoid(proj(COL_GA + c * GROUP_WIDTH, GROUP_WIDTH))
        pieces += [gb_piece, ga_piece]
    n_qkv = 3 * N_GROUPS
    for piece in pieces[:n_qkv]:
        piece()
    for r0 in range(0, tm, CONV_CHUNK):
        conv_chunk(r0)
    for piece in pieces[n_qkv:]:
        piece()
    last = uext[tm:tm + CONV_HIST, :]
    uext[0:CONV_HIST, :] = last
    convp_ref[0] = last
    a = jnp.dot(sbuf[...], wco_ref[...], preferred_element_type=F32)
    ga_ref[...] = (sga[...] * a).astype(BF16)


def _k1_prompt(x, sh1, sc1, norm_g, w_in_b, dw_w, dw_b, ln_g, ln_b, w_co_b):
    B, S, _ = x.shape
    tm = min(TOKEN_TILE, S)
    nt = S // tm
    tail_rows = tuple(min(win, S) for win, _ in DILATED_GROUPS)
    tail_blk = tuple(min(t, tm) for t in tail_rows)

    def tok_spec(width):
        return pl.BlockSpec((tm, width), lambda b, i: (b * nt + i, 0))

    def tail_spec(g):
        first = nt - tail_rows[g] // tail_blk[g]
        return pl.BlockSpec((1, 2 * GROUP_WIDTH, tail_blk[g]), lambda b, i: (b, 0, jnp.maximum(i - first, 0)))

    def stream_shape(dil):
        return jax.ShapeDtypeStruct((B, S // dil, dil * GROUP_WIDTH), BF16)

    def stream_spec(dil):
        return pl.BlockSpec((1, tm // dil, dil * GROUP_WIDTH), lambda b, i: (b, i, 0))

    dils = [dil for _, dil in DILATED_GROUPS]
    qkv_order = [dils[0], dils[1], dils[2]] + [d for d in dils for _ in range(2)]
    mod_spec = pl.BlockSpec((1, 1, D_MODEL), lambda b, i: (b, 0, 0))
    out_shape = ([stream_shape(d) for d in qkv_order]
                 + [jax.ShapeDtypeStruct((B, 2 * GROUP_WIDTH, tail_rows[g]), F32) for g in range(N_GROUPS)]
                 + [jax.ShapeDtypeStruct((B * S, D_MODEL), BF16)] * 2
                 + [jax.ShapeDtypeStruct((B, CONV_HIST, CONV_CH), F32)])
    out_specs = ([stream_spec(d) for d in qkv_order] + [tail_spec(g) for g in range(N_GROUPS)]
                 + [tok_spec(D_MODEL)] * 2 + [pl.BlockSpec((1, CONV_HIST, CONV_CH), lambda b, i: (b, 0, 0))])
    n_stage = 3 * sum(1 for d in dils if d > 1)
    return pl.pallas_call(
        functools.partial(_k1_kernel, tail_rows, nt),
        grid=(B, nt),
        in_specs=[pl.BlockSpec((1, tm, D_MODEL), lambda b, i: (b, i, 0)), mod_spec, mod_spec,
                  _const_spec((1, D_MODEL)), _weight_spec((D_MODEL, IN_COLS)),
                  _const_spec((CONV_WIDTH, SUBLANES, CONV_CH)), _const_spec((SUBLANES, CONV_CH)),
                  _const_spec((1, CONV_CH)), _const_spec((1, CONV_CH)), _weight_spec((CONV_CH, D_MODEL))],
        out_specs=out_specs,
        out_shape=out_shape,
        scratch_shapes=([pltpu.VMEM((CONV_HIST + tm, CONV_CH), F32), pltpu.VMEM((tm, CONV_CH), BF16),
                         pltpu.VMEM((SUBLANES - 1, CONV_HIST + tm - SUBLANES, CONV_CH), F32),
                         pltpu.VMEM((tm, D_MODEL), F32)]
                        + [pltpu.VMEM((tm, LANES), F32)] * (n_stage * STAGE_HALVES)),
        compiler_params=_params(("arbitrary", "arbitrary")),
        name="inproj_prompt",
    )(x, sh1, sc1, norm_g, w_in_b, dw_w, dw_b, ln_g, ln_b, w_co_b)


def _k1s_kernel(n_steps, x_ref, sh_ref, sc_ref, g_ref, win_ref, st_ref, dww_ref, dwb_ref, lng_ref, lnb_ref, wco_ref,
                q0, q1, q2, k0, v0, k1, v1, k2, v2, ga_ref, sgb_ref, conv_ref):
    bd = st_ref.shape[1]
    hist = st_ref.shape[0]
    h = _rms_modulate(x_ref[...], g_ref[...], sc_ref[...], sh_ref[...])
    hb = h.astype(BF16)
    _, u, proj = _project_common(hb, win_ref, ((q0, q1, q2), (k0, k1, k2), (v0, v1, v2)),
                                 (1,) * N_GROUPS, [(None,) * 3] * N_GROUPS)

    def ext(j):
        return st_ref[j] if j < hist else u[(j - hist) * bd:(j - hist + 1) * bd, :]

    outs = []
    for t in range(n_steps):
        acc = jnp.zeros((bd, CONV_CH), F32) + dwb_ref[...]
        for k in range(CONV_WIDTH):
            acc = acc + dww_ref[k:k + 1, :] * ext(t + k + hist - (CONV_WIDTH - 1))
        outs.append(acc)
    for j in range(hist):
        conv_ref[j] = ext(j + n_steps)
    s = _conv_tail(jnp.concatenate(outs, axis=0), lng_ref[...], lnb_ref[...])
    a = jnp.dot(s.astype(BF16), wco_ref[...], preferred_element_type=F32)
    ga_ref[...] = (_sigmoid(proj(COL_GA, D_MODEL)) * a).astype(BF16)
    sgb_ref[...] = _sigmoid(proj(COL_GB, D_MODEL)).astype(BF16)


def _k1_sample(x_tm, sh1, sc1, norm_g, w_in_b, state_tm, dw_w, dw_b, ln_g, ln_b, w_co_b, n_steps):
    T = x_tm.shape[0]
    hist, bd, _ = state_tm.shape
    out_shape = ([jax.ShapeDtypeStruct((T, GROUP_WIDTH), F32)] * 9
                 + [jax.ShapeDtypeStruct((T, D_MODEL), BF16)] * 2
                 + [jax.ShapeDtypeStruct((hist, bd, CONV_CH), F32)])
    out_specs = ([_const_spec((T, GROUP_WIDTH))] * 9 + [_const_spec((T, D_MODEL))] * 2
                 + [_const_spec((hist, bd, CONV_CH))])
    return pl.pallas_call(
        functools.partial(_k1s_kernel, n_steps),
        grid=(1,),
        in_specs=[_const_spec((T, D_MODEL)), _const_spec((T, D_MODEL)), _const_spec((T, D_MODEL)),
                  _const_spec((1, D_MODEL)), _const_spec((D_MODEL, IN_COLS)), _const_spec((hist, bd, CONV_CH)),
                  _const_spec((CONV_WIDTH, CONV_CH)), _const_spec((1, CONV_CH)),
                  _const_spec((1, CONV_CH)), _const_spec((1, CONV_CH)), _const_spec((CONV_CH, D_MODEL))],
        out_specs=out_specs,
        out_shape=out_shape,
        compiler_params=_params(("arbitrary",)),
        name="inproj_sample",
    )(x_tm, sh1, sc1, norm_g, w_in_b, state_tm, dw_w, dw_b, ln_g, ln_b, w_co_b)


def _t5_bucket(dist):
    max_exact = N_BUCKETS // 2
    d_f = jnp.maximum(dist, 1).astype(F32)
    large = max_exact + (jnp.log(d_f / max_exact) / math.log(MAX_DISTANCE / max_exact)
                         * (N_BUCKETS - max_exact)).astype(jnp.int32)
    large = jnp.minimum(large, N_BUCKETS - 1)
    return jnp.where(dist < max_exact, dist, large)


def _bucket_lookup(rel_bias_g, dist):
    onehot = (_t5_bucket(dist)[..., None] == jnp.arange(N_BUCKETS)).astype(F32)
    return jnp.einsum('...b,bh->h...', onehot, rel_bias_g, precision=lax.Precision.HIGHEST)


def _prompt_bias(rel_bias_g, dil, n_keys):
    blk = n_keys
    i = jnp.arange(blk)[:, None]
    j = jnp.arange(2 * blk)[None, :]
    rel = i - j + blk
    valid = (rel >= 0) & (rel <= n_keys)
    bias = _bucket_lookup(rel_bias_g, jnp.clip(rel, 0, n_keys) * dil)
    bias = jnp.where(valid[None], bias, NEG_INF)
    return bias.reshape(HEADS_PER_GROUP * blk, 2 * blk).astype(F32)


def _sample_bias(rel_bias_g, dil, n_keys, lc, n_steps, q_rows):
    t = jnp.arange(q_rows)[:, None]
    pos = jnp.arange(lc)[None, :]
    dist_c = lc + t - pos
    dist_n = t - (jnp.arange(LANES)[None, :] - (LANES - n_steps))

    def table(dist, extra):
        ok = (dist >= 0) & (dist % dil == 0) & (dist // dil <= n_keys) & extra & (t < n_steps)
        b = _bucket_lookup(rel_bias_g, jnp.clip(dist, 0, None))
        b = jnp.where(ok[None], b, NEG_INF)
        return jnp.where((t >= n_steps)[None], 0.0, b).astype(F32)

    lane_ok = jnp.arange(LANES)[None, :] >= LANES - n_steps
    return table(dist_c, True), table(dist_n, lane_ok)


def _k2_kernel(q_ref, k_ref, kh_ref, v_ref, vh_ref, bias_ref, o_ref, lse_ref, kbuf, vbuf):
    i = pl.program_id(2)
    rows = q_ref.shape[1]
    n_streams = q_ref.shape[2] // GROUP_WIDTH
    blk = ATTN_BLOCK
    kbuf[0:blk, :] = kh_ref[0]
    kbuf[blk:blk + rows, :] = k_ref[0]
    vbuf[0:blk, :] = vh_ref[0]
    vbuf[blk:blk + rows, :] = v_ref[0]
    lane_head = lax.broadcasted_iota(jnp.int32, (blk, GROUP_WIDTH), 1) // HEAD_DIM
    col = lax.broadcasted_iota(jnp.int32, (HEADS_PER_GROUP * blk, 2 * blk), 1)
    first_mask = jnp.where((col < blk) & (i == 0), NEG_INF, 0.0).astype(F32)
    bias = bias_ref[...]
    for st in range(n_streams):
        cols = slice(st * GROUP_WIDTH, (st + 1) * GROUP_WIDTH)
        for j in range(rows // blk):
            qb = q_ref[0, j * blk:(j + 1) * blk, cols]
            q4 = jnp.concatenate([jnp.where(lane_head == h, qb, jnp.zeros_like(qb)) for h in range(HEADS_PER_GROUP)],
                                 axis=0)
            kc = kbuf[j * blk:(j + 2) * blk, cols]
            vc = vbuf[j * blk:(j + 2) * blk, cols]
            s = lax.dot_general(q4, kc, (((1,), (1,)), ((), ())), preferred_element_type=F32) + bias
            if j == 0:
                s = s + first_mask
            m = jnp.max(s, axis=-1, keepdims=True)
            p = jnp.exp(s - m)
            l = jnp.sum(p, axis=-1, keepdims=True)
            o4 = jnp.dot(p.astype(BF16), vc, preferred_element_type=F32) * (1.0 / l)
            lse4 = m + jnp.log(l)
            o = jnp.zeros((blk, GROUP_WIDTH), F32)
            lse = jnp.zeros((blk, GROUP_WIDTH), F32)
            for h in range(HEADS_PER_GROUP):
                sel = lane_head == h
                o = jnp.where(sel, o4[h * blk:(h + 1) * blk, :], o)
                lse = jnp.where(sel, lse4[h * blk:(h + 1) * blk, :], lse)
            o_ref[0, j * blk:(j + 1) * blk, cols] = o.astype(o_ref.dtype)
            lse_ref[0, j * blk:(j + 1) * blk, cols] = lse


def _k2_prompt(q, k, v, bias, dil):
    B, L, _ = q.shape
    rows = min(ATTN_ROWS, L)
    per = rows // ATTN_BLOCK
    n_streams = min(dil, max(1, ATTN_ROWS // rows))
    width = n_streams * GROUP_WIDTH
    main = pl.BlockSpec((1, rows, width), lambda b, r, i: (b, i, r))
    halo = pl.BlockSpec((1, ATTN_BLOCK, width), lambda b, r, i: (b, jnp.maximum(i * per - 1, 0), r))
    return pl.pallas_call(
        _k2_kernel,
        grid=(B, dil // n_streams, L // rows),
        in_specs=[main, main, halo, main, halo, _const_spec(bias.shape)],
        out_specs=[main, main],
        out_shape=[jax.ShapeDtypeStruct(q.shape, BF16), jax.ShapeDtypeStruct(q.shape, F32)],
        scratch_shapes=[pltpu.VMEM((ATTN_BLOCK + rows, width), BF16)] * 2,
        compiler_params=_params(("arbitrary", "arbitrary", "arbitrary")),
        name=f"attn_prompt_d{dil}",
    )(q, k, k, v, v, bias)


def _k2s_kernel(n_steps, q_ref, c0, c1, c2, t0, t1, t2, bc0, bc1, bc2, bn0, bn1, bn2,
                nc0, nc1, nc2, o0, o1, o2, l0, l1, l2):
    lane = lax.broadcasted_iota(jnp.int32, (2 * GROUP_WIDTH, LANES), 1)
    keep = LANES - n_steps
    groups = ((c0, t0, bc0, bn0, nc0, o0, l0), (c1, t1, bc1, bn1, nc1, o1, l1), (c2, t2, bc2, bn2, nc2, o2, l2))
    for bi in range(q_ref.shape[0]):
        for g, (c_ref, t_ref, bc_ref, bn_ref, nc_ref, o_ref, l_ref) in enumerate(groups):
            _sample_group(bi, g, lane, keep, q_ref, c_ref, t_ref, bc_ref, bn_ref, nc_ref, o_ref, l_ref)


def _sample_group(bi, g, lane, keep, q_ref, c_ref, t_ref, bc_ref, bn_ref, nc_ref, o_ref, l_ref):
    lc = c_ref.shape[2]
    n_tiles = lc // LANES
    new_rows = t_ref[bi]
    tail = jnp.concatenate([jnp.zeros((LANES - new_rows.shape[0], new_rows.shape[1]), F32), new_rows], axis=0).T
    cur = pltpu.roll(c_ref[bi, :, 0:LANES], keep, 1)
    for c in range(n_tiles):
        nxt = pltpu.roll(c_ref[bi, :, (c + 1) * LANES:(c + 2) * LANES], keep, 1) if c + 1 < n_tiles else tail
        nc_ref[bi, :, c * LANES:(c + 1) * LANES] = jnp.where(lane < keep, cur, nxt)
        cur = nxt
    for h in range(HEADS_PER_GROUP):
        lo = g * GROUP_WIDTH + h * HEAD_DIM
        qh = q_ref[bi, :, lo:lo + HEAD_DIM].astype(BF16)
        kh = c_ref[bi, h * HEAD_DIM:(h + 1) * HEAD_DIM, :].astype(BF16)
        vh = c_ref[bi, GROUP_WIDTH + h * HEAD_DIM:GROUP_WIDTH + (h + 1) * HEAD_DIM, :].astype(BF16)
        kt = tail[h * HEAD_DIM:(h + 1) * HEAD_DIM, :].astype(BF16)
        vt = tail[GROUP_WIDTH + h * HEAD_DIM:GROUP_WIDTH + (h + 1) * HEAD_DIM, :].astype(BF16)
        sc = jnp.dot(qh, kh, preferred_element_type=F32) + bc_ref[h]
        sn = jnp.dot(qh, kt, preferred_element_type=F32) + bn_ref[h]
        m = jnp.maximum(jnp.max(sc, axis=-1, keepdims=True), jnp.max(sn, axis=-1, keepdims=True))
        pc = jnp.exp(sc - m)
        pn = jnp.exp(sn - m)
        l = jnp.sum(pc, axis=-1, keepdims=True) + jnp.sum(pn, axis=-1, keepdims=True)
        nt_dims = (((1,), (1,)), ((), ()))
        o = (lax.dot_general(pc.astype(BF16), vh, nt_dims, preferred_element_type=F32)
             + lax.dot_general(pn.astype(BF16), vt, nt_dims, preferred_element_type=F32)) * (1.0 / l)
        o_ref[bi, :, h * HEAD_DIM:(h + 1) * HEAD_DIM] = o
        l_ref[bi, :, h * HEAD_DIM:(h + 1) * HEAD_DIM] = jnp.broadcast_to(m + jnp.log(l), o.shape)


def _k2_sample(q_b, caches, tails, biases_c, biases_n, n_steps):
    bd, q_rows, _ = q_b.shape
    bb = SAMPLE_BATCH if bd % SAMPLE_BATCH == 0 else 1
    per_b = lambda shape: pl.BlockSpec((bb,) + shape[1:], lambda b: (b,) + (0,) * (len(shape) - 1))
    ins = [q_b] + list(caches) + list(tails) + list(biases_c) + list(biases_n)
    in_specs = ([per_b(q_b.shape)] + [per_b(c.shape) for c in caches] + [per_b(t.shape) for t in tails]
                + [_const_spec(b.shape) for b in biases_c] + [_const_spec(b.shape) for b in biases_n])
    o_shape = jax.ShapeDtypeStruct((bd, q_rows, GROUP_WIDTH), F32)
    out_shape = [jax.ShapeDtypeStruct(c.shape, F32) for c in caches] + [o_shape] * 6
    out_specs = [per_b(c.shape) for c in caches] + [per_b(o_shape.shape)] * 6
    return pl.pallas_call(
        functools.partial(_k2s_kernel, n_steps),
        grid=(bd // bb,),
        in_specs=in_specs,
        out_specs=out_specs,
        out_shape=out_shape,
        compiler_params=_params(("arbitrary",)),
        name="attn_sample",
    )(*ins)


def _k3_kernel(x_ref, o0, o1, o2, l0, l1, l2, ga_ref, sgb_ref, g1_ref, sh2_ref, sc2_ref, gf_ref,
               wao_ref, wo_ref, wr_ref, br_ref, before_ref, cnt_in_ref,
               x2_ref, h2_ref, eid_ref, wts_ref, rank_ref, cnt_ref, carry, *stage_refs):
    @pl.when(pl.program_id(0) == 0)
    def _():
        carry[...] = cnt_in_ref[...]

    dils = tuple(x_ref.shape[0] // r.shape[-2] for r in (o0, o1, o2))
    pairs = [stage_refs[j:j + STAGE_HALVES] for j in range(0, len(stage_refs), STAGE_HALVES)]
    os_ = [_from_streams(r, d, pairs[2 * g]) for g, (r, d) in enumerate(zip((o0, o1, o2), dils))]
    ls = [_from_streams(r, d, pairs[2 * g + 1]) for g, (r, d) in enumerate(zip((l0, l1, l2), dils))]
    m = jnp.maximum(jnp.maximum(ls[0], ls[1]), ls[2])
    ws = [jnp.exp(l - m) for l in ls]
    den = ws[0] + ws[1] + ws[2]
    o = (ws[0] * os_[0] + ws[1] * os_[1] + ws[2] * os_[2]) / den
    b = jnp.dot(o.astype(BF16), wao_ref[...], preferred_element_type=F32)
    mixed = ga_ref[...].astype(F32) + sgb_ref[...].astype(F32) * b
    x2 = x_ref[...] + g1_ref[0] * jnp.dot(mixed.astype(BF16), wo_ref[...], preferred_element_type=F32)
    x2_ref[...] = x2
    h2 = _rms_modulate(x2, gf_ref[...], sc2_ref[0], sh2_ref[0])
    h2_ref[...] = h2
    h_hi = h2.astype(BF16)
    h_lo = (h2 - h_hi.astype(F32)).astype(BF16)
    nt_dims = (((1,), (1,)), ((), ()))
    lt = (lax.dot_general(wr_ref[0], h_hi, nt_dims, preferred_element_type=F32)
          + lax.dot_general(wr_ref[0], h_lo, nt_dims, preferred_element_type=F32)
          + lax.dot_general(wr_ref[1], h_hi, nt_dims, preferred_element_type=F32)) + br_ref[...]
    tm = h2.shape[0]
    gl = lt[0:8, :]
    gmax = jnp.max(gl, axis=0, keepdims=True)
    r8 = lax.broadcasted_iota(jnp.int32, (8, tm), 0)
    grp = jnp.min(jnp.where(gl == gmax, r8, 8), axis=0, keepdims=True)
    p_grp = 1.0 / jnp.sum(jnp.exp(gl - gmax), axis=0, keepdims=True)
    es = jnp.zeros((EXPERTS_PER_GROUP, tm), F32)
    for g in range(N_EXPERT_GROUPS):
        es = jnp.where(grp == g, lt[8 + 8 * g:16 + 8 * g, :], es)
    v1 = jnp.max(es, axis=0, keepdims=True)
    i1 = jnp.min(jnp.where(es == v1, r8, 8), axis=0, keepdims=True)
    rest = jnp.where(r8 == i1, -jnp.inf, es)
    v2 = jnp.max(rest, axis=0, keepdims=True)
    i2 = jnp.min(jnp.where(rest == v2, r8, 8), axis=0, keepdims=True)
    e21 = jnp.exp(v2 - v1)
    w1 = p_grp / (1.0 + e21)
    e1 = grp * EXPERTS_PER_GROUP + i1
    e2 = grp * EXPERTS_PER_GROUP + i2
    eid_ref[0, 0:1, :] = e1
    eid_ref[0, 1:2, :] = e2
    wts_ref[0, 0:1, :] = w1
    wts_ref[0, 1:2, :] = w1 * e21
    r_e = lax.broadcasted_iota(jnp.int32, (N_EXPERTS, tm), 0)
    hit1 = r_e == e1
    hit2 = r_e == e2
    both = jnp.where(hit1 | hit2, 1.0, 0.0)
    base = carry[...] + jnp.dot(both.astype(BF16), before_ref[...], preferred_element_type=F32)
    rank_ref[0, 0:1, :] = jnp.sum(jnp.where(hit1, base, 0.0), axis=0, keepdims=True).astype(jnp.int32)
    rank_ref[0, 1:2, :] = jnp.sum(jnp.where(hit2, base, 0.0), axis=0, keepdims=True).astype(jnp.int32)
    total = carry[...] + jnp.sum(both, axis=1, keepdims=True)
    carry[...] = total
    cnt_ref[...] = total


def _k3(x, o_l, g1, sh2, sc2, ga, sgb, norm_ffn_g, w_ao_b, w_o_b, w_r, b_r, cnt_in, tiles_per_mod):
    T = x.shape[0]
    tm = min(TOKEN_TILE, T)
    nt = T // tm
    tok = lambda w: pl.BlockSpec((tm, w), lambda t: (t, 0))
    mod = pl.BlockSpec((1,) + g1.shape[1:], lambda t: (t // tiles_per_mod, 0, 0))
    small = pl.BlockSpec((1, 2, tm), lambda t: (t, 0, 0))
    small_i = jax.ShapeDtypeStruct((nt, 2, tm), jnp.int32)
    before = (jnp.arange(tm)[:, None] < jnp.arange(tm)[None, :]).astype(BF16)

    def attn_spec(a):
        if a.ndim == 2:
            return tok(GROUP_WIDTH)
        dil = a.shape[2] // GROUP_WIDTH
        return pl.BlockSpec((1, tm // dil, a.shape[2]), lambda t: (t // tiles_per_mod, t % tiles_per_mod, 0))

    attn_in = [o_l[0][0], o_l[1][0], o_l[2][0], o_l[0][1], o_l[1][1], o_l[2][1]]
    return pl.pallas_call(
        _k3_kernel,
        grid=(nt,),
        in_specs=[tok(D_MODEL)] + [attn_spec(a) for a in attn_in] + [tok(D_MODEL)] * 2 + [mod] * 3
                 + [_const_spec((1, D_MODEL)), _weight_spec(w_ao_b.shape), _weight_spec(w_o_b.shape),
                    _const_spec(w_r.shape), _const_spec(b_r.shape), _weight_spec((tm, tm)),
                    _const_spec((N_EXPERTS, 1))],
        out_specs=[tok(D_MODEL), tok(D_MODEL), small, small, small, _const_spec((N_EXPERTS, 1))],
        out_shape=[jax.ShapeDtypeStruct((T, D_MODEL), F32), jax.ShapeDtypeStruct((T, D_MODEL), F32),
                   small_i, jax.ShapeDtypeStruct((nt, 2, tm), F32), small_i,
                   jax.ShapeDtypeStruct((N_EXPERTS, 1), F32)],
        scratch_shapes=([pltpu.VMEM((N_EXPERTS, 1), F32)]
                        + [pltpu.VMEM((tm, LANES), F32)] * (2 * N_GROUPS * STAGE_HALVES)),
        compiler_params=_params(("arbitrary",)),
        name="merge_router",
    )(x, *attn_in, ga, sgb, g1, sh2, sc2, norm_ffn_g, w_ao_b, w_o_b, w_r, b_r, before, cnt_in)


def _rows_wait(n_rows, hbm, vmem, sem):
    pltpu.make_async_copy(hbm.at[pl.ds(0, n_rows)], vmem, sem).wait()


def _dispatch_kernel(n_tiles, n_first, dest_ref, fill_ref, ha_ref, hb_ref, xs_ref, buf, sem, zbuf, zsem):
    i = pl.program_id(0)
    slot = i % 2 if n_tiles > 1 else 0
    tm = ha_ref.shape[0]

    def drain(s):
        for _ in range(2):
            _rows_wait(tm, xs_ref, buf.at[s], sem.at[s])

    @pl.when(i == 0)
    def _():
        zbuf[...] = jnp.zeros(zbuf.shape, F32)

        def fill(row):
            return pltpu.make_async_copy(zbuf, xs_ref.at[pl.ds(pl.multiple_of(row, MOE_ROWS), MOE_ROWS)], zsem)

        n_blocks = xs_ref.shape[0] // MOE_ROWS
        n_used = fill_ref[N_EXPERTS]

        def start_block(j, carry):
            fill(j * MOE_ROWS).start()
            return carry

        def wait_block(j, carry):
            fill(j * MOE_ROWS).wait()
            return carry

        for e in range(N_EXPERTS):
            @pl.when(fill_ref[e] >= 0)
            def _(e=e):
                fill(fill_ref[e]).start()
        lax.fori_loop(n_used, n_blocks, start_block, 0)
        for e in range(N_EXPERTS):
            @pl.when(fill_ref[e] >= 0)
            def _(e=e):
                fill(fill_ref[e]).wait()
        lax.fori_loop(n_used, n_blocks, wait_block, 0)

    if n_tiles > 2:
        @pl.when(i >= 2)
        def _():
            drain(slot)

    @pl.when(i < n_first)
    def _():
        buf[slot] = ha_ref[...]

    @pl.when(i >= n_first)
    def _():
        buf[slot] = hb_ref[...]

    for r in range(tm):
        for k in range(2):
            pltpu.make_async_copy(buf.at[slot, pl.ds(r, 1)], xs_ref.at[pl.ds(dest_ref[0, k, r], 1)], sem.at[slot]).start()

    if n_tiles == 1:
        drain(0)
    else:
        @pl.when(i == n_tiles - 1)
        def _():
            drain(slot)
            drain(1 - slot)


def _dispatch(dest, fill_rows, h_a, h_b, n_slots):
    nt, _, tm = dest.shape
    n_first = h_a.shape[0] // tm
    assert h_a.shape[0] % tm == 0 and h_b.shape[0] == (nt - n_first) * tm
    return pl.pallas_call(
        functools.partial(_dispatch_kernel, nt, n_first),
        grid=(nt,),
        in_specs=[pl.BlockSpec((1, 2, tm), lambda t: (t, 0, 0), memory_space=pltpu.SMEM),
                  pl.BlockSpec(memory_space=pltpu.SMEM),
                  pl.BlockSpec((tm, D_MODEL), lambda t: (jnp.minimum(t, n_first - 1), 0)),
                  pl.BlockSpec((tm, D_MODEL), lambda t: (jnp.maximum(t - n_first, 0), 0))],
        out_specs=pl.BlockSpec(memory_space=pl.ANY),
        out_shape=jax.ShapeDtypeStruct((n_slots, D_MODEL), F32),
        scratch_shapes=[pltpu.VMEM((2, tm, D_MODEL), F32), pltpu.SemaphoreType.DMA((2,)),
                        pltpu.VMEM((MOE_ROWS, D_MODEL), F32), pltpu.SemaphoreType.DMA(())],
        compiler_params=_params(("arbitrary",)),
        name="dispatch_rows",
    )(dest, fill_rows, h_a, h_b)


def _k4_kernel(be_ref, nused_ref, x_ref, wg_ref, wu_ref, wd_ref, y_ref, wg_b, wu_b, wd_b):
    i = pl.program_id(0)

    @pl.when((i == 0) | (be_ref[i] != be_ref[jnp.maximum(i - 1, 0)]))
    def _():
        wg_b[...] = wg_ref[0].astype(BF16)
        wu_b[...] = wu_ref[0].astype(BF16)
        wd_b[...] = wd_ref[0].astype(BF16)

    @pl.when(i < nused_ref[0])
    def _():
        xb = x_ref[...].astype(BF16)
        gate = jnp.dot(xb, wg_b[...], preferred_element_type=F32)
        up = jnp.dot(xb, wu_b[...], preferred_element_type=F32)
        mid = (_silu(gate) * up).astype(BF16)
        y_ref[...] = jnp.dot(mid, wd_b[...], preferred_element_type=F32)

    @pl.when(i >= nused_ref[0])
    def _():
        y_ref[...] = jnp.zeros(y_ref.shape, F32)


def _k4(block_expert, n_used, xs, w_gate, w_up, w_down):
    n_blocks = block_expert.shape[0]
    rows = MOE_ROWS
    weight = lambda shape: pl.BlockSpec((1,) + shape, lambda i, be, nu: (be[i], 0, 0))
    grid_spec = pltpu.PrefetchScalarGridSpec(
        num_scalar_prefetch=2,
        grid=(n_blocks,),
        in_specs=[pl.BlockSpec((rows, D_MODEL), lambda i, be, nu: (jnp.minimum(i, nu[0] - 1), 0)),
                  weight((D_MODEL, D_FF_EXPERT)), weight((D_MODEL, D_FF_EXPERT)), weight((D_FF_EXPERT, D_MODEL))],
        out_specs=pl.BlockSpec((rows, D_MODEL), lambda i, be, nu: (i, 0)),
        scratch_shapes=[pltpu.VMEM((D_MODEL, D_FF_EXPERT), BF16), pltpu.VMEM((D_MODEL, D_FF_EXPERT), BF16),
                        pltpu.VMEM((D_FF_EXPERT, D_MODEL), BF16)],
    )
    return pl.pallas_call(
        _k4_kernel,
        grid_spec=grid_spec,
        out_shape=jax.ShapeDtypeStruct((n_blocks * rows, D_MODEL), F32),
        compiler_params=_params(("arbitrary",)),
        name="expert_blocks",
    )(block_expert, n_used, xs, w_gate, w_up, w_down)


def _k5_kernel(n_tiles, idx_ref, x2_ref, w_ref, g2_ref, gfin_ref, ys_hbm, y_ref, ybuf, sem):
    i = pl.program_id(0)
    tm = x2_ref.shape[0]

    @pl.when(i < n_tiles)
    def _():
        slot = i % 2
        for r in range(tm):
            for k in range(2):
                pltpu.make_async_copy(ys_hbm.at[pl.ds(idx_ref[0, k, r], 1)], ybuf.at[slot, k, pl.ds(r, 1)],
                                      sem.at[slot]).start()

    @pl.when(i >= 1)
    def _():
        slot = (i - 1) % 2
        for k in range(2):
            _rows_wait(tm, ys_hbm, ybuf.at[slot, k], sem.at[slot])
        w = w_ref[...]
        f = ybuf[slot, 0] * w[:, 0:1] + ybuf[slot, 1] * w[:, 1:2]
        y = x2_ref[...] + g2_ref[0] * f
        r = lax.rsqrt(jnp.mean(y * y, axis=-1, keepdims=True) + EPS)
        y_ref[...] = (y * r) * gfin_ref[...]


def _k5(dest, x2, wts, g2, norm_final_g, ys, tiles_per_mod):
    nt, _, tm = dest.shape
    T = x2.shape[0]
    prev = lambda t: jnp.maximum(t - 1, 0)
    return pl.pallas_call(
        functools.partial(_k5_kernel, nt),
        grid=(nt + 1,),
        in_specs=[pl.BlockSpec((1, 2, tm), lambda t: (jnp.minimum(t, nt - 1), 0, 0), memory_space=pltpu.SMEM),
                  pl.BlockSpec((tm, D_MODEL), lambda t: (prev(t), 0)),
                  pl.BlockSpec((tm, 2), lambda t: (prev(t), 0)),
                  pl.BlockSpec((1,) + g2.shape[1:], lambda t: (prev(t) // tiles_per_mod, 0, 0)),
                  _const_spec((1, D_MODEL)), pl.BlockSpec(memory_space=pl.ANY)],
        out_specs=pl.BlockSpec((tm, D_MODEL), lambda t: (prev(t), 0)),
        out_shape=jax.ShapeDtypeStruct((T, D_MODEL), F32),
        scratch_shapes=[pltpu.VMEM((2, 2, tm, D_MODEL), F32), pltpu.SemaphoreType.DMA((2,))],
        compiler_params=_params(("arbitrary",)),
        name="combine_norm",
    )(dest, x2, wts, g2, norm_final_g, ys)


def _slot_tables(counts, n_blocks):
    padded = (counts + MOE_ROWS - 1) // MOE_ROWS * MOE_ROWS
    pend = jnp.cumsum(padded)
    pstart = pend - padded
    block_lo = jnp.arange(n_blocks, dtype=jnp.int32) * MOE_ROWS
    block_expert = jnp.minimum(jnp.sum(pend[None, :] <= block_lo[:, None], axis=1), N_EXPERTS - 1).astype(jnp.int32)
    n_used = (pend[-1] // MOE_ROWS).astype(jnp.int32).reshape(1)
    fill_rows = jnp.concatenate([jnp.where(padded > 0, pend - MOE_ROWS, -1).astype(jnp.int32), n_used])
    return pstart.astype(jnp.int32), block_expert, n_used, fill_rows


def _slots(eid, rank, pstart):
    sel = eid[..., None] == jnp.arange(N_EXPERTS, dtype=jnp.int32)
    return jnp.sum(jnp.where(sel, pstart, 0), axis=-1).astype(jnp.int32) + rank


def _cache_view(cache):
    bd, lc = cache.shape[:2]
    return cache.transpose(0, 2, 3, 4, 1).reshape(bd, 2 * GROUP_WIDTH, lc)


def _cache_unview(view):
    bd, _, lc = view.shape
    return view.reshape(bd, 2, HEADS_PER_GROUP, HEAD_DIM, lc).transpose(0, 4, 1, 2, 3)


def kernel(x_prompt, x_sample, c_prompt, c_sample, cache_kv_w128, cache_kv_w512, cache_kv_w2048, state_conv, rel_bias, norm_mix_g, norm_ffn_g, w_mod, b_mod, w_in, dw_w, dw_b, ln_conv_g, ln_conv_b, w_conv_out, w_attn_out, w_out, w_router_group, b_router_group, w_router_expert, b_router_expert, w_exp_gate, w_exp_up, w_exp_down, norm_final_g):
    assert norm_mix_g.shape[0] == 1, "single layer"
    B, S, D = x_prompt.shape
    Bd, Td, _ = x_sample.shape
    caches_in = (cache_kv_w128[0], cache_kv_w512[0], cache_kv_w2048[0])
    for (win, dil), c in zip(DILATED_GROUPS, caches_in):
        assert c.shape[1] >= (win // dil) * dil and c.shape[1] % LANES == 0 and S % (dil * ATTN_BLOCK) == 0

    wi = w_in[0]
    q_cols = wi[:, :ATTN_WIDTH] * (HEAD_DIM ** -0.5)
    kv_cols = []
    for g in range(N_GROUPS):
        kv_cols += [wi[:, ATTN_WIDTH + g * GROUP_WIDTH:ATTN_WIDTH + (g + 1) * GROUP_WIDTH],
                    wi[:, 2 * ATTN_WIDTH + g * GROUP_WIDTH:2 * ATTN_WIDTH + (g + 1) * GROUP_WIDTH]]
    w_in_b = jnp.concatenate([q_cols] + kv_cols + [wi[:, 3 * ATTN_WIDTH:]], axis=1).astype(BF16)
    w_co_b = w_conv_out[0].astype(BF16)
    w_ao_b = w_attn_out[0].astype(BF16)
    w_o_b = w_out[0].astype(BF16)
    w_r = jnp.zeros((ROUTER_ROWS, D), F32)
    w_r = w_r.at[0:N_EXPERT_GROUPS].set(w_router_group[0].T)
    w_r = w_r.at[8:].set(w_router_expert[0].reshape(D, N_EXPERTS).T)
    w_r_hi = w_r.astype(BF16)
    w_r = jnp.stack([w_r_hi, (w_r - w_r_hi.astype(F32)).astype(BF16)])
    b_r = jnp.full((ROUTER_ROWS, 1), NEG_INF, F32)
    b_r = b_r.at[0:N_EXPERT_GROUPS, 0].set(b_router_group[0])
    b_r = b_r.at[8:, 0].set(b_router_expert[0].reshape(N_EXPERTS))
    row = lambda v: v.reshape(1, -1)

    n_seq = B + Bd
    n_seq_pad = -(-n_seq // 16) * 16
    c_all = jnp.concatenate([c_prompt, c_sample, jnp.zeros((n_seq_pad - n_seq, D), F32)], axis=0)
    mod = _modulation(c_all, w_mod[0], row(b_mod[0]))
    mod_p = [mod[:B, j * D:(j + 1) * D].reshape(B, 1, D) for j in range(6)]
    mod_s = [jnp.tile(mod[B:B + Bd, j * D:(j + 1) * D], (Td, 1)) for j in range(6)]

    dw_w_rows = jnp.broadcast_to(dw_w[0][:, None, :], (CONV_WIDTH, SUBLANES, CONV_CH))
    dw_b_rows = jnp.broadcast_to(dw_b[0][None, :], (SUBLANES, CONV_CH))
    (q0, q1, q2, k0, v0, k1, v1, k2, v2, kvt0, kvt1, kvt2, ga, sgb, convp) = _k1_prompt(
        x_prompt, mod_p[0], mod_p[1], row(norm_mix_g[0]), w_in_b, dw_w_rows, dw_b_rows,
        row(ln_conv_g[0]), row(ln_conv_b[0]), w_co_b)
    o_l = []
    for g, ((win, dil), qg, kg, vg) in enumerate(zip(DILATED_GROUPS, (q0, q1, q2), (k0, k1, k2), (v0, v1, v2))):
        bias = _prompt_bias(rel_bias[:, g * HEADS_PER_GROUP:(g + 1) * HEADS_PER_GROUP], dil, win // dil)
        o_l.append(_k2_prompt(qg, kg, vg, bias, dil))
    xp = x_prompt.reshape(B * S, D)
    tm3 = min(TOKEN_TILE, B * S)
    x2, h2, eid, wts, rank, cnt_p = _k3(xp, o_l, mod_p[2], mod_p[3], mod_p[4], ga, sgb, row(norm_ffn_g[0]),
                                        w_ao_b, w_o_b, w_r, b_r, jnp.zeros((N_EXPERTS, 1), F32), S // tm3)
    kv_p = [_cache_unview(kvt)[None] for kvt in (kvt0, kvt1, kvt2)]
    conv_p = convp[:, CONV_HIST - (CONV_WIDTH - 1):, :][None]

    Ts = Td * Bd
    xs = x_sample.transpose(1, 0, 2).reshape(Ts, D)
    state_tm = state_conv[0].transpose(1, 0, 2)
    (sq0, sq1, sq2, sk0, sv0, sk1, sv1, sk2, sv2, ga_s, sgb_s, conv_tm) = _k1_sample(
        xs, mod_s[0], mod_s[1], row(norm_mix_g[0]), w_in_b, state_tm, dw_w[0], row(dw_b[0]),
        row(ln_conv_g[0]), row(ln_conv_b[0]), w_co_b, Td)
    q_rows = PACKED_ROWS
    assert Td <= SUBLANES
    q_b = jnp.concatenate([sq0, sq1, sq2], axis=1).reshape(Td, Bd, ATTN_WIDTH).transpose(1, 0, 2)
    q_b = jnp.pad(q_b, ((0, 0), (0, q_rows - Td), (0, 0)))
    views, tails, biases_c, biases_n = [], [], [], []
    for g, ((win, dil), c, sk, sv) in enumerate(zip(DILATED_GROUPS, caches_in, (sk0, sk1, sk2), (sv0, sv1, sv2))):
        lc = c.shape[1]
        views.append(_cache_view(c))
        new_kv = jnp.concatenate([sk, sv], axis=1).reshape(Td, Bd, 2 * GROUP_WIDTH).transpose(1, 0, 2)
        tails.append(jnp.pad(new_kv, ((0, 0), (SUBLANES - Td, 0), (0, 0))))
        bc, bn = _sample_bias(rel_bias[:, g * HEADS_PER_GROUP:(g + 1) * HEADS_PER_GROUP], dil, win // dil, lc, Td, q_rows)
        biases_c.append(bc)
        biases_n.append(bn)
    (nc0, nc1, nc2, so0, so1, so2, sl0, sl1, sl2) = _k2_sample(q_b, views, tails, biases_c, biases_n, Td)
    to_tm = lambda a: a[:, :Td, :].transpose(1, 0, 2).reshape(Ts, GROUP_WIDTH)
    o_l_s = [(to_tm(so0), to_tm(sl0)), (to_tm(so1), to_tm(sl1)), (to_tm(so2), to_tm(sl2))]
    tm3s = min(TOKEN_TILE, Ts)
    tiled = lambda m: m.reshape(Ts // tm3s, tm3s, D)
    x2s, h2s, eid_s, wts_s, rank_s, cnt_all = _k3(xs, o_l_s, tiled(mod_s[2]), tiled(mod_s[3]), tiled(mod_s[4]),
                                                  ga_s, sgb_s, row(norm_ffn_g[0]), w_ao_b, w_o_b, w_r, b_r, cnt_p, 1)

    n_assign = 2 * (B * S + Ts)
    n_blocks = -(-n_assign // MOE_ROWS) + N_EXPERTS
    pstart, block_expert, n_used, fill_rows = _slot_tables(cnt_all[:, 0].astype(jnp.int32), n_blocks)
    dest_p = _slots(eid, rank, pstart)
    dest_s = _slots(eid_s, rank_s, pstart)
    n_slots = n_blocks * MOE_ROWS
    assert tm3s == tm3, "prompt and sample token tiles must match to share the dispatch"
    slots = _dispatch(jnp.concatenate([dest_p, dest_s], axis=0), fill_rows, h2, h2s, n_slots)
    ys = _k4(block_expert, n_used, slots, w_exp_gate[0], w_exp_up[0], w_exp_down[0])
    per_token = lambda a: a.transpose(0, 2, 1).reshape(-1, 2)
    y_prompt = _k5(dest_p, x2, per_token(wts), mod_p[5], row(norm_final_g), ys, S // tm3).reshape(B, S, D)
    y_s = _k5(dest_s, x2s, per_token(wts_s), tiled(mod_s[5]), row(norm_final_g), ys, 1)
    y_sample = y_s.reshape(Td, Bd, D).transpose(1, 0, 2)
    kv_s = [_cache_unview(nc)[None] for nc in (nc0, nc1, nc2)]
    conv_s = conv_tm.transpose(1, 0, 2)[None]

    return (y_prompt, y_sample, kv_p[0], kv_p[1], kv_p[2], conv_p,
            kv_s[0], kv_s[1], kv_s[2], conv_s)
```

```python
import functools
import math

import jax
import jax.numpy as jnp
from jax import lax
from jax.experimental import pallas as pl
from jax.experimental.pallas import tpu as pltpu

F32 = jnp.float32
BF16 = jnp.bfloat16

D_MODEL = 1024
HEAD_DIM = 64
HEADS_PER_GROUP = 4
GROUP_WIDTH = HEADS_PER_GROUP * HEAD_DIM
LANES = 128
SUBLANES = 8
CONV_CHUNK = 32
STAGE_HALVES = GROUP_WIDTH // LANES
DILATED_GROUPS = ((128, 1), (512, 4), (2048, 16))
N_GROUPS = len(DILATED_GROUPS)
ATTN_WIDTH = N_GROUPS * GROUP_WIDTH
CONV_CH = D_MODEL // 2
CONV_WIDTH = 31
CONV_HIST = 32
N_BUCKETS = 32
MAX_DISTANCE = 2048
N_EXPERT_GROUPS = 4
EXPERTS_PER_GROUP = 8
N_EXPERTS = N_EXPERT_GROUPS * EXPERTS_PER_GROUP
D_FF_EXPERT = D_MODEL // 2
EPS = 1e-6
NEG_INF = -1e30

COL_KV = ATTN_WIDTH
COL_ULIN = COL_KV + 2 * ATTN_WIDTH
COL_UGATE = COL_ULIN + CONV_CH
COL_GA = COL_UGATE + CONV_CH
COL_GB = COL_GA + D_MODEL
IN_COLS = COL_GB + D_MODEL

ROUTER_ROWS = 8 + N_EXPERTS

V7X_VMEM_LIMIT = 56 * 1024 * 1024
TOKEN_TILE = 512
ATTN_ROWS = 2048
ATTN_BLOCK = 128
MOE_ROWS = 512
SAMPLE_BATCH = 2
PACKED_ROWS = 16


def _sigmoid(x):
    return 1.0 / (1.0 + jnp.exp(-x))


def _silu(x):
    return x * _sigmoid(x)


def _rms_modulate(x, g, sc, sh):
    r = lax.rsqrt(jnp.mean(x * x, axis=-1, keepdims=True) + EPS)
    return ((x * r) * g) * (1.0 + sc) + sh


def _conv_tail(y, g, b):
    mu = jnp.mean(y, axis=-1, keepdims=True)
    yc = y - mu
    var = jnp.mean(yc * yc, axis=-1, keepdims=True)
    return _silu((yc * lax.rsqrt(var + EPS)) * g + b)


def _params(semantics):
    return pltpu.CompilerParams(dimension_semantics=semantics, vmem_limit_bytes=V7X_VMEM_LIMIT)


def _const_spec(shape):
    nd = len(shape)
    return pl.BlockSpec(shape, lambda *_: (0,) * nd)


def _weight_spec(shape):
    nd = len(shape)
    return pl.BlockSpec(shape, lambda *_: (0,) * nd, pipeline_mode=pl.Buffered(1))


def _mod_kernel(c_ref, w_ref, b_ref, o_ref):
    s = _silu(c_ref[...]).astype(BF16)
    o_ref[...] = jnp.dot(s, w_ref[...].astype(BF16), preferred_element_type=F32) + b_ref[...]


def _modulation(c_all, w_mod, b_mod):
    rows = c_all.shape[0]
    n_out = w_mod.shape[1]
    chunk = D_MODEL
    return pl.pallas_call(
        _mod_kernel,
        grid=(n_out // chunk,),
        in_specs=[_const_spec((rows, D_MODEL)),
                  pl.BlockSpec((D_MODEL, chunk), lambda j: (0, j)),
                  pl.BlockSpec((1, chunk), lambda j: (0, j))],
        out_specs=pl.BlockSpec((rows, chunk), lambda j: (0, j)),
        out_shape=jax.ShapeDtypeStruct((rows, n_out), F32),
        compiler_params=_params(("arbitrary",)),
        name="modulation",
    )(c_all, w_mod, b_mod)


def _to_streams(ref, val, dil, stage):
    if dil == 1:
        ref[...] = val.astype(ref.dtype).reshape(ref.shape)
        return
    n = val.shape[0] // dil
    for c, half in enumerate(stage):
        half[...] = val[:, c * LANES:(c + 1) * LANES]
        for r in range(dil):
            lo = r * GROUP_WIDTH + c * LANES
            ref[0, :, lo:lo + LANES] = half[pl.ds(r, n, stride=dil), :].astype(ref.dtype)


def _from_streams(ref, dil, stage):
    if dil == 1:
        return ref[...].astype(F32).reshape(ref.shape[-2:])
    n = ref.shape[1]
    for c, half in enumerate(stage):
        for r in range(dil):
            lo = r * GROUP_WIDTH + c * LANES
            half[pl.ds(r, n, stride=dil), :] = ref[0, :, lo:lo + LANES].astype(F32)
    return jnp.concatenate([half[...] for half in stage], axis=1)


def _project_common(hb, win_ref, outs, dils, stages):
    (q_refs, k_refs, v_refs) = outs

    def proj(lo, width):
        return jnp.dot(hb, win_ref[:, lo:lo + width], preferred_element_type=F32)

    zq = proj(0, ATTN_WIDTH)
    for g in range(N_GROUPS):
        _to_streams(q_refs[g], zq[:, g * GROUP_WIDTH:(g + 1) * GROUP_WIDTH], dils[g], stages[g][0])
    zkvs = []
    for g in range(N_GROUPS):
        zkv = proj(COL_KV + 2 * GROUP_WIDTH * g, 2 * GROUP_WIDTH)
        _to_streams(k_refs[g], zkv[:, :GROUP_WIDTH], dils[g], stages[g][1])
        _to_streams(v_refs[g], zkv[:, GROUP_WIDTH:], dils[g], stages[g][2])
        zkvs.append(zkv)
    u = proj(COL_ULIN, CONV_CH) * _sigmoid(proj(COL_UGATE, CONV_CH))
    return zkvs, u, proj


def _k1_kernel(tail_rows, n_tiles,
               x_ref, sh_ref, sc_ref, g_ref, win_ref, dww_ref, dwb_ref, lng_ref, lnb_ref, wco_ref,
               q0, q1, q2, k0, v0, k1, v1, k2, v2, kvt0, kvt1, kvt2, ga_ref, sgb_ref, convp_ref, uext, sbuf, ushift, sga, *stage_refs):
    i = pl.program_id(1)
    tm = x_ref.shape[1]

    @pl.when(i == 0)
    def _():
        uext[0:CONV_HIST, :] = jnp.zeros((CONV_HIST, CONV_CH), F32)

    h = _rms_modulate(x_ref[0], g_ref[...], sc_ref[0], sh_ref[0])
    hb = h.astype(BF16)

    def proj(lo, width):
        return jnp.dot(hb, win_ref[:, lo:lo + width], preferred_element_type=F32)

    uext[CONV_HIST:CONV_HIST + tm, :] = proj(COL_ULIN, CONV_CH) * _sigmoid(proj(COL_UGATE, CONV_CH))
    base = CONV_HIST - (CONV_WIDTH - 1)
    span = tm + CONV_HIST - SUBLANES
    for b in range(1, SUBLANES):
        ushift[b - 1, 0:span, :] = uext[b:b + span, :]

    def conv_chunk(r0):
        accs = [dwb_ref[...]] * (CONV_CHUNK // SUBLANES)
        for k in range(CONV_WIDTH):
            b = (base + k) % SUBLANES
            lo = r0 + base + k - b
            w8 = dww_ref[k]
            for q in range(len(accs)):
                rows = slice(lo + q * SUBLANES, lo + (q + 1) * SUBLANES)
                src = uext[rows, :] if b == 0 else ushift[b - 1, rows, :]
                accs[q] = accs[q] + w8 * src
        s = _conv_tail(jnp.concatenate(accs, axis=0), lng_ref[...], lnb_ref[...])
        sbuf[r0:r0 + CONV_CHUNK, :] = s.astype(BF16)

    dils = tuple(dil for _, dil in DILATED_GROUPS)
    pairs = [stage_refs[j:j + STAGE_HALVES] for j in range(0, len(stage_refs), STAGE_HALVES)]
    stages = [(None,) * 3] + [pairs[3 * (g - 1):3 * g] for g in range(1, N_GROUPS)]
    pieces = []
    for g, q_ref in enumerate((q0, q1, q2)):
        def q_piece(g=g, q_ref=q_ref):
            _to_streams(q_ref, proj(g * GROUP_WIDTH, GROUP_WIDTH), dils[g], stages[g][0])
        pieces.append(q_piece)
    for g, (k_ref, v_ref, kvt) in enumerate(((k0, v0, kvt0), (k1, v1, kvt1), (k2, v2, kvt2))):
        for half, ref in enumerate((k_ref, v_ref)):
            def kv_piece(g=g, half=half, ref=ref, kvt=kvt):
                z = proj(COL_KV + (2 * g + half) * GROUP_WIDTH, GROUP_WIDTH)
                _to_streams(ref, z, dils[g], stages[g][1 + half])
                tr = kvt.shape[2]

                @pl.when(i >= n_tiles - tail_rows[g] // tr)
                def _():
                    kvt[0, half * GROUP_WIDTH:(half + 1) * GROUP_WIDTH, :] = z[tm - tr:, :].T
            pieces.append(kv_piece)
    for c in range(D_MODEL // GROUP_WIDTH):
        cols = slice(c * GROUP_WIDTH, (c + 1) * GROUP_WIDTH)

        def gb_piece(c=c, cols=cols):
            sgb_ref[:, cols] = _sigmoid(proj(COL_GB + c * GROUP_WIDTH, GROUP_WIDTH)).astype(BF16)

        def ga_piece(c=c, cols=cols):
            sga[:, cols] = _sigmoid(proj(COL_GA + c * GROUP_WIDTH, GROUP_WIDTH))
        pieces += [gb_piece, ga_piece]
    n_qkv = 3 * N_GROUPS
    for piece in pieces[:n_qkv]:
        piece()
    for r0 in range(0, tm, CONV_CHUNK):
        conv_chunk(r0)
    for piece in pieces[n_qkv:]:
        piece()
    last = uext[tm:tm + CONV_HIST, :]
    uext[0:CONV_HIST, :] = last
    convp_ref[0] = last
    a = jnp.dot(sbuf[...], wco_ref[...], preferred_element_type=F32)
    ga_ref[...] = (sga[...] * a).astype(BF16)


def _k1_prompt(x, sh1, sc1, norm_g, w_in_b, dw_w, dw_b, ln_g, ln_b, w_co_b):
    B, S, _ = x.shape
    tm = min(TOKEN_TILE, S)
    nt = S // tm
    tail_rows = tuple(min(win, S) for win, _ in DILATED_GROUPS)
    tail_blk = tuple(min(t, tm) for t in tail_rows)

    def tok_spec(width):
        return pl.BlockSpec((tm, width), lambda b, i: (b * nt + i, 0))

    def tail_spec(g):
        first = nt - tail_rows[g] // tail_blk[g]
        return pl.BlockSpec((1, 2 * GROUP_WIDTH, tail_blk[g]), lambda b, i: (b, 0, jnp.maximum(i - first, 0)))

    def stream_shape(dil):
        return jax.ShapeDtypeStruct((B, S // dil, dil * GROUP_WIDTH), BF16)

    def stream_spec(dil):
        return pl.BlockSpec((1, tm // dil, dil * GROUP_WIDTH), lambda b, i: (b, i, 0))

    dils = [dil for _, dil in DILATED_GROUPS]
    qkv_order = [dils[0], dils[1], dils[2]] + [d for d in dils for _ in range(2)]
    mod_spec = pl.BlockSpec((1, 1, D_MODEL), lambda b, i: (b, 0, 0))
    out_shape = ([stream_shape(d) for d in qkv_order]
                 + [jax.ShapeDtypeStruct((B, 2 * GROUP_WIDTH, tail_rows[g]), F32) for g in range(N_GROUPS)]
                 + [jax.ShapeDtypeStruct((B * S, D_MODEL), BF16)] * 2
                 + [jax.ShapeDtypeStruct((B, CONV_HIST, CONV_CH), F32)])
    out_specs = ([stream_spec(d) for d in qkv_order] + [tail_spec(g) for g in range(N_GROUPS)]
                 + [tok_spec(D_MODEL)] * 2 + [pl.BlockSpec((1, CONV_HIST, CONV_CH), lambda b, i: (b, 0, 0))])
    n_stage = 3 * sum(1 for d in dils if d > 1)
    return pl.pallas_call(
        functools.partial(_k1_kernel, tail_rows, nt),
        grid=(B, nt),
        in_specs=[pl.BlockSpec((1, tm, D_MODEL), lambda b, i: (b, i, 0)), mod_spec, mod_spec,
                  _const_spec((1, D_MODEL)), _weight_spec((D_MODEL, IN_COLS)),
                  _const_spec((CONV_WIDTH, SUBLANES, CONV_CH)), _const_spec((SUBLANES, CONV_CH)),
                  _const_spec((1, CONV_CH)), _const_spec((1, CONV_CH)), _weight_spec((CONV_CH, D_MODEL))],
        out_specs=out_specs,
        out_shape=out_shape,
        scratch_shapes=([pltpu.VMEM((CONV_HIST + tm, CONV_CH), F32), pltpu.VMEM((tm, CONV_CH), BF16),
                         pltpu.VMEM((SUBLANES - 1, CONV_HIST + tm - SUBLANES, CONV_CH), F32),
                         pltpu.VMEM((tm, D_MODEL), F32)]
                        + [pltpu.VMEM((tm, LANES), F32)] * (n_stage * STAGE_HALVES)),
        compiler_params=_params(("arbitrary", "arbitrary")),
        name="inproj_prompt",
    )(x, sh1, sc1, norm_g, w_in_b, dw_w, dw_b, ln_g, ln_b, w_co_b)


def _k1s_kernel(n_steps, x_ref, sh_ref, sc_ref, g_ref, win_ref, st_ref, dww_ref, dwb_ref, lng_ref, lnb_ref, wco_ref,
                q0, q1, q2, k0, v0, k1, v1, k2, v2, ga_ref, sgb_ref, conv_ref):
    bd = st_ref.shape[1]
    hist = st_ref.shape[0]
    h = _rms_modulate(x_ref[...], g_ref[...], sc_ref[...], sh_ref[...])
    hb = h.astype(BF16)
    _, u, proj = _project_common(hb, win_ref, ((q0, q1, q2), (k0, k1, k2), (v0, v1, v2)),
                                 (1,) * N_GROUPS, [(None,) * 3] * N_GROUPS)

    def ext(j):
        return st_ref[j] if j < hist else u[(j - hist) * bd:(j - hist + 1) * bd, :]

    outs = []
    for t in range(n_steps):
        acc = jnp.zeros((bd, CONV_CH), F32) + dwb_ref[...]
        for k in range(CONV_WIDTH):
            acc = acc + dww_ref[k:k + 1, :] * ext(t + k + hist - (CONV_WIDTH - 1))
        outs.append(acc)
    for j in range(hist):
        conv_ref[j] = ext(j + n_steps)
    s = _conv_tail(jnp.concatenate(outs, axis=0), lng_ref[...], lnb_ref[...])
    a = jnp.dot(s.astype(BF16), wco_ref[...], preferred_element_type=F32)
    ga_ref[...] = (_sigmoid(proj(COL_GA, D_MODEL)) * a).astype(BF16)
    sgb_ref[...] = _sigmoid(proj(COL_GB, D_MODEL)).astype(BF16)


def _k1_sample(x_tm, sh1, sc1, norm_g, w_in_b, state_tm, dw_w, dw_b, ln_g, ln_b, w_co_b, n_steps):
    T = x_tm.shape[0]
    hist, bd, _ = state_tm.shape
    out_shape = ([jax.ShapeDtypeStruct((T, GROUP_WIDTH), F32)] * 9
                 + [jax.ShapeDtypeStruct((T, D_MODEL), BF16)] * 2
                 + [jax.ShapeDtypeStruct((hist, bd, CONV_CH), F32)])
    out_specs = ([_const_spec((T, GROUP_WIDTH))] * 9 + [_const_spec((T, D_MODEL))] * 2
                 + [_const_spec((hist, bd, CONV_CH))])
    return pl.pallas_call(
        functools.partial(_k1s_kernel, n_steps),
        grid=(1,),
        in_specs=[_const_spec((T, D_MODEL)), _const_spec((T, D_MODEL)), _const_spec((T, D_MODEL)),
                  _const_spec((1, D_MODEL)), _const_spec((D_MODEL, IN_COLS)), _const_spec((hist, bd, CONV_CH)),
                  _const_spec((CONV_WIDTH, CONV_CH)), _const_spec((1, CONV_CH)),
                  _const_spec((1, CONV_CH)), _const_spec((1, CONV_CH)), _const_spec((CONV_CH, D_MODEL))],
        out_specs=out_specs,
        out_shape=out_shape,
        compiler_params=_params(("arbitrary",)),
        name="inproj_sample",
    )(x_tm, sh1, sc1, norm_g, w_in_b, state_tm, dw_w, dw_b, ln_g, ln_b, w_co_b)


def _t5_bucket(dist):
    max_exact = N_BUCKETS // 2
    d_f = jnp.maximum(dist, 1).astype(F32)
    large = max_exact + (jnp.log(d_f / max_exact) / math.log(MAX_DISTANCE / max_exact)
                         * (N_BUCKETS - max_exact)).astype(jnp.int32)
    large = jnp.minimum(large, N_BUCKETS - 1)
    return jnp.where(dist < max_exact, dist, large)


def _bucket_lookup(rel_bias_g, dist):
    onehot = (_t5_bucket(dist)[..., None] == jnp.arange(N_BUCKETS)).astype(F32)
    return jnp.einsum('...b,bh->h...', onehot, rel_bias_g, precision=lax.Precision.HIGHEST)


def _prompt_bias(rel_bias_g, dil, n_keys):
    blk = n_keys
    i = jnp.arange(blk)[:, None]
    j = jnp.arange(2 * blk)[None, :]
    rel = i - j + blk
    valid = (rel >= 0) & (rel <= n_keys)
    bias = _bucket_lookup(rel_bias_g, jnp.clip(rel, 0, n_keys) * dil)
    bias = jnp.where(valid[None], bias, NEG_INF)
    return bias.reshape(HEADS_PER_GROUP * blk, 2 * blk).astype(F32)


def _sample_bias(rel_bias_g, dil, n_keys, lc, n_steps, q_rows):
    t = jnp.arange(q_rows)[:, None]
    pos = jnp.arange(lc)[None, :]
    dist_c = lc + t - pos
    dist_n = t - (jnp.arange(LANES)[None, :] - (LANES - n_steps))

    def table(dist, extra):
        ok = (dist >= 0) & (dist % dil == 0) & (dist // dil <= n_keys) & extra & (t < n_steps)
        b = _bucket_lookup(rel_bias_g, jnp.clip(dist, 0, None))
        b = jnp.where(ok[None], b, NEG_INF)
        return jnp.where((t >= n_steps)[None], 0.0, b).astype(F32)

    lane_ok = jnp.arange(LANES)[None, :] >= LANES - n_steps
    return table(dist_c, True), table(dist_n, lane_ok)


def _k2_kernel(q_ref, k_ref, kh_ref, v_ref, vh_ref, bias_ref, o_ref, lse_ref, kbuf, vbuf):
    i = pl.program_id(2)
    rows = q_ref.shape[1]
    n_streams = q_ref.shape[2] // GROUP_WIDTH
    blk = ATTN_BLOCK
    kbuf[0:blk, :] = kh_ref[0]
    kbuf[blk:blk + rows, :] = k_ref[0]
    vbuf[0:blk, :] = vh_ref[0]
    vbuf[blk:blk + rows, :] = v_ref[0]
    lane_head = lax.broadcasted_iota(jnp.int32, (blk, GROUP_WIDTH), 1) // HEAD_DIM
    col = lax.broadcasted_iota(jnp.int32, (HEADS_PER_GROUP * blk, 2 * blk), 1)
    first_mask = jnp.where((col < blk) & (i == 0), NEG_INF, 0.0).astype(F32)
    bias = bias_ref[...]
    for st in range(n_streams):
        cols = slice(st * GROUP_WIDTH, (st + 1) * GROUP_WIDTH)
        for j in range(rows // blk):
            qb = q_ref[0, j * blk:(j + 1) * blk, cols]
            q4 = jnp.concatenate([jnp.where(lane_head == h, qb, jnp.zeros_like(qb)) for h in range(HEADS_PER_GROUP)],
                                 axis=0)
            kc = kbuf[j * blk:(j + 2) * blk, cols]
            vc = vbuf[j * blk:(j + 2) * blk, cols]
            s = lax.dot_general(q4, kc, (((1,), (1,)), ((), ())), preferred_element_type=F32) + bias
            if j == 0:
                s = s + first_mask
            m = jnp.max(s, axis=-1, keepdims=True)
            p = jnp.exp(s - m)
            l = jnp.sum(p, axis=-1, keepdims=True)
            o4 = jnp.dot(p.astype(BF16), vc, preferred_element_type=F32) * (1.0 / l)
            lse4 = m + jnp.log(l)
            o = jnp.zeros((blk, GROUP_WIDTH), F32)
            lse = jnp.zeros((blk, GROUP_WIDTH), F32)
            for h in range(HEADS_PER_GROUP):
                sel = lane_head == h
                o = jnp.where(sel, o4[h * blk:(h + 1) * blk, :], o)
                lse = jnp.where(sel, lse4[h * blk:(h + 1) * blk, :], lse)
            o_ref[0, j * blk:(j + 1) * blk, cols] = o.astype(o_ref.dtype)
            lse_ref[0, j * blk:(j + 1) * blk, cols] = lse


def _k2_prompt(q, k, v, bias, dil):
    B, L, _ = q.shape
    rows = min(ATTN_ROWS, L)
    per = rows // ATTN_BLOCK
    n_streams = min(dil, max(1, ATTN_ROWS // rows))
    width = n_streams * GROUP_WIDTH
    main = pl.BlockSpec((1, rows, width), lambda b, r, i: (b, i, r))
    halo = pl.BlockSpec((1, ATTN_BLOCK, width), lambda b, r, i: (b, jnp.maximum(i * per - 1, 0), r))
    return pl.pallas_call(
        _k2_kernel,
        grid=(B, dil // n_streams, L // rows),
        in_specs=[main, main, halo, main, halo, _const_spec(bias.shape)],
        out_specs=[main, main],
        out_shape=[jax.ShapeDtypeStruct(q.shape, BF16), jax.ShapeDtypeStruct(q.shape, F32)],
        scratch_shapes=[pltpu.VMEM((ATTN_BLOCK + rows, width), BF16)] * 2,
        compiler_params=_params(("arbitrary", "arbitrary", "arbitrary")),
        name=f"attn_prompt_d{dil}",
    )(q, k, k, v, v, bias)


def _k2s_kernel(n_steps, q_ref, c0, c1, c2, t0, t1, t2, bc0, bc1, bc2, bn0, bn1, bn2,
                nc0, nc1, nc2, o0, o1, o2, l0, l1, l2):
    lane = lax.broadcasted_iota(jnp.int32, (2 * GROUP_WIDTH, LANES), 1)
    keep = LANES - n_steps
    groups = ((c0, t0, bc0, bn0, nc0, o0, l0), (c1, t1, bc1, bn1, nc1, o1, l1), (c2, t2, bc2, bn2, nc2, o2, l2))
    for bi in range(q_ref.shape[0]):
        for g, (c_ref, t_ref, bc_ref, bn_ref, nc_ref, o_ref, l_ref) in enumerate(groups):
            _sample_group(bi, g, lane, keep, q_ref, c_ref, t_ref, bc_ref, bn_ref, nc_ref, o_ref, l_ref)


def _sample_group(bi, g, lane, keep, q_ref, c_ref, t_ref, bc_ref, bn_ref, nc_ref, o_ref, l_ref):
    lc = c_ref.shape[2]
    n_tiles = lc // LANES
    new_rows = t_ref[bi]
    tail = jnp.concatenate([jnp.zeros((LANES - new_rows.shape[0], new_rows.shape[1]), F32), new_rows], axis=0).T
    cur = pltpu.roll(c_ref[bi, :, 0:LANES], keep, 1)
    for c in range(n_tiles):
        nxt = pltpu.roll(c_ref[bi, :, (c + 1) * LANES:(c + 2) * LANES], keep, 1) if c + 1 < n_tiles else tail
        nc_ref[bi, :, c * LANES:(c + 1) * LANES] = jnp.where(lane < keep, cur, nxt)
        cur = nxt
    for h in range(HEADS_PER_GROUP):
        lo = g * GROUP_WIDTH + h * HEAD_DIM
        qh = q_ref[bi, :, lo:lo + HEAD_DIM].astype(BF16)
        kh = c_ref[bi, h * HEAD_DIM:(h + 1) * HEAD_DIM, :].astype(BF16)
        vh = c_ref[bi, GROUP_WIDTH + h * HEAD_DIM:GROUP_WIDTH + (h + 1) * HEAD_DIM, :].astype(BF16)
        kt = tail[h * HEAD_DIM:(h + 1) * HEAD_DIM, :].astype(BF16)
        vt = tail[GROUP_WIDTH + h * HEAD_DIM:GROUP_WIDTH + (h + 1) * HEAD_DIM, :].astype(BF16)
        sc = jnp.dot(qh, kh, preferred_element_type=F32) + bc_ref[h]
        sn = jnp.dot(qh, kt, preferred_element_type=F32) + bn_ref[h]
        m = jnp.maximum(jnp.max(sc, axis=-1, keepdims=True), jnp.max(sn, axis=-1, keepdims=True))
        pc = jnp.exp(sc - m)
        pn = jnp.exp(sn - m)
        l = jnp.sum(pc, axis=-1, keepdims=True) + jnp.sum(pn, axis=-1, keepdims=True)
        nt_dims = (((1,), (1,)), ((), ()))
        o = (lax.dot_general(pc.astype(BF16), vh, nt_dims, preferred_element_type=F32)
             + lax.dot_general(pn.astype(BF16), vt, nt_dims, preferred_element_type=F32)) * (1.0 / l)
        o_ref[bi, :, h * HEAD_DIM:(h + 1) * HEAD_DIM] = o
        l_ref[bi, :, h * HEAD_DIM:(h + 1) * HEAD_DIM] = jnp.broadcast_to(m + jnp.log(l), o.shape)


def _k2_sample(q_b, caches, tails, biases_c, biases_n, n_steps):
    bd, q_rows, _ = q_b.shape
    bb = SAMPLE_BATCH if bd % SAMPLE_BATCH == 0 else 1
    per_b = lambda shape: pl.BlockSpec((bb,) + shape[1:], lambda b: (b,) + (0,) * (len(shape) - 1))
    ins = [q_b] + list(caches) + list(tails) + list(biases_c) + list(biases_n)
    in_specs = ([per_b(q_b.shape)] + [per_b(c.shape) for c in caches] + [per_b(t.shape) for t in tails]
                + [_const_spec(b.shape) for b in biases_c] + [_const_spec(b.shape) for b in biases_n])
    o_shape = jax.ShapeDtypeStruct((bd, q_rows, GROUP_WIDTH), F32)
    out_shape = [jax.ShapeDtypeStruct(c.shape, F32) for c in caches] + [o_shape] * 6
    out_specs = [per_b(c.shape) for c in caches] + [per_b(o_shape.shape)] * 6
    return pl.pallas_call(
        functools.partial(_k2s_kernel, n_steps),
        grid=(bd // bb,),
        in_specs=in_specs,
        out_specs=out_specs,
        out_shape=out_shape,
        compiler_params=_params(("arbitrary",)),
        name="attn_sample",
    )(*ins)


def _k3_kernel(x_ref, o0, o1, o2, l0, l1, l2, ga_ref, sgb_ref, g1_ref, sh2_ref, sc2_ref, gf_ref,
               wao_ref, wo_ref, wr_ref, br_ref, before_ref, cnt_in_ref,
               x2_ref, h2_ref, eid_ref, wts_ref, rank_ref, cnt_ref, carry, *stage_refs):
    @pl.when(pl.program_id(0) == 0)
    def _():
        carry[...] = cnt_in_ref[...]

    dils = tuple(x_ref.shape[0] // r.shape[-2] for r in (o0, o1, o2))
    pairs = [stage_refs[j:j + STAGE_HALVES] for j in range(0, len(stage_refs), STAGE_HALVES)]
    os_ = [_from_streams(r, d, pairs[2 * g]) for g, (r, d) in enumerate(zip((o0, o1, o2), dils))]
    ls = [_from_streams(r, d, pairs[2 * g + 1]) for g, (r, d) in enumerate(zip((l0, l1, l2), dils))]
    m = jnp.maximum(jnp.maximum(ls[0], ls[1]), ls[2])
    ws = [jnp.exp(l - m) for l in ls]
    den = ws[0] + ws[1] + ws[2]
    o = (ws[0] * os_[0] + ws[1] * os_[1] + ws[2] * os_[2]) / den
    b = jnp.dot(o.astype(BF16), wao_ref[...], preferred_element_type=F32)
    mixed = ga_ref[...].astype(F32) + sgb_ref[...].astype(F32) * b
    x2 = x_ref[...] + g1_ref[0] * jnp.dot(mixed.astype(BF16), wo_ref[...], preferred_element_type=F32)
    x2_ref[...] = x2
    h2 = _rms_modulate(x2, gf_ref[...], sc2_ref[0], sh2_ref[0])
    h2_ref[...] = h2
    h_hi = h2.astype(BF16)
    h_lo = (h2 - h_hi.astype(F32)).astype(BF16)
    nt_dims = (((1,), (1,)), ((), ()))
    lt = (lax.dot_general(wr_ref[0], h_hi, nt_dims, preferred_element_type=F32)
          + lax.dot_general(wr_ref[0], h_lo, nt_dims, preferred_element_type=F32)
          + lax.dot_general(wr_ref[1], h_hi, nt_dims, preferred_element_type=F32)) + br_ref[...]
    tm = h2.shape[0]
    gl = lt[0:8, :]
    gmax = jnp.max(gl, axis=0, keepdims=True)
    r8 = lax.broadcasted_iota(jnp.int32, (8, tm), 0)
    grp = jnp.min(jnp.where(gl == gmax, r8, 8), axis=0, keepdims=True)
    p_grp = 1.0 / jnp.sum(jnp.exp(gl - gmax), axis=0, keepdims=True)
    es = jnp.zeros((EXPERTS_PER_GROUP, tm), F32)
    for g in range(N_EXPERT_GROUPS):
        es = jnp.where(grp == g, lt[8 + 8 * g:16 + 8 * g, :], es)
    v1 = jnp.max(es, axis=0, keepdims=True)
    i1 = jnp.min(jnp.where(es == v1, r8, 8), axis=0, keepdims=True)
    rest = jnp.where(r8 == i1, -jnp.inf, es)
    v2 = jnp.max(rest, axis=0, keepdims=True)
    i2 = jnp.min(jnp.where(rest == v2, r8, 8), axis=0, keepdims=True)
    e21 = jnp.exp(v2 - v1)
    w1 = p_grp / (1.0 + e21)
    e1 = grp * EXPERTS_PER_GROUP + i1
    e2 = grp * EXPERTS_PER_GROUP + i2
    eid_ref[0, 0:1, :] = e1
    eid_ref[0, 1:2, :] = e2
    wts_ref[0, 0:1, :] = w1
    wts_ref[0, 1:2, :] = w1 * e21
    r_e = lax.broadcasted_iota(jnp.int32, (N_EXPERTS, tm), 0)
    hit1 = r_e == e1
    hit2 = r_e == e2
    both = jnp.where(hit1 | hit2, 1.0, 0.0)
    base = carry[...] + jnp.dot(both.astype(BF16), before_ref[...], preferred_element_type=F32)
    rank_ref[0, 0:1, :] = jnp.sum(jnp.where(hit1, base, 0.0), axis=0, keepdims=True).astype(jnp.int32)
    rank_ref[0, 1:2, :] = jnp.sum(jnp.where(hit2, base, 0.0), axis=0, keepdims=True).astype(jnp.int32)
    total = carry[...] + jnp.sum(both, axis=1, keepdims=True)
    carry[...] = total
    cnt_ref[...] = total


def _k3(x, o_l, g1, sh2, sc2, ga, sgb, norm_ffn_g, w_ao_b, w_o_b, w_r, b_r, cnt_in, tiles_per_mod):
    T = x.shape[0]
    tm = min(TOKEN_TILE, T)
    nt = T // tm
    tok = lambda w: pl.BlockSpec((tm, w), lambda t: (t, 0))
    mod = pl.BlockSpec((1,) + g1.shape[1:], lambda t: (t // tiles_per_mod, 0, 0))
    small = pl.BlockSpec((1, 2, tm), lambda t: (t, 0, 0))
    small_i = jax.ShapeDtypeStruct((nt, 2, tm), jnp.int32)
    before = (jnp.arange(tm)[:, None] < jnp.arange(tm)[None, :]).astype(BF16)

    def attn_spec(a):
        if a.ndim == 2:
            return tok(GROUP_WIDTH)
        dil = a.shape[2] // GROUP_WIDTH
        return pl.BlockSpec((1, tm // dil, a.shape[2]), lambda t: (t // tiles_per_mod, t % tiles_per_mod, 0))

    attn_in = [o_l[0][0], o_l[1][0], o_l[2][0], o_l[0][1], o_l[1][1], o_l[2][1]]
    return pl.pallas_call(
        _k3_kernel,
        grid=(nt,),
        in_specs=[tok(D_MODEL)] + [attn_spec(a) for a in attn_in] + [tok(D_MODEL)] * 2 + [mod] * 3
                 + [_const_spec((1, D_MODEL)), _weight_spec(w_ao_b.shape), _weight_spec(w_o_b.shape),
                    _const_spec(w_r.shape), _const_spec(b_r.shape), _weight_spec((tm, tm)),
                    _const_spec((N_EXPERTS, 1))],
        out_specs=[tok(D_MODEL), tok(D_MODEL), small, small, small, _const_spec((N_EXPERTS, 1))],
        out_shape=[jax.ShapeDtypeStruct((T, D_MODEL), F32), jax.ShapeDtypeStruct((T, D_MODEL), F32),
                   small_i, jax.ShapeDtypeStruct((nt, 2, tm), F32), small_i,
                   jax.ShapeDtypeStruct((N_EXPERTS, 1), F32)],
        scratch_shapes=([pltpu.VMEM((N_EXPERTS, 1), F32)]
                        + [pltpu.VMEM((tm, LANES), F32)] * (2 * N_GROUPS * STAGE_HALVES)),
        compiler_params=_params(("arbitrary",)),
        name="merge_router",
    )(x, *attn_in, ga, sgb, g1, sh2, sc2, norm_ffn_g, w_ao_b, w_o_b, w_r, b_r, before, cnt_in)


def _rows_wait(n_rows, hbm, vmem, sem):
    pltpu.make_async_copy(hbm.at[pl.ds(0, n_rows)], vmem, sem).wait()


def _dispatch_kernel(n_tiles, n_first, dest_ref, fill_ref, ha_ref, hb_ref, xs_ref, buf, sem, zbuf, zsem):
    i = pl.program_id(0)
    slot = i % 2 if n_tiles > 1 else 0
    tm = ha_ref.shape[0]

    def drain(s):
        for _ in range(2):
            _rows_wait(tm, xs_ref, buf.at[s], sem.at[s])

    @pl.when(i == 0)
    def _():
        zbuf[...] = jnp.zeros(zbuf.shape, F32)

        def fill(row):
            return pltpu.make_async_copy(zbuf, xs_ref.at[pl.ds(pl.multiple_of(row, MOE_ROWS), MOE_ROWS)], zsem)

        n_blocks = xs_ref.shape[0] // MOE_ROWS
        n_used = fill_ref[N_EXPERTS]

        def start_block(j, carry):
            fill(j * MOE_ROWS).start()
            return carry

        def wait_block(j, carry):
            fill(j * MOE_ROWS).wait()
            return carry

        for e in range(N_EXPERTS):
            @pl.when(fill_ref[e] >= 0)
            def _(e=e):
                fill(fill_ref[e]).start()
        lax.fori_loop(n_used, n_blocks, start_block, 0)
        for e in range(N_EXPERTS):
            @pl.when(fill_ref[e] >= 0)
            def _(e=e):
                fill(fill_ref[e]).wait()
        lax.fori_loop(n_used, n_blocks, wait_block, 0)

    if n_tiles > 2:
        @pl.when(i >= 2)
        def _():
            drain(slot)

    @pl.when(i < n_first)
    def _():
        buf[slot] = ha_ref[...]

    @pl.when(i >= n_first)
    def _():
        buf[slot] = hb_ref[...]

    for r in range(tm):
        for k in range(2):
            pltpu.make_async_copy(buf.at[slot, pl.ds(r, 1)], xs_ref.at[pl.ds(dest_ref[0, k, r], 1)],
                                  sem.at[slot]).start(priority=k)

    if n_tiles == 1:
        drain(0)
    else:
        @pl.when(i == n_tiles - 1)
        def _():
            drain(slot)
            drain(1 - slot)


def _dispatch(dest, fill_rows, h_a, h_b, n_slots):
    nt, _, tm = dest.shape
    n_first = h_a.shape[0] // tm
    assert h_a.shape[0] % tm == 0 and h_b.shape[0] == (nt - n_first) * tm
    return pl.pallas_call(
        functools.partial(_dispatch_kernel, nt, n_first),
        grid=(nt,),
        in_specs=[pl.BlockSpec((1, 2, tm), lambda t: (t, 0, 0), memory_space=pltpu.SMEM),
                  pl.BlockSpec(memory_space=pltpu.SMEM),
                  pl.BlockSpec((tm, D_MODEL), lambda t: (jnp.minimum(t, n_first - 1), 0)),
                  pl.BlockSpec((tm, D_MODEL), lambda t: (jnp.maximum(t - n_first, 0), 0))],
        out_specs=pl.BlockSpec(memory_space=pl.ANY),
        out_shape=jax.ShapeDtypeStruct((n_slots, D_MODEL), F32),
        scratch_shapes=[pltpu.VMEM((2, tm, D_MODEL), F32), pltpu.SemaphoreType.DMA((2,)),
                        pltpu.VMEM((MOE_ROWS, D_MODEL), F32), pltpu.SemaphoreType.DMA(())],
        compiler_params=_params(("arbitrary",)),
        name="dispatch_rows",
    )(dest, fill_rows, h_a, h_b)


def _k4_kernel(be_ref, nused_ref, x_ref, wg_ref, wu_ref, wd_ref, y_ref, wg_b, wu_b, wd_b):
    i = pl.program_id(0)

    @pl.when((i == 0) | (be_ref[i] != be_ref[jnp.maximum(i - 1, 0)]))
    def _():
        wg_b[...] = wg_ref[0].astype(BF16)
        wu_b[...] = wu_ref[0].astype(BF16)
        wd_b[...] = wd_ref[0].astype(BF16)

    @pl.when(i < nused_ref[0])
    def _():
        xb = x_ref[...].astype(BF16)
        gate = jnp.dot(xb, wg_b[...], preferred_element_type=F32)
        up = jnp.dot(xb, wu_b[...], preferred_element_type=F32)
        mid = (_silu(gate) * up).astype(BF16)
        y_ref[...] = jnp.dot(mid, wd_b[...], preferred_element_type=F32)

    @pl.when(i >= nused_ref[0])
    def _():
        y_ref[...] = jnp.zeros(y_ref.shape, F32)


def _k4(block_expert, n_used, xs, w_gate, w_up, w_down):
    n_blocks = block_expert.shape[0]
    rows = MOE_ROWS
    weight = lambda shape: pl.BlockSpec((1,) + shape, lambda i, be, nu: (be[i], 0, 0))
    grid_spec = pltpu.PrefetchScalarGridSpec(
        num_scalar_prefetch=2,
        grid=(n_blocks,),
        in_specs=[pl.BlockSpec((rows, D_MODEL), lambda i, be, nu: (jnp.minimum(i, nu[0] - 1), 0)),
                  weight((D_MODEL, D_FF_EXPERT)), weight((D_MODEL, D_FF_EXPERT)), weight((D_FF_EXPERT, D_MODEL))],
        out_specs=pl.BlockSpec((rows, D_MODEL), lambda i, be, nu: (i, 0)),
        scratch_shapes=[pltpu.VMEM((D_MODEL, D_FF_EXPERT), BF16), pltpu.VMEM((D_MODEL, D_FF_EXPERT), BF16),
                        pltpu.VMEM((D_FF_EXPERT, D_MODEL), BF16)],
    )
    return pl.pallas_call(
        _k4_kernel,
        grid_spec=grid_spec,
        out_shape=jax.ShapeDtypeStruct((n_blocks * rows, D_MODEL), F32),
        compiler_params=_params(("arbitrary",)),
        name="expert_blocks",
    )(block_expert, n_used, xs, w_gate, w_up, w_down)


def _k5_kernel(n_tiles, idx_ref, x2_ref, w_ref, g2_ref, gfin_ref, ys_hbm, y_ref, ybuf, sem):
    i = pl.program_id(0)
    tm = x2_ref.shape[0]

    @pl.when(i < n_tiles)
    def _():
        slot = i % 2
        for r in range(tm):
            for k in range(2):
                pltpu.make_async_copy(ys_hbm.at[pl.ds(idx_ref[0, k, r], 1)], ybuf.at[slot, k, pl.ds(r, 1)],
                                      sem.at[slot]).start(priority=k)

    @pl.when(i >= 1)
    def _():
        slot = (i - 1) % 2
        for k in range(2):
            _rows_wait(tm, ys_hbm, ybuf.at[slot, k], sem.at[slot])
        w = w_ref[...]
        f = ybuf[slot, 0] * w[:, 0:1] + ybuf[slot, 1] * w[:, 1:2]
        y = x2_ref[...] + g2_ref[0] * f
        r = lax.rsqrt(jnp.mean(y * y, axis=-1, keepdims=True) + EPS)
        y_ref[...] = (y * r) * gfin_ref[...]


def _k5(dest, x2, wts, g2, norm_final_g, ys, tiles_per_mod):
    nt, _, tm = dest.shape
    T = x2.shape[0]
    prev = lambda t: jnp.maximum(t - 1, 0)
    return pl.pallas_call(
        functools.partial(_k5_kernel, nt),
        grid=(nt + 1,),
        in_specs=[pl.BlockSpec((1, 2, tm), lambda t: (jnp.minimum(t, nt - 1), 0, 0), memory_space=pltpu.SMEM),
                  pl.BlockSpec((tm, D_MODEL), lambda t: (prev(t), 0)),
                  pl.BlockSpec((tm, 2), lambda t: (prev(t), 0)),
                  pl.BlockSpec((1,) + g2.shape[1:], lambda t: (prev(t) // tiles_per_mod, 0, 0)),
                  _const_spec((1, D_MODEL)), pl.BlockSpec(memory_space=pl.ANY)],
        out_specs=pl.BlockSpec((tm, D_MODEL), lambda t: (prev(t), 0)),
        out_shape=jax.ShapeDtypeStruct((T, D_MODEL), F32),
        scratch_shapes=[pltpu.VMEM((2, 2, tm, D_MODEL), F32), pltpu.SemaphoreType.DMA((2,))],
        compiler_params=_params(("arbitrary",)),
        name="combine_norm",
    )(dest, x2, wts, g2, norm_final_g, ys)


def _slot_tables(counts, n_blocks):
    padded = (counts + MOE_ROWS - 1) // MOE_ROWS * MOE_ROWS
    pend = jnp.cumsum(padded)
    pstart = pend - padded
    block_lo = jnp.arange(n_blocks, dtype=jnp.int32) * MOE_ROWS
    block_expert = jnp.minimum(jnp.sum(pend[None, :] <= block_lo[:, None], axis=1), N_EXPERTS - 1).astype(jnp.int32)
    n_used = (pend[-1] // MOE_ROWS).astype(jnp.int32).reshape(1)
    fill_rows = jnp.concatenate([jnp.where(padded > 0, pend - MOE_ROWS, -1).astype(jnp.int32), n_used])
    return pstart.astype(jnp.int32), block_expert, n_used, fill_rows


def _slots(eid, rank, pstart):
    sel = eid[..., None] == jnp.arange(N_EXPERTS, dtype=jnp.int32)
    return jnp.sum(jnp.where(sel, pstart, 0), axis=-1).astype(jnp.int32) + rank


def _cache_view(cache):
    bd, lc = cache.shape[:2]
    return cache.transpose(0, 2, 3, 4, 1).reshape(bd, 2 * GROUP_WIDTH, lc)


def _cache_unview(view):
    bd, _, lc = view.shape
    return view.reshape(bd, 2, HEADS_PER_GROUP, HEAD_DIM, lc).transpose(0, 4, 1, 2, 3)


def kernel(x_prompt, x_sample, c_prompt, c_sample, cache_kv_w128, cache_kv_w512, cache_kv_w2048, state_conv, rel_bias, norm_mix_g, norm_ffn_g, w_mod, b_mod, w_in, dw_w, dw_b, ln_conv_g, ln_conv_b, w_conv_out, w_attn_out, w_out, w_router_group, b_router_group, w_router_expert, b_router_expert, w_exp_gate, w_exp_up, w_exp_down, norm_final_g):
    assert norm_mix_g.shape[0] == 1, "single layer"
    B, S, D = x_prompt.shape
    Bd, Td, _ = x_sample.shape
    caches_in = (cache_kv_w128[0], cache_kv_w512[0], cache_kv_w2048[0])
    for (win, dil), c in zip(DILATED_GROUPS, caches_in):
        assert c.shape[1] >= (win // dil) * dil and c.shape[1] % LANES == 0 and S % (dil * ATTN_BLOCK) == 0

    wi = w_in[0]
    q_cols = wi[:, :ATTN_WIDTH] * (HEAD_DIM ** -0.5)
    kv_cols = []
    for g in range(N_GROUPS):
        kv_cols += [wi[:, ATTN_WIDTH + g * GROUP_WIDTH:ATTN_WIDTH + (g + 1) * GROUP_WIDTH],
                    wi[:, 2 * ATTN_WIDTH + g * GROUP_WIDTH:2 * ATTN_WIDTH + (g + 1) * GROUP_WIDTH]]
    w_in_b = jnp.concatenate([q_cols] + kv_cols + [wi[:, 3 * ATTN_WIDTH:]], axis=1).astype(BF16)
    w_co_b = w_conv_out[0].astype(BF16)
    w_ao_b = w_attn_out[0].astype(BF16)
    w_o_b = w_out[0].astype(BF16)
    w_r = jnp.zeros((ROUTER_ROWS, D), F32)
    w_r = w_r.at[0:N_EXPERT_GROUPS].set(w_router_group[0].T)
    w_r = w_r.at[8:].set(w_router_expert[0].reshape(D, N_EXPERTS).T)
    w_r_hi = w_r.astype(BF16)
    w_r = jnp.stack([w_r_hi, (w_r - w_r_hi.astype(F32)).astype(BF16)])
    b_r = jnp.full((ROUTER_ROWS, 1), NEG_INF, F32)
    b_r = b_r.at[0:N_EXPERT_GROUPS, 0].set(b_router_group[0])
    b_r = b_r.at[8:, 0].set(b_router_expert[0].reshape(N_EXPERTS))
    row = lambda v: v.reshape(1, -1)

    n_seq = B + Bd
    n_seq_pad = -(-n_seq // 16) * 16
    c_all = jnp.concatenate([c_prompt, c_sample, jnp.zeros((n_seq_pad - n_seq, D), F32)], axis=0)
    mod = _modulation(c_all, w_mod[0], row(b_mod[0]))
    mod_p = [mod[:B, j * D:(j + 1) * D].reshape(B, 1, D) for j in range(6)]
    mod_s = [jnp.tile(mod[B:B + Bd, j * D:(j + 1) * D], (Td, 1)) for j in range(6)]

    dw_w_rows = jnp.broadcast_to(dw_w[0][:, None, :], (CONV_WIDTH, SUBLANES, CONV_CH))
    dw_b_rows = jnp.broadcast_to(dw_b[0][None, :], (SUBLANES, CONV_CH))
    (q0, q1, q2, k0, v0, k1, v1, k2, v2, kvt0, kvt1, kvt2, ga, sgb, convp) = _k1_prompt(
        x_prompt, mod_p[0], mod_p[1], row(norm_mix_g[0]), w_in_b, dw_w_rows, dw_b_rows,
        row(ln_conv_g[0]), row(ln_conv_b[0]), w_co_b)
    o_l = []
    for g, ((win, dil), qg, kg, vg) in enumerate(zip(DILATED_GROUPS, (q0, q1, q2), (k0, k1, k2), (v0, v1, v2))):
        bias = _prompt_bias(rel_bias[:, g * HEADS_PER_GROUP:(g + 1) * HEADS_PER_GROUP], dil, win // dil)
        o_l.append(_k2_prompt(qg, kg, vg, bias, dil))
    xp = x_prompt.reshape(B * S, D)
    tm3 = min(TOKEN_TILE, B * S)
    x2, h2, eid, wts, rank, cnt_p = _k3(xp, o_l, mod_p[2], mod_p[3], mod_p[4], ga, sgb, row(norm_ffn_g[0]),
                                        w_ao_b, w_o_b, w_r, b_r, jnp.zeros((N_EXPERTS, 1), F32), S // tm3)
    kv_p = [_cache_unview(kvt)[None] for kvt in (kvt0, kvt1, kvt2)]
    conv_p = convp[:, CONV_HIST - (CONV_WIDTH - 1):, :][None]

    Ts = Td * Bd
    xs = x_sample.transpose(1, 0, 2).reshape(Ts, D)
    state_tm = state_conv[0].transpose(1, 0, 2)
    (sq0, sq1, sq2, sk0, sv0, sk1, sv1, sk2, sv2, ga_s, sgb_s, conv_tm) = _k1_sample(
        xs, mod_s[0], mod_s[1], row(norm_mix_g[0]), w_in_b, state_tm, dw_w[0], row(dw_b[0]),
        row(ln_conv_g[0]), row(ln_conv_b[0]), w_co_b, Td)
    q_rows = PACKED_ROWS
    assert Td <= SUBLANES
    q_b = jnp.concatenate([sq0, sq1, sq2], axis=1).reshape(Td, Bd, ATTN_WIDTH).transpose(1, 0, 2)
    q_b = jnp.pad(q_b, ((0, 0), (0, q_rows - Td), (0, 0)))
    views, tails, biases_c, biases_n = [], [], [], []
    for g, ((win, dil), c, sk, sv) in enumerate(zip(DILATED_GROUPS, caches_in, (sk0, sk1, sk2), (sv0, sv1, sv2))):
        lc = c.shape[1]
        views.append(_cache_view(c))
        new_kv = jnp.concatenate([sk, sv], axis=1).reshape(Td, Bd, 2 * GROUP_WIDTH).transpose(1, 0, 2)
        tails.append(jnp.pad(new_kv, ((0, 0), (SUBLANES - Td, 0), (0, 0))))
        bc, bn = _sample_bias(rel_bias[:, g * HEADS_PER_GROUP:(g + 1) * HEADS_PER_GROUP], dil, win // dil, lc, Td, q_rows)
        biases_c.append(bc)
        biases_n.append(bn)
    (nc0, nc1, nc2, so0, so1, so2, sl0, sl1, sl2) = _k2_sample(q_b, views, tails, biases_c, biases_n, Td)
    to_tm = lambda a: a[:, :Td, :].transpose(1, 0, 2).reshape(Ts, GROUP_WIDTH)
    o_l_s = [(to_tm(so0), to_tm(sl0)), (to_tm(so1), to_tm(sl1)), (to_tm(so2), to_tm(sl2))]
    tm3s = min(TOKEN_TILE, Ts)
    tiled = lambda m: m.reshape(Ts // tm3s, tm3s, D)
    x2s, h2s, eid_s, wts_s, rank_s, cnt_all = _k3(xs, o_l_s, tiled(mod_s[2]), tiled(mod_s[3]), tiled(mod_s[4]),
                                                  ga_s, sgb_s, row(norm_ffn_g[0]), w_ao_b, w_o_b, w_r, b_r, cnt_p, 1)

    n_assign = 2 * (B * S + Ts)
    n_blocks = -(-n_assign // MOE_ROWS) + N_EXPERTS
    pstart, block_expert, n_used, fill_rows = _slot_tables(cnt_all[:, 0].astype(jnp.int32), n_blocks)
    dest_p = _slots(eid, rank, pstart)
    dest_s = _slots(eid_s, rank_s, pstart)
    n_slots = n_blocks * MOE_ROWS
    assert tm3s == tm3, "prompt and sample token tiles must match to share the dispatch"
    slots = _dispatch(jnp.concatenate([dest_p, dest_s], axis=0), fill_rows, h2, h2s, n_slots)
    ys = _k4(block_expert, n_used, slots, w_exp_gate[0], w_exp_up[0], w_exp_down[0])
    per_token = lambda a: a.transpose(0, 2, 1).reshape(-1, 2)
    y_prompt = _k5(dest_p, x2, per_token(wts), mod_p[5], row(norm_final_g), ys, S // tm3).reshape(B, S, D)
    y_s = _k5(dest_s, x2s, per_token(wts_s), tiled(mod_s[5]), row(norm_final_g), ys, 1)
    y_sample = y_s.reshape(Td, Bd, D).transpose(1, 0, 2)
    kv_s = [_cache_unview(nc)[None] for nc in (nc0, nc1, nc2)]
    conv_s = conv_tm.transpose(1, 0, 2)[None]

    return (y_prompt, y_sample, kv_p[0], kv_p[1], kv_p[2], conv_p,
            kv_s[0], kv_s[1], kv_s[2], conv_s)
```
